```python
import math
import jax, jax.numpy as jnp
from jax import lax
import numpy as np

D_MODEL = 1024
BATCH = 32
SEQ = 256
DEPTH = 2
DEC_BATCH = 2
DEC_SEQ = 1024
PAST_LEN = 256

GRID_W = 64
ROPE_THETA = 10000.0
EPS = 1e-6
NEG_INF = -1e30
Q_BLOCK = 128
CONV_CH = D_MODEL // 2
CONV_WIDTH = 31
DIFF_HEADS = 4
DIFF_SUB_DIM = 64
DIFF_V_DIM = 2 * DIFF_SUB_DIM
DIFF_WIDTH = DIFF_HEADS * DIFF_V_DIM
HEAD_DIM = 64
GQA_HEADS = D_MODEL // HEAD_DIM
GQA_KV_HEADS = 4
GQA_GROUP = GQA_HEADS // GQA_KV_HEADS
WINDOW = 128
FFN_HIDDEN = -(-8 * D_MODEL // (3 * 256)) * 256
N_MOD = 6

kernel_name = 'hybrid_diffusion_prefix_step'


def rms_norm(x, g):
    xf = x.astype(jnp.float32)
    y = xf * lax.rsqrt(jnp.mean(xf * xf, axis=-1, keepdims=True) + EPS)
    return (y * g.astype(jnp.float32)).astype(x.dtype)


def layer_norm(x, g, b):
    xf = x.astype(jnp.float32)
    mu = jnp.mean(xf, axis=-1, keepdims=True)
    xc = xf - mu
    y = xc * lax.rsqrt(jnp.mean(xc * xc, axis=-1, keepdims=True) + EPS)
    return (y * g.astype(jnp.float32) + b.astype(jnp.float32)).astype(x.dtype)


def rope_1d(x, pos):
    d = x.shape[-1]
    inv = 1.0 / (ROPE_THETA ** (jnp.arange(0, d, 2, dtype=jnp.float32) / d))
    ang = pos.astype(jnp.float32)[:, None] * inv[None, :]
    cos = jnp.cos(ang)[None, :, None, :]
    sin = jnp.sin(ang)[None, :, None, :]
    xf = x.astype(jnp.float32)
    x1, x2 = xf[..., : d // 2], xf[..., d // 2:]
    return jnp.concatenate([x1 * cos - x2 * sin, x2 * cos + x1 * sin], axis=-1).astype(x.dtype)


def axial_rope(x):
    shp = x.shape
    B, S, D = shp[0], shp[1], shp[-1]
    xr = x.reshape(B, S, -1, D)
    t = jnp.arange(S)
    rows, cols = t // GRID_W, t % GRID_W
    half = D // 2
    out = jnp.concatenate([rope_1d(xr[..., :half], rows), rope_1d(xr[..., half:], cols)], axis=-1)
    return out.reshape(shp)


def blocked_queries(fn, q):
    B, S = q.shape[:2]
    nb = S // Q_BLOCK
    qb = jnp.moveaxis(q.reshape((B, nb, Q_BLOCK) + q.shape[2:]), 1, 0)
    o = lax.map(fn, qb)
    return jnp.moveaxis(o, 0, 1).reshape((B, S) + o.shape[3:])


def conformer_conv(u, conv_w, conv_b, ln_g, ln_b):
    a, gt = jnp.split(u, 2, axis=-1)
    z = a * jax.nn.sigmoid(gt)
    z = lax.conv_general_dilated(
        z, conv_w.astype(z.dtype), window_strides=(1,),
        padding=[(CONV_WIDTH // 2, CONV_WIDTH // 2)],
        dimension_numbers=('NWC', 'WIO', 'NWC'),
        feature_group_count=CONV_CH) + conv_b
    return jax.nn.silu(layer_norm(z, ln_g, ln_b))


def diff_attention(q, k, v, lam):
    scale = DIFF_SUB_DIM ** -0.5
    def one(qb):
        s = jnp.einsum('bqhjd,bkhjd->bhjqk', qb, k).astype(jnp.float32) * scale
        p = jax.nn.softmax(s, axis=-1)
        w = p[:, :, 0] - lam * p[:, :, 1]
        return jnp.einsum('bhqk,bkhe->bqhe', w.astype(v.dtype), v)
    return blocked_queries(one, q)


def sink_attention(qb, k, v, sink, mask):
    s = jnp.einsum('bqhgd,bkhd->bhgqk', qb, k).astype(jnp.float32) * (HEAD_DIM ** -0.5)
    if mask is not None:
        s = jnp.where(mask, s, NEG_INF)
    sk = jnp.broadcast_to(sink.astype(jnp.float32)[None, :, :, None, None], s.shape[:-1] + (1,))
    p = jax.nn.softmax(jnp.concatenate([s, sk], axis=-1), axis=-1)[..., :-1]
    return jnp.einsum('bhgqk,bkhd->bqhgd', p.astype(v.dtype), v)


def windowed_sink_attention(q, k, v, k_ctx, v_ctx, sink):
    B, S = q.shape[:2]
    nb = S // Q_BLOCK
    span = Q_BLOCK + 2 * WINDOW
    pad = ((0, 0), (WINDOW, WINDOW), (0, 0), (0, 0))
    kp, vp = jnp.pad(k, pad), jnp.pad(v, pad)
    i = jnp.arange(Q_BLOCK)[:, None]
    j = jnp.arange(span)[None, :]
    band = jnp.abs(i - j + WINDOW) <= WINDOW
    ctx_valid = jnp.ones((Q_BLOCK, k_ctx.shape[1]), dtype=bool)
    def one(b):
        start = b * Q_BLOCK
        qb = lax.dynamic_slice_in_dim(q, start, Q_BLOCK, axis=1)
        kw = lax.dynamic_slice_in_dim(kp, start, span, axis=1)
        vw = lax.dynamic_slice_in_dim(vp, start, span, axis=1)
        kpos = start - WINDOW + j
        valid = band & (kpos >= 0) & (kpos < S)
        keys = jnp.concatenate([kw, k_ctx], axis=1)
        vals = jnp.concatenate([vw, v_ctx], axis=1)
        mask = jnp.concatenate([valid, ctx_valid], axis=1)
        return sink_attention(qb, keys, vals, sink, mask)
    o = lax.map(one, jnp.arange(nb))
    return jnp.moveaxis(o, 0, 1).reshape(q.shape)


def mixer_conv_diff(h, p, layer, ctx_kv):
    w_in, conv_w, conv_b, ln_g, ln_b, lam_vec, subln_g, w_out = p
    B, S, _ = h.shape
    proj = h @ w_in
    o0 = 2 * CONV_CH
    u = proj[..., :o0]
    q = proj[..., o0:o0 + DIFF_WIDTH].reshape(B, S, DIFF_HEADS, 2, DIFF_SUB_DIM)
    k = proj[..., o0 + DIFF_WIDTH:o0 + 2 * DIFF_WIDTH].reshape(B, S, DIFF_HEADS, 2, DIFF_SUB_DIM)
    v = proj[..., o0 + 2 * DIFF_WIDTH:].reshape(B, S, DIFF_HEADS, DIFF_V_DIM)
    conv_out = conformer_conv(u, conv_w, conv_b, ln_g, ln_b)
    if ctx_kv is None:
        k_all, v_all, new_kv = k, v, (k, v)
    else:
        q, k = axial_rope(q), axial_rope(k)
        k_all = jnp.concatenate([k, ctx_kv[0]], axis=1)
        v_all = jnp.concatenate([v, ctx_kv[1]], axis=1)
        new_kv = None
    lam_init = 0.8 - 0.6 * math.exp(-0.3 * layer)
    lf = lam_vec.astype(jnp.float32)
    lam = jnp.exp(jnp.sum(lf[0] * lf[1])) - jnp.exp(jnp.sum(lf[2] * lf[3])) + lam_init
    o = diff_attention(q, k_all, v_all, lam)
    o = rms_norm(o, subln_g) * (1.0 - lam_init)
    mixed = jnp.concatenate([conv_out, o.reshape(B, S, DIFF_WIDTH)], axis=-1) @ w_out
    return mixed, new_kv


def mixer_window_gqa(h, p, ctx_kv):
    w_qkv, sink, w_out = p
    B, S, _ = h.shape
    nq, nkv = GQA_HEADS * HEAD_DIM, GQA_KV_HEADS * HEAD_DIM
    proj = h @ w_qkv
    q = proj[..., :nq].reshape(B, S, GQA_KV_HEADS, GQA_GROUP, HEAD_DIM)
    k = proj[..., nq:nq + nkv].reshape(B, S, GQA_KV_HEADS, HEAD_DIM)
    v = proj[..., nq + nkv:].reshape(B, S, GQA_KV_HEADS, HEAD_DIM)
    sink_g = sink.reshape(GQA_KV_HEADS, GQA_GROUP)
    if ctx_kv is None:
        o = blocked_queries(lambda qb: sink_attention(qb, k, v, sink_g, None), q)
        new_kv = (k, v)
    else:
        q, k = axial_rope(q), axial_rope(k)
        o = windowed_sink_attention(q, k, v, ctx_kv[0], ctx_kv[1], sink_g)
        new_kv = None
    return o.reshape(B, S, nq) @ w_out, new_kv


def trunk_layer(x, cond, layer, common, mix, ctx_kv):
    mod_w, mod_b, norm_g, w_gu, w_down = common
    m = jax.nn.silu(cond) @ mod_w + mod_b
    sh1, sc1, g1, sh2, sc2, g2 = jnp.split(m[:, None, :], N_MOD, axis=-1)
    h = rms_norm(x, norm_g[0]) * (1 + sc1) + sh1
    if layer % 2 == 0:
        mixed, kv = mixer_conv_diff(h, mix, layer, ctx_kv)
    else:
        mixed, kv = mixer_window_gqa(h, mix, ctx_kv)
    x = x + g1 * rms_norm(mixed, norm_g[1])
    h = rms_norm(x, norm_g[2]) * (1 + sc2) + sh2
    gate, up = jnp.split(h @ w_gu, 2, axis=-1)
    f = (jax.nn.silu(gate) * up) @ w_down
    x = x + g2 * rms_norm(f, norm_g[3])
    return x, kv


def setup_inputs(seed: int = 0) -> dict:
    key = jax.random.key(seed)
    ks = iter(jax.random.split(key, 40))
    def nrm(shape, scale):
        return jax.random.normal(next(ks), shape, jnp.float32) * scale
    D = D_MODEL
    s = D ** -0.5
    return {
        'x_prompt': nrm((BATCH, SEQ, D), 1.0),
        'x_sample': nrm((DEC_BATCH, DEC_SEQ, D), 1.0),
        'cache_k0': nrm((DEC_BATCH, PAST_LEN, DIFF_HEADS, 2, DIFF_SUB_DIM), 1.0),
        'cache_v0': nrm((DEC_BATCH, PAST_LEN, DIFF_HEADS, DIFF_V_DIM), 1.0),
        'cache_k1': nrm((DEC_BATCH, PAST_LEN, GQA_KV_HEADS, HEAD_DIM), 1.0),
        'cache_v1': nrm((DEC_BATCH, PAST_LEN, GQA_KV_HEADS, HEAD_DIM), 1.0),
        'c': nrm((DEC_BATCH, D), 1.0),
        'c_ctx': nrm((D,), 1.0),
        'l0_mod_w': nrm((D, N_MOD * D), 0.3 * s),
        'l0_mod_b': nrm((N_MOD * D,), 0.02),
        'l0_norm_g': 1.0 + nrm((4, D), 0.02),
        'l0_w_in': nrm((D, 2 * CONV_CH + 3 * DIFF_WIDTH), s),
        'l0_conv_w': nrm((CONV_WIDTH, 1, CONV_CH), CONV_WIDTH ** -0.5),
        'l0_conv_b': nrm((CONV_CH,), 0.02),
        'l0_conv_ln_g': 1.0 + nrm((CONV_CH,), 0.02),
        'l0_conv_ln_b': nrm((CONV_CH,), 0.02),
        'l0_lambda': nrm((4, DIFF_SUB_DIM), 0.1),
        'l0_subln_g': 1.0 + nrm((DIFF_V_DIM,), 0.02),
        'l0_w_out': nrm((CONV_CH + DIFF_WIDTH, D), (CONV_CH + DIFF_WIDTH) ** -0.5),
        'l0_w_gu': nrm((D, 2 * FFN_HIDDEN), s),
        'l0_w_down': nrm((FFN_HIDDEN, D), FFN_HIDDEN ** -0.5),
        'l1_mod_w': nrm((D, N_MOD * D), 0.3 * s),
        'l1_mod_b': nrm((N_MOD * D,), 0.02),
        'l1_norm_g': 1.0 + nrm((4, D), 0.02),
        'l1_w_qkv': nrm((D, (GQA_HEADS + 2 * GQA_KV_HEADS) * HEAD_DIM), s),
        'l1_sink': nrm((GQA_HEADS,), 0.5),
        'l1_w_out': nrm((GQA_HEADS * HEAD_DIM, D), (GQA_HEADS * HEAD_DIM) ** -0.5),
        'l1_w_gu': nrm((D, 2 * FFN_HIDDEN), s),
        'l1_w_down': nrm((FFN_HIDDEN, D), FFN_HIDDEN ** -0.5),
    }


def reference(x_prompt, x_sample, cache_k0, cache_v0, cache_k1, cache_v1, c, c_ctx,
              l0_mod_w, l0_mod_b, l0_norm_g, l0_w_in, l0_conv_w, l0_conv_b, l0_conv_ln_g,
              l0_conv_ln_b, l0_lambda, l0_subln_g, l0_w_out, l0_w_gu, l0_w_down,
              l1_mod_w, l1_mod_b, l1_norm_g, l1_w_qkv, l1_sink, l1_w_out, l1_w_gu, l1_w_down):
    common = ((l0_mod_w, l0_mod_b, l0_norm_g, l0_w_gu, l0_w_down),
              (l1_mod_w, l1_mod_b, l1_norm_g, l1_w_gu, l1_w_down))
    mix = ((l0_w_in, l0_conv_w, l0_conv_b, l0_conv_ln_g, l0_conv_ln_b, l0_lambda, l0_subln_g, l0_w_out),
           (l1_w_qkv, l1_sink, l1_w_out))
    caches = ((cache_k0, cache_v0), (cache_k1, cache_v1))
    ctx_cond = c_ctx[None, :]
    y_prompt = x_prompt
    new_kv = []
    for layer in range(DEPTH):
        y_prompt, kv = trunk_layer(y_prompt, ctx_cond, layer, common[layer], mix[layer], None)
        new_kv.append(kv)
    y_sample = x_sample
    for layer in range(DEPTH):
        y_sample, _ = trunk_layer(y_sample, c, layer, common[layer], mix[layer], caches[layer])
    (new_k0, new_v0), (new_k1, new_v1) = new_kv
    return (y_prompt, y_sample, new_k0, new_v0, new_k1, new_v1)
```

```python
import functools
import math

import numpy as np
import jax
import jax.numpy as jnp
from jax import lax
from jax.experimental import pallas as pl
from jax.experimental.pallas import tpu as pltpu

D_MODEL = 1024
GRID_W = 64
ROPE_THETA = 10000.0
EPS = 1e-6
NEG_INF = -1e30
CONV_CH = 512
CONV_WIDTH = 31
CONV_PAD = 16
DIFF_HEADS = 4
DIFF_WIDTH = 512
HEAD_DIM = 64
GQA_HEADS = 16
GQA_KV_HEADS = 4
GQA_GROUP = 4
WINDOW = 128
FFN_HIDDEN = 2816
N_MOD = 6
LANES = 128
MOD_ROWS = 8
VMEM_LIMIT = 56 * 1024 * 1024

BF16 = jnp.bfloat16
F32 = jnp.float32


def _sigmoid(x):
    return 1.0 / (1.0 + jnp.exp(-x))


def _rms(x, g):
    return x * lax.rsqrt(jnp.mean(x * x, axis=-1, keepdims=True) + EPS) * g


def _dot(a, b):
    return jnp.dot(a, b, preferred_element_type=F32)


def _dot_nt(a, b):
    return lax.dot_general(a, b, (((1,), (1,)), ((), ())), preferred_element_type=F32)


def _params(*sem):
    return pltpu.CompilerParams(dimension_semantics=sem, vmem_limit_bytes=VMEM_LIMIT)


def _mod_kernel(c_ref, w_ref, b_ref, o_ref):
    c = c_ref[...]
    s = (c * _sigmoid(c)).astype(BF16)
    o_ref[...] = _dot(s, w_ref[...].astype(BF16)) + b_ref[...]


def _modulation(cond, mod_w, mod_b):
    n = mod_w.shape[1]
    tn = 1536
    out = pl.pallas_call(
        _mod_kernel,
        grid=(n // tn,),
        in_specs=[
            pl.BlockSpec((MOD_ROWS, D_MODEL), lambda j: (0, 0)),
            pl.BlockSpec((D_MODEL, tn), lambda j: (0, j)),
            pl.BlockSpec((1, tn), lambda j: (0, j)),
        ],
        out_specs=pl.BlockSpec((MOD_ROWS, tn), lambda j: (0, j)),
        out_shape=jax.ShapeDtypeStruct((MOD_ROWS, n), F32),
        compiler_params=_params("arbitrary"),
        name="modulation",
    )(cond, mod_w, mod_b.reshape(1, n))
    return out.reshape(MOD_ROWS, N_MOD, D_MODEL)


def _rope_tables(seq):
    t = np.arange(seq)
    rows, cols = t // GRID_W, t % GRID_W
    half = HEAD_DIM // 2
    inv = 1.0 / (ROPE_THETA ** (np.arange(0, half, 2, dtype=np.float64) / half))
    ar = rows[:, None] * inv[None, :]
    ac = cols[:, None] * inv[None, :]
    cos = np.concatenate([np.cos(ar), np.cos(ar), np.cos(ac), np.cos(ac)], axis=1)
    sin = np.concatenate([-np.sin(ar), np.sin(ar), -np.sin(ac), np.sin(ac)], axis=1)
    reps = LANES // HEAD_DIM
    return (jnp.asarray(np.tile(cos, (1, reps)), F32), jnp.asarray(np.tile(sin, (1, reps)), F32))


def _rope(x, cos, sin):
    quarter = HEAD_DIM // 4
    lane = lax.broadcasted_iota(jnp.int32, x.shape, 1)
    lo = (lane % (2 * quarter)) < quarter
    partner = jnp.where(lo, pltpu.roll(x, LANES - quarter, 1), pltpu.roll(x, quarter, 1))
    return x * cos + partner * sin


def _pre_kernel(*refs, outs, rope):
    if rope:
        x_ref, m_ref, g_ref, w_ref, cos_ref, sin_ref = refs[:6]
        o_refs = refs[6:]
    else:
        x_ref, m_ref, g_ref, w_ref = refs[:4]
        o_refs = refs[4:]
    x = x_ref[...]
    h = _rms(x, g_ref[0:1, :]) * (1.0 + m_ref[1:2, :]) + m_ref[0:1, :]
    p = _dot(h.astype(BF16), w_ref[...])
    for (start, width, _, roped), o_ref in zip(outs, o_refs):
        if rope and roped:
            cos = cos_ref[...]
            sin = sin_ref[...]
            for c in range(width // LANES):
                xc = p[:, start + c * LANES:start + (c + 1) * LANES]
                o_ref[:, c * LANES:(c + 1) * LANES] = _rope(xc, cos, sin).astype(o_ref.dtype)
        else:
            o_ref[...] = p[:, start:start + width].astype(o_ref.dtype)


def _pre(x, m, norm_g, w, outs, *, tm, cond_row, seq, rope):
    rows = x.shape[0]
    n = w.shape[1]
    in_specs = [
        pl.BlockSpec((tm, D_MODEL), lambda i: (i, 0)),
        pl.BlockSpec((None, N_MOD, D_MODEL), lambda i: (cond_row(i), 0, 0)),
        pl.BlockSpec((4, D_MODEL), lambda i: (0, 0)),
        pl.BlockSpec((D_MODEL, n), lambda i: (0, 0)),
    ]
    args = [x, m, norm_g, w]
    if rope:
        cos, sin = _rope_tables(seq)
        nblk = seq // tm
        in_specs += [pl.BlockSpec((tm, LANES), lambda i: (i % nblk, 0))] * 2
        args += [cos, sin]
    return pl.pallas_call(
        functools.partial(_pre_kernel, outs=outs, rope=rope),
        grid=(rows // tm,),
        in_specs=in_specs,
        out_specs=[pl.BlockSpec((tm, wd), lambda i: (i, 0)) for (_, wd, _, _) in outs],
        out_shape=[jax.ShapeDtypeStruct((rows, wd), dt) for (_, wd, dt, _) in outs],
        compiler_params=_params("arbitrary"),
        name="pre_rope" if rope else "pre",
    )(*args)


def _mix0_kernel(*refs, seq, tq, n_cache, lam_init):
    if n_cache:
        (u_ref, q_ref, k_ref, v_ref, ck_ref, cv_ref, x_ref, m_ref, ng_ref, cw_ref, cb_ref,
         lg_ref, lb_ref, lam_ref, sg_ref, wo_ref, o_ref, k_s, v_s, z_s, zw_s, zsh_s) = refs
    else:
        (u_ref, q_ref, k_ref, v_ref, x_ref, m_ref, ng_ref, cw_ref, cb_ref,
         lg_ref, lb_ref, lam_ref, sg_ref, wo_ref, o_ref, k_s, v_s, z_s, zw_s, zsh_s) = refs
    qi = pl.program_id(1)

    @pl.when(qi == 0)
    def _():
        k_s[0:seq, :] = k_ref[...].astype(BF16)
        v_s[0:seq, :] = v_ref[...].astype(BF16)
        if n_cache:
            k_s[seq:seq + n_cache, :] = ck_ref[...].astype(BF16)
            v_s[seq:seq + n_cache, :] = cv_ref[...].astype(BF16)
        z_s[0:CONV_PAD, :] = jnp.zeros((CONV_PAD, CONV_CH), F32)
        z_s[CONV_PAD + seq:2 * CONV_PAD + seq, :] = jnp.zeros((CONV_PAD, CONV_CH), F32)
        z_s[CONV_PAD:CONV_PAD + seq, :] = u_ref[:, 0:CONV_CH] * _sigmoid(u_ref[:, CONV_CH:2 * CONV_CH])

    r0 = pl.multiple_of(qi * tq, tq)
    win = tq + 2 * CONV_PAD
    zw_s[...] = z_s[pl.ds(r0, win), :]
    span = win - 8
    for b in range(1, 8):
        zsh_s[b - 1, 0:span, :] = zw_s[b:b + span, :]
    chunk = 64
    off0 = CONV_PAD - CONV_WIDTH // 2
    conv_rows = []
    for c0 in range(0, tq, chunk):
        acc = jnp.zeros((chunk, CONV_CH), F32)
        for j in range(CONV_WIDTH):
            off = j + off0
            a, b = off // 8, off % 8
            if b == 0:
                zz = zw_s[c0 + 8 * a:c0 + 8 * a + chunk, :]
            else:
                zz = zsh_s[b - 1, c0 + 8 * a:c0 + 8 * a + chunk, :]
            acc = acc + zz * cw_ref[j:j + 1, :]
        acc = acc + cb_ref[...]
        mu = jnp.mean(acc, axis=-1, keepdims=True)
        xc = acc - mu
        y = xc * lax.rsqrt(jnp.mean(xc * xc, axis=-1, keepdims=True) + EPS) * lg_ref[...] + lb_ref[...]
        conv_rows.append((y * _sigmoid(y)).astype(BF16))
    conv_out = jnp.concatenate(conv_rows, axis=0)

    la = jnp.sum(lam_ref[0:1, :] * lam_ref[1:2, :], axis=-1, keepdims=True)
    lb = jnp.sum(lam_ref[2:3, :] * lam_ref[3:4, :], axis=-1, keepdims=True)
    lam = jnp.exp(la) - jnp.exp(lb) + lam_init
    sub = LANES // 2
    scale = sub ** -0.5
    lane = lax.broadcasted_iota(jnp.int32, (tq, LANES), 1)
    heads = []
    for h in range(DIFF_HEADS):
        cols = slice(h * LANES, (h + 1) * LANES)
        qc = q_ref[:, cols] * jnp.asarray(scale, BF16)
        zero = jnp.zeros_like(qc)
        qq = jnp.concatenate([jnp.where(lane < sub, qc, zero), jnp.where(lane >= sub, qc, zero)], axis=0)
        s = _dot_nt(qq, k_s[:, cols])
        e = jnp.exp(s - jnp.max(s, axis=-1, keepdims=True))
        p = e * (1.0 / jnp.sum(e, axis=-1, keepdims=True))
        w = p[0:tq] - lam * p[tq:2 * tq]
        o = _dot(w.astype(BF16), v_s[:, cols])
        heads.append((_rms(o, sg_ref[...]) * (1.0 - lam_init)).astype(BF16))
    cat = jnp.concatenate([conv_out] + heads, axis=1)
    mixed = _dot(cat, wo_ref[...])
    o_ref[...] = x_ref[...] + m_ref[2:3, :] * _rms(mixed, ng_ref[1:2, :])


def _mix0(u, q, k, v, cache, x, m, norm_g, conv_w, conv_b, ln_g, ln_b, lam, subln_g, w_out,
          *, batch, seq, tq, cond_row, lam_init):
    nq = seq // tq
    n_cache = 0 if cache is None else cache[0].shape[1]
    ktot = seq + n_cache
    win = tq + 2 * CONV_PAD
    seq_spec = lambda width: pl.BlockSpec((seq, width), lambda b, i: (b, 0))
    const = lambda shape: pl.BlockSpec(shape, lambda b, i: (0,) * len(shape))
    in_specs = [seq_spec(2 * CONV_CH), pl.BlockSpec((tq, DIFF_WIDTH), lambda b, i: (b * nq + i, 0)),
                seq_spec(DIFF_WIDTH), seq_spec(DIFF_WIDTH)]
    args = [u, q, k, v]
    if n_cache:
        in_specs += [pl.BlockSpec((None, n_cache, DIFF_WIDTH), lambda b, i: (b, 0, 0))] * 2
        args += [cache[0], cache[1]]
    in_specs += [
        pl.BlockSpec((tq, D_MODEL), lambda b, i: (b * nq + i, 0)),
        pl.BlockSpec((None, N_MOD, D_MODEL), lambda b, i: (cond_row(b), 0, 0)),
        const((4, D_MODEL)), const((CONV_WIDTH, CONV_CH)), const((1, CONV_CH)), const((1, CONV_CH)),
        const((1, CONV_CH)), const((4, LANES // 2)), const((1, LANES)), const((D_MODEL, D_MODEL)),
    ]
    args += [x, m, norm_g, conv_w, conv_b, ln_g, ln_b, lam, subln_g, w_out]
    return pl.pallas_call(
        functools.partial(_mix0_kernel, seq=seq, tq=tq, n_cache=n_cache, lam_init=lam_init),
        grid=(batch, nq),
        in_specs=in_specs,
        out_specs=pl.BlockSpec((tq, D_MODEL), lambda b, i: (b * nq + i, 0)),
        out_shape=jax.ShapeDtypeStruct((batch * seq, D_MODEL), F32),
        scratch_shapes=[
            pltpu.VMEM((ktot, DIFF_WIDTH), BF16), pltpu.VMEM((ktot, DIFF_WIDTH), BF16),
            pltpu.VMEM((seq + 2 * CONV_PAD, CONV_CH), F32), pltpu.VMEM((win, CONV_CH), F32),
            pltpu.VMEM((7, win - 8, CONV_CH), F32),
        ],
        compiler_params=_params("arbitrary", "arbitrary"),
        name="mix_conv_diff",
    )(*args)


def _mix1_kernel(*refs, seq, tq, n_cache):
    if n_cache:
        (q_ref, k_ref, v_ref, ck_ref, cv_ref, x_ref, m_ref, ng_ref, sink_ref, wo_ref, o_ref, k_s, v_s) = refs
    else:
        (q_ref, k_ref, v_ref, x_ref, m_ref, ng_ref, sink_ref, wo_ref, o_ref, k_s, v_s) = refs
    qi = pl.program_id(1)
    kvw = GQA_KV_HEADS * HEAD_DIM
    windowed = n_cache > 0

    @pl.when(qi == 0)
    def _():
        if windowed:
            zeros = jnp.zeros((WINDOW, kvw), BF16)
            for ref, src, cache in ((k_s, k_ref, ck_ref), (v_s, v_ref, cv_ref)):
                ref[0:WINDOW, :] = zeros
                ref[WINDOW:WINDOW + seq, :] = src[...].astype(BF16)
                ref[WINDOW + seq:2 * WINDOW + seq, :] = zeros
                ref[2 * WINDOW + seq:2 * WINDOW + seq + n_cache, :] = cache[...].astype(BF16)
        else:
            k_s[...] = k_ref[...].astype(BF16)
            v_s[...] = v_ref[...].astype(BF16)

    scale = HEAD_DIM ** -0.5
    lane = lax.broadcasted_iota(jnp.int32, (tq, LANES), 1)
    if windowed:
        r0 = pl.multiple_of(qi * tq, tq)
        wlen = tq + 2 * WINDOW
        qpos = r0 + lax.broadcasted_iota(jnp.int32, (tq, wlen), 0)
        kpos = r0 - WINDOW + lax.broadcasted_iota(jnp.int32, (tq, wlen), 1)
        valid = (jnp.abs(qpos - kpos) <= WINDOW) & (kpos >= 0) & (kpos < seq)
        c_lo = 2 * WINDOW + seq

    chunks = []
    for c in range(GQA_HEADS // 2):
        qc = q_ref[:, c * LANES:(c + 1) * LANES] * jnp.asarray(scale, BF16)
        halves = []
        for a in range(2):
            h = 2 * c + a
            g = h // GQA_GROUP
            e, b = g // 2, g % 2
            kcols = slice(e * LANES, (e + 1) * LANES)
            qa = qc if a == b else pltpu.roll(qc, HEAD_DIM, 1)
            keep = (lane < HEAD_DIM) if b == 0 else (lane >= HEAD_DIM)
            qm = jnp.where(keep, qa, jnp.zeros_like(qa))
            sink = sink_ref[h]
            if windowed:
                s_w = jnp.where(valid, _dot_nt(qm, k_s[pl.ds(r0, wlen), kcols]), NEG_INF)
                s_c = _dot_nt(qm, k_s[c_lo:c_lo + n_cache, kcols])
                mx = jnp.maximum(jnp.maximum(jnp.max(s_w, axis=-1, keepdims=True),
                                             jnp.max(s_c, axis=-1, keepdims=True)), sink)
                e_w = jnp.exp(s_w - mx)
                e_c = jnp.exp(s_c - mx)
                den = (jnp.sum(e_w, axis=-1, keepdims=True) + jnp.sum(e_c, axis=-1, keepdims=True)
                       + jnp.exp(sink - mx))
                r = 1.0 / den
                o = (_dot((e_w * r).astype(BF16), v_s[pl.ds(r0, wlen), kcols])
                     + _dot((e_c * r).astype(BF16), v_s[c_lo:c_lo + n_cache, kcols]))
            else:
                s = _dot_nt(qm, k_s[:, kcols])
                mx = jnp.maximum(jnp.max(s, axis=-1, keepdims=True), sink)
                ex = jnp.exp(s - mx)
                den = jnp.sum(ex, axis=-1, keepdims=True) + jnp.exp(sink - mx)
                o = _dot((ex * (1.0 / den)).astype(BF16), v_s[:, kcols])
            halves.append(o if a == b else pltpu.roll(o, HEAD_DIM, 1))
        chunks.append(jnp.where(lane < HEAD_DIM, halves[0], halves[1]).astype(BF16))
    cat = jnp.concatenate(chunks, axis=1)
    mixed = _dot(cat, wo_ref[...])
    o_ref[...] = x_ref[...] + m_ref[2:3, :] * _rms(mixed, ng_ref[1:2, :])


def _mix1(q, k, v, cache, x, m, norm_g, sink, w_out, *, batch, seq, tq, cond_row):
    nq = seq // tq
    n_cache = 0 if cache is None else cache[0].shape[1]
    kvw = GQA_KV_HEADS * HEAD_DIM
    krows = seq + (2 * WINDOW + n_cache if n_cache else 0)
    in_specs = [pl.BlockSpec((tq, D_MODEL), lambda b, i: (b * nq + i, 0)),
                pl.BlockSpec((seq, kvw), lambda b, i: (b, 0)), pl.BlockSpec((seq, kvw), lambda b, i: (b, 0))]
    args = [q, k, v]
    if n_cache:
        in_specs += [pl.BlockSpec((None, n_cache, kvw), lambda b, i: (b, 0, 0))] * 2
        args += [cache[0], cache[1]]
    in_specs += [
        pl.BlockSpec((tq, D_MODEL), lambda b, i: (b * nq + i, 0)),
        pl.BlockSpec((None, N_MOD, D_MODEL), lambda b, i: (cond_row(b), 0, 0)),
        pl.BlockSpec((4, D_MODEL), lambda b, i: (0, 0)),
        pl.BlockSpec(memory_space=pltpu.SMEM),
        pl.BlockSpec((D_MODEL, D_MODEL), lambda b, i: (0, 0)),
    ]
    args += [x, m, norm_g, sink, w_out]
    return pl.pallas_call(
        functools.partial(_mix1_kernel, seq=seq, tq=tq, n_cache=n_cache),
        grid=(batch, nq),
        in_specs=in_specs,
        out_specs=pl.BlockSpec((tq, D_MODEL), lambda b, i: (b * nq + i, 0)),
        out_shape=jax.ShapeDtypeStruct((batch * seq, D_MODEL), F32),
        scratch_shapes=[pltpu.VMEM((krows, kvw), BF16), pltpu.VMEM((krows, kvw), BF16)],
        compiler_params=_params("arbitrary", "arbitrary"),
        name="mix_gqa",
    )(*args)


FFN_CHUNKS = (768, 768, 768, 512)


def _ffn_kernel(x_ref, m_ref, ng_ref, wgu_ref, wd_ref, o_ref):
    x = x_ref[...]
    h = (_rms(x, ng_ref[2:3, :]) * (1.0 + m_ref[4:5, :]) + m_ref[3:4, :]).astype(BF16)
    acc = jnp.zeros(x.shape, F32)
    c0 = 0
    for width in FFN_CHUNKS:
        gate = _dot(h, wgu_ref[:, c0:c0 + width])
        up = _dot(h, wgu_ref[:, FFN_HIDDEN + c0:FFN_HIDDEN + c0 + width])
        act = (gate * _sigmoid(gate) * up).astype(BF16)
        acc = acc + _dot(act, wd_ref[c0:c0 + width, :])
        c0 += width
    o_ref[...] = x + m_ref[5:6, :] * _rms(acc, ng_ref[3:4, :])


def _ffn(x, m, norm_g, w_gu, w_down, *, tm, cond_row):
    rows = x.shape[0]
    return pl.pallas_call(
        _ffn_kernel,
        grid=(rows // tm,),
        in_specs=[
            pl.BlockSpec((tm, D_MODEL), lambda i: (i, 0)),
            pl.BlockSpec((None, N_MOD, D_MODEL), lambda i: (cond_row(i), 0, 0)),
            pl.BlockSpec((4, D_MODEL), lambda i: (0, 0)),
            pl.BlockSpec((D_MODEL, 2 * FFN_HIDDEN), lambda i: (0, 0), pipeline_mode=pl.Buffered(1)),
            pl.BlockSpec((FFN_HIDDEN, D_MODEL), lambda i: (0, 0), pipeline_mode=pl.Buffered(1)),
        ],
        out_specs=pl.BlockSpec((tm, D_MODEL), lambda i: (i, 0)),
        out_shape=jax.ShapeDtypeStruct((rows, D_MODEL), F32),
        compiler_params=_params("arbitrary"),
        name="ffn",
    )(x, m, norm_g, w_gu, w_down)


def _trunk(x, m, caches, weights, *, batch, seq, cond_of_batch):
    (norm_g0, w_in, conv_w, conv_b, ln_g, ln_b, lam, subln_g, w_out0, w_gu0, w_down0,
     norm_g1, w_qkv, sink, w_out1, w_gu1, w_down1) = weights
    m0, m1 = m
    rope = caches is not None
    tm = 512
    tq = 256
    row_of_block = lambda i: cond_of_batch((i * tm) // seq)

    o0 = 2 * CONV_CH
    outs0 = ((0, o0, F32, False), (o0, DIFF_WIDTH, BF16, True),
             (o0 + DIFF_WIDTH, DIFF_WIDTH, F32, True), (o0 + 2 * DIFF_WIDTH, DIFF_WIDTH, F32, False))
    u, q, k0, v0 = _pre(x, m0, norm_g0, w_in, outs0, tm=tm, cond_row=row_of_block, seq=seq, rope=rope)
    lam_init = 0.8 - 0.6 * math.exp(-0.3 * 0)
    x = _mix0(u, q, k0, v0, None if caches is None else caches[0], x, m0, norm_g0, conv_w, conv_b,
              ln_g, ln_b, lam, subln_g, w_out0, batch=batch, seq=seq, tq=tq, cond_row=cond_of_batch,
              lam_init=lam_init)
    x = _ffn(x, m0, norm_g0, w_gu0, w_down0, tm=tm, cond_row=row_of_block)

    nq, nkv = GQA_HEADS * HEAD_DIM, GQA_KV_HEADS * HEAD_DIM
    outs1 = ((0, nq, BF16, True), (nq, nkv, F32, True), (nq + nkv, nkv, F32, False))
    q, k1, v1 = _pre(x, m1, norm_g1, w_qkv, outs1, tm=tm, cond_row=row_of_block, seq=seq, rope=rope)
    x = _mix1(q, k1, v1, None if caches is None else caches[1], x, m1, norm_g1, sink, w_out1,
              batch=batch, seq=seq, tq=tq, cond_row=cond_of_batch)
    x = _ffn(x, m1, norm_g1, w_gu1, w_down1, tm=tm, cond_row=row_of_block)
    return x, (k0, v0, k1, v1)


def kernel(x_prompt, x_sample, cache_k0, cache_v0, cache_k1, cache_v1, c, c_ctx, l0_mod_w, l0_mod_b, l0_norm_g, l0_w_in, l0_conv_w, l0_conv_b, l0_conv_ln_g, l0_conv_ln_b, l0_lambda, l0_subln_g, l0_w_out, l0_w_gu, l0_w_down, l1_mod_w, l1_mod_b, l1_norm_g, l1_w_qkv, l1_sink, l1_w_out, l1_w_gu, l1_w_down):
    batch, seq, d = x_prompt.shape
    dec_batch, dec_seq, _ = x_sample.shape
    n_past = cache_k0.shape[1]

    cond = jnp.concatenate([c_ctx[None, :], c, jnp.zeros((MOD_ROWS - 1 - dec_batch, d), F32)], axis=0)
    m = (_modulation(cond, l0_mod_w, l0_mod_b), _modulation(cond, l1_mod_w, l1_mod_b))

    bf = lambda w: w.astype(BF16)
    row = lambda v: v.reshape(1, -1)
    weights = (l0_norm_g, bf(l0_w_in), l0_conv_w.reshape(CONV_WIDTH, CONV_CH), row(l0_conv_b),
               row(l0_conv_ln_g), row(l0_conv_ln_b), l0_lambda, row(l0_subln_g), bf(l0_w_out),
               bf(l0_w_gu), bf(l0_w_down),
               l1_norm_g, bf(l1_w_qkv), l1_sink, bf(l1_w_out), bf(l1_w_gu), bf(l1_w_down))

    y_prompt, (k0, v0, k1, v1) = _trunk(x_prompt.reshape(batch * seq, d), m, None, weights,
                                        batch=batch, seq=seq, cond_of_batch=lambda b: 0)
    caches = ((cache_k0.reshape(dec_batch, n_past, -1), cache_v0.reshape(dec_batch, n_past, -1)),
              (cache_k1.reshape(dec_batch, n_past, -1), cache_v1.reshape(dec_batch, n_past, -1)))
    y_sample, _ = _trunk(x_sample.reshape(dec_batch * dec_seq, d), m, caches, weights,
                         batch=dec_batch, seq=dec_seq, cond_of_batch=lambda b: 1 + b)
    return (y_prompt.reshape(batch, seq, d), y_sample.reshape(dec_batch, dec_seq, d),
            k0.reshape(batch, seq, DIFF_HEADS, 2, HEAD_DIM), v0.reshape(batch, seq, DIFF_HEADS, 2 * HEAD_DIM),
            k1.reshape(batch, seq, GQA_KV_HEADS, HEAD_DIM), v1.reshape(batch, seq, GQA_KV_HEADS, HEAD_DIM))
```

```python
import functools
import math

import numpy as np
import jax
import jax.numpy as jnp
from jax import lax
from jax.experimental import pallas as pl
from jax.experimental.pallas import tpu as pltpu

D_MODEL = 1024
GRID_W = 64
ROPE_THETA = 10000.0
EPS = 1e-6
NEG_INF = -1e30
CONV_CH = 512
CONV_WIDTH = 31
CONV_PAD = 16
DIFF_HEADS = 4
DIFF_WIDTH = 512
HEAD_DIM = 64
GQA_HEADS = 16
GQA_KV_HEADS = 4
GQA_GROUP = 4
KV_WIDTH = GQA_KV_HEADS * HEAD_DIM
WINDOW = 128
FFN_HIDDEN = 2816
N_MOD = 6
LANES = 128
MOD_ROWS = 8
VMEM_LIMIT = 56 * 1024 * 1024

BF16 = jnp.bfloat16
F32 = jnp.float32


def _sigmoid(x):
    return 1.0 / (1.0 + jnp.exp(-x))


def _rms(x, g):
    return x * lax.rsqrt(jnp.mean(x * x, axis=-1, keepdims=True) + EPS) * g


def _dot(a, b):
    return jnp.dot(a, b, preferred_element_type=F32)


def _dot_nt(a, b):
    return lax.dot_general(a, b, (((1,), (1,)), ((), ())), preferred_element_type=F32)


def _params(*sem):
    return pltpu.CompilerParams(dimension_semantics=sem, vmem_limit_bytes=VMEM_LIMIT)


def _mod_kernel(c_ref, w_ref, b_ref, o_ref):
    c = c_ref[...]
    s = (c * _sigmoid(c)).astype(BF16)
    o_ref[...] = _dot(s, w_ref[...].astype(BF16)) + b_ref[...]


def _modulation(cond, mod_w, mod_b):
    n = mod_w.shape[1]
    tn = 1536
    out = pl.pallas_call(
        _mod_kernel,
        grid=(n // tn,),
        in_specs=[
            pl.BlockSpec((MOD_ROWS, D_MODEL), lambda j: (0, 0)),
            pl.BlockSpec((D_MODEL, tn), lambda j: (0, j)),
            pl.BlockSpec((1, tn), lambda j: (0, j)),
        ],
        out_specs=pl.BlockSpec((MOD_ROWS, tn), lambda j: (0, j)),
        out_shape=jax.ShapeDtypeStruct((MOD_ROWS, n), F32),
        compiler_params=_params("arbitrary"),
        name="modulation",
    )(cond, mod_w, mod_b.reshape(1, n))
    return out.reshape(MOD_ROWS, N_MOD, D_MODEL)


def _rope_tables(seq):
    t = np.arange(seq)
    rows, cols = t // GRID_W, t % GRID_W
    half = HEAD_DIM // 2
    inv = 1.0 / (ROPE_THETA ** (np.arange(0, half, 2, dtype=np.float64) / half))
    ar = rows[:, None] * inv[None, :]
    ac = cols[:, None] * inv[None, :]
    cos = np.concatenate([np.cos(ar), np.cos(ar), np.cos(ac), np.cos(ac)], axis=1)
    sin = np.concatenate([-np.sin(ar), np.sin(ar), -np.sin(ac), np.sin(ac)], axis=1)
    reps = LANES // HEAD_DIM
    return (jnp.asarray(np.tile(cos, (1, reps)), F32), jnp.asarray(np.tile(sin, (1, reps)), F32))


def _rope(x, cos, sin):
    quarter = HEAD_DIM // 4
    lane = lax.broadcasted_iota(jnp.int32, x.shape, 1)
    lo = (lane % (2 * quarter)) < quarter
    partner = jnp.where(lo, pltpu.roll(x, LANES - quarter, 1), pltpu.roll(x, quarter, 1))
    return x * cos + partner * sin


def _pre_kernel(*refs, outs, rope, sblk):
    if rope:
        x_ref, m_ref, g_ref, w_ref, cos_ref, sin_ref = refs[:6]
        o_refs = refs[6:]
    else:
        x_ref, m_ref, g_ref, w_ref = refs[:4]
        o_refs = refs[4:]
    x = x_ref[...]
    tm = x.shape[0]
    h = _rms(x, g_ref[0:1, :]) * (1.0 + m_ref[1:2, :]) + m_ref[0:1, :]
    p = _dot(h.astype(BF16), w_ref[...])
    for (kind, start, width, _, roped), o_ref in zip(outs, o_refs):
        if kind == "nat":
            if rope and roped:
                cos = cos_ref[...]
                sin = sin_ref[...]
                for c in range(width // LANES):
                    xc = p[:, start + c * LANES:start + (c + 1) * LANES]
                    o_ref[:, c * LANES:(c + 1) * LANES] = _rope(xc, cos, sin).astype(o_ref.dtype)
            else:
                o_ref[...] = p[:, start:start + width].astype(o_ref.dtype)
        elif kind == "T":
            for s in range(tm // sblk):
                o_ref[s] = p[s * sblk:(s + 1) * sblk, start:start + width].T.astype(o_ref.dtype)
        else:
            for hd in range(width // LANES):
                o_ref[:, hd, :] = p[:, start + hd * LANES:start + (hd + 1) * LANES]


def _pre(x, m, norm_g, w, outs, *, tm, cond_row, seq, rope):
    rows = x.shape[0]
    n = w.shape[1]
    sblk = min(seq, tm)
    per_seq = seq // sblk
    in_specs = [
        pl.BlockSpec((tm, D_MODEL), lambda i: (i, 0)),
        pl.BlockSpec((None, N_MOD, D_MODEL), lambda i: (cond_row(i), 0, 0)),
        pl.BlockSpec((4, D_MODEL), lambda i: (0, 0)),
        pl.BlockSpec((D_MODEL, n), lambda i: (0, 0)),
    ]
    args = [x, m, norm_g, w]
    if rope:
        cos, sin = _rope_tables(seq)
        nblk = seq // tm
        in_specs += [pl.BlockSpec((tm, LANES), lambda i: (i % nblk, 0))] * 2
        args += [cos, sin]
    out_specs, out_shape = [], []
    for (kind, _, wd, dt, _) in outs:
        if kind == "nat":
            out_specs.append(pl.BlockSpec((tm, wd), lambda i: (i, 0)))
            out_shape.append(jax.ShapeDtypeStruct((rows, wd), dt))
        elif kind == "T":
            out_specs.append(pl.BlockSpec((tm // sblk, wd, sblk), lambda i: (i // per_seq, 0, i % per_seq)))
            out_shape.append(jax.ShapeDtypeStruct((rows // seq, wd, seq), dt))
        else:
            out_specs.append(pl.BlockSpec((tm, wd // LANES, LANES), lambda i: (i, 0, 0)))
            out_shape.append(jax.ShapeDtypeStruct((rows, wd // LANES, LANES), dt))
    return pl.pallas_call(
        functools.partial(_pre_kernel, outs=outs, rope=rope, sblk=sblk),
        grid=(rows // tm,),
        in_specs=in_specs,
        out_specs=out_specs,
        out_shape=out_shape,
        compiler_params=_params("arbitrary"),
        name="pre_rope" if rope else "pre",
    )(*args)


def _softmax_stats(s_list, floor=None):
    mx = None
    for s in s_list:
        smax = jnp.max(s, axis=0, keepdims=True)
        mx = smax if mx is None else jnp.maximum(mx, smax)
    if floor is not None:
        mx = jnp.maximum(mx, floor)
    e_list = [jnp.exp(s - mx) for s in s_list]
    den = None
    for e in e_list:
        esum = jnp.sum(e, axis=0, keepdims=True)
        den = esum if den is None else den + esum
    return mx, e_list, den


def _mix0_kernel(*refs, seq, tq, n_cache, lam_init):
    if n_cache:
        (u_ref, q_ref, k_ref, vt_ref, ck_ref, cvt_ref, x_ref, m_ref, ng_ref, cw_ref, cb_ref,
         lg_ref, lb_ref, lam_ref, sg_ref, wo_ref, o_ref, z_s, zw_s, zsh_s) = refs
    else:
        (u_ref, q_ref, k_ref, vt_ref, x_ref, m_ref, ng_ref, cw_ref, cb_ref,
         lg_ref, lb_ref, lam_ref, sg_ref, wo_ref, o_ref, z_s, zw_s, zsh_s) = refs
    qi = pl.program_id(1)

    @pl.when(qi == 0)
    def _():
        z_s[0:CONV_PAD, :] = jnp.zeros((CONV_PAD, CONV_CH), F32)
        z_s[CONV_PAD + seq:2 * CONV_PAD + seq, :] = jnp.zeros((CONV_PAD, CONV_CH), F32)
        z_s[CONV_PAD:CONV_PAD + seq, :] = u_ref[:, 0:CONV_CH] * _sigmoid(u_ref[:, CONV_CH:2 * CONV_CH])

    r0 = pl.multiple_of(qi * tq, tq)
    win = tq + 2 * CONV_PAD
    zw_s[...] = z_s[pl.ds(r0, win), :]
    span = win - 8
    for b in range(1, 8):
        zsh_s[b - 1, 0:span, :] = zw_s[b:b + span, :]
    chunk = 64
    off0 = CONV_PAD - CONV_WIDTH // 2
    conv_rows = []
    for c0 in range(0, tq, chunk):
        acc = jnp.zeros((chunk, CONV_CH), F32)
        for j in range(CONV_WIDTH):
            off = j + off0
            a, b = off // 8, off % 8
            if b == 0:
                zz = zw_s[c0 + 8 * a:c0 + 8 * a + chunk, :]
            else:
                zz = zsh_s[b - 1, c0 + 8 * a:c0 + 8 * a + chunk, :]
            acc = acc + zz * cw_ref[j:j + 1, :]
        acc = acc + cb_ref[...]
        mu = jnp.mean(acc, axis=-1, keepdims=True)
        xc = acc - mu
        y = xc * lax.rsqrt(jnp.mean(xc * xc, axis=-1, keepdims=True) + EPS) * lg_ref[...] + lb_ref[...]
        conv_rows.append((y * _sigmoid(y)).astype(BF16))
    conv_out = jnp.concatenate(conv_rows, axis=0)

    la = jnp.sum(lam_ref[0:1, :] * lam_ref[1:2, :], axis=-1, keepdims=True)
    lb = jnp.sum(lam_ref[2:3, :] * lam_ref[3:4, :], axis=-1, keepdims=True)
    lam = jnp.exp(la) - jnp.exp(lb) + lam_init
    sub = LANES // 2
    scale = sub ** -0.5
    lane = lax.broadcasted_iota(jnp.int32, (tq, LANES), 1)
    heads = []
    for h in range(DIFF_HEADS):
        cols = slice(h * LANES, (h + 1) * LANES)
        qc = q_ref[:, cols] * jnp.asarray(scale, BF16)
        zero = jnp.zeros_like(qc)
        qq = jnp.concatenate([jnp.where(lane < sub, qc, zero), jnp.where(lane >= sub, qc, zero)], axis=0)
        segs = [(k_ref[:, cols], vt_ref[cols, :])]
        if n_cache:
            segs.append((ck_ref[:, cols].astype(BF16), cvt_ref[cols, :].astype(BF16)))
        _, e_list, den = _softmax_stats([_dot_nt(kk, qq) for kk, _ in segs])
        ov = None
        for e, (_, vt) in zip(e_list, segs):
            pv = _dot(vt, e.astype(BF16))
            ov = pv if ov is None else ov + pv
        r = 1.0 / den
        o = ov[:, 0:tq] * r[:, 0:tq] - ov[:, tq:2 * tq] * (lam * r[:, tq:2 * tq])
        o = o * lax.rsqrt(jnp.mean(o * o, axis=0, keepdims=True) + EPS) * sg_ref[...] * (1.0 - lam_init)
        heads.append(o)
    attn = jnp.concatenate(heads, axis=0).T.astype(BF16)
    cat = jnp.concatenate([conv_out, attn], axis=1)
    mixed = _dot(cat, wo_ref[...])
    o_ref[...] = x_ref[...] + m_ref[2:3, :] * _rms(mixed, ng_ref[1:2, :])


def _mix0(u, q, k, vt, cache, x, m, norm_g, conv_w, conv_b, ln_g, ln_b, lam, subln_g, w_out,
          *, batch, seq, tq, cond_row, lam_init):
    nq = seq // tq
    n_cache = 0 if cache is None else cache[0].shape[1]
    win = tq + 2 * CONV_PAD
    const = lambda shape: pl.BlockSpec(shape, lambda b, i: (0,) * len(shape))
    in_specs = [pl.BlockSpec((seq, 2 * CONV_CH), lambda b, i: (b, 0)),
                pl.BlockSpec((tq, DIFF_WIDTH), lambda b, i: (b * nq + i, 0)),
                pl.BlockSpec((seq, DIFF_WIDTH), lambda b, i: (b, 0)),
                pl.BlockSpec((None, DIFF_WIDTH, seq), lambda b, i: (b, 0, 0))]
    args = [u, q, k, vt]
    if n_cache:
        in_specs += [pl.BlockSpec((None, n_cache, DIFF_WIDTH), lambda b, i: (b, 0, 0)),
                     pl.BlockSpec((None, DIFF_WIDTH, n_cache), lambda b, i: (b, 0, 0))]
        args += [cache[0], cache[1]]
    in_specs += [
        pl.BlockSpec((tq, D_MODEL), lambda b, i: (b * nq + i, 0)),
        pl.BlockSpec((None, N_MOD, D_MODEL), lambda b, i: (cond_row(b), 0, 0)),
        const((4, D_MODEL)), const((CONV_WIDTH, CONV_CH)), const((1, CONV_CH)), const((1, CONV_CH)),
        const((1, CONV_CH)), const((4, LANES // 2)), const((LANES, 1)), const((D_MODEL, D_MODEL)),
    ]
    args += [x, m, norm_g, conv_w, conv_b, ln_g, ln_b, lam, subln_g.reshape(LANES, 1), w_out]
    return pl.pallas_call(
        functools.partial(_mix0_kernel, seq=seq, tq=tq, n_cache=n_cache, lam_init=lam_init),
        grid=(batch, nq),
        in_specs=in_specs,
        out_specs=pl.BlockSpec((tq, D_MODEL), lambda b, i: (b * nq + i, 0)),
        out_shape=jax.ShapeDtypeStruct((batch * seq, D_MODEL), F32),
        scratch_shapes=[
            pltpu.VMEM((seq + 2 * CONV_PAD, CONV_CH), F32), pltpu.VMEM((win, CONV_CH), F32),
            pltpu.VMEM((7, win - 8, CONV_CH), F32),
        ],
        compiler_params=_params("arbitrary", "arbitrary"),
        name="mix_conv_diff",
    )(*args)


def _mix1_kernel(*refs, seq, tq, n_cache):
    if n_cache:
        (q_ref, kp_ref, kc_ref, kn_ref, vtp_ref, vtc_ref, vtn_ref, ck_ref, cvt_ref,
         x_ref, m_ref, ng_ref, sink_ref, wo_ref, o_ref) = refs
    else:
        (q_ref, kc_ref, vtc_ref, x_ref, m_ref, ng_ref, sink_ref, wo_ref, o_ref) = refs
    qi = pl.program_id(1)
    scale = HEAD_DIM ** -0.5
    lane = lax.broadcasted_iota(jnp.int32, (tq, LANES), 1)

    def band(base, rows):
        kpos = base + lax.broadcasted_iota(jnp.int32, (rows, tq), 0)
        qpos = qi * tq + lax.broadcasted_iota(jnp.int32, (rows, tq), 1)
        ok = (jnp.abs(qpos - kpos) <= WINDOW) & (kpos >= 0) & (kpos < seq)
        return jnp.concatenate([ok] * GQA_GROUP, axis=1)

    if n_cache:
        segs = [(kp_ref, vtp_ref, band(qi * tq - WINDOW, WINDOW)),
                (kc_ref, vtc_ref, band(qi * tq, tq)),
                (kn_ref, vtn_ref, band(qi * tq + tq, WINDOW)),
                (ck_ref, cvt_ref, None)]
    else:
        segs = [(kc_ref, vtc_ref, None)]

    out_rows = []
    for g in range(GQA_KV_HEADS):
        e_chunk, b = g // 2, g % 2
        kcols = slice(e_chunk * LANES, (e_chunk + 1) * LANES)
        vrows = slice(g * HEAD_DIM, (g + 1) * HEAD_DIM)
        keep = (lane < HEAD_DIM) if b == 0 else (lane >= HEAD_DIM)
        qs, sinks = [], []
        for hh in range(GQA_GROUP):
            h = g * GQA_GROUP + hh
            c, a = h // 2, h % 2
            qc = q_ref[:, c * LANES:(c + 1) * LANES] * jnp.asarray(scale, BF16)
            qa = qc if a == b else pltpu.roll(qc, HEAD_DIM, 1)
            qs.append(jnp.where(keep, qa, jnp.zeros_like(qa)))
            sinks.append(jnp.full((1, tq), sink_ref[h], F32))
        qg = jnp.concatenate(qs, axis=0)
        sink = jnp.concatenate(sinks, axis=1)
        s_list = []
        for k_ref, _, ok in segs:
            s = _dot_nt(k_ref[:, kcols].astype(BF16), qg)
            s_list.append(s if ok is None else jnp.where(ok, s, NEG_INF))
        mx, e_list, den = _softmax_stats(s_list, floor=sink)
        den = den + jnp.exp(sink - mx)
        ov = None
        for e, (_, vt_ref, _) in zip(e_list, segs):
            pv = _dot(vt_ref[vrows, :].astype(BF16), e.astype(BF16))
            ov = pv if ov is None else ov + pv
        ov = ov * (1.0 / den)
        out_rows += [ov[:, hh * tq:(hh + 1) * tq] for hh in range(GQA_GROUP)]
    attn = jnp.concatenate(out_rows, axis=0).T.astype(BF16)
    mixed = _dot(attn, wo_ref[...])
    o_ref[...] = x_ref[...] + m_ref[2:3, :] * _rms(mixed, ng_ref[1:2, :])


def _mix1(q, k, vt, cache, x, m, norm_g, sink, w_out, *, batch, seq, tq, cond_row):
    nq = seq // tq
    n_cache = 0 if cache is None else cache[0].shape[1]
    q_spec = pl.BlockSpec((tq, D_MODEL), lambda b, i: (b * nq + i, 0))
    kc_spec = pl.BlockSpec((tq, KV_WIDTH), lambda b, i: (b * nq + i, 0))
    vtc_spec = pl.BlockSpec((None, KV_WIDTH, tq), lambda b, i: (b, 0, i))
    if n_cache:
        nw = seq // WINDOW
        per = tq // WINDOW
        prev = lambda i: jnp.maximum(i * per - 1, 0)
        nxt = lambda i: jnp.minimum(i * per + per, nw - 1)
        in_specs = [q_spec,
                    pl.BlockSpec((WINDOW, KV_WIDTH), lambda b, i: (b * nw + prev(i), 0)), kc_spec,
                    pl.BlockSpec((WINDOW, KV_WIDTH), lambda b, i: (b * nw + nxt(i), 0)),
                    pl.BlockSpec((None, KV_WIDTH, WINDOW), lambda b, i: (b, 0, prev(i))), vtc_spec,
                    pl.BlockSpec((None, KV_WIDTH, WINDOW), lambda b, i: (b, 0, nxt(i))),
                    pl.BlockSpec((None, n_cache, KV_WIDTH), lambda b, i: (b, 0, 0)),
                    pl.BlockSpec((None, KV_WIDTH, n_cache), lambda b, i: (b, 0, 0))]
        args = [q, k, k, k, vt, vt, vt, cache[0], cache[1]]
    else:
        in_specs = [q_spec, kc_spec, vtc_spec]
        args = [q, k, vt]
    in_specs += [
        pl.BlockSpec((tq, D_MODEL), lambda b, i: (b * nq + i, 0)),
        pl.BlockSpec((None, N_MOD, D_MODEL), lambda b, i: (cond_row(b), 0, 0)),
        pl.BlockSpec((4, D_MODEL), lambda b, i: (0, 0)),
        pl.BlockSpec(memory_space=pltpu.SMEM),
        pl.BlockSpec((D_MODEL, D_MODEL), lambda b, i: (0, 0)),
    ]
    args += [x, m, norm_g, sink, w_out]
    return pl.pallas_call(
        functools.partial(_mix1_kernel, seq=seq, tq=tq, n_cache=n_cache),
        grid=(batch, nq),
        in_specs=in_specs,
        out_specs=pl.BlockSpec((tq, D_MODEL), lambda b, i: (b * nq + i, 0)),
        out_shape=jax.ShapeDtypeStruct((batch * seq, D_MODEL), F32),
        compiler_params=_params("arbitrary", "arbitrary"),
        name="mix_gqa",
    )(*args)


FFN_CHUNKS = (768, 768, 768, 512)


def _ffn_kernel(x_ref, m_ref, ng_ref, wgu_ref, wd_ref, o_ref):
    x = x_ref[...]
    h = (_rms(x, ng_ref[2:3, :]) * (1.0 + m_ref[4:5, :]) + m_ref[3:4, :]).astype(BF16)
    acc = jnp.zeros(x.shape, F32)
    c0 = 0
    for width in FFN_CHUNKS:
        gate = _dot(h, wgu_ref[:, c0:c0 + width])
        up = _dot(h, wgu_ref[:, FFN_HIDDEN + c0:FFN_HIDDEN + c0 + width])
        act = (gate * _sigmoid(gate) * up).astype(BF16)
        acc = acc + _dot(act, wd_ref[c0:c0 + width, :])
        c0 += width
    o_ref[...] = x + m_ref[5:6, :] * _rms(acc, ng_ref[3:4, :])


def _ffn(x, m, norm_g, w_gu, w_down, *, tm, cond_row):
    rows = x.shape[0]
    return pl.pallas_call(
        _ffn_kernel,
        grid=(rows // tm,),
        in_specs=[
            pl.BlockSpec((tm, D_MODEL), lambda i: (i, 0)),
            pl.BlockSpec((None, N_MOD, D_MODEL), lambda i: (cond_row(i), 0, 0)),
            pl.BlockSpec((4, D_MODEL), lambda i: (0, 0)),
            pl.BlockSpec((D_MODEL, 2 * FFN_HIDDEN), lambda i: (0, 0), pipeline_mode=pl.Buffered(1)),
            pl.BlockSpec((FFN_HIDDEN, D_MODEL), lambda i: (0, 0), pipeline_mode=pl.Buffered(1)),
        ],
        out_specs=pl.BlockSpec((tm, D_MODEL), lambda i: (i, 0)),
        out_shape=jax.ShapeDtypeStruct((rows, D_MODEL), F32),
        compiler_params=_params("arbitrary"),
        name="ffn",
    )(x, m, norm_g, w_gu, w_down)


def _trunk(x, m, caches, weights, *, batch, seq, cond_of_batch):
    (norm_g0, w_in, conv_w, conv_b, ln_g, ln_b, lam, subln_g, w_out0, w_gu0, w_down0,
     norm_g1, w_qkv, sink, w_out1, w_gu1, w_down1) = weights
    m0, m1 = m
    ctx = caches is None
    rope = not ctx
    tm = 512
    tq = 256
    row_of_block = lambda i: cond_of_batch((i * tm) // seq)

    o0 = 2 * CONV_CH
    ko, vo = o0 + DIFF_WIDTH, o0 + 2 * DIFF_WIDTH
    outs0 = [("nat", 0, o0, F32, False), ("nat", o0, DIFF_WIDTH, BF16, True), ("nat", ko, DIFF_WIDTH, BF16, True),
             ("T", vo, DIFF_WIDTH, BF16, False)]
    if ctx:
        outs0 += [("T", ko, DIFF_WIDTH, F32, False), ("heads", vo, DIFF_WIDTH, F32, False)]
    res = _pre(x, m0, norm_g0, w_in, tuple(outs0), tm=tm, cond_row=row_of_block, seq=seq, rope=rope)
    u, q, k, vt = res[:4]
    new0 = tuple(res[4:])
    lam_init = 0.8 - 0.6 * math.exp(-0.3 * 0)
    x = _mix0(u, q, k, vt, None if ctx else caches[0], x, m0, norm_g0, conv_w, conv_b,
              ln_g, ln_b, lam, subln_g, w_out0, batch=batch, seq=seq, tq=tq, cond_row=cond_of_batch,
              lam_init=lam_init)
    x = _ffn(x, m0, norm_g0, w_gu0, w_down0, tm=tm, cond_row=row_of_block)

    nq = GQA_HEADS * HEAD_DIM
    outs1 = [("nat", 0, nq, BF16, True), ("nat", nq, KV_WIDTH, BF16, True)]
    if ctx:
        outs1 += [("T", nq + KV_WIDTH, KV_WIDTH, F32, False), ("T", nq, KV_WIDTH, F32, False)]
    else:
        outs1 += [("T", nq + KV_WIDTH, KV_WIDTH, BF16, False)]
    res = _pre(x, m1, norm_g1, w_qkv, tuple(outs1), tm=tm, cond_row=row_of_block, seq=seq, rope=rope)
    q, k, vt = res[:3]
    new1 = (res[3], vt) if ctx else ()
    x = _mix1(q, k, vt, None if ctx else caches[1], x, m1, norm_g1, sink, w_out1,
              batch=batch, seq=seq, tq=tq, cond_row=cond_of_batch)
    x = _ffn(x, m1, norm_g1, w_gu1, w_down1, tm=tm, cond_row=row_of_block)
    return x, new0 + new1


def kernel(x_prompt, x_sample, cache_k0, cache_v0, cache_k1, cache_v1, c, c_ctx, l0_mod_w, l0_mod_b, l0_norm_g, l0_w_in, l0_conv_w, l0_conv_b, l0_conv_ln_g, l0_conv_ln_b, l0_lambda, l0_subln_g, l0_w_out, l0_w_gu, l0_w_down, l1_mod_w, l1_mod_b, l1_norm_g, l1_w_qkv, l1_sink, l1_w_out, l1_w_gu, l1_w_down):
    batch, seq, d = x_prompt.shape
    dec_batch, dec_seq, _ = x_sample.shape
    n_past = cache_k0.shape[1]

    cond = jnp.concatenate([c_ctx[None, :], c, jnp.zeros((MOD_ROWS - 1 - dec_batch, d), F32)], axis=0)
    m = (_modulation(cond, l0_mod_w, l0_mod_b), _modulation(cond, l1_mod_w, l1_mod_b))

    bf = lambda w: w.astype(BF16)
    row = lambda v: v.reshape(1, -1)
    weights = (l0_norm_g, bf(l0_w_in), l0_conv_w.reshape(CONV_WIDTH, CONV_CH), row(l0_conv_b),
               row(l0_conv_ln_g), row(l0_conv_ln_b), l0_lambda, l0_subln_g, bf(l0_w_out),
               bf(l0_w_gu), bf(l0_w_down),
               l1_norm_g, bf(l1_w_qkv), l1_sink, bf(l1_w_out), bf(l1_w_gu), bf(l1_w_down))

    y_prompt, (kt0, v0, kt1, vt1) = _trunk(x_prompt.reshape(batch * seq, d), m, None, weights,
                                           batch=batch, seq=seq, cond_of_batch=lambda b: 0)
    caches = ((cache_k0.reshape(dec_batch, n_past, -1),
               jnp.swapaxes(cache_v0.reshape(dec_batch, n_past, -1), 1, 2)),
              (cache_k1.reshape(dec_batch, n_past, -1),
               jnp.swapaxes(cache_v1.reshape(dec_batch, n_past, -1), 1, 2)))
    y_sample, _ = _trunk(x_sample.reshape(dec_batch * dec_seq, d), m, caches, weights,
                         batch=dec_batch, seq=dec_seq, cond_of_batch=lambda b: 1 + b)
    new_k0 = kt0.reshape(batch, DIFF_HEADS, 2, HEAD_DIM, seq).transpose(0, 4, 1, 2, 3)
    new_v0 = v0.reshape(batch, seq, DIFF_HEADS, 2 * HEAD_DIM)
    new_k1 = kt1.reshape(batch, GQA_KV_HEADS, HEAD_DIM, seq).transpose(0, 3, 1, 2)
    new_v1 = vt1.reshape(batch, GQA_KV_HEADS, HEAD_DIM, seq).transpose(0, 3, 1, 2)
    return (y_prompt.reshape(batch, seq, d), y_sample.reshape(dec_batch, dec_seq, d),
            new_k0, new_v0, new_k1, new_v1)
```

```python
import functools
import math

import numpy as np
import jax
import jax.numpy as jnp
from jax import lax
from jax.experimental import pallas as pl
from jax.experimental.pallas import tpu as pltpu

D_MODEL = 1024
GRID_W = 64
ROPE_THETA = 10000.0
EPS = 1e-6
NEG_INF = -1e30
CONV_CH = 512
CONV_WIDTH = 31
CONV_PAD = 16
DIFF_HEADS = 4
DIFF_WIDTH = 512
HEAD_DIM = 64
GQA_HEADS = 16
GQA_KV_HEADS = 4
GQA_GROUP = 4
Q_WIDTH = GQA_HEADS * HEAD_DIM
KV_WIDTH = GQA_KV_HEADS * HEAD_DIM
WINDOW = 128
FFN_HIDDEN = 2816
N_MOD = 6
LANES = 128
MOD_ROWS = 8
VMEM_LIMIT = 56 * 1024 * 1024

BF16 = jnp.bfloat16
F32 = jnp.float32


def _sigmoid(x):
    return 1.0 / (1.0 + jnp.exp(-x))


def _rms(x, g):
    return x * lax.rsqrt(jnp.mean(x * x, axis=-1, keepdims=True) + EPS) * g


def _dot(a, b):
    return jnp.dot(a, b, preferred_element_type=F32)


def _dot_nt(a, b):
    return lax.dot_general(a, b, (((1,), (1,)), ((), ())), preferred_element_type=F32)


def _params(*sem, flags=None):
    return pltpu.CompilerParams(dimension_semantics=sem, vmem_limit_bytes=VMEM_LIMIT, flags=flags)


def _resident(shape):
    return pl.BlockSpec(shape, lambda *_: (0,) * len(shape), pipeline_mode=pl.Buffered(1))


def _mod_kernel(c_ref, w_ref, b_ref, o_ref):
    c = c_ref[...]
    s = (c * _sigmoid(c)).astype(BF16)
    o_ref[...] = _dot(s, w_ref[...].astype(BF16)) + b_ref[...]


def _modulation(cond, mod_w, mod_b):
    n = mod_w.shape[1]
    tn = 1536
    out = pl.pallas_call(
        _mod_kernel,
        grid=(n // tn,),
        in_specs=[
            pl.BlockSpec((MOD_ROWS, D_MODEL), lambda j: (0, 0)),
            pl.BlockSpec((D_MODEL, tn), lambda j: (0, j)),
            pl.BlockSpec((1, tn), lambda j: (0, j)),
        ],
        out_specs=pl.BlockSpec((MOD_ROWS, tn), lambda j: (0, j)),
        out_shape=jax.ShapeDtypeStruct((MOD_ROWS, n), F32),
        compiler_params=_params("arbitrary"),
        name="modulation",
    )(cond, mod_w, mod_b.reshape(1, n))
    return out.reshape(MOD_ROWS, N_MOD, D_MODEL)


def _rope_tables(seq):
    t = np.arange(seq)
    rows, cols = t // GRID_W, t % GRID_W
    half = HEAD_DIM // 2
    inv = 1.0 / (ROPE_THETA ** (np.arange(0, half, 2, dtype=np.float64) / half))
    ar = rows[:, None] * inv[None, :]
    ac = cols[:, None] * inv[None, :]
    cos = np.concatenate([np.cos(ar), np.cos(ar), np.cos(ac), np.cos(ac)], axis=1)
    sin = np.concatenate([-np.sin(ar), np.sin(ar), -np.sin(ac), np.sin(ac)], axis=1)
    reps = LANES // HEAD_DIM
    return (jnp.asarray(np.tile(cos, (1, reps)), F32), jnp.asarray(np.tile(sin, (1, reps)), F32))


def _rope(x, cos, sin):
    quarter = HEAD_DIM // 4
    lane = lax.broadcasted_iota(jnp.int32, x.shape, 1)
    lo = (lane % (2 * quarter)) < quarter
    partner = jnp.where(lo, pltpu.roll(x, LANES - quarter, 1), pltpu.roll(x, quarter, 1))
    return x * cos + partner * sin


def _pre_body(x, m_ref, g_ref, w_ref, trig, outs, o_refs, sblk):
    tm = x.shape[0]
    h = _rms(x, g_ref[0:1, :]) * (1.0 + m_ref[1:2, :]) + m_ref[0:1, :]
    p = _dot(h.astype(BF16), w_ref[...])
    for (kind, start, width, _, roped), o_ref in zip(outs, o_refs):
        if kind == "nat":
            if trig is not None and roped:
                cos = trig[0][...]
                sin = trig[1][...]
                for c in range(width // LANES):
                    xc = p[:, start + c * LANES:start + (c + 1) * LANES]
                    o_ref[:, c * LANES:(c + 1) * LANES] = _rope(xc, cos, sin).astype(o_ref.dtype)
            else:
                o_ref[...] = p[:, start:start + width].astype(o_ref.dtype)
        elif kind == "T":
            for s in range(tm // sblk):
                o_ref[s] = p[s * sblk:(s + 1) * sblk, start:start + width].T.astype(o_ref.dtype)
        else:
            for hd in range(width // LANES):
                o_ref[:, hd, :] = p[:, start + hd * LANES:start + (hd + 1) * LANES]


def _pre_kernel(*refs, outs, rope, sblk):
    n_in = 6 if rope else 4
    x_ref, m_ref, g_ref, w_ref = refs[:4]
    trig = refs[4:6] if rope else None
    _pre_body(x_ref[...], m_ref, g_ref, w_ref, trig, outs, refs[n_in:], sblk)


def _pre(x, m, norm_g, w, outs, *, tm, cond_row, seq, rope):
    rows = x.shape[0]
    n = w.shape[1]
    sblk = min(seq, tm)
    per_seq = seq // sblk
    in_specs = [
        pl.BlockSpec((tm, D_MODEL), lambda i: (i, 0)),
        pl.BlockSpec((None, N_MOD, D_MODEL), lambda i: (cond_row(i), 0, 0)),
        pl.BlockSpec((4, D_MODEL), lambda i: (0, 0)),
        pl.BlockSpec((D_MODEL, n), lambda i: (0, 0)),
    ]
    args = [x, m, norm_g, w]
    if rope:
        cos, sin = _rope_tables(seq)
        nblk = seq // tm
        in_specs += [pl.BlockSpec((tm, LANES), lambda i: (i % nblk, 0))] * 2
        args += [cos, sin]
    out_specs, out_shape = [], []
    for (kind, _, wd, dt, _) in outs:
        if kind == "nat":
            out_specs.append(pl.BlockSpec((tm, wd), lambda i: (i, 0)))
            out_shape.append(jax.ShapeDtypeStruct((rows, wd), dt))
        elif kind == "T":
            out_specs.append(pl.BlockSpec((tm // sblk, wd, sblk), lambda i: (i // per_seq, 0, i % per_seq)))
            out_shape.append(jax.ShapeDtypeStruct((rows // seq, wd, seq), dt))
        else:
            out_specs.append(pl.BlockSpec((tm, wd // LANES, LANES), lambda i: (i, 0, 0)))
            out_shape.append(jax.ShapeDtypeStruct((rows, wd // LANES, LANES), dt))
    return pl.pallas_call(
        functools.partial(_pre_kernel, outs=outs, rope=rope, sblk=sblk),
        grid=(rows // tm,),
        in_specs=in_specs,
        out_specs=out_specs,
        out_shape=out_shape,
        compiler_params=_params("arbitrary"),
        name="pre_rope" if rope else "pre",
    )(*args)


def _softmax_stats(s_list, floor=None):
    mx = None
    for s in s_list:
        smax = jnp.max(s, axis=0, keepdims=True)
        mx = smax if mx is None else jnp.maximum(mx, smax)
    if floor is not None:
        mx = jnp.maximum(mx, floor)
    e_list = [jnp.exp(s - mx) for s in s_list]
    den = None
    for e in e_list:
        esum = jnp.sum(e, axis=0, keepdims=True)
        den = esum if den is None else den + esum
    return mx, e_list, den


def _drain(stages):
    for _ in stages:
        pass


def _interleave(a, b, order):
    gens = {"a": a, "b": b}
    for ch in order:
        next(gens[ch], None)
    _drain(a)
    _drain(b)


def _mix0_stages(qi, u_ref, q_ref, k_ref, vt_ref, cache, load_x, m_ref, ng_ref, cw_ref, cb_ref, lg_ref, lb_ref,
                 lam_ref, sg_ref, wo_ref, z_s, zw_s, zsh_s, store, *, seq, tq, lam_init):
    def glu():
        z_s[0:CONV_PAD, :] = jnp.zeros((CONV_PAD, CONV_CH), F32)
        z_s[CONV_PAD + seq:2 * CONV_PAD + seq, :] = jnp.zeros((CONV_PAD, CONV_CH), F32)
        z_s[CONV_PAD:CONV_PAD + seq, :] = u_ref[:, 0:CONV_CH] * _sigmoid(u_ref[:, CONV_CH:2 * CONV_CH])

    win = tq + 2 * CONV_PAD
    if isinstance(qi, int):
        glu()
        zw = z_s
    else:
        pl.when(qi == 0)(glu)
        zw_s[...] = z_s[pl.ds(pl.multiple_of(qi * tq, tq), win), :]
        zw = zw_s
    yield

    span = win - 8
    for b in range(1, 8):
        zsh_s[b - 1, 0:span, :] = zw[b:b + span, :]
    yield
    chunk = 64
    off0 = CONV_PAD - CONV_WIDTH // 2
    conv_rows = []
    for c0 in range(0, tq, chunk):
        acc = jnp.zeros((chunk, CONV_CH), F32)
        for j in range(CONV_WIDTH):
            off = j + off0
            a, b = off // 8, off % 8
            if b == 0:
                zz = zw[c0 + 8 * a:c0 + 8 * a + chunk, :]
            else:
                zz = zsh_s[b - 1, c0 + 8 * a:c0 + 8 * a + chunk, :]
            acc = acc + zz * cw_ref[j:j + 1, :]
        acc = acc + cb_ref[...]
        mu = jnp.mean(acc, axis=-1, keepdims=True)
        xc = acc - mu
        y = xc * lax.rsqrt(jnp.mean(xc * xc, axis=-1, keepdims=True) + EPS) * lg_ref[...] + lb_ref[...]
        conv_rows.append((y * _sigmoid(y)).astype(BF16))
        yield
    conv_out = jnp.concatenate(conv_rows, axis=0)

    la = jnp.sum(lam_ref[0:1, :] * lam_ref[1:2, :], axis=-1, keepdims=True)
    lb = jnp.sum(lam_ref[2:3, :] * lam_ref[3:4, :], axis=-1, keepdims=True)
    lam = jnp.exp(la) - jnp.exp(lb) + lam_init
    sub = LANES // 2
    scale = sub ** -0.5
    lane = lax.broadcasted_iota(jnp.int32, (tq, LANES), 1)
    heads = []
    for h in range(DIFF_HEADS):
        cols = slice(h * LANES, (h + 1) * LANES)
        qc = q_ref[:, cols] * jnp.asarray(scale, BF16)
        zero = jnp.zeros_like(qc)
        qq = jnp.concatenate([jnp.where(lane < sub, qc, zero), jnp.where(lane >= sub, qc, zero)], axis=0)
        segs = [(k_ref[:, cols], vt_ref[cols, :])]
        if cache is not None:
            segs.append((cache[0][:, cols].astype(BF16), cache[1][cols, :].astype(BF16)))
        _, e_list, den = _softmax_stats([_dot_nt(kk, qq) for kk, _ in segs])
        ov = None
        for e, (_, vt) in zip(e_list, segs):
            pv = _dot(vt, e.astype(BF16))
            ov = pv if ov is None else ov + pv
        r = 1.0 / den
        o = ov[:, 0:tq] * r[:, 0:tq] - ov[:, tq:2 * tq] * (lam * r[:, tq:2 * tq])
        o = o * lax.rsqrt(jnp.mean(o * o, axis=0, keepdims=True) + EPS) * sg_ref[...] * (1.0 - lam_init)
        heads.append(o)
        yield
    attn = jnp.concatenate(heads, axis=0).T.astype(BF16)
    cat = jnp.concatenate([conv_out, attn], axis=1)
    mixed = _dot(cat, wo_ref[...])
    store(load_x() + m_ref[2:3, :] * _rms(mixed, ng_ref[1:2, :]))


def _set(ref, idx=None):
    def store(value):
        if idx is None:
            ref[...] = value
        else:
            ref[idx] = value
    return store


def _mix0_kernel(*refs, seq, tq, n_cache, lam_init):
    u_ref, q_ref, k_ref, vt_ref = refs[:4]
    cache = refs[4:6] if n_cache else None
    rest = refs[6:] if n_cache else refs[4:]
    x_ref, params, o_ref, scratch = rest[0], rest[1:10], rest[10], rest[11:]
    _drain(_mix0_stages(pl.program_id(1), u_ref, q_ref, k_ref, vt_ref, cache, lambda: x_ref[...], *params,
                        *scratch, _set(o_ref), seq=seq, tq=tq, lam_init=lam_init))


def _mix0(u, q, k, vt, cache, x, m, norm_g, conv_w, conv_b, ln_g, ln_b, lam, subln_g, w_out,
          *, batch, seq, tq, cond_row, lam_init):
    nq = seq // tq
    n_cache = 0 if cache is None else cache[0].shape[1]
    win = tq + 2 * CONV_PAD
    const = lambda shape: pl.BlockSpec(shape, lambda b, i: (0,) * len(shape))
    in_specs = [pl.BlockSpec((seq, 2 * CONV_CH), lambda b, i: (b, 0)),
                pl.BlockSpec((tq, DIFF_WIDTH), lambda b, i: (b * nq + i, 0)),
                pl.BlockSpec((seq, DIFF_WIDTH), lambda b, i: (b, 0)),
                pl.BlockSpec((None, DIFF_WIDTH, seq), lambda b, i: (b, 0, 0))]
    args = [u, q, k, vt]
    if n_cache:
        in_specs += [pl.BlockSpec((None, n_cache, DIFF_WIDTH), lambda b, i: (b, 0, 0)),
                     pl.BlockSpec((None, DIFF_WIDTH, n_cache), lambda b, i: (b, 0, 0))]
        args += [cache[0], cache[1]]
    in_specs += [
        pl.BlockSpec((tq, D_MODEL), lambda b, i: (b * nq + i, 0)),
        pl.BlockSpec((None, N_MOD, D_MODEL), lambda b, i: (cond_row(b), 0, 0)),
        const((4, D_MODEL)), const((CONV_WIDTH, CONV_CH)), const((1, CONV_CH)), const((1, CONV_CH)),
        const((1, CONV_CH)), const((4, LANES // 2)), const((LANES, 1)), const((D_MODEL, D_MODEL)),
    ]
    args += [x, m, norm_g, conv_w, conv_b, ln_g, ln_b, lam, subln_g, w_out]
    return pl.pallas_call(
        functools.partial(_mix0_kernel, seq=seq, tq=tq, n_cache=n_cache, lam_init=lam_init),
        grid=(batch, nq),
        in_specs=in_specs,
        out_specs=pl.BlockSpec((tq, D_MODEL), lambda b, i: (b * nq + i, 0)),
        out_shape=jax.ShapeDtypeStruct((batch * seq, D_MODEL), F32),
        scratch_shapes=[
            pltpu.VMEM((seq + 2 * CONV_PAD, CONV_CH), F32), pltpu.VMEM((win, CONV_CH), F32),
            pltpu.VMEM((7, win - 8, CONV_CH), F32),
        ],
        compiler_params=_params("arbitrary", "arbitrary"),
        name="mix_conv_diff",
    )(*args)


def _mix1_stages(q_ref, segs, load_x, m_ref, ng_ref, sink_ref, wo_ref, store, *, tq):
    scale = HEAD_DIM ** -0.5
    lane = lax.broadcasted_iota(jnp.int32, (tq, LANES), 1)
    out_rows = []
    for g in range(GQA_KV_HEADS):
        e_chunk, b = g // 2, g % 2
        kcols = slice(e_chunk * LANES, (e_chunk + 1) * LANES)
        vrows = slice(g * HEAD_DIM, (g + 1) * HEAD_DIM)
        keep = (lane < HEAD_DIM) if b == 0 else (lane >= HEAD_DIM)
        qs, sinks = [], []
        for hh in range(GQA_GROUP):
            h = g * GQA_GROUP + hh
            c, a = h // 2, h % 2
            qc = q_ref[:, c * LANES:(c + 1) * LANES] * jnp.asarray(scale, BF16)
            qa = qc if a == b else pltpu.roll(qc, HEAD_DIM, 1)
            qs.append(jnp.where(keep, qa, jnp.zeros_like(qa)))
            sinks.append(jnp.full((1, tq), sink_ref[h], F32))
        qg = jnp.concatenate(qs, axis=0)
        sink = jnp.concatenate(sinks, axis=1)
        s_list = []
        for k_ref, _, ok in segs:
            s = _dot_nt(k_ref[:, kcols].astype(BF16), qg)
            s_list.append(s if ok is None else jnp.where(ok, s, NEG_INF))
        mx, e_list, den = _softmax_stats(s_list, floor=sink)
        den = den + jnp.exp(sink - mx)
        ov = None
        for e, (_, vt_ref, _) in zip(e_list, segs):
            pv = _dot(vt_ref[vrows, :].astype(BF16), e.astype(BF16))
            ov = pv if ov is None else ov + pv
        ov = ov * (1.0 / den)
        out_rows += [ov[:, hh * tq:(hh + 1) * tq] for hh in range(GQA_GROUP)]
        yield
    attn = jnp.concatenate(out_rows, axis=0).T.astype(BF16)
    mixed = _dot(attn, wo_ref[...])
    store(load_x() + m_ref[2:3, :] * _rms(mixed, ng_ref[1:2, :]))


def _mix1_kernel(q_ref, kp_ref, kc_ref, kn_ref, vtp_ref, vtc_ref, vtn_ref, ck_ref, cvt_ref,
                 x_ref, m_ref, ng_ref, sink_ref, wo_ref, o_ref, *, seq, tq):
    qi = pl.program_id(1)

    def band(base, rows):
        kpos = base + lax.broadcasted_iota(jnp.int32, (rows, tq), 0)
        qpos = qi * tq + lax.broadcasted_iota(jnp.int32, (rows, tq), 1)
        ok = (jnp.abs(qpos - kpos) <= WINDOW) & (kpos >= 0) & (kpos < seq)
        return jnp.concatenate([ok] * GQA_GROUP, axis=1)

    segs = [(kp_ref, vtp_ref, band(qi * tq - WINDOW, WINDOW)),
            (kc_ref, vtc_ref, band(qi * tq, tq)),
            (kn_ref, vtn_ref, band(qi * tq + tq, WINDOW)),
            (ck_ref, cvt_ref, None)]
    _drain(_mix1_stages(q_ref, segs, lambda: x_ref[...], m_ref, ng_ref, sink_ref, wo_ref, _set(o_ref), tq=tq))


def _mix1(q, k, vt, cache, x, m, norm_g, sink, w_out, *, batch, seq, tq, cond_row):
    nq = seq // tq
    n_cache = cache[0].shape[1]
    nw = seq // WINDOW
    per = tq // WINDOW
    prev = lambda i: jnp.maximum(i * per - 1, 0)
    nxt = lambda i: jnp.minimum(i * per + per, nw - 1)
    in_specs = [
        pl.BlockSpec((tq, D_MODEL), lambda b, i: (b * nq + i, 0)),
        pl.BlockSpec((WINDOW, KV_WIDTH), lambda b, i: (b * nw + prev(i), 0)),
        pl.BlockSpec((tq, KV_WIDTH), lambda b, i: (b * nq + i, 0)),
        pl.BlockSpec((WINDOW, KV_WIDTH), lambda b, i: (b * nw + nxt(i), 0)),
        pl.BlockSpec((None, KV_WIDTH, WINDOW), lambda b, i: (b, 0, prev(i))),
        pl.BlockSpec((None, KV_WIDTH, tq), lambda b, i: (b, 0, i)),
        pl.BlockSpec((None, KV_WIDTH, WINDOW), lambda b, i: (b, 0, nxt(i))),
        pl.BlockSpec((None, n_cache, KV_WIDTH), lambda b, i: (b, 0, 0)),
        pl.BlockSpec((None, KV_WIDTH, n_cache), lambda b, i: (b, 0, 0)),
        pl.BlockSpec((tq, D_MODEL), lambda b, i: (b * nq + i, 0)),
        pl.BlockSpec((None, N_MOD, D_MODEL), lambda b, i: (cond_row(b), 0, 0)),
        pl.BlockSpec((4, D_MODEL), lambda b, i: (0, 0)),
        pl.BlockSpec(memory_space=pltpu.SMEM),
        pl.BlockSpec((D_MODEL, D_MODEL), lambda b, i: (0, 0)),
    ]
    return pl.pallas_call(
        functools.partial(_mix1_kernel, seq=seq, tq=tq),
        grid=(batch, nq),
        in_specs=in_specs,
        out_specs=pl.BlockSpec((tq, D_MODEL), lambda b, i: (b * nq + i, 0)),
        out_shape=jax.ShapeDtypeStruct((batch * seq, D_MODEL), F32),
        compiler_params=_params("arbitrary", "arbitrary"),
        name="mix_gqa",
    )(q, k, k, k, vt, vt, vt, cache[0], cache[1], x, m, norm_g, sink, w_out)


FFN_CHUNKS = (768, 768, 768, 512)


def _ffn_stages(load_x, m_ref, ng_ref, wgu_ref, wd_ref, store, chunks=FFN_CHUNKS):
    h = (_rms(load_x(), ng_ref[2:3, :]) * (1.0 + m_ref[4:5, :]) + m_ref[3:4, :]).astype(BF16)
    yield
    acc = jnp.zeros(h.shape, F32)
    c0 = 0
    for width in chunks:
        gate = _dot(h, wgu_ref[:, c0:c0 + width])
        up = _dot(h, wgu_ref[:, FFN_HIDDEN + c0:FFN_HIDDEN + c0 + width])
        act = (gate * _sigmoid(gate) * up).astype(BF16)
        acc = acc + _dot(act, wd_ref[c0:c0 + width, :])
        c0 += width
        yield
    store(load_x() + m_ref[5:6, :] * _rms(acc, ng_ref[3:4, :]))


def _ffn_kernel(x_ref, m_ref, ng_ref, wgu_ref, wd_ref, o_ref):
    _drain(_ffn_stages(lambda: x_ref[...], m_ref, ng_ref, wgu_ref, wd_ref, _set(o_ref)))


def _ffn(x, m, norm_g, w_gu, w_down, *, tm, cond_row):
    rows = x.shape[0]
    return pl.pallas_call(
        _ffn_kernel,
        grid=(rows // tm,),
        in_specs=[
            pl.BlockSpec((tm, D_MODEL), lambda i: (i, 0)),
            pl.BlockSpec((None, N_MOD, D_MODEL), lambda i: (cond_row(i), 0, 0)),
            pl.BlockSpec((4, D_MODEL), lambda i: (0, 0)),
            _resident((D_MODEL, 2 * FFN_HIDDEN)),
            _resident((FFN_HIDDEN, D_MODEL)),
        ],
        out_specs=pl.BlockSpec((tm, D_MODEL), lambda i: (i, 0)),
        out_shape=jax.ShapeDtypeStruct((rows, D_MODEL), F32),
        compiler_params=_params("arbitrary"),
        name="ffn",
    )(x, m, norm_g, w_gu, w_down)


CTX_FFN_CHUNKS = (512, 512, 512, 512, 512, 256)


def _layer0_ctx_kernel(x_ref, m_ref, ng_ref, win_ref, cw_ref, cb_ref, lg_ref, lb_ref, lam_ref, sg_ref, wo_ref,
                       wgu_ref, wd_ref, y_ref, kt_ref, v_ref, x1_s, u_s, q_s, k_s, vt_s, z_s, zsh_s,
                       *, seq, outs, lam_init):
    i = pl.program_id(0)

    @pl.when(i == 0)
    def _():
        x1_s[...] = jnp.zeros(x1_s.shape, F32)

    load_x = lambda: x_ref[...]

    def mixer():
        _pre_body(load_x(), m_ref, ng_ref, win_ref, None, outs, (u_s, q_s, k_s, vt_s, kt_ref, v_ref), seq)
        yield
        yield from _mix0_stages(0, u_s, q_s, k_s, vt_s.at[0], None, load_x, m_ref, ng_ref, cw_ref, cb_ref, lg_ref,
                                lb_ref, lam_ref, sg_ref, wo_ref, z_s, None, zsh_s, _set(x1_s, i % 2),
                                seq=seq, tq=seq, lam_init=lam_init)

    ffn = _ffn_stages(lambda: x1_s[(i + 1) % 2], m_ref, ng_ref, wgu_ref, wd_ref, _set(y_ref), CTX_FFN_CHUNKS)
    _interleave(mixer(), ffn, "babaabaabaabaabaabba")


def _layer1_ctx_kernel(x_ref, m_ref, ng_ref, wqkv_ref, sink_ref, wo_ref, wgu_ref, wd_ref,
                       y_ref, kt_ref, vt_ref, x1_s, q_s, k_s, *, seq, outs):
    i = pl.program_id(0)

    @pl.when(i == 0)
    def _():
        x1_s[...] = jnp.zeros(x1_s.shape, F32)

    load_x = lambda: x_ref[...]

    def mixer():
        _pre_body(load_x(), m_ref, ng_ref, wqkv_ref, None, outs, (q_s, k_s, vt_ref, kt_ref), seq)
        yield
        yield from _mix1_stages(q_s, [(k_s, vt_ref.at[0], None)], load_x, m_ref, ng_ref, sink_ref, wo_ref,
                                _set(x1_s, i % 2), tq=seq)

    ffn = _ffn_stages(lambda: x1_s[(i + 1) % 2], m_ref, ng_ref, wgu_ref, wd_ref, _set(y_ref), CTX_FFN_CHUNKS)
    _interleave(mixer(), ffn, "bababbabababba")


def _ctx_specs(batch, seq):
    last = batch - 1
    cur = lambda i: jnp.minimum(i, last)
    x_spec = pl.BlockSpec((seq, D_MODEL), lambda i: (cur(i), 0))
    y_spec = pl.BlockSpec((seq, D_MODEL), lambda i: (jnp.maximum(i - 1, 0), 0))
    t_spec = lambda width: pl.BlockSpec((1, width, seq), lambda i: (cur(i), 0, 0))
    return cur, x_spec, y_spec, t_spec


def _layer0_ctx(x, m, norm_g, w_in, conv_w, conv_b, ln_g, ln_b, lam, subln_g, w_out, w_gu, w_down,
                *, batch, seq, lam_init):
    cur, x_spec, y_spec, t_spec = _ctx_specs(batch, seq)
    o0 = 2 * CONV_CH
    ko, vo = o0 + DIFF_WIDTH, o0 + 2 * DIFF_WIDTH
    outs = (("nat", 0, o0, F32, False), ("nat", o0, DIFF_WIDTH, BF16, False), ("nat", ko, DIFF_WIDTH, BF16, False),
            ("T", vo, DIFF_WIDTH, BF16, False), ("T", ko, DIFF_WIDTH, F32, False), ("heads", vo, DIFF_WIDTH, F32, False))
    const = lambda shape: pl.BlockSpec(shape, lambda i: (0,) * len(shape))
    rows = batch * seq
    return pl.pallas_call(
        functools.partial(_layer0_ctx_kernel, seq=seq, outs=outs, lam_init=lam_init),
        grid=(batch + 1,),
        in_specs=[
            x_spec, pl.BlockSpec((None, N_MOD, D_MODEL), lambda i: (0, 0, 0)), const((4, D_MODEL)),
            _resident(w_in.shape), const((CONV_WIDTH, CONV_CH)), const((1, CONV_CH)), const((1, CONV_CH)),
            const((1, CONV_CH)), const((4, LANES // 2)), const((LANES, 1)), _resident(w_out.shape),
            _resident(w_gu.shape), _resident(w_down.shape),
        ],
        out_specs=[y_spec, t_spec(DIFF_WIDTH),
                   pl.BlockSpec((seq, DIFF_HEADS, LANES), lambda i: (cur(i), 0, 0))],
        out_shape=[jax.ShapeDtypeStruct((rows, D_MODEL), F32),
                   jax.ShapeDtypeStruct((batch, DIFF_WIDTH, seq), F32),
                   jax.ShapeDtypeStruct((rows, DIFF_HEADS, LANES), F32)],
        scratch_shapes=[
            pltpu.VMEM((2, seq, D_MODEL), F32), pltpu.VMEM((seq, o0), F32),
            pltpu.VMEM((seq, DIFF_WIDTH), BF16), pltpu.VMEM((seq, DIFF_WIDTH), BF16),
            pltpu.VMEM((1, DIFF_WIDTH, seq), BF16),
            pltpu.VMEM((seq + 2 * CONV_PAD, CONV_CH), F32), pltpu.VMEM((7, seq + 2 * CONV_PAD - 8, CONV_CH), F32),
        ],
        compiler_params=_params("arbitrary"),
        name="layer0_ctx",
    )(x, m, norm_g, w_in, conv_w, conv_b, ln_g, ln_b, lam, subln_g, w_out, w_gu, w_down)


def _layer1_ctx(x, m, norm_g, w_qkv, sink, w_out, w_gu, w_down, *, batch, seq):
    _, x_spec, y_spec, t_spec = _ctx_specs(batch, seq)
    outs = (("nat", 0, Q_WIDTH, BF16, False), ("nat", Q_WIDTH, KV_WIDTH, BF16, False),
            ("T", Q_WIDTH + KV_WIDTH, KV_WIDTH, F32, False), ("T", Q_WIDTH, KV_WIDTH, F32, False))
    rows = batch * seq
    return pl.pallas_call(
        functools.partial(_layer1_ctx_kernel, seq=seq, outs=outs),
        grid=(batch + 1,),
        in_specs=[
            x_spec, pl.BlockSpec((None, N_MOD, D_MODEL), lambda i: (0, 0, 0)),
            pl.BlockSpec((4, D_MODEL), lambda i: (0, 0)), _resident(w_qkv.shape),
            pl.BlockSpec(memory_space=pltpu.SMEM), _resident(w_out.shape),
            _resident(w_gu.shape), _resident(w_down.shape),
        ],
        out_specs=[y_spec, t_spec(KV_WIDTH), t_spec(KV_WIDTH)],
        out_shape=[jax.ShapeDtypeStruct((rows, D_MODEL), F32),
                   jax.ShapeDtypeStruct((batch, KV_WIDTH, seq), F32),
                   jax.ShapeDtypeStruct((batch, KV_WIDTH, seq), F32)],
        scratch_shapes=[pltpu.VMEM((2, seq, D_MODEL), F32), pltpu.VMEM((seq, Q_WIDTH), BF16),
                        pltpu.VMEM((seq, KV_WIDTH), BF16)],
        compiler_params=_params("arbitrary"),
        name="layer1_ctx",
    )(x, m, norm_g, w_qkv, sink, w_out, w_gu, w_down)


LAM_INIT_0 = 0.8 - 0.6 * math.exp(-0.3 * 0)


def _context_pass(x, m, weights, *, batch, seq):
    (norm_g0, w_in, conv_w, conv_b, ln_g, ln_b, lam, subln_g, w_out0, w_gu0, w_down0,
     norm_g1, w_qkv, sink, w_out1, w_gu1, w_down1) = weights
    x, kt0, v0 = _layer0_ctx(x, m[0], norm_g0, w_in, conv_w, conv_b, ln_g, ln_b, lam, subln_g, w_out0,
                             w_gu0, w_down0, batch=batch, seq=seq, lam_init=LAM_INIT_0)
    x, kt1, vt1 = _layer1_ctx(x, m[1], norm_g1, w_qkv, sink, w_out1, w_gu1, w_down1, batch=batch, seq=seq)
    return x, (kt0, v0, kt1, vt1)


def _denoise_pass(x, m, caches, weights, *, batch, seq):
    (norm_g0, w_in, conv_w, conv_b, ln_g, ln_b, lam, subln_g, w_out0, w_gu0, w_down0,
     norm_g1, w_qkv, sink, w_out1, w_gu1, w_down1) = weights
    m0, m1 = m
    tm = 512
    tq = 256
    cond_of_batch = lambda b: 1 + b
    row_of_block = lambda i: cond_of_batch((i * tm) // seq)

    o0 = 2 * CONV_CH
    ko, vo = o0 + DIFF_WIDTH, o0 + 2 * DIFF_WIDTH
    outs0 = (("nat", 0, o0, F32, False), ("nat", o0, DIFF_WIDTH, BF16, True), ("nat", ko, DIFF_WIDTH, BF16, True),
             ("T", vo, DIFF_WIDTH, BF16, False))
    u, q, k, vt = _pre(x, m0, norm_g0, w_in, outs0, tm=tm, cond_row=row_of_block, seq=seq, rope=True)
    x = _mix0(u, q, k, vt, caches[0], x, m0, norm_g0, conv_w, conv_b, ln_g, ln_b, lam, subln_g, w_out0,
              batch=batch, seq=seq, tq=tq, cond_row=cond_of_batch, lam_init=LAM_INIT_0)
    x = _ffn(x, m0, norm_g0, w_gu0, w_down0, tm=tm, cond_row=row_of_block)

    outs1 = (("nat", 0, Q_WIDTH, BF16, True), ("nat", Q_WIDTH, KV_WIDTH, BF16, True),
             ("T", Q_WIDTH + KV_WIDTH, KV_WIDTH, BF16, False))
    q, k, vt = _pre(x, m1, norm_g1, w_qkv, outs1, tm=tm, cond_row=row_of_block, seq=seq, rope=True)
    x = _mix1(q, k, vt, caches[1], x, m1, norm_g1, sink, w_out1, batch=batch, seq=seq, tq=tq,
              cond_row=cond_of_batch)
    return _ffn(x, m1, norm_g1, w_gu1, w_down1, tm=tm, cond_row=row_of_block)


def kernel(x_prompt, x_sample, cache_k0, cache_v0, cache_k1, cache_v1, c, c_ctx, l0_mod_w, l0_mod_b, l0_norm_g, l0_w_in, l0_conv_w, l0_conv_b, l0_conv_ln_g, l0_conv_ln_b, l0_lambda, l0_subln_g, l0_w_out, l0_w_gu, l0_w_down, l1_mod_w, l1_mod_b, l1_norm_g, l1_w_qkv, l1_sink, l1_w_out, l1_w_gu, l1_w_down):
    batch, seq, d = x_prompt.shape
    dec_batch, dec_seq, _ = x_sample.shape
    n_past = cache_k0.shape[1]

    cond = jnp.concatenate([c_ctx[None, :], c, jnp.zeros((MOD_ROWS - 1 - dec_batch, d), F32)], axis=0)
    m = (_modulation(cond, l0_mod_w, l0_mod_b), _modulation(cond, l1_mod_w, l1_mod_b))

    bf = lambda w: w.astype(BF16)
    row = lambda v: v.reshape(1, -1)
    weights = (l0_norm_g, bf(l0_w_in), l0_conv_w.reshape(CONV_WIDTH, CONV_CH), row(l0_conv_b),
               row(l0_conv_ln_g), row(l0_conv_ln_b), l0_lambda, l0_subln_g.reshape(LANES, 1), bf(l0_w_out),
               bf(l0_w_gu), bf(l0_w_down),
               l1_norm_g, bf(l1_w_qkv), l1_sink, bf(l1_w_out), bf(l1_w_gu), bf(l1_w_down))

    y_prompt, (kt0, v0, kt1, vt1) = _context_pass(x_prompt.reshape(batch * seq, d), m, weights,
                                                  batch=batch, seq=seq)
    caches = ((cache_k0.reshape(dec_batch, n_past, -1),
               jnp.swapaxes(cache_v0.reshape(dec_batch, n_past, -1), 1, 2)),
              (cache_k1.reshape(dec_batch, n_past, -1),
               jnp.swapaxes(cache_v1.reshape(dec_batch, n_past, -1), 1, 2)))
    y_sample = _denoise_pass(x_sample.reshape(dec_batch * dec_seq, d), m, caches, weights,
                             batch=dec_batch, seq=dec_seq)
    new_k0 = kt0.reshape(batch, DIFF_HEADS, 2, HEAD_DIM, seq).transpose(0, 4, 1, 2, 3)
    new_v0 = v0.reshape(batch, seq, DIFF_HEADS, 2 * HEAD_DIM)
    new_k1 = kt1.reshape(batch, GQA_KV_HEADS, HEAD_DIM, seq).transpose(0, 3, 1, 2)
    new_v1 = vt1.reshape(batch, GQA_KV_HEADS, HEAD_DIM, seq).transpose(0, 3, 1, 2)
    return (y_prompt.reshape(batch, seq, d), y_sample.reshape(dec_batch, dec_seq, d),
            new_k0, new_v0, new_k1, new_v1)
```

```python
import functools
import math

import numpy as np
import jax
import jax.numpy as jnp
from jax import lax
from jax.experimental import pallas as pl
from jax.experimental.pallas import tpu as pltpu

D_MODEL = 1024
GRID_W = 64
ROPE_THETA = 10000.0
EPS = 1e-6
NEG_INF = -1e30
LOG2E = math.log2(math.e)
CONV_CH = 512
CONV_WIDTH = 31
CONV_PAD = 16
DIFF_HEADS = 4
DIFF_WIDTH = 512
HEAD_DIM = 64
GQA_HEADS = 16
GQA_KV_HEADS = 4
GQA_GROUP = 4
Q_WIDTH = GQA_HEADS * HEAD_DIM
KV_WIDTH = GQA_KV_HEADS * HEAD_DIM
Q_SCALE = LOG2E * HEAD_DIM ** -0.5
WINDOW = 128
FFN_HIDDEN = 2816
N_MOD = 6
LANES = 128
BF16_ROWS = 16
MOD_ROWS = 8
VMEM_LIMIT = 56 * 1024 * 1024

BF16 = jnp.bfloat16
F32 = jnp.float32


def _sigmoid(x):
    return 1.0 / (1.0 + jnp.exp(-x))


def _rms(x, g):
    return x * lax.rsqrt(jnp.mean(x * x, axis=-1, keepdims=True) + EPS) * g


def _dot(a, b):
    return jnp.dot(a, b, preferred_element_type=F32)


def _dot_nt(a, b):
    return lax.dot_general(a, b, (((1,), (1,)), ((), ())), preferred_element_type=F32)


def _params(*sem):
    return pltpu.CompilerParams(dimension_semantics=sem, vmem_limit_bytes=VMEM_LIMIT)


def _resident(shape):
    return pl.BlockSpec(shape, lambda *_: (0,) * len(shape), pipeline_mode=pl.Buffered(1))


def _mod_kernel(c_ref, w_ref, b_ref, o_ref):
    c = c_ref[...]
    s = (c * _sigmoid(c)).astype(BF16)
    o_ref[...] = _dot(s, w_ref[...].astype(BF16)) + b_ref[...]


def _modulation(cond, mod_w, mod_b):
    n = mod_w.shape[1]
    tn = 1536
    out = pl.pallas_call(
        _mod_kernel,
        grid=(n // tn,),
        in_specs=[
            pl.BlockSpec((MOD_ROWS, D_MODEL), lambda j: (0, 0)),
            pl.BlockSpec((D_MODEL, tn), lambda j: (0, j)),
            pl.BlockSpec((1, tn), lambda j: (0, j)),
        ],
        out_specs=pl.BlockSpec((MOD_ROWS, tn), lambda j: (0, j)),
        out_shape=jax.ShapeDtypeStruct((MOD_ROWS, n), F32),
        compiler_params=_params("arbitrary"),
        name="modulation",
    )(cond, mod_w, mod_b.reshape(1, n))
    return out.reshape(MOD_ROWS, N_MOD, D_MODEL)


def _rope_tables(seq):
    t = np.arange(seq)
    rows, cols = t // GRID_W, t % GRID_W
    half = HEAD_DIM // 2
    inv = 1.0 / (ROPE_THETA ** (np.arange(0, half, 2, dtype=np.float64) / half))
    ar = rows[:, None] * inv[None, :]
    ac = cols[:, None] * inv[None, :]
    cos = np.concatenate([np.cos(ar), np.cos(ar), np.cos(ac), np.cos(ac)], axis=1)
    sin = np.concatenate([-np.sin(ar), np.sin(ar), -np.sin(ac), np.sin(ac)], axis=1)
    reps = LANES // HEAD_DIM
    return (jnp.asarray(np.tile(cos, (1, reps)), F32), jnp.asarray(np.tile(sin, (1, reps)), F32))


def _rope(x, cos, sin):
    quarter = HEAD_DIM // 4
    lane = lax.broadcasted_iota(jnp.int32, x.shape, 1)
    lo = (lane % (2 * quarter)) < quarter
    partner = jnp.where(lo, pltpu.roll(x, LANES - quarter, 1), pltpu.roll(x, quarter, 1))
    return x * cos + partner * sin


def _pre_kernel(*refs, outs, rope, sblk):
    n_in = 6 if rope else 4
    x_ref, m_ref, g_ref, w_ref = refs[:4]
    o_refs = refs[n_in:]
    x = x_ref[...]
    tm = x.shape[0]
    h = _rms(x, g_ref[0:1, :]) * (1.0 + m_ref[1:2, :]) + m_ref[0:1, :]
    p = _dot(h.astype(BF16), w_ref[...])
    for (kind, start, width, _, roped, scale), o_ref in zip(outs, o_refs):
        if kind == "nat":
            for c in range(width // LANES):
                xc = p[:, start + c * LANES:start + (c + 1) * LANES]
                if rope and roped:
                    xc = _rope(xc, refs[4][...], refs[5][...])
                if scale is not None:
                    xc = xc * scale
                o_ref[:, c * LANES:(c + 1) * LANES] = xc.astype(o_ref.dtype)
        elif kind == "T":
            for s in range(tm // sblk):
                o_ref[s] = p[s * sblk:(s + 1) * sblk, start:start + width].T.astype(o_ref.dtype)
        else:
            for hd in range(width // LANES):
                o_ref[:, hd, :] = p[:, start + hd * LANES:start + (hd + 1) * LANES]


def _pre(x, m, norm_g, w, outs, *, tm, cond_row, seq, rope):
    rows = x.shape[0]
    n = w.shape[1]
    sblk = min(seq, tm)
    per_seq = seq // sblk
    in_specs = [
        pl.BlockSpec((tm, D_MODEL), lambda i: (i, 0)),
        pl.BlockSpec((None, N_MOD, D_MODEL), lambda i: (cond_row(i), 0, 0)),
        pl.BlockSpec((4, D_MODEL), lambda i: (0, 0)),
        pl.BlockSpec((D_MODEL, n), lambda i: (0, 0)),
    ]
    args = [x, m, norm_g, w]
    if rope:
        cos, sin = _rope_tables(seq)
        nblk = seq // tm
        in_specs += [pl.BlockSpec((tm, LANES), lambda i: (i % nblk, 0))] * 2
        args += [cos, sin]
    out_specs, out_shape = [], []
    for (kind, _, wd, dt, _, _) in outs:
        if kind == "nat":
            out_specs.append(pl.BlockSpec((tm, wd), lambda i: (i, 0)))
            out_shape.append(jax.ShapeDtypeStruct((rows, wd), dt))
        elif kind == "T":
            out_specs.append(pl.BlockSpec((tm // sblk, wd, sblk), lambda i: (i // per_seq, 0, i % per_seq)))
            out_shape.append(jax.ShapeDtypeStruct((rows // seq, wd, seq), dt))
        else:
            out_specs.append(pl.BlockSpec((tm, wd // LANES, LANES), lambda i: (i, 0, 0)))
            out_shape.append(jax.ShapeDtypeStruct((rows, wd // LANES, LANES), dt))
    return pl.pallas_call(
        functools.partial(_pre_kernel, outs=outs, rope=rope, sblk=sblk),
        grid=(rows // tm,),
        in_specs=in_specs,
        out_specs=out_specs,
        out_shape=out_shape,
        compiler_params=_params("arbitrary"),
        name="pre_rope" if rope else "pre",
    )(*args)


def _attend(s_list, vt_list, floor=None):
    mx = None
    for s in s_list:
        smax = jnp.max(s, axis=0, keepdims=True)
        mx = smax if mx is None else jnp.maximum(mx, smax)
    if floor is not None:
        mx = jnp.maximum(mx, floor)
    d = vt_list[0].shape[0]
    ov = None
    for s, vt in zip(s_list, vt_list):
        ones = jnp.ones((BF16_ROWS, vt.shape[1]), BF16)
        pv = _dot(jnp.concatenate([vt, ones], axis=0), jnp.exp2(s - mx).astype(BF16))
        ov = pv if ov is None else ov + pv
    return ov[0:d, :], ov[d:d + 1, :], mx


def _pipelined(n, first, second):
    out = []
    nxt = first(0)
    for u in range(n):
        cur = nxt
        if u + 1 < n:
            nxt = first(u + 1)
        out.append(second(u, cur))
    return out


def _mix0_kernel(*refs, seq, tq, n_cache, lam_init):
    if n_cache:
        (u_ref, q_ref, k_ref, vt_ref, ck_ref, cvt_ref, x_ref, m_ref, ng_ref, cw_ref, cb_ref,
         lg_ref, lb_ref, lam_ref, sg_ref, wo_ref, o_ref, z_s, zw_s, zsh_s) = refs
    else:
        (u_ref, q_ref, k_ref, vt_ref, x_ref, m_ref, ng_ref, cw_ref, cb_ref,
         lg_ref, lb_ref, lam_ref, sg_ref, wo_ref, o_ref, z_s, zw_s, zsh_s) = refs
    qi = pl.program_id(1)

    @pl.when(qi == 0)
    def _():
        z_s[0:CONV_PAD, :] = jnp.zeros((CONV_PAD, CONV_CH), F32)
        z_s[CONV_PAD + seq:2 * CONV_PAD + seq, :] = jnp.zeros((CONV_PAD, CONV_CH), F32)
        z_s[CONV_PAD:CONV_PAD + seq, :] = u_ref[:, 0:CONV_CH] * _sigmoid(u_ref[:, CONV_CH:2 * CONV_CH])

    r0 = pl.multiple_of(qi * tq, tq)
    win = tq + 2 * CONV_PAD
    zw_s[...] = z_s[pl.ds(r0, win), :]
    span = win - 8
    for b in range(1, 8):
        zsh_s[b - 1, 0:span, :] = zw_s[b:b + span, :]
    chunk = 64
    off0 = CONV_PAD - CONV_WIDTH // 2
    conv_rows = []
    for c0 in range(0, tq, chunk):
        acc = jnp.zeros((chunk, CONV_CH), F32)
        for j in range(CONV_WIDTH):
            off = j + off0
            a, b = off // 8, off % 8
            if b == 0:
                zz = zw_s[c0 + 8 * a:c0 + 8 * a + chunk, :]
            else:
                zz = zsh_s[b - 1, c0 + 8 * a:c0 + 8 * a + chunk, :]
            acc = acc + zz * cw_ref[j:j + 1, :]
        acc = acc + cb_ref[...]
        mu = jnp.mean(acc, axis=-1, keepdims=True)
        xc = acc - mu
        y = xc * lax.rsqrt(jnp.mean(xc * xc, axis=-1, keepdims=True) + EPS) * lg_ref[...] + lb_ref[...]
        conv_rows.append((y * _sigmoid(y)).astype(BF16))
    conv_out = jnp.concatenate(conv_rows, axis=0)

    la = jnp.sum(lam_ref[0:1, :] * lam_ref[1:2, :], axis=-1, keepdims=True)
    lb = jnp.sum(lam_ref[2:3, :] * lam_ref[3:4, :], axis=-1, keepdims=True)
    lam = jnp.exp(la) - jnp.exp(lb) + lam_init
    sub = LANES // 2
    lane = lax.broadcasted_iota(jnp.int32, (tq, LANES), 1)

    def segments(h):
        cols = slice(h * LANES, (h + 1) * LANES)
        segs = [(k_ref[:, cols], vt_ref[cols, :])]
        if n_cache:
            segs.append((ck_ref[:, cols].astype(BF16), cvt_ref[cols, :].astype(BF16)))
        return segs

    def scores(h):
        qc = q_ref[:, h * LANES:(h + 1) * LANES]
        zero = jnp.zeros_like(qc)
        qq = jnp.concatenate([jnp.where(lane < sub, qc, zero), jnp.where(lane >= sub, qc, zero)], axis=0)
        return [_dot_nt(kk, qq) for kk, _ in segments(h)]

    def head(h, s_list):
        ov, den, _ = _attend(s_list, [vt for _, vt in segments(h)])
        r = 1.0 / den
        o = ov[:, 0:tq] * r[:, 0:tq] - ov[:, tq:2 * tq] * (lam * r[:, tq:2 * tq])
        return o * lax.rsqrt(jnp.mean(o * o, axis=0, keepdims=True) + EPS) * sg_ref[...] * (1.0 - lam_init)

    heads = _pipelined(DIFF_HEADS, scores, head)
    attn = jnp.concatenate(heads, axis=0).T.astype(BF16)
    cat = jnp.concatenate([conv_out, attn], axis=1)
    mixed = _dot(cat, wo_ref[...])
    o_ref[...] = x_ref[...] + m_ref[2:3, :] * _rms(mixed, ng_ref[1:2, :])


def _mix0(u, q, k, vt, cache, x, m, norm_g, conv_w, conv_b, ln_g, ln_b, lam, subln_g, w_out,
          *, batch, seq, tq, cond_row, lam_init):
    nq = seq // tq
    n_cache = 0 if cache is None else cache[0].shape[1]
    win = tq + 2 * CONV_PAD
    const = lambda shape: pl.BlockSpec(shape, lambda b, i: (0,) * len(shape))
    in_specs = [pl.BlockSpec((seq, 2 * CONV_CH), lambda b, i: (b, 0)),
                pl.BlockSpec((tq, DIFF_WIDTH), lambda b, i: (b * nq + i, 0)),
                pl.BlockSpec((seq, DIFF_WIDTH), lambda b, i: (b, 0)),
                pl.BlockSpec((None, DIFF_WIDTH, seq), lambda b, i: (b, 0, 0))]
    args = [u, q, k, vt]
    if n_cache:
        in_specs += [pl.BlockSpec((None, n_cache, DIFF_WIDTH), lambda b, i: (b, 0, 0)),
                     pl.BlockSpec((None, DIFF_WIDTH, n_cache), lambda b, i: (b, 0, 0))]
        args += [cache[0], cache[1]]
    in_specs += [
        pl.BlockSpec((tq, D_MODEL), lambda b, i: (b * nq + i, 0)),
        pl.BlockSpec((None, N_MOD, D_MODEL), lambda b, i: (cond_row(b), 0, 0)),
        const((4, D_MODEL)), const((CONV_WIDTH, CONV_CH)), const((1, CONV_CH)), const((1, CONV_CH)),
        const((1, CONV_CH)), const((4, LANES // 2)), const((LANES, 1)), const((D_MODEL, D_MODEL)),
    ]
    args += [x, m, norm_g, conv_w, conv_b, ln_g, ln_b, lam, subln_g, w_out]
    return pl.pallas_call(
        functools.partial(_mix0_kernel, seq=seq, tq=tq, n_cache=n_cache, lam_init=lam_init),
        grid=(batch, nq),
        in_specs=in_specs,
        out_specs=pl.BlockSpec((tq, D_MODEL), lambda b, i: (b * nq + i, 0)),
        out_shape=jax.ShapeDtypeStruct((batch * seq, D_MODEL), F32),
        scratch_shapes=[
            pltpu.VMEM((seq + 2 * CONV_PAD, CONV_CH), F32), pltpu.VMEM((win, CONV_CH), F32),
            pltpu.VMEM((7, win - 8, CONV_CH), F32),
        ],
        compiler_params=_params("arbitrary", "arbitrary"),
        name="mix_conv_diff",
    )(*args)


def _mix1_kernel(*refs, seq, tq, n_cache):
    if n_cache:
        (q_ref, kp_ref, kc_ref, kn_ref, vtp_ref, vtc_ref, vtn_ref, ck_ref, cvt_ref,
         x_ref, m_ref, ng_ref, sink_ref, wo_ref, o_ref) = refs
    else:
        (q_ref, kc_ref, vtc_ref, x_ref, m_ref, ng_ref, sink_ref, wo_ref, o_ref) = refs
    qi = pl.program_id(1)
    lane = lax.broadcasted_iota(jnp.int32, (tq, LANES), 1)

    def band(base, rows):
        kpos = base + lax.broadcasted_iota(jnp.int32, (rows, tq), 0)
        qpos = qi * tq + lax.broadcasted_iota(jnp.int32, (rows, tq), 1)
        ok = (jnp.abs(qpos - kpos) <= WINDOW) & (kpos >= 0) & (kpos < seq)
        return jnp.concatenate([ok] * GQA_GROUP, axis=1)

    if n_cache:
        segs = [(kp_ref, vtp_ref, band(qi * tq - WINDOW, WINDOW)),
                (kc_ref, vtc_ref, band(qi * tq, tq)),
                (kn_ref, vtn_ref, band(qi * tq + tq, WINDOW)),
                (ck_ref, cvt_ref, None)]
    else:
        segs = [(kc_ref, vtc_ref, None)]

    def scores(g):
        b = g % 2
        kcols = slice((g // 2) * LANES, (g // 2 + 1) * LANES)
        keep = (lane < HEAD_DIM) if b == 0 else (lane >= HEAD_DIM)
        qs = []
        for h in range(g * GQA_GROUP, (g + 1) * GQA_GROUP):
            qc = q_ref[:, (h // 2) * LANES:(h // 2 + 1) * LANES]
            qa = qc if h % 2 == b else pltpu.roll(qc, HEAD_DIM, 1)
            qs.append(jnp.where(keep, qa, jnp.zeros_like(qa)))
        qg = jnp.concatenate(qs, axis=0)
        s_list = []
        for k_ref, _, ok in segs:
            s = _dot_nt(k_ref[:, kcols].astype(BF16), qg)
            s_list.append(s if ok is None else jnp.where(ok, s, NEG_INF))
        return s_list

    def group(g, s_list):
        vrows = slice(g * HEAD_DIM, (g + 1) * HEAD_DIM)
        sink = jnp.concatenate([jnp.full((1, tq), sink_ref[h] * LOG2E, F32)
                                for h in range(g * GQA_GROUP, (g + 1) * GQA_GROUP)], axis=1)
        ov, den, mx = _attend(s_list, [vt_ref[vrows, :].astype(BF16) for _, vt_ref, _ in segs], floor=sink)
        ov = ov * (1.0 / (den + jnp.exp2(sink - mx)))
        return jnp.concatenate([ov[:, hh * tq:(hh + 1) * tq] for hh in range(GQA_GROUP)], axis=0)

    groups = _pipelined(GQA_KV_HEADS, scores, group)
    attn = jnp.concatenate(groups, axis=0).T.astype(BF16)
    mixed = _dot(attn, wo_ref[...])
    o_ref[...] = x_ref[...] + m_ref[2:3, :] * _rms(mixed, ng_ref[1:2, :])


def _mix1(q, k, vt, cache, x, m, norm_g, sink, w_out, *, batch, seq, tq, cond_row):
    nq = seq // tq
    n_cache = 0 if cache is None else cache[0].shape[1]
    q_spec = pl.BlockSpec((tq, D_MODEL), lambda b, i: (b * nq + i, 0))
    kc_spec = pl.BlockSpec((tq, KV_WIDTH), lambda b, i: (b * nq + i, 0))
    vtc_spec = pl.BlockSpec((None, KV_WIDTH, tq), lambda b, i: (b, 0, i))
    if n_cache:
        nw = seq // WINDOW
        per = tq // WINDOW
        prev = lambda i: jnp.maximum(i * per - 1, 0)
        nxt = lambda i: jnp.minimum(i * per + per, nw - 1)
        in_specs = [q_spec,
                    pl.BlockSpec((WINDOW, KV_WIDTH), lambda b, i: (b * nw + prev(i), 0)), kc_spec,
                    pl.BlockSpec((WINDOW, KV_WIDTH), lambda b, i: (b * nw + nxt(i), 0)),
                    pl.BlockSpec((None, KV_WIDTH, WINDOW), lambda b, i: (b, 0, prev(i))), vtc_spec,
                    pl.BlockSpec((None, KV_WIDTH, WINDOW), lambda b, i: (b, 0, nxt(i))),
                    pl.BlockSpec((None, n_cache, KV_WIDTH), lambda b, i: (b, 0, 0)),
                    pl.BlockSpec((None, KV_WIDTH, n_cache), lambda b, i: (b, 0, 0))]
        args = [q, k, k, k, vt, vt, vt, cache[0], cache[1]]
    else:
        in_specs = [q_spec, kc_spec, vtc_spec]
        args = [q, k, vt]
    in_specs += [
        pl.BlockSpec((tq, D_MODEL), lambda b, i: (b * nq + i, 0)),
        pl.BlockSpec((None, N_MOD, D_MODEL), lambda b, i: (cond_row(b), 0, 0)),
        pl.BlockSpec((4, D_MODEL), lambda b, i: (0, 0)),
        pl.BlockSpec(memory_space=pltpu.SMEM),
        pl.BlockSpec((D_MODEL, D_MODEL), lambda b, i: (0, 0)),
    ]
    args += [x, m, norm_g, sink, w_out]
    return pl.pallas_call(
        functools.partial(_mix1_kernel, seq=seq, tq=tq, n_cache=n_cache),
        grid=(batch, nq),
        in_specs=in_specs,
        out_specs=pl.BlockSpec((tq, D_MODEL), lambda b, i: (b * nq + i, 0)),
        out_shape=jax.ShapeDtypeStruct((batch * seq, D_MODEL), F32),
        compiler_params=_params("arbitrary", "arbitrary"),
        name="mix_gqa",
    )(*args)


FFN_CHUNKS = (768, 768, 768, 512)


def _ffn_kernel(x_ref, m_ref, ng_ref, wgu_ref, wd_ref, o_ref):
    h = (_rms(x_ref[...], ng_ref[2:3, :]) * (1.0 + m_ref[4:5, :]) + m_ref[3:4, :]).astype(BF16)
    acc = jnp.zeros(h.shape, F32)
    c0 = 0
    for width in FFN_CHUNKS:
        gate = _dot(h, wgu_ref[:, c0:c0 + width])
        up = _dot(h, wgu_ref[:, FFN_HIDDEN + c0:FFN_HIDDEN + c0 + width])
        act = (gate * _sigmoid(gate) * up).astype(BF16)
        acc = acc + _dot(act, wd_ref[c0:c0 + width, :])
        c0 += width
    o_ref[...] = x_ref[...] + m_ref[5:6, :] * _rms(acc, ng_ref[3:4, :])


def _ffn(x, m, norm_g, w_gu, w_down, *, tm, cond_row):
    rows = x.shape[0]
    return pl.pallas_call(
        _ffn_kernel,
        grid=(rows // tm,),
        in_specs=[
            pl.BlockSpec((tm, D_MODEL), lambda i: (i, 0)),
            pl.BlockSpec((None, N_MOD, D_MODEL), lambda i: (cond_row(i), 0, 0)),
            pl.BlockSpec((4, D_MODEL), lambda i: (0, 0)),
            _resident((D_MODEL, 2 * FFN_HIDDEN)),
            _resident((FFN_HIDDEN, D_MODEL)),
        ],
        out_specs=pl.BlockSpec((tm, D_MODEL), lambda i: (i, 0)),
        out_shape=jax.ShapeDtypeStruct((rows, D_MODEL), F32),
        compiler_params=_params("arbitrary"),
        name="ffn",
    )(x, m, norm_g, w_gu, w_down)


LAM_INIT_0 = 0.8 - 0.6 * math.exp(-0.3 * 0)


def _trunk(x, m, caches, weights, *, batch, seq, cond_of_batch):
    (norm_g0, w_in, conv_w, conv_b, ln_g, ln_b, lam, subln_g, w_out0, w_gu0, w_down0,
     norm_g1, w_qkv, sink, w_out1, w_gu1, w_down1) = weights
    m0, m1 = m
    ctx = caches is None
    rope = not ctx
    tm = 512
    tm_ffn = 1024 if ctx else 512
    tq = 256
    row_of = lambda t: (lambda i: cond_of_batch((i * t) // seq))

    o0 = 2 * CONV_CH
    ko, vo = o0 + DIFF_WIDTH, o0 + 2 * DIFF_WIDTH
    outs0 = [("nat", 0, o0, F32, False, None), ("nat", o0, DIFF_WIDTH, BF16, True, Q_SCALE),
             ("nat", ko, DIFF_WIDTH, BF16, True, None), ("T", vo, DIFF_WIDTH, BF16, False, None)]
    if ctx:
        outs0 += [("T", ko, DIFF_WIDTH, F32, False, None), ("heads", vo, DIFF_WIDTH, F32, False, None)]
    res = _pre(x, m0, norm_g0, w_in, tuple(outs0), tm=tm, cond_row=row_of(tm), seq=seq, rope=rope)
    u, q, k, vt = res[:4]
    new0 = tuple(res[4:])
    x = _mix0(u, q, k, vt, None if ctx else caches[0], x, m0, norm_g0, conv_w, conv_b,
              ln_g, ln_b, lam, subln_g, w_out0, batch=batch, seq=seq, tq=tq, cond_row=cond_of_batch,
              lam_init=LAM_INIT_0)
    x = _ffn(x, m0, norm_g0, w_gu0, w_down0, tm=tm_ffn, cond_row=row_of(tm_ffn))

    outs1 = [("nat", 0, Q_WIDTH, BF16, True, Q_SCALE), ("nat", Q_WIDTH, KV_WIDTH, BF16, True, None)]
    if ctx:
        outs1 += [("T", Q_WIDTH + KV_WIDTH, KV_WIDTH, F32, False, None), ("T", Q_WIDTH, KV_WIDTH, F32, False, None)]
    else:
        outs1 += [("T", Q_WIDTH + KV_WIDTH, KV_WIDTH, BF16, False, None)]
    res = _pre(x, m1, norm_g1, w_qkv, tuple(outs1), tm=tm, cond_row=row_of(tm), seq=seq, rope=rope)
    q, k, vt = res[:3]
    new1 = (res[3], vt) if ctx else ()
    x = _mix1(q, k, vt, None if ctx else caches[1], x, m1, norm_g1, sink, w_out1,
              batch=batch, seq=seq, tq=tq, cond_row=cond_of_batch)
    x = _ffn(x, m1, norm_g1, w_gu1, w_down1, tm=tm_ffn, cond_row=row_of(tm_ffn))
    return x, new0 + new1


def kernel(x_prompt, x_sample, cache_k0, cache_v0, cache_k1, cache_v1, c, c_ctx, l0_mod_w, l0_mod_b, l0_norm_g, l0_w_in, l0_conv_w, l0_conv_b, l0_conv_ln_g, l0_conv_ln_b, l0_lambda, l0_subln_g, l0_w_out, l0_w_gu, l0_w_down, l1_mod_w, l1_mod_b, l1_norm_g, l1_w_qkv, l1_sink, l1_w_out, l1_w_gu, l1_w_down):
    batch, seq, d = x_prompt.shape
    dec_batch, dec_seq, _ = x_sample.shape
    n_past = cache_k0.shape[1]

    cond = jnp.concatenate([c_ctx[None, :], c, jnp.zeros((MOD_ROWS - 1 - dec_batch, d), F32)], axis=0)
    m = (_modulation(cond, l0_mod_w, l0_mod_b), _modulation(cond, l1_mod_w, l1_mod_b))

    bf = lambda w: w.astype(BF16)
    row = lambda v: v.reshape(1, -1)
    weights = (l0_norm_g, bf(l0_w_in), l0_conv_w.reshape(CONV_WIDTH, CONV_CH), row(l0_conv_b),
               row(l0_conv_ln_g), row(l0_conv_ln_b), l0_lambda, l0_subln_g.reshape(LANES, 1), bf(l0_w_out),
               bf(l0_w_gu), bf(l0_w_down),
               l1_norm_g, bf(l1_w_qkv), l1_sink, bf(l1_w_out), bf(l1_w_gu), bf(l1_w_down))

    y_prompt, (kt0, v0, kt1, vt1) = _trunk(x_prompt.reshape(batch * seq, d), m, None, weights,
                                           batch=batch, seq=seq, cond_of_batch=lambda b: 0)
    caches = ((cache_k0.reshape(dec_batch, n_past, -1),
               jnp.swapaxes(cache_v0.reshape(dec_batch, n_past, -1), 1, 2)),
              (cache_k1.reshape(dec_batch, n_past, -1),
               jnp.swapaxes(cache_v1.reshape(dec_batch, n_past, -1), 1, 2)))
    y_sample, _ = _trunk(x_sample.reshape(dec_batch * dec_seq, d), m, caches, weights,
                         batch=dec_batch, seq=dec_seq, cond_of_batch=lambda b: 1 + b)
    new_k0 = kt0.reshape(batch, DIFF_HEADS, 2, HEAD_DIM, seq).transpose(0, 4, 1, 2, 3)
    new_v0 = v0.reshape(batch, seq, DIFF_HEADS, 2 * HEAD_DIM)
    new_k1 = kt1.reshape(batch, GQA_KV_HEADS, HEAD_DIM, seq).transpose(0, 3, 1, 2)
    new_v1 = vt1.reshape(batch, GQA_KV_HEADS, HEAD_DIM, seq).transpose(0, 3, 1, 2)
    return (y_prompt.reshape(batch, seq, d), y_sample.reshape(dec_batch, dec_seq, d),
            new_k0, new_v0, new_k1, new_v1)
```

```python
import functools
import math

import numpy as np
import jax
import jax.numpy as jnp
from jax import lax
from jax.experimental import pallas as pl
from jax.experimental.pallas import tpu as pltpu

D_MODEL = 1024
GRID_W = 64
ROPE_THETA = 10000.0
EPS = 1e-6
NEG_INF = -1e30
LOG2E = math.log2(math.e)
CONV_CH = 512
CONV_WIDTH = 31
CONV_PAD = 16
DIFF_HEADS = 4
DIFF_WIDTH = 512
HEAD_DIM = 64
GQA_HEADS = 16
GQA_KV_HEADS = 4
GQA_GROUP = 4
Q_WIDTH = GQA_HEADS * HEAD_DIM
KV_WIDTH = GQA_KV_HEADS * HEAD_DIM
Q_SCALE = LOG2E * HEAD_DIM ** -0.5
WINDOW = 128
FFN_HIDDEN = 2816
N_MOD = 6
LANES = 128
BF16_ROWS = 16
MOD_ROWS = 8
VMEM_LIMIT = 56 * 1024 * 1024

BF16 = jnp.bfloat16
F32 = jnp.float32


def _sigmoid(x):
    return 1.0 / (1.0 + jnp.exp(-x))


def _rms(x, g):
    return x * lax.rsqrt(jnp.mean(x * x, axis=-1, keepdims=True) + EPS) * g


def _dot(a, b):
    return jnp.dot(a, b, preferred_element_type=F32)


def _dot_nt(a, b):
    return lax.dot_general(a, b, (((1,), (1,)), ((), ())), preferred_element_type=F32)


def _params(*sem):
    return pltpu.CompilerParams(dimension_semantics=sem, vmem_limit_bytes=VMEM_LIMIT)


def _resident(shape):
    return pl.BlockSpec(shape, lambda *_: (0,) * len(shape), pipeline_mode=pl.Buffered(1))


def _mod_kernel(c_ref, w_ref, b_ref, o_ref):
    c = c_ref[...]
    s = (c * _sigmoid(c)).astype(BF16)
    o_ref[...] = _dot(s, w_ref[...].astype(BF16)) + b_ref[...]


def _modulation(cond, mod_w, mod_b):
    n = mod_w.shape[1]
    tn = 1536
    out = pl.pallas_call(
        _mod_kernel,
        grid=(n // tn,),
        in_specs=[
            pl.BlockSpec((MOD_ROWS, D_MODEL), lambda j: (0, 0)),
            pl.BlockSpec((D_MODEL, tn), lambda j: (0, j)),
            pl.BlockSpec((1, tn), lambda j: (0, j)),
        ],
        out_specs=pl.BlockSpec((MOD_ROWS, tn), lambda j: (0, j)),
        out_shape=jax.ShapeDtypeStruct((MOD_ROWS, n), F32),
        compiler_params=_params("arbitrary"),
        name="modulation",
    )(cond, mod_w, mod_b.reshape(1, n))
    return out.reshape(MOD_ROWS, N_MOD, D_MODEL)


def _rope_tables(seq):
    t = np.arange(seq)
    rows, cols = t // GRID_W, t % GRID_W
    half = HEAD_DIM // 2
    inv = 1.0 / (ROPE_THETA ** (np.arange(0, half, 2, dtype=np.float64) / half))
    ar = rows[:, None] * inv[None, :]
    ac = cols[:, None] * inv[None, :]
    cos = np.concatenate([np.cos(ar), np.cos(ar), np.cos(ac), np.cos(ac)], axis=1)
    sin = np.concatenate([-np.sin(ar), np.sin(ar), -np.sin(ac), np.sin(ac)], axis=1)
    reps = LANES // HEAD_DIM
    return (jnp.asarray(np.tile(cos, (1, reps)), F32), jnp.asarray(np.tile(sin, (1, reps)), F32))


def _rope(x, cos, sin):
    quarter = HEAD_DIM // 4
    lane = lax.broadcasted_iota(jnp.int32, x.shape, 1)
    lo = (lane % (2 * quarter)) < quarter
    partner = jnp.where(lo, pltpu.roll(x, LANES - quarter, 1), pltpu.roll(x, quarter, 1))
    return x * cos + partner * sin


def _cast_once(first, w_ref, w_s):
    @pl.when(first)
    def _():
        w_s[...] = w_ref[...].astype(BF16)
    return w_s


def _pre_kernel(*refs, outs, rope, sblk, cast_w):
    n_in = 6 if rope else 4
    x_ref, m_ref, g_ref, w_ref = refs[:4]
    o_refs = refs[n_in:n_in + len(outs)]
    if cast_w:
        w_ref = _cast_once(pl.program_id(0) == 0, w_ref, refs[n_in + len(outs)])
    x = x_ref[...]
    tm = x.shape[0]
    h = _rms(x, g_ref[0:1, :]) * (1.0 + m_ref[1:2, :]) + m_ref[0:1, :]
    p = _dot(h.astype(BF16), w_ref[...])
    for (kind, start, width, _, roped, scale), o_ref in zip(outs, o_refs):
        if kind == "nat":
            for c in range(width // LANES):
                xc = p[:, start + c * LANES:start + (c + 1) * LANES]
                if rope and roped:
                    xc = _rope(xc, refs[4][...], refs[5][...])
                if scale is not None:
                    xc = xc * scale
                o_ref[:, c * LANES:(c + 1) * LANES] = xc.astype(o_ref.dtype)
        elif kind == "T":
            for s in range(tm // sblk):
                o_ref[s] = p[s * sblk:(s + 1) * sblk, start:start + width].T.astype(o_ref.dtype)
        else:
            for hd in range(width // LANES):
                o_ref[:, hd, :] = p[:, start + hd * LANES:start + (hd + 1) * LANES]


def _pre(x, m, norm_g, w, outs, *, tm, cond_row, seq, rope):
    rows = x.shape[0]
    n = w.shape[1]
    sblk = min(seq, tm)
    per_seq = seq // sblk
    cast_w = w.dtype == F32
    in_specs = [
        pl.BlockSpec((tm, D_MODEL), lambda i: (i, 0)),
        pl.BlockSpec((None, N_MOD, D_MODEL), lambda i: (cond_row(i), 0, 0)),
        pl.BlockSpec((4, D_MODEL), lambda i: (0, 0)),
        _resident((D_MODEL, n)),
    ]
    args = [x, m, norm_g, w]
    if rope:
        cos, sin = _rope_tables(seq)
        nblk = seq // tm
        in_specs += [pl.BlockSpec((tm, LANES), lambda i: (i % nblk, 0))] * 2
        args += [cos, sin]
    out_specs, out_shape = [], []
    for (kind, _, wd, dt, _, _) in outs:
        if kind == "nat":
            out_specs.append(pl.BlockSpec((tm, wd), lambda i: (i, 0)))
            out_shape.append(jax.ShapeDtypeStruct((rows, wd), dt))
        elif kind == "T":
            out_specs.append(pl.BlockSpec((tm // sblk, wd, sblk), lambda i: (i // per_seq, 0, i % per_seq)))
            out_shape.append(jax.ShapeDtypeStruct((rows // seq, wd, seq), dt))
        else:
            out_specs.append(pl.BlockSpec((tm, wd // LANES, LANES), lambda i: (i, 0, 0)))
            out_shape.append(jax.ShapeDtypeStruct((rows, wd // LANES, LANES), dt))
    return pl.pallas_call(
        functools.partial(_pre_kernel, outs=outs, rope=rope, sblk=sblk, cast_w=cast_w),
        grid=(rows // tm,),
        in_specs=in_specs,
        out_specs=out_specs,
        out_shape=out_shape,
        scratch_shapes=[pltpu.VMEM((D_MODEL, n), BF16)] if cast_w else [],
        compiler_params=_params("arbitrary"),
        name="pre_rope" if rope else "pre",
    )(*args)


def _attend(s_list, vt_list, floor=None):
    mx = None
    for s in s_list:
        smax = jnp.max(s, axis=0, keepdims=True)
        mx = smax if mx is None else jnp.maximum(mx, smax)
    if floor is not None:
        mx = jnp.maximum(mx, floor)
    d = vt_list[0].shape[0]
    ov = None
    for s, vt in zip(s_list, vt_list):
        ones = jnp.ones((BF16_ROWS, vt.shape[1]), BF16)
        pv = _dot(jnp.concatenate([vt, ones], axis=0), jnp.exp2(s - mx).astype(BF16))
        ov = pv if ov is None else ov + pv
    return ov[0:d, :], ov[d:d + 1, :], mx


def _pipelined(n, first, second):
    out = []
    nxt = first(0)
    for u in range(n):
        cur = nxt
        if u + 1 < n:
            nxt = first(u + 1)
        out.append(second(u, cur))
    return out


def _side_convert(src_refs, dst_refs):
    for src, dst in zip(src_refs, dst_refs):
        dst[...] = src[...].astype(BF16)


def _mix0_kernel(*refs, seq, tq, n_cache, lam_init, n_side):
    n_in = 16 if n_cache else 14
    u_ref, q_ref, k_ref, vt_ref = refs[:4]
    ck_ref, cvt_ref = refs[4:6] if n_cache else (None, None)
    x_ref, m_ref, ng_ref, cw_ref, cb_ref, lg_ref, lb_ref, lam_ref, sg_ref, wo_ref = refs[n_in - 10:n_in]
    o_ref = refs[n_in + n_side]
    z_s, zw_s, zsh_s, wo_s = refs[n_in + 2 * n_side + 1:]
    _side_convert(refs[n_in:n_in + n_side], refs[n_in + n_side + 1:n_in + 2 * n_side + 1])
    qi = pl.program_id(1)
    wo_ref = _cast_once((pl.program_id(0) == 0) & (qi == 0), wo_ref, wo_s)

    @pl.when(qi == 0)
    def _():
        z_s[0:CONV_PAD, :] = jnp.zeros((CONV_PAD, CONV_CH), F32)
        z_s[CONV_PAD + seq:2 * CONV_PAD + seq, :] = jnp.zeros((CONV_PAD, CONV_CH), F32)
        z_s[CONV_PAD:CONV_PAD + seq, :] = u_ref[:, 0:CONV_CH] * _sigmoid(u_ref[:, CONV_CH:2 * CONV_CH])

    r0 = pl.multiple_of(qi * tq, tq)
    win = tq + 2 * CONV_PAD
    zw_s[...] = z_s[pl.ds(r0, win), :]
    span = win - 8
    for b in range(1, 8):
        zsh_s[b - 1, 0:span, :] = zw_s[b:b + span, :]
    chunk = 64
    off0 = CONV_PAD - CONV_WIDTH // 2
    conv_rows = []
    for c0 in range(0, tq, chunk):
        acc = jnp.zeros((chunk, CONV_CH), F32)
        for j in range(CONV_WIDTH):
            off = j + off0
            a, b = off // 8, off % 8
            if b == 0:
                zz = zw_s[c0 + 8 * a:c0 + 8 * a + chunk, :]
            else:
                zz = zsh_s[b - 1, c0 + 8 * a:c0 + 8 * a + chunk, :]
            acc = acc + zz * cw_ref[j:j + 1, :]
        acc = acc + cb_ref[...]
        mu = jnp.mean(acc, axis=-1, keepdims=True)
        xc = acc - mu
        y = xc * lax.rsqrt(jnp.mean(xc * xc, axis=-1, keepdims=True) + EPS) * lg_ref[...] + lb_ref[...]
        conv_rows.append((y * _sigmoid(y)).astype(BF16))
    conv_out = jnp.concatenate(conv_rows, axis=0)

    la = jnp.sum(lam_ref[0:1, :] * lam_ref[1:2, :], axis=-1, keepdims=True)
    lb = jnp.sum(lam_ref[2:3, :] * lam_ref[3:4, :], axis=-1, keepdims=True)
    lam = jnp.exp(la) - jnp.exp(lb) + lam_init
    sub = LANES // 2
    lane = lax.broadcasted_iota(jnp.int32, (tq, LANES), 1)

    def segments(h):
        cols = slice(h * LANES, (h + 1) * LANES)
        segs = [(k_ref[:, cols], vt_ref[cols, :])]
        if n_cache:
            segs.append((ck_ref[:, cols].astype(BF16), cvt_ref[cols, :].astype(BF16)))
        return segs

    def scores(h):
        qc = q_ref[:, h * LANES:(h + 1) * LANES]
        zero = jnp.zeros_like(qc)
        qq = jnp.concatenate([jnp.where(lane < sub, qc, zero), jnp.where(lane >= sub, qc, zero)], axis=0)
        return [_dot_nt(kk, qq) for kk, _ in segments(h)]

    def head(h, s_list):
        ov, den, _ = _attend(s_list, [vt for _, vt in segments(h)])
        r = 1.0 / den
        o = ov[:, 0:tq] * r[:, 0:tq] - ov[:, tq:2 * tq] * (lam * r[:, tq:2 * tq])
        return o * lax.rsqrt(jnp.mean(o * o, axis=0, keepdims=True) + EPS) * sg_ref[...] * (1.0 - lam_init)

    heads = _pipelined(DIFF_HEADS, scores, head)
    attn = jnp.concatenate(heads, axis=0).T.astype(BF16)
    cat = jnp.concatenate([conv_out, attn], axis=1)
    mixed = _dot(cat, wo_ref[...])
    o_ref[...] = x_ref[...] + m_ref[2:3, :] * _rms(mixed, ng_ref[1:2, :])


def _side_specs(side, steps, index):
    in_specs, out_specs, out_shape = [], [], []
    for w in side:
        rows, cols = w.shape
        n_blk = math.gcd(steps, rows // BF16_ROWS)
        per = steps // n_blk
        spec = pl.BlockSpec((rows // n_blk, cols), lambda *ids, per=per: (index(*ids) // per, 0))
        in_specs.append(spec)
        out_specs.append(spec)
        out_shape.append(jax.ShapeDtypeStruct(w.shape, BF16))
    return in_specs, out_specs, out_shape


def _mix0(u, q, k, vt, cache, x, m, norm_g, conv_w, conv_b, ln_g, ln_b, lam, subln_g, w_out,
          *, batch, seq, tq, cond_row, lam_init, side=()):
    nq = seq // tq
    n_cache = 0 if cache is None else cache[0].shape[1]
    win = tq + 2 * CONV_PAD
    const = lambda shape: pl.BlockSpec(shape, lambda b, i: (0,) * len(shape))
    in_specs = [pl.BlockSpec((seq, 2 * CONV_CH), lambda b, i: (b, 0)),
                pl.BlockSpec((tq, DIFF_WIDTH), lambda b, i: (b * nq + i, 0)),
                pl.BlockSpec((seq, DIFF_WIDTH), lambda b, i: (b, 0)),
                pl.BlockSpec((None, DIFF_WIDTH, seq), lambda b, i: (b, 0, 0))]
    args = [u, q, k, vt]
    if n_cache:
        in_specs += [pl.BlockSpec((None, n_cache, DIFF_WIDTH), lambda b, i: (b, 0, 0)),
                     pl.BlockSpec((None, DIFF_WIDTH, n_cache), lambda b, i: (b, 0, 0))]
        args += [cache[0], cache[1]]
    in_specs += [
        pl.BlockSpec((tq, D_MODEL), lambda b, i: (b * nq + i, 0)),
        pl.BlockSpec((None, N_MOD, D_MODEL), lambda b, i: (cond_row(b), 0, 0)),
        const((4, D_MODEL)), const((CONV_WIDTH, CONV_CH)), const((1, CONV_CH)), const((1, CONV_CH)),
        const((1, CONV_CH)), const((4, LANES // 2)), const((LANES, 1)), _resident((D_MODEL, D_MODEL)),
    ]
    args += [x, m, norm_g, conv_w, conv_b, ln_g, ln_b, lam, subln_g, w_out]
    side_in, side_out, side_shape = _side_specs(side, batch * nq, lambda b, i: b * nq + i)
    return pl.pallas_call(
        functools.partial(_mix0_kernel, seq=seq, tq=tq, n_cache=n_cache, lam_init=lam_init, n_side=len(side)),
        grid=(batch, nq),
        in_specs=in_specs + side_in,
        out_specs=[pl.BlockSpec((tq, D_MODEL), lambda b, i: (b * nq + i, 0))] + side_out,
        out_shape=[jax.ShapeDtypeStruct((batch * seq, D_MODEL), F32)] + side_shape,
        scratch_shapes=[
            pltpu.VMEM((seq + 2 * CONV_PAD, CONV_CH), F32), pltpu.VMEM((win, CONV_CH), F32),
            pltpu.VMEM((7, win - 8, CONV_CH), F32), pltpu.VMEM((D_MODEL, D_MODEL), BF16),
        ],
        compiler_params=_params("arbitrary", "arbitrary"),
        name="mix_conv_diff",
    )(*args, *side)


def _mix1_kernel(*refs, seq, tq, n_cache):
    if n_cache:
        (q_ref, kp_ref, kc_ref, kn_ref, vtp_ref, vtc_ref, vtn_ref, ck_ref, cvt_ref,
         x_ref, m_ref, ng_ref, sink_ref, wo_ref, o_ref) = refs
    else:
        (q_ref, kc_ref, vtc_ref, x_ref, m_ref, ng_ref, sink_ref, wo_ref, o_ref) = refs
    qi = pl.program_id(1)
    lane = lax.broadcasted_iota(jnp.int32, (tq, LANES), 1)

    def band(base, rows):
        kpos = base + lax.broadcasted_iota(jnp.int32, (rows, tq), 0)
        qpos = qi * tq + lax.broadcasted_iota(jnp.int32, (rows, tq), 1)
        ok = (jnp.abs(qpos - kpos) <= WINDOW) & (kpos >= 0) & (kpos < seq)
        return jnp.concatenate([ok] * GQA_GROUP, axis=1)

    if n_cache:
        segs = [(kp_ref, vtp_ref, band(qi * tq - WINDOW, WINDOW)),
                (kc_ref, vtc_ref, band(qi * tq, tq)),
                (kn_ref, vtn_ref, band(qi * tq + tq, WINDOW)),
                (ck_ref, cvt_ref, None)]
    else:
        segs = [(kc_ref, vtc_ref, None)]

    def scores(g):
        b = g % 2
        kcols = slice((g // 2) * LANES, (g // 2 + 1) * LANES)
        keep = (lane < HEAD_DIM) if b == 0 else (lane >= HEAD_DIM)
        qs = []
        for h in range(g * GQA_GROUP, (g + 1) * GQA_GROUP):
            qc = q_ref[:, (h // 2) * LANES:(h // 2 + 1) * LANES]
            qa = qc if h % 2 == b else pltpu.roll(qc, HEAD_DIM, 1)
            qs.append(jnp.where(keep, qa, jnp.zeros_like(qa)))
        qg = jnp.concatenate(qs, axis=0)
        s_list = []
        for k_ref, _, ok in segs:
            s = _dot_nt(k_ref[:, kcols].astype(BF16), qg)
            s_list.append(s if ok is None else jnp.where(ok, s, NEG_INF))
        return s_list

    def group(g, s_list):
        vrows = slice(g * HEAD_DIM, (g + 1) * HEAD_DIM)
        sink = jnp.concatenate([jnp.full((1, tq), sink_ref[h] * LOG2E, F32)
                                for h in range(g * GQA_GROUP, (g + 1) * GQA_GROUP)], axis=1)
        ov, den, mx = _attend(s_list, [vt_ref[vrows, :].astype(BF16) for _, vt_ref, _ in segs], floor=sink)
        ov = ov * (1.0 / (den + jnp.exp2(sink - mx)))
        return jnp.concatenate([ov[:, hh * tq:(hh + 1) * tq] for hh in range(GQA_GROUP)], axis=0)

    groups = _pipelined(GQA_KV_HEADS, scores, group)
    attn = jnp.concatenate(groups, axis=0).T.astype(BF16)
    mixed = _dot(attn, wo_ref[...])
    o_ref[...] = x_ref[...] + m_ref[2:3, :] * _rms(mixed, ng_ref[1:2, :])


def _mix1(q, k, vt, cache, x, m, norm_g, sink, w_out, *, batch, seq, tq, cond_row):
    nq = seq // tq
    n_cache = 0 if cache is None else cache[0].shape[1]
    q_spec = pl.BlockSpec((tq, D_MODEL), lambda b, i: (b * nq + i, 0))
    kc_spec = pl.BlockSpec((tq, KV_WIDTH), lambda b, i: (b * nq + i, 0))
    vtc_spec = pl.BlockSpec((None, KV_WIDTH, tq), lambda b, i: (b, 0, i))
    if n_cache:
        nw = seq // WINDOW
        per = tq // WINDOW
        prev = lambda i: jnp.maximum(i * per - 1, 0)
        nxt = lambda i: jnp.minimum(i * per + per, nw - 1)
        in_specs = [q_spec,
                    pl.BlockSpec((WINDOW, KV_WIDTH), lambda b, i: (b * nw + prev(i), 0)), kc_spec,
                    pl.BlockSpec((WINDOW, KV_WIDTH), lambda b, i: (b * nw + nxt(i), 0)),
                    pl.BlockSpec((None, KV_WIDTH, WINDOW), lambda b, i: (b, 0, prev(i))), vtc_spec,
                    pl.BlockSpec((None, KV_WIDTH, WINDOW), lambda b, i: (b, 0, nxt(i))),
                    pl.BlockSpec((None, n_cache, KV_WIDTH), lambda b, i: (b, 0, 0)),
                    pl.BlockSpec((None, KV_WIDTH, n_cache), lambda b, i: (b, 0, 0))]
        args = [q, k, k, k, vt, vt, vt, cache[0], cache[1]]
    else:
        in_specs = [q_spec, kc_spec, vtc_spec]
        args = [q, k, vt]
    in_specs += [
        pl.BlockSpec((tq, D_MODEL), lambda b, i: (b * nq + i, 0)),
        pl.BlockSpec((None, N_MOD, D_MODEL), lambda b, i: (cond_row(b), 0, 0)),
        pl.BlockSpec((4, D_MODEL), lambda b, i: (0, 0)),
        pl.BlockSpec(memory_space=pltpu.SMEM),
        pl.BlockSpec((D_MODEL, D_MODEL), lambda b, i: (0, 0)),
    ]
    args += [x, m, norm_g, sink, w_out]
    return pl.pallas_call(
        functools.partial(_mix1_kernel, seq=seq, tq=tq, n_cache=n_cache),
        grid=(batch, nq),
        in_specs=in_specs,
        out_specs=pl.BlockSpec((tq, D_MODEL), lambda b, i: (b * nq + i, 0)),
        out_shape=jax.ShapeDtypeStruct((batch * seq, D_MODEL), F32),
        compiler_params=_params("arbitrary", "arbitrary"),
        name="mix_gqa",
    )(*args)


FFN_CHUNKS = (768, 768, 768, 512)


def _ffn_kernel(*refs, n_side):
    x_ref, m_ref, ng_ref, wgu_ref, wd_ref = refs[:5]
    o_ref = refs[5 + n_side]
    _side_convert(refs[5:5 + n_side], refs[6 + n_side:])
    h = (_rms(x_ref[...], ng_ref[2:3, :]) * (1.0 + m_ref[4:5, :]) + m_ref[3:4, :]).astype(BF16)
    acc = jnp.zeros(h.shape, F32)
    c0 = 0
    for width in FFN_CHUNKS:
        gate = _dot(h, wgu_ref[:, c0:c0 + width])
        up = _dot(h, wgu_ref[:, FFN_HIDDEN + c0:FFN_HIDDEN + c0 + width])
        act = (gate * _sigmoid(gate) * up).astype(BF16)
        acc = acc + _dot(act, wd_ref[c0:c0 + width, :])
        c0 += width
    o_ref[...] = x_ref[...] + m_ref[5:6, :] * _rms(acc, ng_ref[3:4, :])


def _ffn(x, m, norm_g, w_gu, w_down, *, tm, cond_row, side=()):
    rows = x.shape[0]
    side_in, side_out, side_shape = _side_specs(side, rows // tm, lambda i: i)
    return pl.pallas_call(
        functools.partial(_ffn_kernel, n_side=len(side)),
        grid=(rows // tm,),
        in_specs=[
            pl.BlockSpec((tm, D_MODEL), lambda i: (i, 0)),
            pl.BlockSpec((None, N_MOD, D_MODEL), lambda i: (cond_row(i), 0, 0)),
            pl.BlockSpec((4, D_MODEL), lambda i: (0, 0)),
            _resident((D_MODEL, 2 * FFN_HIDDEN)),
            _resident((FFN_HIDDEN, D_MODEL)),
        ] + side_in,
        out_specs=[pl.BlockSpec((tm, D_MODEL), lambda i: (i, 0))] + side_out,
        out_shape=[jax.ShapeDtypeStruct((rows, D_MODEL), F32)] + side_shape,
        compiler_params=_params("arbitrary"),
        name="ffn",
    )(x, m, norm_g, w_gu, w_down, *side)


LAM_INIT_0 = 0.8 - 0.6 * math.exp(-0.3 * 0)


def _trunk(x, m, caches, weights, *, batch, seq, cond_of_batch):
    (norm_g0, w_in, conv_w, conv_b, ln_g, ln_b, lam, subln_g, w_out0, w_gu0, w_down0,
     norm_g1, w_qkv, sink, w_out1, w_gu1, w_down1) = weights
    m0, m1 = m
    ctx = caches is None
    rope = not ctx
    tm = 512
    tm_ffn = 512
    tq = 256
    row_of = lambda t: (lambda i: cond_of_batch((i * t) // seq))

    o0 = 2 * CONV_CH
    ko, vo = o0 + DIFF_WIDTH, o0 + 2 * DIFF_WIDTH
    outs0 = [("nat", 0, o0, F32, False, None), ("nat", o0, DIFF_WIDTH, BF16, True, Q_SCALE),
             ("nat", ko, DIFF_WIDTH, BF16, True, None), ("T", vo, DIFF_WIDTH, BF16, False, None)]
    if ctx:
        outs0 += [("T", ko, DIFF_WIDTH, F32, False, None), ("heads", vo, DIFF_WIDTH, F32, False, None)]
    res = _pre(x, m0, norm_g0, w_in, tuple(outs0), tm=tm, cond_row=row_of(tm), seq=seq, rope=rope)
    u, q, k, vt = res[:4]
    new0 = tuple(res[4:])
    x, *conv0 = _mix0(u, q, k, vt, None if ctx else caches[0], x, m0, norm_g0, conv_w, conv_b,
                      ln_g, ln_b, lam, subln_g, w_out0, batch=batch, seq=seq, tq=tq, cond_row=cond_of_batch,
                      lam_init=LAM_INIT_0, side=(w_gu0, w_down0) if ctx else ())
    if ctx:
        w_gu0, w_down0 = conv0
    x, *conv1 = _ffn(x, m0, norm_g0, w_gu0, w_down0, tm=tm_ffn, cond_row=row_of(tm_ffn),
                     side=(w_qkv, w_out1, w_gu1, w_down1) if ctx else ())
    if ctx:
        w_qkv, w_out1, w_gu1, w_down1 = conv1

    outs1 = [("nat", 0, Q_WIDTH, BF16, True, Q_SCALE), ("nat", Q_WIDTH, KV_WIDTH, BF16, True, None)]
    if ctx:
        outs1 += [("T", Q_WIDTH + KV_WIDTH, KV_WIDTH, F32, False, None), ("T", Q_WIDTH, KV_WIDTH, F32, False, None)]
    else:
        outs1 += [("T", Q_WIDTH + KV_WIDTH, KV_WIDTH, BF16, False, None)]
    res = _pre(x, m1, norm_g1, w_qkv, tuple(outs1), tm=tm, cond_row=row_of(tm), seq=seq, rope=rope)
    q, k, vt = res[:3]
    new1 = (res[3], vt) if ctx else ()
    x = _mix1(q, k, vt, None if ctx else caches[1], x, m1, norm_g1, sink, w_out1,
              batch=batch, seq=seq, tq=tq, cond_row=cond_of_batch)
    x, = _ffn(x, m1, norm_g1, w_gu1, w_down1, tm=tm_ffn, cond_row=row_of(tm_ffn))
    return x, new0 + new1, (w_gu0, w_down0, w_qkv, w_out1, w_gu1, w_down1)


def kernel(x_prompt, x_sample, cache_k0, cache_v0, cache_k1, cache_v1, c, c_ctx, l0_mod_w, l0_mod_b, l0_norm_g, l0_w_in, l0_conv_w, l0_conv_b, l0_conv_ln_g, l0_conv_ln_b, l0_lambda, l0_subln_g, l0_w_out, l0_w_gu, l0_w_down, l1_mod_w, l1_mod_b, l1_norm_g, l1_w_qkv, l1_sink, l1_w_out, l1_w_gu, l1_w_down):
    batch, seq, d = x_prompt.shape
    dec_batch, dec_seq, _ = x_sample.shape
    n_past = cache_k0.shape[1]

    cond = jnp.concatenate([c_ctx[None, :], c, jnp.zeros((MOD_ROWS - 1 - dec_batch, d), F32)], axis=0)
    m = (_modulation(cond, l0_mod_w, l0_mod_b), _modulation(cond, l1_mod_w, l1_mod_b))

    row = lambda v: v.reshape(1, -1)
    layer0 = (l0_norm_g, l0_w_in, l0_conv_w.reshape(CONV_WIDTH, CONV_CH), row(l0_conv_b),
              row(l0_conv_ln_g), row(l0_conv_ln_b), l0_lambda, l0_subln_g.reshape(LANES, 1), l0_w_out)
    weights = layer0 + (l0_w_gu, l0_w_down, l1_norm_g, l1_w_qkv, l1_sink, l1_w_out, l1_w_gu, l1_w_down)

    y_prompt, (kt0, v0, kt1, vt1), bf16_w = _trunk(x_prompt.reshape(batch * seq, d), m, None, weights,
                                                   batch=batch, seq=seq, cond_of_batch=lambda b: 0)
    w_gu0, w_down0, w_qkv, w_out1, w_gu1, w_down1 = bf16_w
    weights = layer0 + (w_gu0, w_down0, l1_norm_g, w_qkv, l1_sink, w_out1, w_gu1, w_down1)
    caches = ((cache_k0.reshape(dec_batch, n_past, -1),
               jnp.swapaxes(cache_v0.reshape(dec_batch, n_past, -1), 1, 2)),
              (cache_k1.reshape(dec_batch, n_past, -1),
               jnp.swapaxes(cache_v1.reshape(dec_batch, n_past, -1), 1, 2)))
    y_sample, _, _ = _trunk(x_sample.reshape(dec_batch * dec_seq, d), m, caches, weights,
                            batch=dec_batch, seq=dec_seq, cond_of_batch=lambda b: 1 + b)
    new_k0 = kt0.reshape(batch, DIFF_HEADS, 2, HEAD_DIM, seq).transpose(0, 4, 1, 2, 3)
    new_v0 = v0.reshape(batch, seq, DIFF_HEADS, 2 * HEAD_DIM)
    new_k1 = kt1.reshape(batch, GQA_KV_HEADS, HEAD_DIM, seq).transpose(0, 3, 1, 2)
    new_v1 = vt1.reshape(batch, GQA_KV_HEADS, HEAD_DIM, seq).transpose(0, 3, 1, 2)
    return (y_prompt.reshape(batch, seq, d), y_sample.reshape(dec_batch, dec_seq, d),
            new_k0, new_v0, new_k1, new_v1)
```

```python
import functools
import math

import numpy as np
import jax
import jax.numpy as jnp
from jax import lax
from jax.experimental import pallas as pl
from jax.experimental.pallas import tpu as pltpu

D_MODEL = 1024
GRID_W = 64
ROPE_THETA = 10000.0
EPS = 1e-6
NEG_INF = -1e30
LOG2E = math.log2(math.e)
CONV_CH = 512
CONV_WIDTH = 31
CONV_PAD = 16
DIFF_HEADS = 4
DIFF_WIDTH = 512
HEAD_DIM = 64
GQA_HEADS = 16
GQA_KV_HEADS = 4
GQA_GROUP = 4
Q_WIDTH = GQA_HEADS * HEAD_DIM
KV_WIDTH = GQA_KV_HEADS * HEAD_DIM
Q_SCALE = LOG2E * HEAD_DIM ** -0.5
WINDOW = 128
FFN_HIDDEN = 2816
N_MOD = 6
LANES = 128
BF16_ROWS = 16
MOD_ROWS = 8
VMEM_LIMIT = 56 * 1024 * 1024

BF16 = jnp.bfloat16
F32 = jnp.float32


def _sigmoid(x):
    return 1.0 / (1.0 + jnp.exp(-x))


def _rms(x, g):
    return x * lax.rsqrt(jnp.mean(x * x, axis=-1, keepdims=True) + EPS) * g


def _dot(a, b):
    return jnp.dot(a, b, preferred_element_type=F32)


def _dot_nt(a, b):
    return lax.dot_general(a, b, (((1,), (1,)), ((), ())), preferred_element_type=F32)


def _params(*sem):
    return pltpu.CompilerParams(dimension_semantics=sem, vmem_limit_bytes=VMEM_LIMIT)


def _resident(shape):
    return pl.BlockSpec(shape, lambda *_: (0,) * len(shape), pipeline_mode=pl.Buffered(1))


def _mod_kernel(c_ref, w_ref, b_ref, o_ref):
    c = c_ref[...]
    s = (c * _sigmoid(c)).astype(BF16)
    o_ref[...] = _dot(s, w_ref[...].astype(BF16)) + b_ref[...]


def _modulation(cond, mod_w, mod_b):
    n = mod_w.shape[1]
    tn = 1536
    out = pl.pallas_call(
        _mod_kernel,
        grid=(n // tn,),
        in_specs=[
            pl.BlockSpec((MOD_ROWS, D_MODEL), lambda j: (0, 0)),
            pl.BlockSpec((D_MODEL, tn), lambda j: (0, j)),
            pl.BlockSpec((1, tn), lambda j: (0, j)),
        ],
        out_specs=pl.BlockSpec((MOD_ROWS, tn), lambda j: (0, j)),
        out_shape=jax.ShapeDtypeStruct((MOD_ROWS, n), F32),
        compiler_params=_params("arbitrary"),
        name="modulation",
    )(cond, mod_w, mod_b.reshape(1, n))
    return out.reshape(MOD_ROWS, N_MOD, D_MODEL)


def _rope_tables(seq):
    t = np.arange(seq)
    rows, cols = t // GRID_W, t % GRID_W
    half = HEAD_DIM // 2
    inv = 1.0 / (ROPE_THETA ** (np.arange(0, half, 2, dtype=np.float64) / half))
    ar = rows[:, None] * inv[None, :]
    ac = cols[:, None] * inv[None, :]
    cos = np.concatenate([np.cos(ar), np.cos(ar), np.cos(ac), np.cos(ac)], axis=1)
    sin = np.concatenate([-np.sin(ar), np.sin(ar), -np.sin(ac), np.sin(ac)], axis=1)
    reps = LANES // HEAD_DIM
    return (jnp.asarray(np.tile(cos, (1, reps)), F32), jnp.asarray(np.tile(sin, (1, reps)), F32))


def _rope(x, cos, sin):
    quarter = HEAD_DIM // 4
    lane = lax.broadcasted_iota(jnp.int32, x.shape, 1)
    lo = (lane % (2 * quarter)) < quarter
    partner = jnp.where(lo, pltpu.roll(x, LANES - quarter, 1), pltpu.roll(x, quarter, 1))
    return x * cos + partner * sin


def _pre_kernel(*refs, outs, rope, sblk):
    n_in = 6 if rope else 4
    x_ref, m_ref, g_ref, w_ref = refs[:4]
    o_refs = refs[n_in:]
    x = x_ref[...]
    tm = x.shape[0]
    h = _rms(x, g_ref[0:1, :]) * (1.0 + m_ref[1:2, :]) + m_ref[0:1, :]
    p = _dot(h.astype(BF16), w_ref[...])
    for (kind, start, width, _, roped, scale), o_ref in zip(outs, o_refs):
        if kind == "nat":
            for c in range(width // LANES):
                xc = p[:, start + c * LANES:start + (c + 1) * LANES]
                if rope and roped:
                    xc = _rope(xc, refs[4][...], refs[5][...])
                if scale is not None:
                    xc = xc * scale
                o_ref[:, c * LANES:(c + 1) * LANES] = xc.astype(o_ref.dtype)
        elif kind == "T":
            for s in range(tm // sblk):
                o_ref[s] = p[s * sblk:(s + 1) * sblk, start:start + width].T.astype(o_ref.dtype)
        else:
            for hd in range(width // LANES):
                o_ref[:, hd, :] = p[:, start + hd * LANES:start + (hd + 1) * LANES]


def _pre(x, m, norm_g, w, outs, *, tm, cond_row, seq, rope):
    rows = x.shape[0]
    n = w.shape[1]
    sblk = min(seq, tm)
    per_seq = seq // sblk
    in_specs = [
        pl.BlockSpec((tm, D_MODEL), lambda i: (i, 0)),
        pl.BlockSpec((None, N_MOD, D_MODEL), lambda i: (cond_row(i), 0, 0)),
        pl.BlockSpec((4, D_MODEL), lambda i: (0, 0)),
        _resident((D_MODEL, n)),
    ]
    args = [x, m, norm_g, w]
    if rope:
        cos, sin = _rope_tables(seq)
        nblk = seq // tm
        in_specs += [pl.BlockSpec((tm, LANES), lambda i: (i % nblk, 0))] * 2
        args += [cos, sin]
    out_specs, out_shape = [], []
    for (kind, _, wd, dt, _, _) in outs:
        if kind == "nat":
            out_specs.append(pl.BlockSpec((tm, wd), lambda i: (i, 0)))
            out_shape.append(jax.ShapeDtypeStruct((rows, wd), dt))
        elif kind == "T":
            out_specs.append(pl.BlockSpec((tm // sblk, wd, sblk), lambda i: (i // per_seq, 0, i % per_seq)))
            out_shape.append(jax.ShapeDtypeStruct((rows // seq, wd, seq), dt))
        else:
            out_specs.append(pl.BlockSpec((tm, wd // LANES, LANES), lambda i: (i, 0, 0)))
            out_shape.append(jax.ShapeDtypeStruct((rows, wd // LANES, LANES), dt))
    return pl.pallas_call(
        functools.partial(_pre_kernel, outs=outs, rope=rope, sblk=sblk),
        grid=(rows // tm,),
        in_specs=in_specs,
        out_specs=out_specs,
        out_shape=out_shape,
        compiler_params=_params("arbitrary"),
        name="pre_rope" if rope else "pre",
    )(*args)


def _attend(s_list, vt_list, floor=None):
    mx = None
    for s in s_list:
        smax = jnp.max(s, axis=0, keepdims=True)
        mx = smax if mx is None else jnp.maximum(mx, smax)
    if floor is not None:
        mx = jnp.maximum(mx, floor)
    d = vt_list[0].shape[0]
    ov = None
    for s, vt in zip(s_list, vt_list):
        ones = jnp.ones((BF16_ROWS, vt.shape[1]), BF16)
        pv = _dot(jnp.concatenate([vt, ones], axis=0), jnp.exp2(s - mx).astype(BF16))
        ov = pv if ov is None else ov + pv
    return ov[0:d, :], ov[d:d + 1, :], mx


def _pipelined(n, first, second):
    out = []
    nxt = first(0)
    for u in range(n):
        cur = nxt
        if u + 1 < n:
            nxt = first(u + 1)
        out.append(second(u, cur))
    return out


def _side_convert(src_refs, dst_refs):
    for src, dst in zip(src_refs, dst_refs):
        dst[...] = src[...].astype(BF16)


def _mix0_kernel(*refs, seq, tq, n_cache, lam_init, n_side):
    n_in = 16 if n_cache else 14
    u_ref, q_ref, k_ref, vt_ref = refs[:4]
    ck_ref, cvt_ref = refs[4:6] if n_cache else (None, None)
    x_ref, m_ref, ng_ref, cw_ref, cb_ref, lg_ref, lb_ref, lam_ref, sg_ref, wo_ref = refs[n_in - 10:n_in]
    o_ref = refs[n_in + n_side]
    z_s, zw_s, zsh_s = refs[n_in + 2 * n_side + 1:]
    _side_convert(refs[n_in:n_in + n_side], refs[n_in + n_side + 1:n_in + 2 * n_side + 1])
    qi = pl.program_id(1)

    @pl.when(qi == 0)
    def _():
        z_s[0:CONV_PAD, :] = jnp.zeros((CONV_PAD, CONV_CH), F32)
        z_s[CONV_PAD + seq:2 * CONV_PAD + seq, :] = jnp.zeros((CONV_PAD, CONV_CH), F32)
        z_s[CONV_PAD:CONV_PAD + seq, :] = u_ref[:, 0:CONV_CH] * _sigmoid(u_ref[:, CONV_CH:2 * CONV_CH])

    r0 = pl.multiple_of(qi * tq, tq)
    win = tq + 2 * CONV_PAD
    zw_s[...] = z_s[pl.ds(r0, win), :]
    span = win - 8
    for b in range(1, 8):
        zsh_s[b - 1, 0:span, :] = zw_s[b:b + span, :]
    chunk = 64
    off0 = CONV_PAD - CONV_WIDTH // 2
    conv_rows = []
    for c0 in range(0, tq, chunk):
        acc = jnp.zeros((chunk, CONV_CH), F32)
        for j in range(CONV_WIDTH):
            off = j + off0
            a, b = off // 8, off % 8
            if b == 0:
                zz = zw_s[c0 + 8 * a:c0 + 8 * a + chunk, :]
            else:
                zz = zsh_s[b - 1, c0 + 8 * a:c0 + 8 * a + chunk, :]
            acc = acc + zz * cw_ref[j:j + 1, :]
        acc = acc + cb_ref[...]
        mu = jnp.mean(acc, axis=-1, keepdims=True)
        xc = acc - mu
        y = xc * lax.rsqrt(jnp.mean(xc * xc, axis=-1, keepdims=True) + EPS) * lg_ref[...] + lb_ref[...]
        conv_rows.append((y * _sigmoid(y)).astype(BF16))
    conv_out = jnp.concatenate(conv_rows, axis=0)

    la = jnp.sum(lam_ref[0:1, :] * lam_ref[1:2, :], axis=-1, keepdims=True)
    lb = jnp.sum(lam_ref[2:3, :] * lam_ref[3:4, :], axis=-1, keepdims=True)
    lam = jnp.exp(la) - jnp.exp(lb) + lam_init
    sub = LANES // 2
    lane = lax.broadcasted_iota(jnp.int32, (tq, LANES), 1)

    def segments(h):
        cols = slice(h * LANES, (h + 1) * LANES)
        segs = [(k_ref[:, cols], vt_ref[cols, :])]
        if n_cache:
            segs.append((ck_ref[:, cols].astype(BF16), cvt_ref[cols, :].astype(BF16)))
        return segs

    def scores(h):
        qc = q_ref[:, h * LANES:(h + 1) * LANES]
        zero = jnp.zeros_like(qc)
        qq = jnp.concatenate([jnp.where(lane < sub, qc, zero), jnp.where(lane >= sub, qc, zero)], axis=0)
        return [_dot_nt(kk, qq) for kk, _ in segments(h)]

    def head(h, s_list):
        ov, den, _ = _attend(s_list, [vt for _, vt in segments(h)])
        r = 1.0 / den
        o = ov[:, 0:tq] * r[:, 0:tq] - ov[:, tq:2 * tq] * (lam * r[:, tq:2 * tq])
        return o * lax.rsqrt(jnp.mean(o * o, axis=0, keepdims=True) + EPS) * sg_ref[...] * (1.0 - lam_init)

    heads = _pipelined(DIFF_HEADS, scores, head)
    attn = jnp.concatenate(heads, axis=0).T.astype(BF16)
    cat = jnp.concatenate([conv_out, attn], axis=1)
    mixed = _dot(cat, wo_ref[...])
    o_ref[...] = x_ref[...] + m_ref[2:3, :] * _rms(mixed, ng_ref[1:2, :])


def _side_specs(side, steps, index):
    in_specs, out_specs, out_shape = [], [], []
    for w in side:
        rows, cols = w.shape
        n_blk = math.gcd(steps, rows // BF16_ROWS)
        per = steps // n_blk
        spec = pl.BlockSpec((rows // n_blk, cols), lambda *ids, per=per: (index(*ids) // per, 0))
        in_specs.append(spec)
        out_specs.append(spec)
        out_shape.append(jax.ShapeDtypeStruct(w.shape, BF16))
    return in_specs, out_specs, out_shape


def _mix0(u, q, k, vt, cache, x, m, norm_g, conv_w, conv_b, ln_g, ln_b, lam, subln_g, w_out,
          *, batch, seq, tq, cond_row, lam_init, side=()):
    nq = seq // tq
    n_cache = 0 if cache is None else cache[0].shape[1]
    win = tq + 2 * CONV_PAD
    const = lambda shape: pl.BlockSpec(shape, lambda b, i: (0,) * len(shape))
    in_specs = [pl.BlockSpec((seq, 2 * CONV_CH), lambda b, i: (b, 0)),
                pl.BlockSpec((tq, DIFF_WIDTH), lambda b, i: (b * nq + i, 0)),
                pl.BlockSpec((seq, DIFF_WIDTH), lambda b, i: (b, 0)),
                pl.BlockSpec((None, DIFF_WIDTH, seq), lambda b, i: (b, 0, 0))]
    args = [u, q, k, vt]
    if n_cache:
        in_specs += [pl.BlockSpec((None, n_cache, DIFF_WIDTH), lambda b, i: (b, 0, 0)),
                     pl.BlockSpec((None, DIFF_WIDTH, n_cache), lambda b, i: (b, 0, 0))]
        args += [cache[0], cache[1]]
    in_specs += [
        pl.BlockSpec((tq, D_MODEL), lambda b, i: (b * nq + i, 0)),
        pl.BlockSpec((None, N_MOD, D_MODEL), lambda b, i: (cond_row(b), 0, 0)),
        const((4, D_MODEL)), const((CONV_WIDTH, CONV_CH)), const((1, CONV_CH)), const((1, CONV_CH)),
        const((1, CONV_CH)), const((4, LANES // 2)), const((LANES, 1)), _resident((D_MODEL, D_MODEL)),
    ]
    args += [x, m, norm_g, conv_w, conv_b, ln_g, ln_b, lam, subln_g, w_out]
    side_in, side_out, side_shape = _side_specs(side, batch * nq, lambda b, i: b * nq + i)
    return pl.pallas_call(
        functools.partial(_mix0_kernel, seq=seq, tq=tq, n_cache=n_cache, lam_init=lam_init, n_side=len(side)),
        grid=(batch, nq),
        in_specs=in_specs + side_in,
        out_specs=[pl.BlockSpec((tq, D_MODEL), lambda b, i: (b * nq + i, 0))] + side_out,
        out_shape=[jax.ShapeDtypeStruct((batch * seq, D_MODEL), F32)] + side_shape,
        scratch_shapes=[
            pltpu.VMEM((seq + 2 * CONV_PAD, CONV_CH), F32), pltpu.VMEM((win, CONV_CH), F32),
            pltpu.VMEM((7, win - 8, CONV_CH), F32),
        ],
        compiler_params=_params("arbitrary", "arbitrary"),
        name="mix_conv_diff",
    )(*args, *side)


def _mix1_kernel(*refs, seq, tq, n_cache):
    if n_cache:
        (q_ref, kp_ref, kc_ref, kn_ref, vtp_ref, vtc_ref, vtn_ref, ck_ref, cvt_ref,
         x_ref, m_ref, ng_ref, sink_ref, wo_ref, o_ref) = refs
    else:
        (q_ref, kc_ref, vtc_ref, x_ref, m_ref, ng_ref, sink_ref, wo_ref, o_ref) = refs
    qi = pl.program_id(1)
    lane = lax.broadcasted_iota(jnp.int32, (tq, LANES), 1)

    def band(base, rows):
        kpos = base + lax.broadcasted_iota(jnp.int32, (rows, tq), 0)
        qpos = qi * tq + lax.broadcasted_iota(jnp.int32, (rows, tq), 1)
        ok = (jnp.abs(qpos - kpos) <= WINDOW) & (kpos >= 0) & (kpos < seq)
        return jnp.concatenate([ok] * GQA_GROUP, axis=1)

    if n_cache:
        segs = [(kp_ref, vtp_ref, band(qi * tq - WINDOW, WINDOW)),
                (kc_ref, vtc_ref, band(qi * tq, tq)),
                (kn_ref, vtn_ref, band(qi * tq + tq, WINDOW)),
                (ck_ref, cvt_ref, None)]
    else:
        segs = [(kc_ref, vtc_ref, None)]

    def scores(g):
        b = g % 2
        kcols = slice((g // 2) * LANES, (g // 2 + 1) * LANES)
        keep = (lane < HEAD_DIM) if b == 0 else (lane >= HEAD_DIM)
        qs = []
        for h in range(g * GQA_GROUP, (g + 1) * GQA_GROUP):
            qc = q_ref[:, (h // 2) * LANES:(h // 2 + 1) * LANES]
            qa = qc if h % 2 == b else pltpu.roll(qc, HEAD_DIM, 1)
            qs.append(jnp.where(keep, qa, jnp.zeros_like(qa)))
        qg = jnp.concatenate(qs, axis=0)
        s_list = []
        for k_ref, _, ok in segs:
            s = _dot_nt(k_ref[:, kcols].astype(BF16), qg)
            s_list.append(s if ok is None else jnp.where(ok, s, NEG_INF))
        return s_list

    def group(g, s_list):
        vrows = slice(g * HEAD_DIM, (g + 1) * HEAD_DIM)
        sink = jnp.concatenate([jnp.full((1, tq), sink_ref[h] * LOG2E, F32)
                                for h in range(g * GQA_GROUP, (g + 1) * GQA_GROUP)], axis=1)
        ov, den, mx = _attend(s_list, [vt_ref[vrows, :].astype(BF16) for _, vt_ref, _ in segs], floor=sink)
        ov = ov * (1.0 / (den + jnp.exp2(sink - mx)))
        return jnp.concatenate([ov[:, hh * tq:(hh + 1) * tq] for hh in range(GQA_GROUP)], axis=0)

    groups = _pipelined(GQA_KV_HEADS, scores, group)
    attn = jnp.concatenate(groups, axis=0).T.astype(BF16)
    mixed = _dot(attn, wo_ref[...])
    o_ref[...] = x_ref[...] + m_ref[2:3, :] * _rms(mixed, ng_ref[1:2, :])


def _mix1(q, k, vt, cache, x, m, norm_g, sink, w_out, *, batch, seq, tq, cond_row):
    nq = seq // tq
    n_cache = 0 if cache is None else cache[0].shape[1]
    q_spec = pl.BlockSpec((tq, D_MODEL), lambda b, i: (b * nq + i, 0))
    kc_spec = pl.BlockSpec((tq, KV_WIDTH), lambda b, i: (b * nq + i, 0))
    vtc_spec = pl.BlockSpec((None, KV_WIDTH, tq), lambda b, i: (b, 0, i))
    if n_cache:
        nw = seq // WINDOW
        per = tq // WINDOW
        prev = lambda i: jnp.maximum(i * per - 1, 0)
        nxt = lambda i: jnp.minimum(i * per + per, nw - 1)
        in_specs = [q_spec,
                    pl.BlockSpec((WINDOW, KV_WIDTH), lambda b, i: (b * nw + prev(i), 0)), kc_spec,
                    pl.BlockSpec((WINDOW, KV_WIDTH), lambda b, i: (b * nw + nxt(i), 0)),
                    pl.BlockSpec((None, KV_WIDTH, WINDOW), lambda b, i: (b, 0, prev(i))), vtc_spec,
                    pl.BlockSpec((None, KV_WIDTH, WINDOW), lambda b, i: (b, 0, nxt(i))),
                    pl.BlockSpec((None, n_cache, KV_WIDTH), lambda b, i: (b, 0, 0)),
                    pl.BlockSpec((None, KV_WIDTH, n_cache), lambda b, i: (b, 0, 0))]
        args = [q, k, k, k, vt, vt, vt, cache[0], cache[1]]
    else:
        in_specs = [q_spec, kc_spec, vtc_spec]
        args = [q, k, vt]
    in_specs += [
        pl.BlockSpec((tq, D_MODEL), lambda b, i: (b * nq + i, 0)),
        pl.BlockSpec((None, N_MOD, D_MODEL), lambda b, i: (cond_row(b), 0, 0)),
        pl.BlockSpec((4, D_MODEL), lambda b, i: (0, 0)),
        pl.BlockSpec(memory_space=pltpu.SMEM),
        pl.BlockSpec((D_MODEL, D_MODEL), lambda b, i: (0, 0)),
    ]
    args += [x, m, norm_g, sink, w_out]
    return pl.pallas_call(
        functools.partial(_mix1_kernel, seq=seq, tq=tq, n_cache=n_cache),
        grid=(batch, nq),
        in_specs=in_specs,
        out_specs=pl.BlockSpec((tq, D_MODEL), lambda b, i: (b * nq + i, 0)),
        out_shape=jax.ShapeDtypeStruct((batch * seq, D_MODEL), F32),
        compiler_params=_params("arbitrary", "arbitrary"),
        name="mix_gqa",
    )(*args)


FFN_CHUNKS = (768, 768, 768, 512)


def _ffn_kernel(*refs, n_side):
    x_ref, m_ref, ng_ref, wgu_ref, wd_ref = refs[:5]
    o_ref = refs[5 + n_side]
    _side_convert(refs[5:5 + n_side], refs[6 + n_side:])
    h = (_rms(x_ref[...], ng_ref[2:3, :]) * (1.0 + m_ref[4:5, :]) + m_ref[3:4, :]).astype(BF16)
    acc = jnp.zeros(h.shape, F32)
    c0 = 0
    for width in FFN_CHUNKS:
        gate = _dot(h, wgu_ref[:, c0:c0 + width])
        up = _dot(h, wgu_ref[:, FFN_HIDDEN + c0:FFN_HIDDEN + c0 + width])
        act = (gate * _sigmoid(gate) * up).astype(BF16)
        acc = acc + _dot(act, wd_ref[c0:c0 + width, :])
        c0 += width
    o_ref[...] = x_ref[...] + m_ref[5:6, :] * _rms(acc, ng_ref[3:4, :])


def _ffn(x, m, norm_g, w_gu, w_down, *, tm, cond_row, side=()):
    rows = x.shape[0]
    side_in, side_out, side_shape = _side_specs(side, rows // tm, lambda i: i)
    return pl.pallas_call(
        functools.partial(_ffn_kernel, n_side=len(side)),
        grid=(rows // tm,),
        in_specs=[
            pl.BlockSpec((tm, D_MODEL), lambda i: (i, 0)),
            pl.BlockSpec((None, N_MOD, D_MODEL), lambda i: (cond_row(i), 0, 0)),
            pl.BlockSpec((4, D_MODEL), lambda i: (0, 0)),
            _resident((D_MODEL, 2 * FFN_HIDDEN)),
            _resident((FFN_HIDDEN, D_MODEL)),
        ] + side_in,
        out_specs=[pl.BlockSpec((tm, D_MODEL), lambda i: (i, 0))] + side_out,
        out_shape=[jax.ShapeDtypeStruct((rows, D_MODEL), F32)] + side_shape,
        compiler_params=_params("arbitrary"),
        name="ffn",
    )(x, m, norm_g, w_gu, w_down, *side)


LAM_INIT_0 = 0.8 - 0.6 * math.exp(-0.3 * 0)


def _trunk(x, m, caches, weights, *, batch, seq, cond_of_batch):
    (norm_g0, w_in, conv_w, conv_b, ln_g, ln_b, lam, subln_g, w_out0, w_gu0, w_down0,
     norm_g1, w_qkv, sink, w_out1, w_gu1, w_down1) = weights
    m0, m1 = m
    ctx = caches is None
    rope = not ctx
    tm = 512
    tm_ffn = 512
    tq = 256
    row_of = lambda t: (lambda i: cond_of_batch((i * t) // seq))

    o0 = 2 * CONV_CH
    ko, vo = o0 + DIFF_WIDTH, o0 + 2 * DIFF_WIDTH
    outs0 = [("nat", 0, o0, F32, False, None), ("nat", o0, DIFF_WIDTH, BF16, True, Q_SCALE),
             ("nat", ko, DIFF_WIDTH, BF16, True, None), ("T", vo, DIFF_WIDTH, BF16, False, None)]
    if ctx:
        outs0 += [("T", ko, DIFF_WIDTH, F32, False, None), ("heads", vo, DIFF_WIDTH, F32, False, None)]
    res = _pre(x, m0, norm_g0, w_in, tuple(outs0), tm=tm, cond_row=row_of(tm), seq=seq, rope=rope)
    u, q, k, vt = res[:4]
    new0 = tuple(res[4:])
    x, *conv0 = _mix0(u, q, k, vt, None if ctx else caches[0], x, m0, norm_g0, conv_w, conv_b,
                      ln_g, ln_b, lam, subln_g, w_out0, batch=batch, seq=seq, tq=tq, cond_row=cond_of_batch,
                      lam_init=LAM_INIT_0, side=(w_gu0, w_down0) if ctx else ())
    if ctx:
        w_gu0, w_down0 = conv0
    x, *conv1 = _ffn(x, m0, norm_g0, w_gu0, w_down0, tm=tm_ffn, cond_row=row_of(tm_ffn),
                     side=(w_qkv, w_out1, w_gu1, w_down1) if ctx else ())
    if ctx:
        w_qkv, w_out1, w_gu1, w_down1 = conv1

    outs1 = [("nat", 0, Q_WIDTH, BF16, True, Q_SCALE), ("nat", Q_WIDTH, KV_WIDTH, BF16, True, None)]
    if ctx:
        outs1 += [("T", Q_WIDTH + KV_WIDTH, KV_WIDTH, F32, False, None), ("T", Q_WIDTH, KV_WIDTH, F32, False, None)]
    else:
        outs1 += [("T", Q_WIDTH + KV_WIDTH, KV_WIDTH, BF16, False, None)]
    res = _pre(x, m1, norm_g1, w_qkv, tuple(outs1), tm=tm, cond_row=row_of(tm), seq=seq, rope=rope)
    q, k, vt = res[:3]
    new1 = (res[3], vt) if ctx else ()
    x = _mix1(q, k, vt, None if ctx else caches[1], x, m1, norm_g1, sink, w_out1,
              batch=batch, seq=seq, tq=tq, cond_row=cond_of_batch)
    x, = _ffn(x, m1, norm_g1, w_gu1, w_down1, tm=tm_ffn, cond_row=row_of(tm_ffn))
    return x, new0 + new1, (w_gu0, w_down0, w_qkv, w_out1, w_gu1, w_down1)


def kernel(x_prompt, x_sample, cache_k0, cache_v0, cache_k1, cache_v1, c, c_ctx, l0_mod_w, l0_mod_b, l0_norm_g, l0_w_in, l0_conv_w, l0_conv_b, l0_conv_ln_g, l0_conv_ln_b, l0_lambda, l0_subln_g, l0_w_out, l0_w_gu, l0_w_down, l1_mod_w, l1_mod_b, l1_norm_g, l1_w_qkv, l1_sink, l1_w_out, l1_w_gu, l1_w_down):
    batch, seq, d = x_prompt.shape
    dec_batch, dec_seq, _ = x_sample.shape
    n_past = cache_k0.shape[1]

    cond = jnp.concatenate([c_ctx[None, :], c, jnp.zeros((MOD_ROWS - 1 - dec_batch, d), F32)], axis=0)
    m = (_modulation(cond, l0_mod_w, l0_mod_b), _modulation(cond, l1_mod_w, l1_mod_b))

    row = lambda v: v.reshape(1, -1)
    layer0 = (l0_norm_g, l0_w_in.astype(BF16), l0_conv_w.reshape(CONV_WIDTH, CONV_CH), row(l0_conv_b),
              row(l0_conv_ln_g), row(l0_conv_ln_b), l0_lambda, l0_subln_g.reshape(LANES, 1),
              l0_w_out.astype(BF16))
    weights = layer0 + (l0_w_gu, l0_w_down, l1_norm_g, l1_w_qkv, l1_sink, l1_w_out, l1_w_gu, l1_w_down)

    y_prompt, (kt0, v0, kt1, vt1), bf16_w = _trunk(x_prompt.reshape(batch * seq, d), m, None, weights,
                                                   batch=batch, seq=seq, cond_of_batch=lambda b: 0)
    w_gu0, w_down0, w_qkv, w_out1, w_gu1, w_down1 = bf16_w
    weights = layer0 + (w_gu0, w_down0, l1_norm_g, w_qkv, l1_sink, w_out1, w_gu1, w_down1)
    caches = ((cache_k0.reshape(dec_batch, n_past, -1),
               jnp.swapaxes(cache_v0.reshape(dec_batch, n_past, -1), 1, 2)),
              (cache_k1.reshape(dec_batch, n_past, -1),
               jnp.swapaxes(cache_v1.reshape(dec_batch, n_past, -1), 1, 2)))
    y_sample, _, _ = _trunk(x_sample.reshape(dec_batch * dec_seq, d), m, caches, weights,
                            batch=dec_batch, seq=dec_seq, cond_of_batch=lambda b: 1 + b)
    new_k0 = kt0.reshape(batch, DIFF_HEADS, 2, HEAD_DIM, seq).transpose(0, 4, 1, 2, 3)
    new_v0 = v0.reshape(batch, seq, DIFF_HEADS, 2 * HEAD_DIM)
    new_k1 = kt1.reshape(batch, GQA_KV_HEADS, HEAD_DIM, seq).transpose(0, 3, 1, 2)
    new_v1 = vt1.reshape(batch, GQA_KV_HEADS, HEAD_DIM, seq).transpose(0, 3, 1, 2)
    return (y_prompt.reshape(batch, seq, d), y_sample.reshape(dec_batch, dec_seq, d),
            new_k0, new_v0, new_k1, new_v1)
```

```python
import functools
import math

import numpy as np
import jax
import jax.numpy as jnp
from jax import lax
from jax.experimental import pallas as pl
from jax.experimental.pallas import tpu as pltpu

D_MODEL = 1024
GRID_W = 64
ROPE_THETA = 10000.0
EPS = 1e-6
NEG_INF = -1e30
LOG2E = math.log2(math.e)
CONV_CH = 512
CONV_WIDTH = 31
CONV_PAD = 16
DIFF_HEADS = 4
DIFF_WIDTH = 512
HEAD_DIM = 64
GQA_HEADS = 16
GQA_KV_HEADS = 4
GQA_GROUP = 4
Q_WIDTH = GQA_HEADS * HEAD_DIM
KV_WIDTH = GQA_KV_HEADS * HEAD_DIM
Q_SCALE = LOG2E * HEAD_DIM ** -0.5
WINDOW = 128
FFN_HIDDEN = 2816
N_MOD = 6
LANES = 128
BF16_ROWS = 16
MOD_ROWS = 8
VMEM_LIMIT = 56 * 1024 * 1024

BF16 = jnp.bfloat16
F32 = jnp.float32


def _sigmoid(x):
    return 1.0 / (1.0 + jnp.exp(-x))


def _rms(x, g):
    return x * lax.rsqrt(jnp.mean(x * x, axis=-1, keepdims=True) + EPS) * g


def _dot(a, b):
    return jnp.dot(a, b, preferred_element_type=F32)


def _dot_nt(a, b):
    return lax.dot_general(a, b, (((1,), (1,)), ((), ())), preferred_element_type=F32)


def _params(*sem):
    return pltpu.CompilerParams(dimension_semantics=sem, vmem_limit_bytes=VMEM_LIMIT)


def _resident(shape):
    return pl.BlockSpec(shape, lambda *_: (0,) * len(shape), pipeline_mode=pl.Buffered(1))


def _mod_kernel(c_ref, w_ref, b_ref, o_ref):
    c = c_ref[...]
    s = (c * _sigmoid(c)).astype(BF16)
    o_ref[...] = _dot(s, w_ref[...].astype(BF16)) + b_ref[...]


def _modulation(cond, mod_w, mod_b):
    n = mod_w.shape[1]
    tn = 1536
    out = pl.pallas_call(
        _mod_kernel,
        grid=(n // tn,),
        in_specs=[
            pl.BlockSpec((MOD_ROWS, D_MODEL), lambda j: (0, 0)),
            pl.BlockSpec((D_MODEL, tn), lambda j: (0, j)),
            pl.BlockSpec((1, tn), lambda j: (0, j)),
        ],
        out_specs=pl.BlockSpec((MOD_ROWS, tn), lambda j: (0, j)),
        out_shape=jax.ShapeDtypeStruct((MOD_ROWS, n), F32),
        compiler_params=_params("arbitrary"),
        name="modulation",
    )(cond, mod_w, mod_b.reshape(1, n))
    return out.reshape(MOD_ROWS, N_MOD, D_MODEL)


def _rope_tables(seq):
    t = np.arange(seq)
    rows, cols = t // GRID_W, t % GRID_W
    half = HEAD_DIM // 2
    inv = 1.0 / (ROPE_THETA ** (np.arange(0, half, 2, dtype=np.float64) / half))
    ar = rows[:, None] * inv[None, :]
    ac = cols[:, None] * inv[None, :]
    cos = np.concatenate([np.cos(ar), np.cos(ar), np.cos(ac), np.cos(ac)], axis=1)
    sin = np.concatenate([-np.sin(ar), np.sin(ar), -np.sin(ac), np.sin(ac)], axis=1)
    reps = LANES // HEAD_DIM
    return (jnp.asarray(np.tile(cos, (1, reps)), F32), jnp.asarray(np.tile(sin, (1, reps)), F32))


def _rope(x, cos, sin):
    quarter = HEAD_DIM // 4
    lane = lax.broadcasted_iota(jnp.int32, x.shape, 1)
    lo = (lane % (2 * quarter)) < quarter
    partner = jnp.where(lo, pltpu.roll(x, LANES - quarter, 1), pltpu.roll(x, quarter, 1))
    return x * cos + partner * sin


def _pre_kernel(*refs, outs, rope, sblk):
    n_in = 6 if rope else 4
    x_ref, m_ref, g_ref, w_ref = refs[:4]
    o_refs = refs[n_in:]
    x = x_ref[...]
    tm = x.shape[0]
    h = _rms(x, g_ref[0:1, :]) * (1.0 + m_ref[1:2, :]) + m_ref[0:1, :]
    p = _dot(h.astype(BF16), w_ref[...])
    for (kind, start, width, _, roped, scale), o_ref in zip(outs, o_refs):
        if kind == "nat":
            for c in range(width // LANES):
                xc = p[:, start + c * LANES:start + (c + 1) * LANES]
                if rope and roped:
                    xc = _rope(xc, refs[4][...], refs[5][...])
                if scale is not None:
                    xc = xc * scale
                o_ref[:, c * LANES:(c + 1) * LANES] = xc.astype(o_ref.dtype)
        elif kind == "T":
            for s in range(tm // sblk):
                o_ref[s] = p[s * sblk:(s + 1) * sblk, start:start + width].T.astype(o_ref.dtype)
        else:
            for hd in range(width // LANES):
                o_ref[:, hd, :] = p[:, start + hd * LANES:start + (hd + 1) * LANES]


def _pre(x, m, norm_g, w, outs, *, tm, cond_row, seq, rope):
    rows = x.shape[0]
    n = w.shape[1]
    sblk = min(seq, tm)
    per_seq = seq // sblk
    in_specs = [
        pl.BlockSpec((tm, D_MODEL), lambda i: (i, 0)),
        pl.BlockSpec((None, N_MOD, D_MODEL), lambda i: (cond_row(i), 0, 0)),
        pl.BlockSpec((4, D_MODEL), lambda i: (0, 0)),
        _resident((D_MODEL, n)),
    ]
    args = [x, m, norm_g, w]
    if rope:
        cos, sin = _rope_tables(seq)
        nblk = seq // tm
        in_specs += [pl.BlockSpec((tm, LANES), lambda i: (i % nblk, 0))] * 2
        args += [cos, sin]
    out_specs, out_shape = [], []
    for (kind, _, wd, dt, _, _) in outs:
        if kind == "nat":
            out_specs.append(pl.BlockSpec((tm, wd), lambda i: (i, 0)))
            out_shape.append(jax.ShapeDtypeStruct((rows, wd), dt))
        elif kind == "T":
            out_specs.append(pl.BlockSpec((tm // sblk, wd, sblk), lambda i: (i // per_seq, 0, i % per_seq)))
            out_shape.append(jax.ShapeDtypeStruct((rows // seq, wd, seq), dt))
        else:
            out_specs.append(pl.BlockSpec((tm, wd // LANES, LANES), lambda i: (i, 0, 0)))
            out_shape.append(jax.ShapeDtypeStruct((rows, wd // LANES, LANES), dt))
    return pl.pallas_call(
        functools.partial(_pre_kernel, outs=outs, rope=rope, sblk=sblk),
        grid=(rows // tm,),
        in_specs=in_specs,
        out_specs=out_specs,
        out_shape=out_shape,
        compiler_params=_params("arbitrary"),
        name="pre_rope" if rope else "pre",
    )(*args)


def _attend(s_list, vt_list, floor=None):
    mx = None
    for s in s_list:
        smax = jnp.max(s, axis=0, keepdims=True)
        mx = smax if mx is None else jnp.maximum(mx, smax)
    if floor is not None:
        mx = jnp.maximum(mx, floor)
    d = vt_list[0].shape[0]
    ov = None
    for s, vt in zip(s_list, vt_list):
        ones = jnp.ones((BF16_ROWS, vt.shape[1]), BF16)
        pv = _dot(jnp.concatenate([vt, ones], axis=0), jnp.exp2(s - mx).astype(BF16))
        ov = pv if ov is None else ov + pv
    return ov[0:d, :], ov[d:d + 1, :], mx


def _pipelined(n, first, second):
    out = []
    nxt = first(0)
    for u in range(n):
        cur = nxt
        if u + 1 < n:
            nxt = first(u + 1)
        out.append(second(u, cur))
    return out


def _side_convert(src_refs, dst_refs):
    for src, dst in zip(src_refs, dst_refs):
        dst[...] = src[...].astype(BF16)


def _mix0_kernel(*refs, seq, tq, n_cache, lam_init, n_side, nseq):
    n_in = 16 if n_cache else 14
    u_ref, q_ref, k_ref, vt_ref = refs[:4]
    ck_ref, cvt_ref = refs[4:6] if n_cache else (None, None)
    x_ref, m_ref, ng_ref, cw_ref, cb_ref, lg_ref, lb_ref, lam_ref, sg_ref, wo_ref = refs[n_in - 10:n_in]
    o_ref = refs[n_in + n_side]
    z_s, zw_s, zsh_s = refs[n_in + 2 * n_side + 1:]
    _side_convert(refs[n_in:n_in + n_side], refs[n_in + n_side + 1:n_in + 2 * n_side + 1])
    qi = pl.program_id(1)
    whole = tq == seq
    win = tq + 2 * CONV_PAD
    span = win - 8
    chunk = 64
    off0 = CONV_PAD - CONV_WIDTH // 2
    la = jnp.sum(lam_ref[0:1, :] * lam_ref[1:2, :], axis=-1, keepdims=True)
    lb = jnp.sum(lam_ref[2:3, :] * lam_ref[3:4, :], axis=-1, keepdims=True)
    lam = jnp.exp(la) - jnp.exp(lb) + lam_init
    sub = LANES // 2
    lane = lax.broadcasted_iota(jnp.int32, (tq, LANES), 1)

    def block(s):
        urows = slice(s * seq, (s + 1) * seq)

        def glu():
            z_s[0:CONV_PAD, :] = jnp.zeros((CONV_PAD, CONV_CH), F32)
            z_s[CONV_PAD + seq:2 * CONV_PAD + seq, :] = jnp.zeros((CONV_PAD, CONV_CH), F32)
            z_s[CONV_PAD:CONV_PAD + seq, :] = u_ref[urows, 0:CONV_CH] * _sigmoid(u_ref[urows, CONV_CH:2 * CONV_CH])

        if whole:
            glu()
            zw = z_s
        else:
            pl.when(qi == 0)(glu)
            zw_s[...] = z_s[pl.ds(pl.multiple_of(qi * tq, tq), win), :]
            zw = zw_s

        for b in range(1, 8):
            zsh_s[b - 1, 0:span, :] = zw[b:b + span, :]
        conv_rows = []
        for c0 in range(0, tq, chunk):
            acc = jnp.zeros((chunk // 8, 8, CONV_CH), F32)
            for j in range(CONV_WIDTH):
                off = j + off0
                a, b = off // 8, off % 8
                if b == 0:
                    zz = zw[c0 + 8 * a:c0 + 8 * a + chunk, :]
                else:
                    zz = zsh_s[b - 1, c0 + 8 * a:c0 + 8 * a + chunk, :]
                acc = acc + zz.reshape(chunk // 8, 8, CONV_CH) * cw_ref[j]
            acc = acc.reshape(chunk, CONV_CH) + cb_ref[...]
            mu = jnp.mean(acc, axis=-1, keepdims=True)
            xc = acc - mu
            y = xc * lax.rsqrt(jnp.mean(xc * xc, axis=-1, keepdims=True) + EPS) * lg_ref[...] + lb_ref[...]
            conv_rows.append((y * _sigmoid(y)).astype(BF16))

        def segments(h):
            cols = slice(h * LANES, (h + 1) * LANES)
            segs = [(k_ref[urows, cols], vt_ref[s, cols, :])]
            if n_cache:
                segs.append((ck_ref[:, cols].astype(BF16), cvt_ref[cols, :].astype(BF16)))
            return segs

        def scores(h):
            qc = q_ref[s * tq:(s + 1) * tq, h * LANES:(h + 1) * LANES]
            zero = jnp.zeros_like(qc)
            qq = jnp.concatenate([jnp.where(lane < sub, qc, zero), jnp.where(lane >= sub, qc, zero)], axis=0)
            return [_dot_nt(kk, qq) for kk, _ in segments(h)]

        def head(h, s_list):
            ov, den, _ = _attend(s_list, [vt for _, vt in segments(h)])
            r = 1.0 / den
            o = ov[:, 0:tq] * r[:, 0:tq] - ov[:, tq:2 * tq] * (lam * r[:, tq:2 * tq])
            return o * lax.rsqrt(jnp.mean(o * o, axis=0, keepdims=True) + EPS) * sg_ref[...] * (1.0 - lam_init)

        heads = _pipelined(DIFF_HEADS, scores, head)
        attn = jnp.concatenate(heads, axis=0).T.astype(BF16)
        return jnp.concatenate([jnp.concatenate(conv_rows, axis=0), attn], axis=1)

    cat = jnp.concatenate([block(s) for s in range(nseq)], axis=0)
    mixed = _dot(cat, wo_ref[...])
    o_ref[...] = x_ref[...] + m_ref[2:3, :] * _rms(mixed, ng_ref[1:2, :])


def _side_specs(side, steps, index):
    in_specs, out_specs, out_shape = [], [], []
    for w in side:
        rows, cols = w.shape
        n_blk = math.gcd(steps, rows // BF16_ROWS)
        per = steps // n_blk
        spec = pl.BlockSpec((rows // n_blk, cols), lambda *ids, per=per: (index(*ids) // per, 0))
        in_specs.append(spec)
        out_specs.append(spec)
        out_shape.append(jax.ShapeDtypeStruct(w.shape, BF16))
    return in_specs, out_specs, out_shape


def _mix0(u, q, k, vt, cache, x, m, norm_g, conv_w, conv_b, ln_g, ln_b, lam, subln_g, w_out,
          *, batch, seq, tq, cond_row, lam_init, side=(), nseq=1):
    nq = seq // tq
    n_cache = 0 if cache is None else cache[0].shape[1]
    win = tq + 2 * CONV_PAD
    steps = (batch // nseq) * nq
    const = lambda shape: pl.BlockSpec(shape, lambda b, i: (0,) * len(shape))
    in_specs = [pl.BlockSpec((nseq * seq, 2 * CONV_CH), lambda b, i: (b, 0)),
                pl.BlockSpec((nseq * tq, DIFF_WIDTH), lambda b, i: (b * nq + i, 0)),
                pl.BlockSpec((nseq * seq, DIFF_WIDTH), lambda b, i: (b, 0)),
                pl.BlockSpec((nseq, DIFF_WIDTH, seq), lambda b, i: (b, 0, 0))]
    args = [u, q, k, vt]
    if n_cache:
        in_specs += [pl.BlockSpec((None, n_cache, DIFF_WIDTH), lambda b, i: (b, 0, 0)),
                     pl.BlockSpec((None, DIFF_WIDTH, n_cache), lambda b, i: (b, 0, 0))]
        args += [cache[0], cache[1]]
    in_specs += [
        pl.BlockSpec((nseq * tq, D_MODEL), lambda b, i: (b * nq + i, 0)),
        pl.BlockSpec((None, N_MOD, D_MODEL), lambda b, i: (cond_row(b * nseq), 0, 0)),
        const((4, D_MODEL)), const((CONV_WIDTH, 8, CONV_CH)), const((1, CONV_CH)), const((1, CONV_CH)),
        const((1, CONV_CH)), const((4, LANES // 2)), const((LANES, 1)), _resident((D_MODEL, D_MODEL)),
    ]
    args += [x, m, norm_g, conv_w, conv_b, ln_g, ln_b, lam, subln_g, w_out]
    side_in, side_out, side_shape = _side_specs(side, steps, lambda b, i: b * nq + i)
    return pl.pallas_call(
        functools.partial(_mix0_kernel, seq=seq, tq=tq, n_cache=n_cache, lam_init=lam_init, n_side=len(side),
                          nseq=nseq),
        grid=(batch // nseq, nq),
        in_specs=in_specs + side_in,
        out_specs=[pl.BlockSpec((nseq * tq, D_MODEL), lambda b, i: (b * nq + i, 0))] + side_out,
        out_shape=[jax.ShapeDtypeStruct((batch * seq, D_MODEL), F32)] + side_shape,
        scratch_shapes=[
            pltpu.VMEM((seq + 2 * CONV_PAD, CONV_CH), F32), pltpu.VMEM((win, CONV_CH), F32),
            pltpu.VMEM((7, win - 8, CONV_CH), F32),
        ],
        compiler_params=_params("arbitrary", "arbitrary"),
        name="mix_conv_diff",
    )(*args, *side)


def _mix1_kernel(*refs, seq, tq, n_cache, nseq):
    if n_cache:
        (q_ref, kp_ref, kc_ref, kn_ref, vtp_ref, vtc_ref, vtn_ref, ck_ref, cvt_ref,
         x_ref, m_ref, ng_ref, sink_ref, wo_ref, o_ref) = refs
    else:
        (q_ref, kc_ref, vtc_ref, x_ref, m_ref, ng_ref, sink_ref, wo_ref, o_ref) = refs
    qi = pl.program_id(1)
    lane = lax.broadcasted_iota(jnp.int32, (tq, LANES), 1)

    def band(base, rows):
        kpos = base + lax.broadcasted_iota(jnp.int32, (rows, tq), 0)
        qpos = qi * tq + lax.broadcasted_iota(jnp.int32, (rows, tq), 1)
        ok = (jnp.abs(qpos - kpos) <= WINDOW) & (kpos >= 0) & (kpos < seq)
        return jnp.concatenate([ok] * GQA_GROUP, axis=1)

    def attention(r0, segs):
        def scores(g):
            b = g % 2
            kcols = slice((g // 2) * LANES, (g // 2 + 1) * LANES)
            keep = (lane < HEAD_DIM) if b == 0 else (lane >= HEAD_DIM)
            qs = []
            for h in range(g * GQA_GROUP, (g + 1) * GQA_GROUP):
                qc = q_ref[r0:r0 + tq, (h // 2) * LANES:(h // 2 + 1) * LANES]
                qa = qc if h % 2 == b else pltpu.roll(qc, HEAD_DIM, 1)
                qs.append(jnp.where(keep, qa, jnp.zeros_like(qa)))
            qg = jnp.concatenate(qs, axis=0)
            s_list = []
            for k_ref, _, ok in segs:
                s = _dot_nt(k_ref[:, kcols].astype(BF16), qg)
                s_list.append(s if ok is None else jnp.where(ok, s, NEG_INF))
            return s_list

        def group(g, s_list):
            vrows = slice(g * HEAD_DIM, (g + 1) * HEAD_DIM)
            sink = jnp.concatenate([jnp.full((1, tq), sink_ref[h] * LOG2E, F32)
                                    for h in range(g * GQA_GROUP, (g + 1) * GQA_GROUP)], axis=1)
            ov, den, mx = _attend(s_list, [vt_ref[vrows, :].astype(BF16) for _, vt_ref, _ in segs], floor=sink)
            ov = ov * (1.0 / (den + jnp.exp2(sink - mx)))
            return jnp.concatenate([ov[:, hh * tq:(hh + 1) * tq] for hh in range(GQA_GROUP)], axis=0)

        groups = _pipelined(GQA_KV_HEADS, scores, group)
        return jnp.concatenate(groups, axis=0).T.astype(BF16)

    if n_cache:
        attn = attention(0, [(kp_ref, vtp_ref, band(qi * tq - WINDOW, WINDOW)),
                             (kc_ref, vtc_ref, band(qi * tq, tq)),
                             (kn_ref, vtn_ref, band(qi * tq + tq, WINDOW)),
                             (ck_ref, cvt_ref, None)])
    else:
        attn = jnp.concatenate([attention(s * tq, [(kc_ref.at[pl.ds(s * tq, tq)], vtc_ref.at[s], None)])
                                for s in range(nseq)], axis=0)
    mixed = _dot(attn, wo_ref[...])
    o_ref[...] = x_ref[...] + m_ref[2:3, :] * _rms(mixed, ng_ref[1:2, :])


def _mix1(q, k, vt, cache, x, m, norm_g, sink, w_out, *, batch, seq, tq, cond_row, nseq=1):
    nq = seq // tq
    n_cache = 0 if cache is None else cache[0].shape[1]
    rows = nseq * tq
    q_spec = pl.BlockSpec((rows, D_MODEL), lambda b, i: (b * nq + i, 0))
    kc_spec = pl.BlockSpec((rows, KV_WIDTH), lambda b, i: (b * nq + i, 0))
    vtc_spec = pl.BlockSpec((None, KV_WIDTH, tq), lambda b, i: (b, 0, i))
    if n_cache:
        nw = seq // WINDOW
        per = tq // WINDOW
        prev = lambda i: jnp.maximum(i * per - 1, 0)
        nxt = lambda i: jnp.minimum(i * per + per, nw - 1)
        in_specs = [q_spec,
                    pl.BlockSpec((WINDOW, KV_WIDTH), lambda b, i: (b * nw + prev(i), 0)), kc_spec,
                    pl.BlockSpec((WINDOW, KV_WIDTH), lambda b, i: (b * nw + nxt(i), 0)),
                    pl.BlockSpec((None, KV_WIDTH, WINDOW), lambda b, i: (b, 0, prev(i))), vtc_spec,
                    pl.BlockSpec((None, KV_WIDTH, WINDOW), lambda b, i: (b, 0, nxt(i))),
                    pl.BlockSpec((None, n_cache, KV_WIDTH), lambda b, i: (b, 0, 0)),
                    pl.BlockSpec((None, KV_WIDTH, n_cache), lambda b, i: (b, 0, 0))]
        args = [q, k, k, k, vt, vt, vt, cache[0], cache[1]]
    else:
        in_specs = [q_spec, kc_spec, pl.BlockSpec((nseq, KV_WIDTH, tq), lambda b, i: (b, 0, 0))]
        args = [q, k, vt]
    in_specs += [
        pl.BlockSpec((rows, D_MODEL), lambda b, i: (b * nq + i, 0)),
        pl.BlockSpec((None, N_MOD, D_MODEL), lambda b, i: (cond_row(b * nseq), 0, 0)),
        pl.BlockSpec((4, D_MODEL), lambda b, i: (0, 0)),
        pl.BlockSpec(memory_space=pltpu.SMEM),
        pl.BlockSpec((D_MODEL, D_MODEL), lambda b, i: (0, 0)),
    ]
    args += [x, m, norm_g, sink, w_out]
    return pl.pallas_call(
        functools.partial(_mix1_kernel, seq=seq, tq=tq, n_cache=n_cache, nseq=nseq),
        grid=(batch // nseq, nq),
        in_specs=in_specs,
        out_specs=pl.BlockSpec((rows, D_MODEL), lambda b, i: (b * nq + i, 0)),
        out_shape=jax.ShapeDtypeStruct((batch * seq, D_MODEL), F32),
        compiler_params=_params("arbitrary", "arbitrary"),
        name="mix_gqa",
    )(*args)


FFN_CHUNKS = (1536, 1280)


def _ffn_kernel(*refs, n_side):
    x_ref, m_ref, ng_ref, wgu_ref, wd_ref = refs[:5]
    o_ref = refs[5 + n_side]
    _side_convert(refs[5:5 + n_side], refs[6 + n_side:])
    h = (_rms(x_ref[...], ng_ref[2:3, :]) * (1.0 + m_ref[4:5, :]) + m_ref[3:4, :]).astype(BF16)
    acc = jnp.zeros(h.shape, F32)
    c0 = 0
    for width in FFN_CHUNKS:
        gate = _dot(h, wgu_ref[:, c0:c0 + width])
        up = _dot(h, wgu_ref[:, FFN_HIDDEN + c0:FFN_HIDDEN + c0 + width])
        act = (gate * _sigmoid(gate) * up).astype(BF16)
        acc = acc + _dot(act, wd_ref[c0:c0 + width, :])
        c0 += width
    o_ref[...] = x_ref[...] + m_ref[5:6, :] * _rms(acc, ng_ref[3:4, :])


def _ffn(x, m, norm_g, w_gu, w_down, *, tm, cond_row, side=()):
    rows = x.shape[0]
    side_in, side_out, side_shape = _side_specs(side, rows // tm, lambda i: i)
    return pl.pallas_call(
        functools.partial(_ffn_kernel, n_side=len(side)),
        grid=(rows // tm,),
        in_specs=[
            pl.BlockSpec((tm, D_MODEL), lambda i: (i, 0)),
            pl.BlockSpec((None, N_MOD, D_MODEL), lambda i: (cond_row(i), 0, 0)),
            pl.BlockSpec((4, D_MODEL), lambda i: (0, 0)),
            _resident((D_MODEL, 2 * FFN_HIDDEN)),
            _resident((FFN_HIDDEN, D_MODEL)),
        ] + side_in,
        out_specs=[pl.BlockSpec((tm, D_MODEL), lambda i: (i, 0))] + side_out,
        out_shape=[jax.ShapeDtypeStruct((rows, D_MODEL), F32)] + side_shape,
        compiler_params=_params("arbitrary"),
        name="ffn",
    )(x, m, norm_g, w_gu, w_down, *side)


LAM_INIT_0 = 0.8 - 0.6 * math.exp(-0.3 * 0)
CTX_SEQS_PER_STEP = 4


def _trunk(x, m, caches, weights, *, batch, seq, cond_of_batch):
    (norm_g0, w_in, conv_w, conv_b, ln_g, ln_b, lam, subln_g, w_out0, w_gu0, w_down0,
     norm_g1, w_qkv, sink, w_out1, w_gu1, w_down1) = weights
    m0, m1 = m
    ctx = caches is None
    rope = not ctx
    tm = 512
    tm_ffn = 512
    tq = 256
    row_of = lambda t: (lambda i: cond_of_batch((i * t) // seq))

    o0 = 2 * CONV_CH
    ko, vo = o0 + DIFF_WIDTH, o0 + 2 * DIFF_WIDTH
    outs0 = [("nat", 0, o0, F32, False, None), ("nat", o0, DIFF_WIDTH, BF16, True, Q_SCALE),
             ("nat", ko, DIFF_WIDTH, BF16, True, None), ("T", vo, DIFF_WIDTH, BF16, False, None)]
    if ctx:
        outs0 += [("T", ko, DIFF_WIDTH, F32, False, None), ("heads", vo, DIFF_WIDTH, F32, False, None)]
    res = _pre(x, m0, norm_g0, w_in, tuple(outs0), tm=tm, cond_row=row_of(tm), seq=seq, rope=rope)
    u, q, k, vt = res[:4]
    new0 = tuple(res[4:])
    x, *conv0 = _mix0(u, q, k, vt, None if ctx else caches[0], x, m0, norm_g0, conv_w, conv_b,
                      ln_g, ln_b, lam, subln_g, w_out0, batch=batch, seq=seq, tq=tq, cond_row=cond_of_batch,
                      lam_init=LAM_INIT_0, side=(w_gu0, w_down0) if ctx else (),
                      nseq=CTX_SEQS_PER_STEP if ctx else 1)
    if ctx:
        w_gu0, w_down0 = conv0
    x, *conv1 = _ffn(x, m0, norm_g0, w_gu0, w_down0, tm=tm_ffn, cond_row=row_of(tm_ffn),
                     side=(w_qkv, w_out1, w_gu1, w_down1) if ctx else ())
    if ctx:
        w_qkv, w_out1, w_gu1, w_down1 = conv1

    outs1 = [("nat", 0, Q_WIDTH, BF16, True, Q_SCALE), ("nat", Q_WIDTH, KV_WIDTH, BF16, True, None)]
    if ctx:
        outs1 += [("T", Q_WIDTH + KV_WIDTH, KV_WIDTH, F32, False, None), ("T", Q_WIDTH, KV_WIDTH, F32, False, None)]
    else:
        outs1 += [("T", Q_WIDTH + KV_WIDTH, KV_WIDTH, BF16, False, None)]
    res = _pre(x, m1, norm_g1, w_qkv, tuple(outs1), tm=tm, cond_row=row_of(tm), seq=seq, rope=rope)
    q, k, vt = res[:3]
    new1 = (res[3], vt) if ctx else ()
    x = _mix1(q, k, vt, None if ctx else caches[1], x, m1, norm_g1, sink, w_out1,
              batch=batch, seq=seq, tq=tq, cond_row=cond_of_batch, nseq=CTX_SEQS_PER_STEP if ctx else 1)
    x, = _ffn(x, m1, norm_g1, w_gu1, w_down1, tm=tm_ffn, cond_row=row_of(tm_ffn))
    return x, new0 + new1, (w_gu0, w_down0, w_qkv, w_out1, w_gu1, w_down1)


def kernel(x_prompt, x_sample, cache_k0, cache_v0, cache_k1, cache_v1, c, c_ctx, l0_mod_w, l0_mod_b, l0_norm_g, l0_w_in, l0_conv_w, l0_conv_b, l0_conv_ln_g, l0_conv_ln_b, l0_lambda, l0_subln_g, l0_w_out, l0_w_gu, l0_w_down, l1_mod_w, l1_mod_b, l1_norm_g, l1_w_qkv, l1_sink, l1_w_out, l1_w_gu, l1_w_down):
    batch, seq, d = x_prompt.shape
    dec_batch, dec_seq, _ = x_sample.shape
    n_past = cache_k0.shape[1]

    cond = jnp.concatenate([c_ctx[None, :], c, jnp.zeros((MOD_ROWS - 1 - dec_batch, d), F32)], axis=0)
    m = (_modulation(cond, l0_mod_w, l0_mod_b), _modulation(cond, l1_mod_w, l1_mod_b))

    row = lambda v: v.reshape(1, -1)
    layer0 = (l0_norm_g, l0_w_in.astype(BF16), jnp.broadcast_to(l0_conv_w, (CONV_WIDTH, 8, CONV_CH)), row(l0_conv_b),
              row(l0_conv_ln_g), row(l0_conv_ln_b), l0_lambda, l0_subln_g.reshape(LANES, 1),
              l0_w_out.astype(BF16))
    weights = layer0 + (l0_w_gu, l0_w_down, l1_norm_g, l1_w_qkv, l1_sink, l1_w_out, l1_w_gu, l1_w_down)

    y_prompt, (kt0, v0, kt1, vt1), bf16_w = _trunk(x_prompt.reshape(batch * seq, d), m, None, weights,
                                                   batch=batch, seq=seq, cond_of_batch=lambda b: 0)
    w_gu0, w_down0, w_qkv, w_out1, w_gu1, w_down1 = bf16_w
    weights = layer0 + (w_gu0, w_down0, l1_norm_g, w_qkv, l1_sink, w_out1, w_gu1, w_down1)
    caches = ((cache_k0.reshape(dec_batch, n_past, -1),
               jnp.swapaxes(cache_v0.reshape(dec_batch, n_past, -1), 1, 2)),
              (cache_k1.reshape(dec_batch, n_past, -1),
               jnp.swapaxes(cache_v1.reshape(dec_batch, n_past, -1), 1, 2)))
    y_sample, _, _ = _trunk(x_sample.reshape(dec_batch * dec_seq, d), m, caches, weights,
                            batch=dec_batch, seq=dec_seq, cond_of_batch=lambda b: 1 + b)
    new_k0 = kt0.reshape(batch, DIFF_HEADS, 2, HEAD_DIM, seq).transpose(0, 4, 1, 2, 3)
    new_v0 = v0.reshape(batch, seq, DIFF_HEADS, 2 * HEAD_DIM)
    new_k1 = kt1.reshape(batch, GQA_KV_HEADS, HEAD_DIM, seq).transpose(0, 3, 1, 2)
    new_v1 = vt1.reshape(batch, GQA_KV_HEADS, HEAD_DIM, seq).transpose(0, 3, 1, 2)
    return (y_prompt.reshape(batch, seq, d), y_sample.reshape(dec_batch, dec_seq, d),
            new_k0, new_v0, new_k1, new_v1)
```

```python
import functools
import math

import numpy as np
import jax
import jax.numpy as jnp
from jax import lax
from jax.experimental import pallas as pl
from jax.experimental.pallas import tpu as pltpu

D_MODEL = 1024
GRID_W = 64
ROPE_THETA = 10000.0
EPS = 1e-6
NEG_INF = -1e30
LOG2E = math.log2(math.e)
CONV_CH = 512
CONV_WIDTH = 31
CONV_PAD = 16
DIFF_HEADS = 4
DIFF_WIDTH = 512
HEAD_DIM = 64
GQA_HEADS = 16
GQA_KV_HEADS = 4
GQA_GROUP = 4
Q_WIDTH = GQA_HEADS * HEAD_DIM
KV_WIDTH = GQA_KV_HEADS * HEAD_DIM
Q_SCALE = LOG2E * HEAD_DIM ** -0.5
WINDOW = 128
FFN_HIDDEN = 2816
N_MOD = 6
LANES = 128
BF16_ROWS = 16
MOD_ROWS = 8
VMEM_LIMIT = 56 * 1024 * 1024

BF16 = jnp.bfloat16
F32 = jnp.float32


def _sigmoid(x):
    return 1.0 / (1.0 + jnp.exp(-x))


def _rms(x, g):
    return x * lax.rsqrt(jnp.mean(x * x, axis=-1, keepdims=True) + EPS) * g


def _dot(a, b):
    return jnp.dot(a, b, preferred_element_type=F32)


def _dot_nt(a, b):
    return lax.dot_general(a, b, (((1,), (1,)), ((), ())), preferred_element_type=F32)


def _params(*sem):
    return pltpu.CompilerParams(dimension_semantics=sem, vmem_limit_bytes=VMEM_LIMIT)


def _resident(shape):
    return pl.BlockSpec(shape, lambda *_: (0,) * len(shape), pipeline_mode=pl.Buffered(1))


def _mod_kernel(c_ref, w_ref, b_ref, o_ref):
    c = c_ref[...]
    s = (c * _sigmoid(c)).astype(BF16)
    o_ref[...] = _dot(s, w_ref[...].astype(BF16)) + b_ref[...]


def _modulation(cond, mod_w, mod_b):
    n = mod_w.shape[1]
    tn = 1536
    out = pl.pallas_call(
        _mod_kernel,
        grid=(n // tn,),
        in_specs=[
            pl.BlockSpec((MOD_ROWS, D_MODEL), lambda j: (0, 0)),
            pl.BlockSpec((D_MODEL, tn), lambda j: (0, j)),
            pl.BlockSpec((1, tn), lambda j: (0, j)),
        ],
        out_specs=pl.BlockSpec((MOD_ROWS, tn), lambda j: (0, j)),
        out_shape=jax.ShapeDtypeStruct((MOD_ROWS, n), F32),
        compiler_params=_params("arbitrary"),
        name="modulation",
    )(cond, mod_w, mod_b.reshape(1, n))
    return out.reshape(MOD_ROWS, N_MOD, D_MODEL)


def _rope_tables(seq):
    t = np.arange(seq)
    rows, cols = t // GRID_W, t % GRID_W
    half = HEAD_DIM // 2
    inv = 1.0 / (ROPE_THETA ** (np.arange(0, half, 2, dtype=np.float64) / half))
    ar = rows[:, None] * inv[None, :]
    ac = cols[:, None] * inv[None, :]
    cos = np.concatenate([np.cos(ar), np.cos(ar), np.cos(ac), np.cos(ac)], axis=1)
    sin = np.concatenate([-np.sin(ar), np.sin(ar), -np.sin(ac), np.sin(ac)], axis=1)
    reps = LANES // HEAD_DIM
    return (jnp.asarray(np.tile(cos, (1, reps)), F32), jnp.asarray(np.tile(sin, (1, reps)), F32))


def _rope(x, cos, sin):
    quarter = HEAD_DIM // 4
    lane = lax.broadcasted_iota(jnp.int32, x.shape, 1)
    lo = (lane % (2 * quarter)) < quarter
    partner = jnp.where(lo, pltpu.roll(x, LANES - quarter, 1), pltpu.roll(x, quarter, 1))
    return x * cos + partner * sin


def _pre_kernel(*refs, outs, rope, sblk):
    n_in = 6 if rope else 4
    x_ref, m_ref, g_ref, w_ref = refs[:4]
    o_refs = refs[n_in:]
    x = x_ref[...]
    tm = x.shape[0]
    h = (_rms(x, g_ref[0:1, :]) * (1.0 + m_ref[1:2, :]) + m_ref[0:1, :]).astype(BF16)
    cost = lambda o: o[2] * {"T": 5.0, "heads": 0.5}.get(o[0], 3.0 if (rope and o[4]) else 1.0)
    ranges = sorted({(o[1], o[2]) for o in outs},
                    key=lambda r: (-sum(cost(o) for o in outs if (o[1], o[2]) == r), r[0]))
    for start, width in ranges:
        p = _dot(h, w_ref[:, start:start + width])
        for (kind, s0, w0, _, roped, scale), o_ref in zip(outs, o_refs):
            if (s0, w0) != (start, width):
                continue
            if kind == "nat":
                for c in range(width // LANES):
                    xc = p[:, c * LANES:(c + 1) * LANES]
                    if rope and roped:
                        xc = _rope(xc, refs[4][...], refs[5][...])
                    if scale is not None:
                        xc = xc * scale
                    o_ref[:, c * LANES:(c + 1) * LANES] = xc.astype(o_ref.dtype)
            elif kind == "T":
                for s in range(tm // sblk):
                    o_ref[s] = p[s * sblk:(s + 1) * sblk, :].T.astype(o_ref.dtype)
            else:
                heads = width // LANES
                for hd in range(heads):
                    o_ref[pl.ds(hd, tm, stride=heads), :] = p[:, hd * LANES:(hd + 1) * LANES]


def _pre(x, m, norm_g, w, outs, *, tm, cond_row, seq, rope):
    rows = x.shape[0]
    n = w.shape[1]
    sblk = min(seq, tm)
    per_seq = seq // sblk
    in_specs = [
        pl.BlockSpec((tm, D_MODEL), lambda i: (i, 0)),
        pl.BlockSpec((None, N_MOD, D_MODEL), lambda i: (cond_row(i), 0, 0)),
        pl.BlockSpec((4, D_MODEL), lambda i: (0, 0)),
        _resident((D_MODEL, n)),
    ]
    args = [x, m, norm_g, w]
    if rope:
        cos, sin = _rope_tables(seq)
        nblk = seq // tm
        in_specs += [pl.BlockSpec((tm, LANES), lambda i: (i % nblk, 0))] * 2
        args += [cos, sin]
    out_specs, out_shape = [], []
    for (kind, _, wd, dt, _, _) in outs:
        if kind == "nat":
            out_specs.append(pl.BlockSpec((tm, wd), lambda i: (i, 0)))
            out_shape.append(jax.ShapeDtypeStruct((rows, wd), dt))
        elif kind == "T":
            out_specs.append(pl.BlockSpec((tm // sblk, wd, sblk), lambda i: (i // per_seq, 0, i % per_seq)))
            out_shape.append(jax.ShapeDtypeStruct((rows // seq, wd, seq), dt))
        else:
            out_specs.append(pl.BlockSpec((tm * (wd // LANES), LANES), lambda i: (i, 0)))
            out_shape.append(jax.ShapeDtypeStruct((rows * (wd // LANES), LANES), dt))
    return pl.pallas_call(
        functools.partial(_pre_kernel, outs=outs, rope=rope, sblk=sblk),
        grid=(rows // tm,),
        in_specs=in_specs,
        out_specs=out_specs,
        out_shape=out_shape,
        compiler_params=_params("arbitrary"),
        name="pre_rope" if rope else "pre",
    )(*args)


def _attend(s_list, vt_list, floor=None):
    mx = None
    for s in s_list:
        smax = jnp.max(s, axis=0, keepdims=True)
        mx = smax if mx is None else jnp.maximum(mx, smax)
    if floor is not None:
        mx = jnp.maximum(mx, floor)
    d = vt_list[0].shape[0]
    ov = None
    for s, vt in zip(s_list, vt_list):
        ones = jnp.ones((BF16_ROWS, vt.shape[1]), BF16)
        pv = _dot(jnp.concatenate([vt, ones], axis=0), jnp.exp2(s - mx).astype(BF16))
        ov = pv if ov is None else ov + pv
    return ov[0:d, :], ov[d:d + 1, :], mx


def _pipelined(n, first, second):
    out = []
    nxt = first(0)
    for u in range(n):
        cur = nxt
        if u + 1 < n:
            nxt = first(u + 1)
        out.append(second(u, cur))
    return out


def _side_convert(src_refs, dst_refs):
    for src, dst in zip(src_refs, dst_refs):
        dst[...] = src[...].astype(BF16)


def _mix0_kernel(*refs, seq, tq, n_cache, lam_init, n_side, nseq):
    n_in = 16 if n_cache else 14
    u_ref, q_ref, k_ref, vt_ref = refs[:4]
    ck_ref, cvt_ref = refs[4:6] if n_cache else (None, None)
    x_ref, m_ref, ng_ref, cw_ref, cb_ref, lg_ref, lb_ref, lam_ref, sg_ref, wo_ref = refs[n_in - 10:n_in]
    o_ref = refs[n_in + n_side]
    z_s, zw_s, zsh_s = refs[n_in + 2 * n_side + 1:]
    _side_convert(refs[n_in:n_in + n_side], refs[n_in + n_side + 1:n_in + 2 * n_side + 1])
    qi = pl.program_id(1)
    whole = tq == seq
    win = tq + 2 * CONV_PAD
    span = win - 8
    chunk = 64
    off0 = CONV_PAD - CONV_WIDTH // 2
    la = jnp.sum(lam_ref[0:1, :] * lam_ref[1:2, :], axis=-1, keepdims=True)
    lb = jnp.sum(lam_ref[2:3, :] * lam_ref[3:4, :], axis=-1, keepdims=True)
    lam = jnp.exp(la) - jnp.exp(lb) + lam_init
    sub = LANES // 2
    lane = lax.broadcasted_iota(jnp.int32, (tq, LANES), 1)

    def block(s):
        urows = slice(s * seq, (s + 1) * seq)

        def glu():
            z_s[0:CONV_PAD, :] = jnp.zeros((CONV_PAD, CONV_CH), F32)
            z_s[CONV_PAD + seq:2 * CONV_PAD + seq, :] = jnp.zeros((CONV_PAD, CONV_CH), F32)
            z_s[CONV_PAD:CONV_PAD + seq, :] = u_ref[urows, 0:CONV_CH] * _sigmoid(u_ref[urows, CONV_CH:2 * CONV_CH])

        if whole:
            glu()
            zw = z_s
        else:
            pl.when(qi == 0)(glu)
            zw_s[...] = z_s[pl.ds(pl.multiple_of(qi * tq, tq), win), :]
            zw = zw_s

        for b in range(1, 8):
            zsh_s[b - 1, 0:span, :] = zw[b:b + span, :]
        conv_rows = []
        for c0 in range(0, tq, chunk):
            acc = jnp.zeros((chunk // 8, 8, CONV_CH), F32)
            for j in range(CONV_WIDTH):
                off = j + off0
                a, b = off // 8, off % 8
                if b == 0:
                    zz = zw[c0 + 8 * a:c0 + 8 * a + chunk, :]
                else:
                    zz = zsh_s[b - 1, c0 + 8 * a:c0 + 8 * a + chunk, :]
                acc = acc + zz.reshape(chunk // 8, 8, CONV_CH) * cw_ref[j]
            acc = acc.reshape(chunk, CONV_CH) + cb_ref[...]
            mu = jnp.mean(acc, axis=-1, keepdims=True)
            xc = acc - mu
            y = xc * lax.rsqrt(jnp.mean(xc * xc, axis=-1, keepdims=True) + EPS) * lg_ref[...] + lb_ref[...]
            conv_rows.append((y * _sigmoid(y)).astype(BF16))

        def segments(h):
            cols = slice(h * LANES, (h + 1) * LANES)
            segs = [(k_ref[urows, cols], vt_ref[s, cols, :])]
            if n_cache:
                segs.append((ck_ref[:, cols].astype(BF16), cvt_ref[cols, :].astype(BF16)))
            return segs

        def scores(h):
            qc = q_ref[s * tq:(s + 1) * tq, h * LANES:(h + 1) * LANES]
            zero = jnp.zeros_like(qc)
            qq = jnp.concatenate([jnp.where(lane < sub, qc, zero), jnp.where(lane >= sub, qc, zero)], axis=0)
            return [_dot_nt(kk, qq) for kk, _ in segments(h)]

        def head(h, s_list):
            ov, den, _ = _attend(s_list, [vt for _, vt in segments(h)])
            r = 1.0 / den
            o = ov[:, 0:tq] * r[:, 0:tq] - ov[:, tq:2 * tq] * (lam * r[:, tq:2 * tq])
            return o * lax.rsqrt(jnp.mean(o * o, axis=0, keepdims=True) + EPS) * sg_ref[...] * (1.0 - lam_init)

        heads = _pipelined(DIFF_HEADS, scores, head)
        attn = jnp.concatenate(heads, axis=0).T.astype(BF16)
        return jnp.concatenate([jnp.concatenate(conv_rows, axis=0), attn], axis=1)

    cat = jnp.concatenate([block(s) for s in range(nseq)], axis=0)
    mixed = _dot(cat, wo_ref[...])
    o_ref[...] = x_ref[...] + m_ref[2:3, :] * _rms(mixed, ng_ref[1:2, :])


def _side_specs(side, steps, index):
    in_specs, out_specs, out_shape = [], [], []
    for w in side:
        rows, cols = w.shape
        n_blk = math.gcd(steps, rows // BF16_ROWS)
        per = steps // n_blk
        spec = pl.BlockSpec((rows // n_blk, cols), lambda *ids, per=per: (index(*ids) // per, 0))
        in_specs.append(spec)
        out_specs.append(spec)
        out_shape.append(jax.ShapeDtypeStruct(w.shape, BF16))
    return in_specs, out_specs, out_shape


def _mix0(u, q, k, vt, cache, x, m, norm_g, conv_w, conv_b, ln_g, ln_b, lam, subln_g, w_out,
          *, batch, seq, tq, cond_row, lam_init, side=(), nseq=1):
    nq = seq // tq
    n_cache = 0 if cache is None else cache[0].shape[1]
    win = tq + 2 * CONV_PAD
    steps = (batch // nseq) * nq
    const = lambda shape: pl.BlockSpec(shape, lambda b, i: (0,) * len(shape))
    in_specs = [pl.BlockSpec((nseq * seq, 2 * CONV_CH), lambda b, i: (b, 0)),
                pl.BlockSpec((nseq * tq, DIFF_WIDTH), lambda b, i: (b * nq + i, 0)),
                pl.BlockSpec((nseq * seq, DIFF_WIDTH), lambda b, i: (b, 0)),
                pl.BlockSpec((nseq, DIFF_WIDTH, seq), lambda b, i: (b, 0, 0))]
    args = [u, q, k, vt]
    if n_cache:
        in_specs += [pl.BlockSpec((None, n_cache, DIFF_WIDTH), lambda b, i: (b, 0, 0)),
                     pl.BlockSpec((None, DIFF_WIDTH, n_cache), lambda b, i: (b, 0, 0))]
        args += [cache[0], cache[1]]
    in_specs += [
        pl.BlockSpec((nseq * tq, D_MODEL), lambda b, i: (b * nq + i, 0)),
        pl.BlockSpec((None, N_MOD, D_MODEL), lambda b, i: (cond_row(b * nseq), 0, 0)),
        const((4, D_MODEL)), const((CONV_WIDTH, 8, CONV_CH)), const((1, CONV_CH)), const((1, CONV_CH)),
        const((1, CONV_CH)), const((4, LANES // 2)), const((LANES, 1)), _resident((D_MODEL, D_MODEL)),
    ]
    args += [x, m, norm_g, conv_w, conv_b, ln_g, ln_b, lam, subln_g, w_out]
    side_in, side_out, side_shape = _side_specs(side, steps, lambda b, i: b * nq + i)
    return pl.pallas_call(
        functools.partial(_mix0_kernel, seq=seq, tq=tq, n_cache=n_cache, lam_init=lam_init, n_side=len(side),
                          nseq=nseq),
        grid=(batch // nseq, nq),
        in_specs=in_specs + side_in,
        out_specs=[pl.BlockSpec((nseq * tq, D_MODEL), lambda b, i: (b * nq + i, 0))] + side_out,
        out_shape=[jax.ShapeDtypeStruct((batch * seq, D_MODEL), F32)] + side_shape,
        scratch_shapes=[
            pltpu.VMEM((seq + 2 * CONV_PAD, CONV_CH), F32), pltpu.VMEM((win, CONV_CH), F32),
            pltpu.VMEM((7, win - 8, CONV_CH), F32),
        ],
        compiler_params=_params("arbitrary", "arbitrary"),
        name="mix_conv_diff",
    )(*args, *side)


def _mix1_kernel(*refs, seq, tq, n_cache, nseq):
    if n_cache:
        (q_ref, kp_ref, kc_ref, kn_ref, vtp_ref, vtc_ref, vtn_ref, ck_ref, cvt_ref,
         x_ref, m_ref, ng_ref, sink_ref, wo_ref, o_ref) = refs
    else:
        (q_ref, kc_ref, vtc_ref, x_ref, m_ref, ng_ref, sink_ref, wo_ref, o_ref) = refs
    qi = pl.program_id(1)
    lane = lax.broadcasted_iota(jnp.int32, (tq, LANES), 1)

    def band(base, rows):
        kpos = base + lax.broadcasted_iota(jnp.int32, (rows, tq), 0)
        qpos = qi * tq + lax.broadcasted_iota(jnp.int32, (rows, tq), 1)
        ok = (jnp.abs(qpos - kpos) <= WINDOW) & (kpos >= 0) & (kpos < seq)
        return jnp.concatenate([ok] * GQA_GROUP, axis=1)

    def attention(r0, segs):
        def scores(g):
            b = g % 2
            kcols = slice((g // 2) * LANES, (g // 2 + 1) * LANES)
            keep = (lane < HEAD_DIM) if b == 0 else (lane >= HEAD_DIM)
            qs = []
            for h in range(g * GQA_GROUP, (g + 1) * GQA_GROUP):
                qc = q_ref[r0:r0 + tq, (h // 2) * LANES:(h // 2 + 1) * LANES]
                qa = qc if h % 2 == b else pltpu.roll(qc, HEAD_DIM, 1)
                qs.append(jnp.where(keep, qa, jnp.zeros_like(qa)))
            qg = jnp.concatenate(qs, axis=0)
            s_list = []
            for k_ref, _, ok in segs:
                s = _dot_nt(k_ref[:, kcols].astype(BF16), qg)
                s_list.append(s if ok is None else jnp.where(ok, s, NEG_INF))
            return s_list

        def group(g, s_list):
            vrows = slice(g * HEAD_DIM, (g + 1) * HEAD_DIM)
            sink = jnp.concatenate([jnp.full((1, tq), sink_ref[h] * LOG2E, F32)
                                    for h in range(g * GQA_GROUP, (g + 1) * GQA_GROUP)], axis=1)
            ov, den, mx = _attend(s_list, [vt_ref[vrows, :].astype(BF16) for _, vt_ref, _ in segs], floor=sink)
            ov = ov * (1.0 / (den + jnp.exp2(sink - mx)))
            return jnp.concatenate([ov[:, hh * tq:(hh + 1) * tq] for hh in range(GQA_GROUP)], axis=0)

        groups = _pipelined(GQA_KV_HEADS, scores, group)
        return jnp.concatenate(groups, axis=0).T.astype(BF16)

    if n_cache:
        attn = attention(0, [(kp_ref, vtp_ref, band(qi * tq - WINDOW, WINDOW)),
                             (kc_ref, vtc_ref, band(qi * tq, tq)),
                             (kn_ref, vtn_ref, band(qi * tq + tq, WINDOW)),
                             (ck_ref, cvt_ref, None)])
    else:
        attn = jnp.concatenate([attention(s * tq, [(kc_ref.at[pl.ds(s * tq, tq)], vtc_ref.at[s], None)])
                                for s in range(nseq)], axis=0)
    mixed = _dot(attn, wo_ref[...])
    o_ref[...] = x_ref[...] + m_ref[2:3, :] * _rms(mixed, ng_ref[1:2, :])


def _mix1(q, k, vt, cache, x, m, norm_g, sink, w_out, *, batch, seq, tq, cond_row, nseq=1):
    nq = seq // tq
    n_cache = 0 if cache is None else cache[0].shape[1]
    rows = nseq * tq
    q_spec = pl.BlockSpec((rows, D_MODEL), lambda b, i: (b * nq + i, 0))
    kc_spec = pl.BlockSpec((rows, KV_WIDTH), lambda b, i: (b * nq + i, 0))
    vtc_spec = pl.BlockSpec((None, KV_WIDTH, tq), lambda b, i: (b, 0, i))
    if n_cache:
        nw = seq // WINDOW
        per = tq // WINDOW
        prev = lambda i: jnp.maximum(i * per - 1, 0)
        nxt = lambda i: jnp.minimum(i * per + per, nw - 1)
        in_specs = [q_spec,
                    pl.BlockSpec((WINDOW, KV_WIDTH), lambda b, i: (b * nw + prev(i), 0)), kc_spec,
                    pl.BlockSpec((WINDOW, KV_WIDTH), lambda b, i: (b * nw + nxt(i), 0)),
                    pl.BlockSpec((None, KV_WIDTH, WINDOW), lambda b, i: (b, 0, prev(i))), vtc_spec,
                    pl.BlockSpec((None, KV_WIDTH, WINDOW), lambda b, i: (b, 0, nxt(i))),
                    pl.BlockSpec((None, n_cache, KV_WIDTH), lambda b, i: (b, 0, 0)),
                    pl.BlockSpec((None, KV_WIDTH, n_cache), lambda b, i: (b, 0, 0))]
        args = [q, k, k, k, vt, vt, vt, cache[0], cache[1]]
    else:
        in_specs = [q_spec, kc_spec, pl.BlockSpec((nseq, KV_WIDTH, tq), lambda b, i: (b, 0, 0))]
        args = [q, k, vt]
    in_specs += [
        pl.BlockSpec((rows, D_MODEL), lambda b, i: (b * nq + i, 0)),
        pl.BlockSpec((None, N_MOD, D_MODEL), lambda b, i: (cond_row(b * nseq), 0, 0)),
        pl.BlockSpec((4, D_MODEL), lambda b, i: (0, 0)),
        pl.BlockSpec(memory_space=pltpu.SMEM),
        pl.BlockSpec((D_MODEL, D_MODEL), lambda b, i: (0, 0)),
    ]
    args += [x, m, norm_g, sink, w_out]
    return pl.pallas_call(
        functools.partial(_mix1_kernel, seq=seq, tq=tq, n_cache=n_cache, nseq=nseq),
        grid=(batch // nseq, nq),
        in_specs=in_specs,
        out_specs=pl.BlockSpec((rows, D_MODEL), lambda b, i: (b * nq + i, 0)),
        out_shape=jax.ShapeDtypeStruct((batch * seq, D_MODEL), F32),
        compiler_params=_params("arbitrary", "arbitrary"),
        name="mix_gqa",
    )(*args)


FFN_CHUNKS = (1536, 1280)


def _ffn_kernel(*refs, n_side):
    x_ref, m_ref, ng_ref, wgu_ref, wd_ref = refs[:5]
    o_ref = refs[5 + n_side]
    _side_convert(refs[5:5 + n_side], refs[6 + n_side:])
    h = (_rms(x_ref[...], ng_ref[2:3, :]) * (1.0 + m_ref[4:5, :]) + m_ref[3:4, :]).astype(BF16)
    acc = jnp.zeros(h.shape, F32)
    c0 = 0
    for width in FFN_CHUNKS:
        gate = _dot(h, wgu_ref[:, c0:c0 + width])
        up = _dot(h, wgu_ref[:, FFN_HIDDEN + c0:FFN_HIDDEN + c0 + width])
        act = (gate * _sigmoid(gate) * up).astype(BF16)
        acc = acc + _dot(act, wd_ref[c0:c0 + width, :])
        c0 += width
    o_ref[...] = x_ref[...] + m_ref[5:6, :] * _rms(acc, ng_ref[3:4, :])


def _ffn(x, m, norm_g, w_gu, w_down, *, tm, cond_row, side=()):
    rows = x.shape[0]
    side_in, side_out, side_shape = _side_specs(side, rows // tm, lambda i: i)
    return pl.pallas_call(
        functools.partial(_ffn_kernel, n_side=len(side)),
        grid=(rows // tm,),
        in_specs=[
            pl.BlockSpec((tm, D_MODEL), lambda i: (i, 0)),
            pl.BlockSpec((None, N_MOD, D_MODEL), lambda i: (cond_row(i), 0, 0)),
            pl.BlockSpec((4, D_MODEL), lambda i: (0, 0)),
            _resident((D_MODEL, 2 * FFN_HIDDEN)),
            _resident((FFN_HIDDEN, D_MODEL)),
        ] + side_in,
        out_specs=[pl.BlockSpec((tm, D_MODEL), lambda i: (i, 0))] + side_out,
        out_shape=[jax.ShapeDtypeStruct((rows, D_MODEL), F32)] + side_shape,
        compiler_params=_params("arbitrary"),
        name="ffn",
    )(x, m, norm_g, w_gu, w_down, *side)


LAM_INIT_0 = 0.8 - 0.6 * math.exp(-0.3 * 0)
CTX_SEQS_PER_STEP = 4


def _trunk(x, m, caches, weights, *, batch, seq, cond_of_batch):
    (norm_g0, w_in, conv_w, conv_b, ln_g, ln_b, lam, subln_g, w_out0, w_gu0, w_down0,
     norm_g1, w_qkv, sink, w_out1, w_gu1, w_down1) = weights
    m0, m1 = m
    ctx = caches is None
    rope = not ctx
    tm = 512
    tm_ffn = 512
    tq = 256
    row_of = lambda t: (lambda i: cond_of_batch((i * t) // seq))

    o0 = 2 * CONV_CH
    ko, vo = o0 + DIFF_WIDTH, o0 + 2 * DIFF_WIDTH
    outs0 = [("nat", 0, o0, F32, False, None), ("nat", o0, DIFF_WIDTH, BF16, True, Q_SCALE),
             ("nat", ko, DIFF_WIDTH, BF16, True, None), ("T", vo, DIFF_WIDTH, BF16, False, None)]
    if ctx:
        outs0 += [("T", ko, DIFF_WIDTH, F32, False, None), ("heads", vo, DIFF_WIDTH, F32, False, None)]
    res = _pre(x, m0, norm_g0, w_in, tuple(outs0), tm=tm, cond_row=row_of(tm), seq=seq, rope=rope)
    u, q, k, vt = res[:4]
    new0 = tuple(res[4:])
    x, *conv0 = _mix0(u, q, k, vt, None if ctx else caches[0], x, m0, norm_g0, conv_w, conv_b,
                      ln_g, ln_b, lam, subln_g, w_out0, batch=batch, seq=seq, tq=tq, cond_row=cond_of_batch,
                      lam_init=LAM_INIT_0, side=(w_gu0, w_down0) if ctx else (),
                      nseq=CTX_SEQS_PER_STEP if ctx else 1)
    if ctx:
        w_gu0, w_down0 = conv0
    x, *conv1 = _ffn(x, m0, norm_g0, w_gu0, w_down0, tm=tm_ffn, cond_row=row_of(tm_ffn),
                     side=(w_qkv, w_out1, w_gu1, w_down1) if ctx else ())
    if ctx:
        w_qkv, w_out1, w_gu1, w_down1 = conv1

    outs1 = [("nat", 0, Q_WIDTH, BF16, True, Q_SCALE), ("nat", Q_WIDTH, KV_WIDTH, BF16, True, None)]
    if ctx:
        outs1 += [("T", Q_WIDTH + KV_WIDTH, KV_WIDTH, F32, False, None), ("T", Q_WIDTH, KV_WIDTH, F32, False, None)]
    else:
        outs1 += [("T", Q_WIDTH + KV_WIDTH, KV_WIDTH, BF16, False, None)]
    res = _pre(x, m1, norm_g1, w_qkv, tuple(outs1), tm=tm, cond_row=row_of(tm), seq=seq, rope=rope)
    q, k, vt = res[:3]
    new1 = (res[3], vt) if ctx else ()
    x = _mix1(q, k, vt, None if ctx else caches[1], x, m1, norm_g1, sink, w_out1,
              batch=batch, seq=seq, tq=tq, cond_row=cond_of_batch, nseq=CTX_SEQS_PER_STEP if ctx else 1)
    x, = _ffn(x, m1, norm_g1, w_gu1, w_down1, tm=tm_ffn, cond_row=row_of(tm_ffn))
    return x, new0 + new1, (w_gu0, w_down0, w_qkv, w_out1, w_gu1, w_down1)


def kernel(x_prompt, x_sample, cache_k0, cache_v0, cache_k1, cache_v1, c, c_ctx, l0_mod_w, l0_mod_b, l0_norm_g, l0_w_in, l0_conv_w, l0_conv_b, l0_conv_ln_g, l0_conv_ln_b, l0_lambda, l0_subln_g, l0_w_out, l0_w_gu, l0_w_down, l1_mod_w, l1_mod_b, l1_norm_g, l1_w_qkv, l1_sink, l1_w_out, l1_w_gu, l1_w_down):
    batch, seq, d = x_prompt.shape
    dec_batch, dec_seq, _ = x_sample.shape
    n_past = cache_k0.shape[1]

    cond = jnp.concatenate([c_ctx[None, :], c, jnp.zeros((MOD_ROWS - 1 - dec_batch, d), F32)], axis=0)
    m = (_modulation(cond, l0_mod_w, l0_mod_b), _modulation(cond, l1_mod_w, l1_mod_b))

    row = lambda v: v.reshape(1, -1)
    layer0 = (l0_norm_g, l0_w_in.astype(BF16), jnp.broadcast_to(l0_conv_w, (CONV_WIDTH, 8, CONV_CH)), row(l0_conv_b),
              row(l0_conv_ln_g), row(l0_conv_ln_b), l0_lambda, l0_subln_g.reshape(LANES, 1),
              l0_w_out.astype(BF16))
    weights = layer0 + (l0_w_gu, l0_w_down, l1_norm_g, l1_w_qkv, l1_sink, l1_w_out, l1_w_gu, l1_w_down)

    y_prompt, (kt0, v0, kt1, vt1), bf16_w = _trunk(x_prompt.reshape(batch * seq, d), m, None, weights,
                                                   batch=batch, seq=seq, cond_of_batch=lambda b: 0)
    w_gu0, w_down0, w_qkv, w_out1, w_gu1, w_down1 = bf16_w
    weights = layer0 + (w_gu0, w_down0, l1_norm_g, w_qkv, l1_sink, w_out1, w_gu1, w_down1)
    caches = ((cache_k0.reshape(dec_batch, n_past, -1),
               jnp.swapaxes(cache_v0.reshape(dec_batch, n_past, -1), 1, 2)),
              (cache_k1.reshape(dec_batch, n_past, -1),
               jnp.swapaxes(cache_v1.reshape(dec_batch, n_past, -1), 1, 2)))
    y_sample, _, _ = _trunk(x_sample.reshape(dec_batch * dec_seq, d), m, caches, weights,
                            batch=dec_batch, seq=dec_seq, cond_of_batch=lambda b: 1 + b)
    new_k0 = kt0.reshape(batch, DIFF_HEADS, 2, HEAD_DIM, seq).transpose(0, 4, 1, 2, 3)
    new_v0 = v0.reshape(batch, seq, DIFF_HEADS, 2 * HEAD_DIM)
    new_k1 = kt1.reshape(batch, GQA_KV_HEADS, HEAD_DIM, seq).transpose(0, 3, 1, 2)
    new_v1 = vt1.reshape(batch, GQA_KV_HEADS, HEAD_DIM, seq).transpose(0, 3, 1, 2)
    return (y_prompt.reshape(batch, seq, d), y_sample.reshape(dec_batch, dec_seq, d),
            new_k0, new_v0, new_k1, new_v1)
```

```python
import functools
import math

import numpy as np
import jax
import jax.numpy as jnp
from jax import lax
from jax.experimental import pallas as pl
from jax.experimental.pallas import tpu as pltpu

D_MODEL = 1024
GRID_W = 64
ROPE_THETA = 10000.0
EPS = 1e-6
NEG_INF = -1e30
LOG2E = math.log2(math.e)
CONV_CH = 512
CONV_WIDTH = 31
CONV_PAD = 16
DIFF_HEADS = 4
DIFF_WIDTH = 512
HEAD_DIM = 64
GQA_HEADS = 16
GQA_KV_HEADS = 4
GQA_GROUP = 4
Q_WIDTH = GQA_HEADS * HEAD_DIM
KV_WIDTH = GQA_KV_HEADS * HEAD_DIM
Q_SCALE = LOG2E * HEAD_DIM ** -0.5
WINDOW = 128
FFN_HIDDEN = 2816
N_MOD = 6
LANES = 128
BF16_ROWS = 16
MOD_ROWS = 8
VMEM_LIMIT = 56 * 1024 * 1024

BF16 = jnp.bfloat16
F32 = jnp.float32


def _sigmoid(x):
    return 1.0 / (1.0 + jnp.exp(-x))


def _rms(x, g):
    return x * lax.rsqrt(jnp.mean(x * x, axis=-1, keepdims=True) + EPS) * g


def _dot(a, b):
    return jnp.dot(a, b, preferred_element_type=F32)


def _dot_nt(a, b):
    return lax.dot_general(a, b, (((1,), (1,)), ((), ())), preferred_element_type=F32)


def _params(*sem):
    return pltpu.CompilerParams(dimension_semantics=sem, vmem_limit_bytes=VMEM_LIMIT)


def _resident(shape):
    return pl.BlockSpec(shape, lambda *_: (0,) * len(shape), pipeline_mode=pl.Buffered(1))


def _mod_kernel(c_ref, w_ref, b_ref, o_ref):
    c = c_ref[...]
    s = (c * _sigmoid(c)).astype(BF16)
    o_ref[...] = _dot(s, w_ref[...].astype(BF16)) + b_ref[...]


def _modulation(cond, mod_w, mod_b):
    n = mod_w.shape[1]
    tn = 1536
    out = pl.pallas_call(
        _mod_kernel,
        grid=(n // tn,),
        in_specs=[
            pl.BlockSpec((MOD_ROWS, D_MODEL), lambda j: (0, 0)),
            pl.BlockSpec((D_MODEL, tn), lambda j: (0, j)),
            pl.BlockSpec((1, tn), lambda j: (0, j)),
        ],
        out_specs=pl.BlockSpec((MOD_ROWS, tn), lambda j: (0, j)),
        out_shape=jax.ShapeDtypeStruct((MOD_ROWS, n), F32),
        compiler_params=_params("arbitrary"),
        name="modulation",
    )(cond, mod_w, mod_b.reshape(1, n))
    return out.reshape(MOD_ROWS, N_MOD, D_MODEL)


def _rope_tables(seq):
    t = np.arange(seq)
    rows, cols = t // GRID_W, t % GRID_W
    half = HEAD_DIM // 2
    inv = 1.0 / (ROPE_THETA ** (np.arange(0, half, 2, dtype=np.float64) / half))
    ar = rows[:, None] * inv[None, :]
    ac = cols[:, None] * inv[None, :]
    cos = np.concatenate([np.cos(ar), np.cos(ar), np.cos(ac), np.cos(ac)], axis=1)
    sin = np.concatenate([-np.sin(ar), np.sin(ar), -np.sin(ac), np.sin(ac)], axis=1)
    reps = LANES // HEAD_DIM
    return (jnp.asarray(np.tile(cos, (1, reps)), F32), jnp.asarray(np.tile(sin, (1, reps)), F32))


def _rope(x, cos, sin):
    quarter = HEAD_DIM // 4
    lane = lax.broadcasted_iota(jnp.int32, x.shape, 1)
    lo = (lane % (2 * quarter)) < quarter
    partner = jnp.where(lo, pltpu.roll(x, LANES - quarter, 1), pltpu.roll(x, quarter, 1))
    return x * cos + partner * sin


def _pre_kernel(*refs, outs, rope, sblk):
    n_in = 6 if rope else 4
    x_ref, m_ref, g_ref, w_ref = refs[:4]
    _project(x_ref[...], m_ref, g_ref, w_ref, refs[4:6] if rope else None, outs, refs[n_in:], sblk)


def _project(x, m_ref, g_ref, w_ref, trig, outs, o_refs, sblk):
    rope = trig is not None
    tm = x.shape[0]
    h = (_rms(x, g_ref[0:1, :]) * (1.0 + m_ref[1:2, :]) + m_ref[0:1, :]).astype(BF16)
    cost = lambda o: o[2] * {"T": 5.0, "heads": 0.5}.get(o[0], 3.0 if (rope and o[4]) else 1.0)
    ranges = sorted({(o[1], o[2]) for o in outs},
                    key=lambda r: (-sum(cost(o) for o in outs if (o[1], o[2]) == r), r[0]))
    for start, width in ranges:
        p = _dot(h, w_ref[:, start:start + width])
        for (kind, s0, w0, _, roped, scale), o_ref in zip(outs, o_refs):
            if (s0, w0) != (start, width):
                continue
            if kind == "nat":
                for c in range(width // LANES):
                    xc = p[:, c * LANES:(c + 1) * LANES]
                    if rope and roped:
                        xc = _rope(xc, trig[0][...], trig[1][...])
                    if scale is not None:
                        xc = xc * scale
                    o_ref[:, c * LANES:(c + 1) * LANES] = xc.astype(o_ref.dtype)
            elif kind == "T":
                for s in range(tm // sblk):
                    o_ref[s] = p[s * sblk:(s + 1) * sblk, :].T.astype(o_ref.dtype)
            else:
                heads = width // LANES
                for hd in range(heads):
                    o_ref[pl.ds(hd, tm, stride=heads), :] = p[:, hd * LANES:(hd + 1) * LANES]


def _project_specs(m, norm_g, w, outs, *, rows, tm, cond_row, seq, rope):
    n = w.shape[1]
    sblk = min(seq, tm)
    per_seq = seq // sblk
    in_specs = [
        pl.BlockSpec((None, N_MOD, D_MODEL), lambda i: (cond_row(i), 0, 0)),
        pl.BlockSpec((4, D_MODEL), lambda i: (0, 0)),
        _resident((D_MODEL, n)),
    ]
    args = [m, norm_g, w]
    if rope:
        cos, sin = _rope_tables(seq)
        nblk = seq // tm
        in_specs += [pl.BlockSpec((tm, LANES), lambda i: (i % nblk, 0))] * 2
        args += [cos, sin]
    out_specs, out_shape = [], []
    for (kind, _, wd, dt, _, _) in outs:
        if kind == "nat":
            out_specs.append(pl.BlockSpec((tm, wd), lambda i: (i, 0)))
            out_shape.append(jax.ShapeDtypeStruct((rows, wd), dt))
        elif kind == "T":
            out_specs.append(pl.BlockSpec((tm // sblk, wd, sblk), lambda i: (i // per_seq, 0, i % per_seq)))
            out_shape.append(jax.ShapeDtypeStruct((rows // seq, wd, seq), dt))
        else:
            out_specs.append(pl.BlockSpec((tm * (wd // LANES), LANES), lambda i: (i, 0)))
            out_shape.append(jax.ShapeDtypeStruct((rows * (wd // LANES), LANES), dt))
    return in_specs, args, out_specs, out_shape, sblk


def _pre(x, m, norm_g, w, outs, *, tm, cond_row, seq, rope):
    rows = x.shape[0]
    in_specs = [pl.BlockSpec((tm, D_MODEL), lambda i: (i, 0))]
    proj_in, proj_args, out_specs, out_shape, sblk = _project_specs(m, norm_g, w, outs, rows=rows, tm=tm,
                                                                    cond_row=cond_row, seq=seq, rope=rope)
    in_specs += proj_in
    args = [x] + proj_args
    return pl.pallas_call(
        functools.partial(_pre_kernel, outs=outs, rope=rope, sblk=sblk),
        grid=(rows // tm,),
        in_specs=in_specs,
        out_specs=out_specs,
        out_shape=out_shape,
        compiler_params=_params("arbitrary"),
        name="pre_rope" if rope else "pre",
    )(*args)


def _attend(s_list, vt_list, floor=None):
    mx = None
    for s in s_list:
        smax = jnp.max(s, axis=0, keepdims=True)
        mx = smax if mx is None else jnp.maximum(mx, smax)
    if floor is not None:
        mx = jnp.maximum(mx, floor)
    d = vt_list[0].shape[0]
    ov = None
    for s, vt in zip(s_list, vt_list):
        ones = jnp.ones((BF16_ROWS, vt.shape[1]), BF16)
        pv = _dot(jnp.concatenate([vt, ones], axis=0), jnp.exp2(s - mx).astype(BF16))
        ov = pv if ov is None else ov + pv
    return ov[0:d, :], ov[d:d + 1, :], mx


def _pipelined(n, first, second):
    out = []
    nxt = first(0)
    for u in range(n):
        cur = nxt
        if u + 1 < n:
            nxt = first(u + 1)
        out.append(second(u, cur))
    return out


def _side_convert(src_refs, dst_refs):
    for src, dst in zip(src_refs, dst_refs):
        dst[...] = src[...].astype(BF16)


def _mix0_kernel(*refs, seq, tq, n_cache, lam_init, n_side, nseq):
    n_in = 16 if n_cache else 14
    u_ref, q_ref, k_ref, vt_ref = refs[:4]
    ck_ref, cvt_ref = refs[4:6] if n_cache else (None, None)
    x_ref, m_ref, ng_ref, cw_ref, cb_ref, lg_ref, lb_ref, lam_ref, sg_ref, wo_ref = refs[n_in - 10:n_in]
    o_ref = refs[n_in + n_side]
    z_s, zw_s, zsh_s = refs[n_in + 2 * n_side + 1:]
    _side_convert(refs[n_in:n_in + n_side], refs[n_in + n_side + 1:n_in + 2 * n_side + 1])
    qi = pl.program_id(1)
    whole = tq == seq
    win = tq + 2 * CONV_PAD
    span = win - 8
    chunk = 64
    off0 = CONV_PAD - CONV_WIDTH // 2
    la = jnp.sum(lam_ref[0:1, :] * lam_ref[1:2, :], axis=-1, keepdims=True)
    lb = jnp.sum(lam_ref[2:3, :] * lam_ref[3:4, :], axis=-1, keepdims=True)
    lam = jnp.exp(la) - jnp.exp(lb) + lam_init
    sub = LANES // 2
    lane = lax.broadcasted_iota(jnp.int32, (tq, LANES), 1)

    def block(s):
        urows = slice(s * seq, (s + 1) * seq)

        def glu():
            z_s[0:CONV_PAD, :] = jnp.zeros((CONV_PAD, CONV_CH), F32)
            z_s[CONV_PAD + seq:2 * CONV_PAD + seq, :] = jnp.zeros((CONV_PAD, CONV_CH), F32)
            z_s[CONV_PAD:CONV_PAD + seq, :] = u_ref[urows, 0:CONV_CH] * _sigmoid(u_ref[urows, CONV_CH:2 * CONV_CH])

        if whole:
            glu()
            zw = z_s
        else:
            pl.when(qi == 0)(glu)
            zw_s[...] = z_s[pl.ds(pl.multiple_of(qi * tq, tq), win), :]
            zw = zw_s

        for b in range(1, 8):
            zsh_s[b - 1, 0:span, :] = zw[b:b + span, :]
        conv_rows = []
        for c0 in range(0, tq, chunk):
            acc = jnp.zeros((chunk // 8, 8, CONV_CH), F32)
            for j in range(CONV_WIDTH):
                off = j + off0
                a, b = off // 8, off % 8
                if b == 0:
                    zz = zw[c0 + 8 * a:c0 + 8 * a + chunk, :]
                else:
                    zz = zsh_s[b - 1, c0 + 8 * a:c0 + 8 * a + chunk, :]
                acc = acc + zz.reshape(chunk // 8, 8, CONV_CH) * cw_ref[j]
            acc = acc.reshape(chunk, CONV_CH) + cb_ref[...]
            mu = jnp.mean(acc, axis=-1, keepdims=True)
            xc = acc - mu
            y = xc * lax.rsqrt(jnp.mean(xc * xc, axis=-1, keepdims=True) + EPS) * lg_ref[...] + lb_ref[...]
            conv_rows.append((y * _sigmoid(y)).astype(BF16))

        def segments(h):
            cols = slice(h * LANES, (h + 1) * LANES)
            segs = [(k_ref[urows, cols], vt_ref[s, cols, :])]
            if n_cache:
                segs.append((ck_ref[:, cols].astype(BF16), cvt_ref[cols, :].astype(BF16)))
            return segs

        def scores(h):
            qc = q_ref[s * tq:(s + 1) * tq, h * LANES:(h + 1) * LANES]
            zero = jnp.zeros_like(qc)
            qq = jnp.concatenate([jnp.where(lane < sub, qc, zero), jnp.where(lane >= sub, qc, zero)], axis=0)
            return [_dot_nt(kk, qq) for kk, _ in segments(h)]

        def head(h, s_list):
            ov, den, _ = _attend(s_list, [vt for _, vt in segments(h)])
            r = 1.0 / den
            o = ov[:, 0:tq] * r[:, 0:tq] - ov[:, tq:2 * tq] * (lam * r[:, tq:2 * tq])
            return o * lax.rsqrt(jnp.mean(o * o, axis=0, keepdims=True) + EPS) * sg_ref[...] * (1.0 - lam_init)

        heads = _pipelined(DIFF_HEADS, scores, head)
        attn = jnp.concatenate(heads, axis=0).T.astype(BF16)
        return jnp.concatenate([jnp.concatenate(conv_rows, axis=0), attn], axis=1)

    cat = jnp.concatenate([block(s) for s in range(nseq)], axis=0)
    mixed = _dot(cat, wo_ref[...])
    o_ref[...] = x_ref[...] + m_ref[2:3, :] * _rms(mixed, ng_ref[1:2, :])


def _side_specs(side, steps, index):
    in_specs, out_specs, out_shape = [], [], []
    for w in side:
        rows, cols = w.shape
        n_blk = math.gcd(steps, rows // BF16_ROWS)
        per = steps // n_blk
        spec = pl.BlockSpec((rows // n_blk, cols), lambda *ids, per=per: (index(*ids) // per, 0))
        in_specs.append(spec)
        out_specs.append(spec)
        out_shape.append(jax.ShapeDtypeStruct(w.shape, BF16))
    return in_specs, out_specs, out_shape


def _mix0(u, q, k, vt, cache, x, m, norm_g, conv_w, conv_b, ln_g, ln_b, lam, subln_g, w_out,
          *, batch, seq, tq, cond_row, lam_init, side=(), nseq=1):
    nq = seq // tq
    n_cache = 0 if cache is None else cache[0].shape[1]
    win = tq + 2 * CONV_PAD
    steps = (batch // nseq) * nq
    const = lambda shape: pl.BlockSpec(shape, lambda b, i: (0,) * len(shape))
    in_specs = [pl.BlockSpec((nseq * seq, 2 * CONV_CH), lambda b, i: (b, 0)),
                pl.BlockSpec((nseq * tq, DIFF_WIDTH), lambda b, i: (b * nq + i, 0)),
                pl.BlockSpec((nseq * seq, DIFF_WIDTH), lambda b, i: (b, 0)),
                pl.BlockSpec((nseq, DIFF_WIDTH, seq), lambda b, i: (b, 0, 0))]
    args = [u, q, k, vt]
    if n_cache:
        in_specs += [pl.BlockSpec((None, n_cache, DIFF_WIDTH), lambda b, i: (b, 0, 0)),
                     pl.BlockSpec((None, DIFF_WIDTH, n_cache), lambda b, i: (b, 0, 0))]
        args += [cache[0], cache[1]]
    in_specs += [
        pl.BlockSpec((nseq * tq, D_MODEL), lambda b, i: (b * nq + i, 0)),
        pl.BlockSpec((None, N_MOD, D_MODEL), lambda b, i: (cond_row(b * nseq), 0, 0)),
        const((4, D_MODEL)), const((CONV_WIDTH, 8, CONV_CH)), const((1, CONV_CH)), const((1, CONV_CH)),
        const((1, CONV_CH)), const((4, LANES // 2)), const((LANES, 1)), _resident((D_MODEL, D_MODEL)),
    ]
    args += [x, m, norm_g, conv_w, conv_b, ln_g, ln_b, lam, subln_g, w_out]
    side_in, side_out, side_shape = _side_specs(side, steps, lambda b, i: b * nq + i)
    return pl.pallas_call(
        functools.partial(_mix0_kernel, seq=seq, tq=tq, n_cache=n_cache, lam_init=lam_init, n_side=len(side),
                          nseq=nseq),
        grid=(batch // nseq, nq),
        in_specs=in_specs + side_in,
        out_specs=[pl.BlockSpec((nseq * tq, D_MODEL), lambda b, i: (b * nq + i, 0))] + side_out,
        out_shape=[jax.ShapeDtypeStruct((batch * seq, D_MODEL), F32)] + side_shape,
        scratch_shapes=[
            pltpu.VMEM((seq + 2 * CONV_PAD, CONV_CH), F32), pltpu.VMEM((win, CONV_CH), F32),
            pltpu.VMEM((7, win - 8, CONV_CH), F32),
        ],
        compiler_params=_params("arbitrary", "arbitrary"),
        name="mix_conv_diff",
    )(*args, *side)


def _mix1_kernel(*refs, seq, tq, n_cache, nseq):
    if n_cache:
        (q_ref, kp_ref, kc_ref, kn_ref, vtp_ref, vtc_ref, vtn_ref, ck_ref, cvt_ref,
         x_ref, m_ref, ng_ref, sink_ref, wo_ref, o_ref) = refs
    else:
        (q_ref, kc_ref, vtc_ref, x_ref, m_ref, ng_ref, sink_ref, wo_ref, o_ref) = refs
    qi = pl.program_id(1)
    lane = lax.broadcasted_iota(jnp.int32, (tq, LANES), 1)

    def band(base, rows):
        kpos = base + lax.broadcasted_iota(jnp.int32, (rows, tq), 0)
        qpos = qi * tq + lax.broadcasted_iota(jnp.int32, (rows, tq), 1)
        ok = (jnp.abs(qpos - kpos) <= WINDOW) & (kpos >= 0) & (kpos < seq)
        return jnp.concatenate([ok] * GQA_GROUP, axis=1)

    def attention(r0, segs):
        def scores(g):
            b = g % 2
            kcols = slice((g // 2) * LANES, (g // 2 + 1) * LANES)
            keep = (lane < HEAD_DIM) if b == 0 else (lane >= HEAD_DIM)
            qs = []
            for h in range(g * GQA_GROUP, (g + 1) * GQA_GROUP):
                qc = q_ref[r0:r0 + tq, (h // 2) * LANES:(h // 2 + 1) * LANES]
                qa = qc if h % 2 == b else pltpu.roll(qc, HEAD_DIM, 1)
                qs.append(jnp.where(keep, qa, jnp.zeros_like(qa)))
            qg = jnp.concatenate(qs, axis=0)
            s_list = []
            for k_ref, _, ok in segs:
                s = _dot_nt(k_ref[:, kcols].astype(BF16), qg)
                s_list.append(s if ok is None else jnp.where(ok, s, NEG_INF))
            return s_list

        def group(g, s_list):
            vrows = slice(g * HEAD_DIM, (g + 1) * HEAD_DIM)
            sink = jnp.concatenate([jnp.full((1, tq), sink_ref[h] * LOG2E, F32)
                                    for h in range(g * GQA_GROUP, (g + 1) * GQA_GROUP)], axis=1)
            ov, den, mx = _attend(s_list, [vt_ref[vrows, :].astype(BF16) for _, vt_ref, _ in segs], floor=sink)
            ov = ov * (1.0 / (den + jnp.exp2(sink - mx)))
            return jnp.concatenate([ov[:, hh * tq:(hh + 1) * tq] for hh in range(GQA_GROUP)], axis=0)

        groups = _pipelined(GQA_KV_HEADS, scores, group)
        return jnp.concatenate(groups, axis=0).T.astype(BF16)

    if n_cache:
        attn = attention(0, [(kp_ref, vtp_ref, band(qi * tq - WINDOW, WINDOW)),
                             (kc_ref, vtc_ref, band(qi * tq, tq)),
                             (kn_ref, vtn_ref, band(qi * tq + tq, WINDOW)),
                             (ck_ref, cvt_ref, None)])
    else:
        attn = jnp.concatenate([attention(s * tq, [(kc_ref.at[pl.ds(s * tq, tq)], vtc_ref.at[s], None)])
                                for s in range(nseq)], axis=0)
    mixed = _dot(attn, wo_ref[...])
    o_ref[...] = x_ref[...] + m_ref[2:3, :] * _rms(mixed, ng_ref[1:2, :])


def _mix1(q, k, vt, cache, x, m, norm_g, sink, w_out, *, batch, seq, tq, cond_row, nseq=1):
    nq = seq // tq
    n_cache = 0 if cache is None else cache[0].shape[1]
    rows = nseq * tq
    q_spec = pl.BlockSpec((rows, D_MODEL), lambda b, i: (b * nq + i, 0))
    kc_spec = pl.BlockSpec((rows, KV_WIDTH), lambda b, i: (b * nq + i, 0))
    vtc_spec = pl.BlockSpec((None, KV_WIDTH, tq), lambda b, i: (b, 0, i))
    if n_cache:
        nw = seq // WINDOW
        per = tq // WINDOW
        prev = lambda i: jnp.maximum(i * per - 1, 0)
        nxt = lambda i: jnp.minimum(i * per + per, nw - 1)
        in_specs = [q_spec,
                    pl.BlockSpec((WINDOW, KV_WIDTH), lambda b, i: (b * nw + prev(i), 0)), kc_spec,
                    pl.BlockSpec((WINDOW, KV_WIDTH), lambda b, i: (b * nw + nxt(i), 0)),
                    pl.BlockSpec((None, KV_WIDTH, WINDOW), lambda b, i: (b, 0, prev(i))), vtc_spec,
                    pl.BlockSpec((None, KV_WIDTH, WINDOW), lambda b, i: (b, 0, nxt(i))),
                    pl.BlockSpec((None, n_cache, KV_WIDTH), lambda b, i: (b, 0, 0)),
                    pl.BlockSpec((None, KV_WIDTH, n_cache), lambda b, i: (b, 0, 0))]
        args = [q, k, k, k, vt, vt, vt, cache[0], cache[1]]
    else:
        in_specs = [q_spec, kc_spec, pl.BlockSpec((nseq, KV_WIDTH, tq), lambda b, i: (b, 0, 0))]
        args = [q, k, vt]
    in_specs += [
        pl.BlockSpec((rows, D_MODEL), lambda b, i: (b * nq + i, 0)),
        pl.BlockSpec((None, N_MOD, D_MODEL), lambda b, i: (cond_row(b * nseq), 0, 0)),
        pl.BlockSpec((4, D_MODEL), lambda b, i: (0, 0)),
        pl.BlockSpec(memory_space=pltpu.SMEM),
        pl.BlockSpec((D_MODEL, D_MODEL), lambda b, i: (0, 0)),
    ]
    args += [x, m, norm_g, sink, w_out]
    return pl.pallas_call(
        functools.partial(_mix1_kernel, seq=seq, tq=tq, n_cache=n_cache, nseq=nseq),
        grid=(batch // nseq, nq),
        in_specs=in_specs,
        out_specs=pl.BlockSpec((rows, D_MODEL), lambda b, i: (b * nq + i, 0)),
        out_shape=jax.ShapeDtypeStruct((batch * seq, D_MODEL), F32),
        compiler_params=_params("arbitrary", "arbitrary"),
        name="mix_gqa",
    )(*args)


FFN_CHUNKS = (1536, 1280)


def _ffn_kernel(*refs, n_side, proj):
    x_ref, m_ref, ng_ref, wgu_ref, wd_ref = refs[:5]
    n_pin = 0 if proj is None else (5 if proj[1] else 3)
    n_pout = 0 if proj is None else len(proj[0])
    n_in = 5 + n_pin + n_side
    o_ref = refs[n_in]
    _side_convert(refs[5 + n_pin:n_in], refs[n_in + 1 + n_pout:])
    h = (_rms(x_ref[...], ng_ref[2:3, :]) * (1.0 + m_ref[4:5, :]) + m_ref[3:4, :]).astype(BF16)
    acc = jnp.zeros(h.shape, F32)
    c0 = 0
    for width in FFN_CHUNKS:
        gate = _dot(h, wgu_ref[:, c0:c0 + width])
        up = _dot(h, wgu_ref[:, FFN_HIDDEN + c0:FFN_HIDDEN + c0 + width])
        act = (gate * _sigmoid(gate) * up).astype(BF16)
        acc = acc + _dot(act, wd_ref[c0:c0 + width, :])
        c0 += width
    y = x_ref[...] + m_ref[5:6, :] * _rms(acc, ng_ref[3:4, :])
    o_ref[...] = y
    if proj is not None:
        outs, rope, sblk = proj
        pm_ref, pg_ref, pw_ref = refs[5:8]
        _project(y, pm_ref, pg_ref, pw_ref, refs[8:10] if rope else None, outs, refs[n_in + 1:n_in + 1 + n_pout], sblk)


def _ffn(x, m, norm_g, w_gu, w_down, *, tm, cond_row, side=(), proj=None):
    rows = x.shape[0]
    side_in, side_out, side_shape = _side_specs(side, rows // tm, lambda i: i)
    proj_in, proj_args, proj_out, proj_shape, kproj = [], [], [], [], None
    if proj is not None:
        pm, pg, pw, outs, seq, rope = proj
        proj_in, proj_args, proj_out, proj_shape, sblk = _project_specs(pm, pg, pw, outs, rows=rows, tm=tm,
                                                                        cond_row=cond_row, seq=seq, rope=rope)
        kproj = (outs, rope, sblk)
    return pl.pallas_call(
        functools.partial(_ffn_kernel, n_side=len(side), proj=kproj),
        grid=(rows // tm,),
        in_specs=[
            pl.BlockSpec((tm, D_MODEL), lambda i: (i, 0)),
            pl.BlockSpec((None, N_MOD, D_MODEL), lambda i: (cond_row(i), 0, 0)),
            pl.BlockSpec((4, D_MODEL), lambda i: (0, 0)),
            _resident((D_MODEL, 2 * FFN_HIDDEN)),
            _resident((FFN_HIDDEN, D_MODEL)),
        ] + proj_in + side_in,
        out_specs=[pl.BlockSpec((tm, D_MODEL), lambda i: (i, 0))] + proj_out + side_out,
        out_shape=[jax.ShapeDtypeStruct((rows, D_MODEL), F32)] + proj_shape + side_shape,
        compiler_params=_params("arbitrary"),
        name="ffn",
    )(x, m, norm_g, w_gu, w_down, *proj_args, *side)


LAM_INIT_0 = 0.8 - 0.6 * math.exp(-0.3 * 0)
CTX_SEQS_PER_STEP = 4


def _trunk(x, m, caches, weights, *, batch, seq, cond_of_batch):
    (norm_g0, w_in, conv_w, conv_b, ln_g, ln_b, lam, subln_g, w_out0, w_gu0, w_down0,
     norm_g1, w_qkv, sink, w_out1, w_gu1, w_down1) = weights
    m0, m1 = m
    ctx = caches is None
    rope = not ctx
    tm = 512
    tm_ffn = 512
    tq = 256
    row_of = lambda t: (lambda i: cond_of_batch((i * t) // seq))

    o0 = 2 * CONV_CH
    ko, vo = o0 + DIFF_WIDTH, o0 + 2 * DIFF_WIDTH
    outs0 = [("nat", 0, o0, F32, False, None), ("nat", o0, DIFF_WIDTH, BF16, True, Q_SCALE),
             ("nat", ko, DIFF_WIDTH, BF16, True, None), ("T", vo, DIFF_WIDTH, BF16, False, None)]
    if ctx:
        outs0 += [("T", ko, DIFF_WIDTH, F32, False, None), ("heads", vo, DIFF_WIDTH, F32, False, None)]
    res = _pre(x, m0, norm_g0, w_in, tuple(outs0), tm=tm, cond_row=row_of(tm), seq=seq, rope=rope)
    u, q, k, vt = res[:4]
    new0 = tuple(res[4:])
    x, *conv0 = _mix0(u, q, k, vt, None if ctx else caches[0], x, m0, norm_g0, conv_w, conv_b,
                      ln_g, ln_b, lam, subln_g, w_out0, batch=batch, seq=seq, tq=tq, cond_row=cond_of_batch,
                      lam_init=LAM_INIT_0, side=(w_gu0, w_down0, w_qkv) if ctx else (),
                      nseq=CTX_SEQS_PER_STEP if ctx else 1)
    if ctx:
        w_gu0, w_down0, w_qkv = conv0

    outs1 = [("nat", 0, Q_WIDTH, BF16, True, Q_SCALE), ("nat", Q_WIDTH, KV_WIDTH, BF16, True, None)]
    if ctx:
        outs1 += [("T", Q_WIDTH + KV_WIDTH, KV_WIDTH, F32, False, None), ("T", Q_WIDTH, KV_WIDTH, F32, False, None)]
    else:
        outs1 += [("T", Q_WIDTH + KV_WIDTH, KV_WIDTH, BF16, False, None)]
    res = _ffn(x, m0, norm_g0, w_gu0, w_down0, tm=tm_ffn, cond_row=row_of(tm_ffn),
               side=(w_out1, w_gu1, w_down1) if ctx else (),
               proj=(m1, norm_g1, w_qkv, tuple(outs1), seq, rope))
    x, q, k, vt = res[:4]
    new1 = (res[4], vt) if ctx else ()
    if ctx:
        w_out1, w_gu1, w_down1 = res[5:]
    x = _mix1(q, k, vt, None if ctx else caches[1], x, m1, norm_g1, sink, w_out1,
              batch=batch, seq=seq, tq=tq, cond_row=cond_of_batch, nseq=CTX_SEQS_PER_STEP if ctx else 1)
    x, = _ffn(x, m1, norm_g1, w_gu1, w_down1, tm=tm_ffn, cond_row=row_of(tm_ffn))
    return x, new0 + new1, (w_gu0, w_down0, w_qkv, w_out1, w_gu1, w_down1)


def kernel(x_prompt, x_sample, cache_k0, cache_v0, cache_k1, cache_v1, c, c_ctx, l0_mod_w, l0_mod_b, l0_norm_g, l0_w_in, l0_conv_w, l0_conv_b, l0_conv_ln_g, l0_conv_ln_b, l0_lambda, l0_subln_g, l0_w_out, l0_w_gu, l0_w_down, l1_mod_w, l1_mod_b, l1_norm_g, l1_w_qkv, l1_sink, l1_w_out, l1_w_gu, l1_w_down):
    batch, seq, d = x_prompt.shape
    dec_batch, dec_seq, _ = x_sample.shape
    n_past = cache_k0.shape[1]

    cond = jnp.concatenate([c_ctx[None, :], c, jnp.zeros((MOD_ROWS - 1 - dec_batch, d), F32)], axis=0)
    m = (_modulation(cond, l0_mod_w, l0_mod_b), _modulation(cond, l1_mod_w, l1_mod_b))

    row = lambda v: v.reshape(1, -1)
    layer0 = (l0_norm_g, l0_w_in.astype(BF16), jnp.broadcast_to(l0_conv_w, (CONV_WIDTH, 8, CONV_CH)), row(l0_conv_b),
              row(l0_conv_ln_g), row(l0_conv_ln_b), l0_lambda, l0_subln_g.reshape(LANES, 1),
              l0_w_out.astype(BF16))
    weights = layer0 + (l0_w_gu, l0_w_down, l1_norm_g, l1_w_qkv, l1_sink, l1_w_out, l1_w_gu, l1_w_down)

    y_prompt, (kt0, v0, kt1, vt1), bf16_w = _trunk(x_prompt.reshape(batch * seq, d), m, None, weights,
                                                   batch=batch, seq=seq, cond_of_batch=lambda b: 0)
    w_gu0, w_down0, w_qkv, w_out1, w_gu1, w_down1 = bf16_w
    weights = layer0 + (w_gu0, w_down0, l1_norm_g, w_qkv, l1_sink, w_out1, w_gu1, w_down1)
    caches = ((cache_k0.reshape(dec_batch, n_past, -1),
               jnp.swapaxes(cache_v0.reshape(dec_batch, n_past, -1), 1, 2)),
              (cache_k1.reshape(dec_batch, n_past, -1),
               jnp.swapaxes(cache_v1.reshape(dec_batch, n_past, -1), 1, 2)))
    y_sample, _, _ = _trunk(x_sample.reshape(dec_batch * dec_seq, d), m, caches, weights,
                            batch=dec_batch, seq=dec_seq, cond_of_batch=lambda b: 1 + b)
    new_k0 = kt0.reshape(batch, DIFF_HEADS, 2, HEAD_DIM, seq).transpose(0, 4, 1, 2, 3)
    new_v0 = v0.reshape(batch, seq, DIFF_HEADS, 2 * HEAD_DIM)
    new_k1 = kt1.reshape(batch, GQA_KV_HEADS, HEAD_DIM, seq).transpose(0, 3, 1, 2)
    new_v1 = vt1.reshape(batch, GQA_KV_HEADS, HEAD_DIM, seq).transpose(0, 3, 1, 2)
    return (y_prompt.reshape(batch, seq, d), y_sample.reshape(dec_batch, dec_seq, d),
            new_k0, new_v0, new_k1, new_v1)
```

```python
import functools
import math

import numpy as np
import jax
import jax.numpy as jnp
from jax import lax
from jax.experimental import pallas as pl
from jax.experimental.pallas import tpu as pltpu

D_MODEL = 1024
GRID_W = 64
ROPE_THETA = 10000.0
EPS = 1e-6
NEG_INF = -1e30
LOG2E = math.log2(math.e)
CONV_CH = 512
CONV_WIDTH = 31
CONV_PAD = 16
DIFF_HEADS = 4
DIFF_WIDTH = 512
HEAD_DIM = 64
GQA_HEADS = 16
GQA_KV_HEADS = 4
GQA_GROUP = 4
Q_WIDTH = GQA_HEADS * HEAD_DIM
KV_WIDTH = GQA_KV_HEADS * HEAD_DIM
Q_SCALE = LOG2E * HEAD_DIM ** -0.5
WINDOW = 128
FFN_HIDDEN = 2816
N_MOD = 6
LANES = 128
BF16_ROWS = 16
MOD_ROWS = 8
VMEM_LIMIT = 56 * 1024 * 1024

BF16 = jnp.bfloat16
F32 = jnp.float32


def _sigmoid(x):
    return 0.5 * jnp.tanh(0.5 * x) + 0.5


def _rms(x, g):
    return x * lax.rsqrt(jnp.mean(x * x, axis=-1, keepdims=True) + EPS) * g


def _dot(a, b):
    return jnp.dot(a, b, preferred_element_type=F32)


def _dot_nt(a, b):
    return lax.dot_general(a, b, (((1,), (1,)), ((), ())), preferred_element_type=F32)


def _params(*sem):
    return pltpu.CompilerParams(dimension_semantics=sem, vmem_limit_bytes=VMEM_LIMIT)


def _resident(shape):
    return pl.BlockSpec(shape, lambda *_: (0,) * len(shape), pipeline_mode=pl.Buffered(1))


def _mod_kernel(c_ref, w_ref, b_ref, o_ref):
    c = c_ref[...]
    s = (c * _sigmoid(c)).astype(BF16)
    o_ref[...] = _dot(s, w_ref[...].astype(BF16)) + b_ref[...]


def _modulation(cond, mod_w, mod_b):
    n = mod_w.shape[1]
    tn = 768
    out = pl.pallas_call(
        _mod_kernel,
        grid=(n // tn,),
        in_specs=[
            pl.BlockSpec((MOD_ROWS, D_MODEL), lambda j: (0, 0)),
            pl.BlockSpec((D_MODEL, tn), lambda j: (0, j)),
            pl.BlockSpec((1, tn), lambda j: (0, j)),
        ],
        out_specs=pl.BlockSpec((MOD_ROWS, tn), lambda j: (0, j)),
        out_shape=jax.ShapeDtypeStruct((MOD_ROWS, n), F32),
        compiler_params=_params("arbitrary"),
        name="modulation",
    )(cond, mod_w, mod_b.reshape(1, n))
    return out.reshape(MOD_ROWS, N_MOD, D_MODEL)


def _rope_tables(seq):
    t = np.arange(seq)
    rows, cols = t // GRID_W, t % GRID_W
    half = HEAD_DIM // 2
    inv = 1.0 / (ROPE_THETA ** (np.arange(0, half, 2, dtype=np.float64) / half))
    ar = rows[:, None] * inv[None, :]
    ac = cols[:, None] * inv[None, :]
    cos = np.concatenate([np.cos(ar), np.cos(ar), np.cos(ac), np.cos(ac)], axis=1)
    sin = np.concatenate([-np.sin(ar), np.sin(ar), -np.sin(ac), np.sin(ac)], axis=1)
    reps = LANES // HEAD_DIM
    return (jnp.asarray(np.tile(cos, (1, reps)), F32), jnp.asarray(np.tile(sin, (1, reps)), F32))


def _rope(x, cos, sin):
    quarter = HEAD_DIM // 4
    lane = lax.broadcasted_iota(jnp.int32, x.shape, 1)
    lo = (lane % (2 * quarter)) < quarter
    partner = jnp.where(lo, pltpu.roll(x, LANES - quarter, 1), pltpu.roll(x, quarter, 1))
    return x * cos + partner * sin


def _pre_kernel(*refs, outs, rope, sblk):
    n_in = 6 if rope else 4
    x_ref, m_ref, g_ref, w_ref = refs[:4]
    _project(x_ref[...], m_ref, g_ref, w_ref, refs[4:6] if rope else None, outs, refs[n_in:], sblk)


def _project(x, m_ref, g_ref, w_ref, trig, outs, o_refs, sblk):
    rope = trig is not None
    tm = x.shape[0]
    h = (_rms(x, g_ref[0:1, :] * (1.0 + m_ref[1:2, :])) + m_ref[0:1, :]).astype(BF16)
    cost = lambda o: o[2] * {"T": 5.0, "heads": 0.5}.get(o[0], 3.0 if (rope and o[4]) else 1.0)
    ranges = sorted({(o[1], o[2]) for o in outs},
                    key=lambda r: (-sum(cost(o) for o in outs if (o[1], o[2]) == r), r[0]))
    for start, width in ranges:
        p = _dot(h, w_ref[:, start:start + width])
        for (kind, s0, w0, _, roped, scale), o_ref in zip(outs, o_refs):
            if (s0, w0) != (start, width):
                continue
            if kind == "nat":
                for c in range(width // LANES):
                    xc = p[:, c * LANES:(c + 1) * LANES]
                    if rope and roped:
                        xc = _rope(xc, trig[0][...], trig[1][...])
                    if scale is not None:
                        xc = xc * scale
                    o_ref[:, c * LANES:(c + 1) * LANES] = xc.astype(o_ref.dtype)
            elif kind == "T":
                for s in range(tm // sblk):
                    o_ref[s] = p[s * sblk:(s + 1) * sblk, :].T.astype(o_ref.dtype)
            else:
                heads = width // LANES
                for hd in range(heads):
                    o_ref[pl.ds(hd, tm, stride=heads), :] = p[:, hd * LANES:(hd + 1) * LANES]


def _project_specs(m, norm_g, w, outs, *, rows, tm, cond_row, seq, rope):
    n = w.shape[1]
    sblk = min(seq, tm)
    per_seq = seq // sblk
    in_specs = [
        pl.BlockSpec((None, N_MOD, D_MODEL), lambda i: (cond_row(i), 0, 0)),
        pl.BlockSpec((4, D_MODEL), lambda i: (0, 0)),
        _resident((D_MODEL, n)),
    ]
    args = [m, norm_g, w]
    if rope:
        cos, sin = _rope_tables(seq)
        nblk = seq // tm
        in_specs += [pl.BlockSpec((tm, LANES), lambda i: (i % nblk, 0))] * 2
        args += [cos, sin]
    out_specs, out_shape = [], []
    for (kind, _, wd, dt, _, _) in outs:
        if kind == "nat":
            out_specs.append(pl.BlockSpec((tm, wd), lambda i: (i, 0)))
            out_shape.append(jax.ShapeDtypeStruct((rows, wd), dt))
        elif kind == "T":
            out_specs.append(pl.BlockSpec((tm // sblk, wd, sblk), lambda i: (i // per_seq, 0, i % per_seq)))
            out_shape.append(jax.ShapeDtypeStruct((rows // seq, wd, seq), dt))
        else:
            out_specs.append(pl.BlockSpec((tm * (wd // LANES), LANES), lambda i: (i, 0)))
            out_shape.append(jax.ShapeDtypeStruct((rows * (wd // LANES), LANES), dt))
    return in_specs, args, out_specs, out_shape, sblk


def _pre(x, m, norm_g, w, outs, *, tm, cond_row, seq, rope):
    rows = x.shape[0]
    in_specs = [pl.BlockSpec((tm, D_MODEL), lambda i: (i, 0))]
    proj_in, proj_args, out_specs, out_shape, sblk = _project_specs(m, norm_g, w, outs, rows=rows, tm=tm,
                                                                    cond_row=cond_row, seq=seq, rope=rope)
    in_specs += proj_in
    args = [x] + proj_args
    return pl.pallas_call(
        functools.partial(_pre_kernel, outs=outs, rope=rope, sblk=sblk),
        grid=(rows // tm,),
        in_specs=in_specs,
        out_specs=out_specs,
        out_shape=out_shape,
        compiler_params=_params("arbitrary"),
        name="pre_rope" if rope else "pre",
    )(*args)


def _attend(s_list, vt_list, floor=None):
    mx = None
    for s in s_list:
        smax = jnp.max(s, axis=0, keepdims=True)
        mx = smax if mx is None else jnp.maximum(mx, smax)
    if floor is not None:
        mx = jnp.maximum(mx, floor)
    d = vt_list[0].shape[0]
    ov = None
    for s, vt in zip(s_list, vt_list):
        ones = jnp.ones((BF16_ROWS, vt.shape[1]), BF16)
        pv = _dot(jnp.concatenate([vt, ones], axis=0), jnp.exp2(s - mx).astype(BF16))
        ov = pv if ov is None else ov + pv
    return ov[0:d, :], ov[d:d + 1, :], mx


def _pipelined(n, first, second):
    out = []
    nxt = first(0)
    for u in range(n):
        cur = nxt
        if u + 1 < n:
            nxt = first(u + 1)
        out.append(second(u, cur))
    return out


def _side_convert(src_refs, dst_refs):
    for src, dst in zip(src_refs, dst_refs):
        dst[...] = src[...].astype(BF16)


def _mix0_kernel(*refs, seq, tq, n_cache, lam_init, n_side, nseq):
    n_in = 16 if n_cache else 14
    u_ref, q_ref, k_ref, vt_ref = refs[:4]
    ck_ref, cvt_ref = refs[4:6] if n_cache else (None, None)
    x_ref, m_ref, ng_ref, cw_ref, cb_ref, lg_ref, lb_ref, lam_ref, sg_ref, wo_ref = refs[n_in - 10:n_in]
    o_ref = refs[n_in + n_side]
    z_s, zw_s, zsh_s = refs[n_in + 2 * n_side + 1:]
    _side_convert(refs[n_in:n_in + n_side], refs[n_in + n_side + 1:n_in + 2 * n_side + 1])
    qi = pl.program_id(1)
    whole = tq == seq
    win = tq + 2 * CONV_PAD
    span = win - 8
    chunk = 64
    off0 = CONV_PAD - CONV_WIDTH // 2
    la = jnp.sum(lam_ref[0:1, :] * lam_ref[1:2, :], axis=-1, keepdims=True)
    lb = jnp.sum(lam_ref[2:3, :] * lam_ref[3:4, :], axis=-1, keepdims=True)
    lam = jnp.exp(la) - jnp.exp(lb) + lam_init
    sub = LANES // 2
    lane = lax.broadcasted_iota(jnp.int32, (tq, LANES), 1)

    def block(s):
        urows = slice(s * seq, (s + 1) * seq)

        def glu():
            z_s[0:CONV_PAD, :] = jnp.zeros((CONV_PAD, CONV_CH), F32)
            z_s[CONV_PAD + seq:2 * CONV_PAD + seq, :] = jnp.zeros((CONV_PAD, CONV_CH), F32)
            z_s[CONV_PAD:CONV_PAD + seq, :] = u_ref[urows, 0:CONV_CH] * _sigmoid(u_ref[urows, CONV_CH:2 * CONV_CH])

        if whole:
            glu()
            zw = z_s
        else:
            pl.when(qi == 0)(glu)
            zw_s[...] = z_s[pl.ds(pl.multiple_of(qi * tq, tq), win), :]
            zw = zw_s

        for b in range(1, 8):
            zsh_s[b - 1, 0:span, :] = zw[b:b + span, :]
        conv_rows = []
        for c0 in range(0, tq, chunk):
            acc = jnp.zeros((chunk // 8, 8, CONV_CH), F32)
            for j in range(CONV_WIDTH):
                off = j + off0
                a, b = off // 8, off % 8
                if b == 0:
                    zz = zw[c0 + 8 * a:c0 + 8 * a + chunk, :]
                else:
                    zz = zsh_s[b - 1, c0 + 8 * a:c0 + 8 * a + chunk, :]
                acc = acc + zz.reshape(chunk // 8, 8, CONV_CH) * cw_ref[j]
            acc = acc.reshape(chunk, CONV_CH) + cb_ref[...]
            mu = jnp.mean(acc, axis=-1, keepdims=True)
            xc = acc - mu
            y = xc * lax.rsqrt(jnp.mean(xc * xc, axis=-1, keepdims=True) + EPS) * lg_ref[...] + lb_ref[...]
            conv_rows.append((y * _sigmoid(y)).astype(BF16))

        def segments(h):
            cols = slice(h * LANES, (h + 1) * LANES)
            segs = [(k_ref[urows, cols], vt_ref[s, cols, :])]
            if n_cache:
                segs.append((ck_ref[:, cols].astype(BF16), cvt_ref[cols, :].astype(BF16)))
            return segs

        def scores(h):
            qc = q_ref[s * tq:(s + 1) * tq, h * LANES:(h + 1) * LANES]
            zero = jnp.zeros_like(qc)
            qq = jnp.concatenate([jnp.where(lane < sub, qc, zero), jnp.where(lane >= sub, qc, zero)], axis=0)
            return [_dot_nt(kk, qq) for kk, _ in segments(h)]

        def head(h, s_list):
            ov, den, _ = _attend(s_list, [vt for _, vt in segments(h)])
            r = 1.0 / den
            o = ov[:, 0:tq] * r[:, 0:tq] - ov[:, tq:2 * tq] * (lam * r[:, tq:2 * tq])
            return o * lax.rsqrt(jnp.mean(o * o, axis=0, keepdims=True) + EPS) * sg_ref[...] * (1.0 - lam_init)

        heads = _pipelined(DIFF_HEADS, scores, head)
        attn = jnp.concatenate(heads, axis=0).T.astype(BF16)
        return jnp.concatenate([jnp.concatenate(conv_rows, axis=0), attn], axis=1)

    cat = jnp.concatenate([block(s) for s in range(nseq)], axis=0)
    mixed = _dot(cat, wo_ref[...])
    o_ref[...] = x_ref[...] + _rms(mixed, m_ref[2:3, :] * ng_ref[1:2, :])


def _side_specs(side, steps, index):
    in_specs, out_specs, out_shape = [], [], []
    for w in side:
        rows, cols = w.shape
        n_blk = math.gcd(steps, rows // BF16_ROWS)
        per = steps // n_blk
        spec = pl.BlockSpec((rows // n_blk, cols), lambda *ids, per=per: (index(*ids) // per, 0))
        in_specs.append(spec)
        out_specs.append(spec)
        out_shape.append(jax.ShapeDtypeStruct(w.shape, BF16))
    return in_specs, out_specs, out_shape


def _mix0(u, q, k, vt, cache, x, m, norm_g, conv_w, conv_b, ln_g, ln_b, lam, subln_g, w_out,
          *, batch, seq, tq, cond_row, lam_init, side=(), nseq=1):
    nq = seq // tq
    n_cache = 0 if cache is None else cache[0].shape[1]
    win = tq + 2 * CONV_PAD
    steps = (batch // nseq) * nq
    const = lambda shape: pl.BlockSpec(shape, lambda b, i: (0,) * len(shape))
    in_specs = [pl.BlockSpec((nseq * seq, 2 * CONV_CH), lambda b, i: (b, 0)),
                pl.BlockSpec((nseq * tq, DIFF_WIDTH), lambda b, i: (b * nq + i, 0)),
                pl.BlockSpec((nseq * seq, DIFF_WIDTH), lambda b, i: (b, 0)),
                pl.BlockSpec((nseq, DIFF_WIDTH, seq), lambda b, i: (b, 0, 0))]
    args = [u, q, k, vt]
    if n_cache:
        in_specs += [pl.BlockSpec((None, n_cache, DIFF_WIDTH), lambda b, i: (b, 0, 0)),
                     pl.BlockSpec((None, DIFF_WIDTH, n_cache), lambda b, i: (b, 0, 0))]
        args += [cache[0], cache[1]]
    in_specs += [
        pl.BlockSpec((nseq * tq, D_MODEL), lambda b, i: (b * nq + i, 0)),
        pl.BlockSpec((None, N_MOD, D_MODEL), lambda b, i: (cond_row(b * nseq), 0, 0)),
        const((4, D_MODEL)), const((CONV_WIDTH, 8, CONV_CH)), const((1, CONV_CH)), const((1, CONV_CH)),
        const((1, CONV_CH)), const((4, LANES // 2)), const((LANES, 1)), _resident((D_MODEL, D_MODEL)),
    ]
    args += [x, m, norm_g, conv_w, conv_b, ln_g, ln_b, lam, subln_g, w_out]
    side_in, side_out, side_shape = _side_specs(side, steps, lambda b, i: b * nq + i)
    return pl.pallas_call(
        functools.partial(_mix0_kernel, seq=seq, tq=tq, n_cache=n_cache, lam_init=lam_init, n_side=len(side),
                          nseq=nseq),
        grid=(batch // nseq, nq),
        in_specs=in_specs + side_in,
        out_specs=[pl.BlockSpec((nseq * tq, D_MODEL), lambda b, i: (b * nq + i, 0))] + side_out,
        out_shape=[jax.ShapeDtypeStruct((batch * seq, D_MODEL), F32)] + side_shape,
        scratch_shapes=[
            pltpu.VMEM((seq + 2 * CONV_PAD, CONV_CH), F32), pltpu.VMEM((win, CONV_CH), F32),
            pltpu.VMEM((7, win - 8, CONV_CH), F32),
        ],
        compiler_params=_params("arbitrary", "arbitrary"),
        name="mix_conv_diff",
    )(*args, *side)


def _mix1_kernel(*refs, seq, tq, n_cache, nseq):
    if n_cache:
        (q_ref, kp_ref, kc_ref, kn_ref, vtp_ref, vtc_ref, vtn_ref, ck_ref, cvt_ref,
         x_ref, m_ref, ng_ref, sink_ref, wo_ref, o_ref) = refs
    else:
        (q_ref, kc_ref, vtc_ref, x_ref, m_ref, ng_ref, sink_ref, wo_ref, o_ref) = refs
    qi = pl.program_id(1)
    lane = lax.broadcasted_iota(jnp.int32, (tq, LANES), 1)

    def band(base, rows):
        kpos = base + lax.broadcasted_iota(jnp.int32, (rows, tq), 0)
        qpos = qi * tq + lax.broadcasted_iota(jnp.int32, (rows, tq), 1)
        ok = (jnp.abs(qpos - kpos) <= WINDOW) & (kpos >= 0) & (kpos < seq)
        return jnp.concatenate([ok] * GQA_GROUP, axis=1)

    def attention(r0, segs):
        def scores(g):
            b = g % 2
            kcols = slice((g // 2) * LANES, (g // 2 + 1) * LANES)
            keep = (lane < HEAD_DIM) if b == 0 else (lane >= HEAD_DIM)
            qs = []
            for h in range(g * GQA_GROUP, (g + 1) * GQA_GROUP):
                qc = q_ref[r0:r0 + tq, (h // 2) * LANES:(h // 2 + 1) * LANES]
                qa = qc if h % 2 == b else pltpu.roll(qc, HEAD_DIM, 1)
                qs.append(jnp.where(keep, qa, jnp.zeros_like(qa)))
            qg = jnp.concatenate(qs, axis=0)
            s_list = []
            for k_ref, _, ok in segs:
                s = _dot_nt(k_ref[:, kcols].astype(BF16), qg)
                s_list.append(s if ok is None else jnp.where(ok, s, NEG_INF))
            return s_list

        def group(g, s_list):
            vrows = slice(g * HEAD_DIM, (g + 1) * HEAD_DIM)
            sink = jnp.concatenate([jnp.full((1, tq), sink_ref[h] * LOG2E, F32)
                                    for h in range(g * GQA_GROUP, (g + 1) * GQA_GROUP)], axis=1)
            ov, den, mx = _attend(s_list, [vt_ref[vrows, :].astype(BF16) for _, vt_ref, _ in segs], floor=sink)
            ov = ov * (1.0 / (den + jnp.exp2(sink - mx)))
            return jnp.concatenate([ov[:, hh * tq:(hh + 1) * tq] for hh in range(GQA_GROUP)], axis=0)

        groups = _pipelined(GQA_KV_HEADS, scores, group)
        return jnp.concatenate(groups, axis=0).T.astype(BF16)

    if n_cache:
        attn = attention(0, [(kp_ref, vtp_ref, band(qi * tq - WINDOW, WINDOW)),
                             (kc_ref, vtc_ref, band(qi * tq, tq)),
                             (kn_ref, vtn_ref, band(qi * tq + tq, WINDOW)),
                             (ck_ref, cvt_ref, None)])
    else:
        attn = jnp.concatenate([attention(s * tq, [(kc_ref.at[pl.ds(s * tq, tq)], vtc_ref.at[s], None)])
                                for s in range(nseq)], axis=0)
    mixed = _dot(attn, wo_ref[...])
    o_ref[...] = x_ref[...] + _rms(mixed, m_ref[2:3, :] * ng_ref[1:2, :])


def _mix1(q, k, vt, cache, x, m, norm_g, sink, w_out, *, batch, seq, tq, cond_row, nseq=1):
    nq = seq // tq
    n_cache = 0 if cache is None else cache[0].shape[1]
    rows = nseq * tq
    q_spec = pl.BlockSpec((rows, D_MODEL), lambda b, i: (b * nq + i, 0))
    kc_spec = pl.BlockSpec((rows, KV_WIDTH), lambda b, i: (b * nq + i, 0))
    vtc_spec = pl.BlockSpec((None, KV_WIDTH, tq), lambda b, i: (b, 0, i))
    if n_cache:
        nw = seq // WINDOW
        per = tq // WINDOW
        prev = lambda i: jnp.maximum(i * per - 1, 0)
        nxt = lambda i: jnp.minimum(i * per + per, nw - 1)
        in_specs = [q_spec,
                    pl.BlockSpec((WINDOW, KV_WIDTH), lambda b, i: (b * nw + prev(i), 0)), kc_spec,
                    pl.BlockSpec((WINDOW, KV_WIDTH), lambda b, i: (b * nw + nxt(i), 0)),
                    pl.BlockSpec((None, KV_WIDTH, WINDOW), lambda b, i: (b, 0, prev(i))), vtc_spec,
                    pl.BlockSpec((None, KV_WIDTH, WINDOW), lambda b, i: (b, 0, nxt(i))),
                    pl.BlockSpec((None, n_cache, KV_WIDTH), lambda b, i: (b, 0, 0)),
                    pl.BlockSpec((None, KV_WIDTH, n_cache), lambda b, i: (b, 0, 0))]
        args = [q, k, k, k, vt, vt, vt, cache[0], cache[1]]
    else:
        in_specs = [q_spec, kc_spec, pl.BlockSpec((nseq, KV_WIDTH, tq), lambda b, i: (b, 0, 0))]
        args = [q, k, vt]
    in_specs += [
        pl.BlockSpec((rows, D_MODEL), lambda b, i: (b * nq + i, 0)),
        pl.BlockSpec((None, N_MOD, D_MODEL), lambda b, i: (cond_row(b * nseq), 0, 0)),
        pl.BlockSpec((4, D_MODEL), lambda b, i: (0, 0)),
        pl.BlockSpec(memory_space=pltpu.SMEM),
        pl.BlockSpec((D_MODEL, D_MODEL), lambda b, i: (0, 0)),
    ]
    args += [x, m, norm_g, sink, w_out]
    return pl.pallas_call(
        functools.partial(_mix1_kernel, seq=seq, tq=tq, n_cache=n_cache, nseq=nseq),
        grid=(batch // nseq, nq),
        in_specs=in_specs,
        out_specs=pl.BlockSpec((rows, D_MODEL), lambda b, i: (b * nq + i, 0)),
        out_shape=jax.ShapeDtypeStruct((batch * seq, D_MODEL), F32),
        compiler_params=_params("arbitrary", "arbitrary"),
        name="mix_gqa",
    )(*args)


FFN_CHUNKS = (1536, 1280)


def _ffn_kernel(*refs, n_side, proj):
    x_ref, m_ref, ng_ref, wgu_ref, wd_ref = refs[:5]
    n_pin = 0 if proj is None else (5 if proj[1] else 3)
    n_pout = 0 if proj is None else len(proj[0])
    n_in = 5 + n_pin + n_side
    o_ref = refs[n_in]
    _side_convert(refs[5 + n_pin:n_in], refs[n_in + 1 + n_pout:])
    h = (_rms(x_ref[...], ng_ref[2:3, :] * (1.0 + m_ref[4:5, :])) + m_ref[3:4, :]).astype(BF16)
    acc = jnp.zeros(h.shape, F32)
    c0 = 0
    for width in FFN_CHUNKS:
        gate = _dot(h, wgu_ref[:, c0:c0 + width])
        up = _dot(h, wgu_ref[:, FFN_HIDDEN + c0:FFN_HIDDEN + c0 + width])
        act = (gate * _sigmoid(gate) * up).astype(BF16)
        acc = acc + _dot(act, wd_ref[c0:c0 + width, :])
        c0 += width
    y = x_ref[...] + _rms(acc, m_ref[5:6, :] * ng_ref[3:4, :])
    o_ref[...] = y
    if proj is not None:
        outs, rope, sblk = proj
        pm_ref, pg_ref, pw_ref = refs[5:8]
        _project(y, pm_ref, pg_ref, pw_ref, refs[8:10] if rope else None, outs, refs[n_in + 1:n_in + 1 + n_pout], sblk)


def _ffn(x, m, norm_g, w_gu, w_down, *, tm, cond_row, side=(), proj=None):
    rows = x.shape[0]
    side_in, side_out, side_shape = _side_specs(side, rows // tm, lambda i: i)
    proj_in, proj_args, proj_out, proj_shape, kproj = [], [], [], [], None
    if proj is not None:
        pm, pg, pw, outs, seq, rope = proj
        proj_in, proj_args, proj_out, proj_shape, sblk = _project_specs(pm, pg, pw, outs, rows=rows, tm=tm,
                                                                        cond_row=cond_row, seq=seq, rope=rope)
        kproj = (outs, rope, sblk)
    return pl.pallas_call(
        functools.partial(_ffn_kernel, n_side=len(side), proj=kproj),
        grid=(rows // tm,),
        in_specs=[
            pl.BlockSpec((tm, D_MODEL), lambda i: (i, 0)),
            pl.BlockSpec((None, N_MOD, D_MODEL), lambda i: (cond_row(i), 0, 0)),
            pl.BlockSpec((4, D_MODEL), lambda i: (0, 0)),
            _resident((D_MODEL, 2 * FFN_HIDDEN)),
            _resident((FFN_HIDDEN, D_MODEL)),
        ] + proj_in + side_in,
        out_specs=[pl.BlockSpec((tm, D_MODEL), lambda i: (i, 0))] + proj_out + side_out,
        out_shape=[jax.ShapeDtypeStruct((rows, D_MODEL), F32)] + proj_shape + side_shape,
        compiler_params=_params("arbitrary"),
        name="ffn",
    )(x, m, norm_g, w_gu, w_down, *proj_args, *side)


LAM_INIT_0 = 0.8 - 0.6 * math.exp(-0.3 * 0)
CTX_SEQS_PER_STEP = 4


def _trunk(x, m, caches, weights, *, batch, seq, cond_of_batch):
    (norm_g0, w_in, conv_w, conv_b, ln_g, ln_b, lam, subln_g, w_out0, w_gu0, w_down0,
     norm_g1, w_qkv, sink, w_out1, w_gu1, w_down1) = weights
    m0, m1 = m
    ctx = caches is None
    rope = not ctx
    tm = 512
    tm_ffn = 512
    tq = 256
    row_of = lambda t: (lambda i: cond_of_batch((i * t) // seq))

    o0 = 2 * CONV_CH
    ko, vo = o0 + DIFF_WIDTH, o0 + 2 * DIFF_WIDTH
    outs0 = [("nat", 0, o0, F32, False, None), ("nat", o0, DIFF_WIDTH, BF16, True, Q_SCALE),
             ("nat", ko, DIFF_WIDTH, BF16, True, None), ("T", vo, DIFF_WIDTH, BF16, False, None)]
    if ctx:
        outs0 += [("T", ko, DIFF_WIDTH, F32, False, None), ("heads", vo, DIFF_WIDTH, F32, False, None)]
    res = _pre(x, m0, norm_g0, w_in, tuple(outs0), tm=tm, cond_row=row_of(tm), seq=seq, rope=rope)
    u, q, k, vt = res[:4]
    new0 = tuple(res[4:])
    x, *conv0 = _mix0(u, q, k, vt, None if ctx else caches[0], x, m0, norm_g0, conv_w, conv_b,
                      ln_g, ln_b, lam, subln_g, w_out0, batch=batch, seq=seq, tq=tq, cond_row=cond_of_batch,
                      lam_init=LAM_INIT_0, side=(w_gu0, w_down0, w_qkv) if ctx else (),
                      nseq=CTX_SEQS_PER_STEP if ctx else 1)
    if ctx:
        w_gu0, w_down0, w_qkv = conv0

    outs1 = [("nat", 0, Q_WIDTH, BF16, True, Q_SCALE), ("nat", Q_WIDTH, KV_WIDTH, BF16, True, None)]
    if ctx:
        outs1 += [("T", Q_WIDTH + KV_WIDTH, KV_WIDTH, F32, False, None), ("T", Q_WIDTH, KV_WIDTH, F32, False, None)]
    else:
        outs1 += [("T", Q_WIDTH + KV_WIDTH, KV_WIDTH, BF16, False, None)]
    res = _ffn(x, m0, norm_g0, w_gu0, w_down0, tm=tm_ffn, cond_row=row_of(tm_ffn),
               side=(w_out1, w_gu1, w_down1) if ctx else (),
               proj=(m1, norm_g1, w_qkv, tuple(outs1), seq, rope))
    x, q, k, vt = res[:4]
    new1 = (res[4], vt) if ctx else ()
    if ctx:
        w_out1, w_gu1, w_down1 = res[5:]
    x = _mix1(q, k, vt, None if ctx else caches[1], x, m1, norm_g1, sink, w_out1,
              batch=batch, seq=seq, tq=tq, cond_row=cond_of_batch, nseq=CTX_SEQS_PER_STEP if ctx else 1)
    x, = _ffn(x, m1, norm_g1, w_gu1, w_down1, tm=tm_ffn, cond_row=row_of(tm_ffn))
    return x, new0 + new1, (w_gu0, w_down0, w_qkv, w_out1, w_gu1, w_down1)


def kernel(x_prompt, x_sample, cache_k0, cache_v0, cache_k1, cache_v1, c, c_ctx, l0_mod_w, l0_mod_b, l0_norm_g, l0_w_in, l0_conv_w, l0_conv_b, l0_conv_ln_g, l0_conv_ln_b, l0_lambda, l0_subln_g, l0_w_out, l0_w_gu, l0_w_down, l1_mod_w, l1_mod_b, l1_norm_g, l1_w_qkv, l1_sink, l1_w_out, l1_w_gu, l1_w_down):
    batch, seq, d = x_prompt.shape
    dec_batch, dec_seq, _ = x_sample.shape
    n_past = cache_k0.shape[1]

    cond = jnp.concatenate([c_ctx[None, :], c, jnp.zeros((MOD_ROWS - 1 - dec_batch, d), F32)], axis=0)
    m = (_modulation(cond, l0_mod_w, l0_mod_b), _modulation(cond, l1_mod_w, l1_mod_b))

    row = lambda v: v.reshape(1, -1)
    layer0 = (l0_norm_g, l0_w_in.astype(BF16), jnp.broadcast_to(l0_conv_w, (CONV_WIDTH, 8, CONV_CH)), row(l0_conv_b),
              row(l0_conv_ln_g), row(l0_conv_ln_b), l0_lambda, l0_subln_g.reshape(LANES, 1),
              l0_w_out.astype(BF16))
    weights = layer0 + (l0_w_gu, l0_w_down, l1_norm_g, l1_w_qkv, l1_sink, l1_w_out, l1_w_gu, l1_w_down)

    y_prompt, (kt0, v0, kt1, vt1), bf16_w = _trunk(x_prompt.reshape(batch * seq, d), m, None, weights,
                                                   batch=batch, seq=seq, cond_of_batch=lambda b: 0)
    w_gu0, w_down0, w_qkv, w_out1, w_gu1, w_down1 = bf16_w
    weights = layer0 + (w_gu0, w_down0, l1_norm_g, w_qkv, l1_sink, w_out1, w_gu1, w_down1)
    caches = ((cache_k0.reshape(dec_batch, n_past, -1),
               jnp.swapaxes(cache_v0.reshape(dec_batch, n_past, -1), 1, 2)),
              (cache_k1.reshape(dec_batch, n_past, -1),
               jnp.swapaxes(cache_v1.reshape(dec_batch, n_past, -1), 1, 2)))
    y_sample, _, _ = _trunk(x_sample.reshape(dec_batch * dec_seq, d), m, caches, weights,
                            batch=dec_batch, seq=dec_seq, cond_of_batch=lambda b: 1 + b)
    new_k0 = kt0.reshape(batch, DIFF_HEADS, 2, HEAD_DIM, seq).transpose(0, 4, 1, 2, 3)
    new_v0 = v0.reshape(batch, seq, DIFF_HEADS, 2 * HEAD_DIM)
    new_k1 = kt1.reshape(batch, GQA_KV_HEADS, HEAD_DIM, seq).transpose(0, 3, 1, 2)
    new_v1 = vt1.reshape(batch, GQA_KV_HEADS, HEAD_DIM, seq).transpose(0, 3, 1, 2)
    return (y_prompt.reshape(batch, seq, d), y_sample.reshape(dec_batch, dec_seq, d),
            new_k0, new_v0, new_k1, new_v1)
```

```python
import functools
import math

import numpy as np
import jax
import jax.numpy as jnp
from jax import lax
from jax.experimental import pallas as pl
from jax.experimental.pallas import tpu as pltpu

D_MODEL = 1024
GRID_W = 64
ROPE_THETA = 10000.0
EPS = 1e-6
NEG_INF = -1e30
LOG2E = math.log2(math.e)
CONV_CH = 512
CONV_WIDTH = 31
CONV_PAD = 16
DIFF_HEADS = 4
DIFF_WIDTH = 512
HEAD_DIM = 64
GQA_HEADS = 16
GQA_KV_HEADS = 4
GQA_GROUP = 4
Q_WIDTH = GQA_HEADS * HEAD_DIM
KV_WIDTH = GQA_KV_HEADS * HEAD_DIM
Q_SCALE = LOG2E * HEAD_DIM ** -0.5
WINDOW = 128
FFN_HIDDEN = 2816
N_MOD = 6
LANES = 128
BF16_ROWS = 16
MOD_ROWS = 8
VMEM_LIMIT = 56 * 1024 * 1024

BF16 = jnp.bfloat16
F32 = jnp.float32


def _sigmoid(x):
    return 0.5 * jnp.tanh(0.5 * x) + 0.5


def _rms(x, g):
    return x * lax.rsqrt(jnp.mean(x * x, axis=-1, keepdims=True) + EPS) * g


def _dot(a, b):
    return jnp.dot(a, b, preferred_element_type=F32)


def _dot_nt(a, b):
    return lax.dot_general(a, b, (((1,), (1,)), ((), ())), preferred_element_type=F32)


def _params(*sem):
    return pltpu.CompilerParams(dimension_semantics=sem, vmem_limit_bytes=VMEM_LIMIT)


def _resident(shape):
    return pl.BlockSpec(shape, lambda *_: (0,) * len(shape), pipeline_mode=pl.Buffered(1))


def _mod_kernel(c_ref, w_ref, b_ref, o_ref):
    c = c_ref[...]
    s = (c * _sigmoid(c)).astype(BF16)
    o_ref[...] = _dot(s, w_ref[...].astype(BF16)) + b_ref[...]


def _modulation(cond, mod_w, mod_b):
    n = mod_w.shape[1]
    tn = 1536
    out = pl.pallas_call(
        _mod_kernel,
        grid=(n // tn,),
        in_specs=[
            pl.BlockSpec((MOD_ROWS, D_MODEL), lambda j: (0, 0)),
            pl.BlockSpec((D_MODEL, tn), lambda j: (0, j)),
            pl.BlockSpec((1, tn), lambda j: (0, j)),
        ],
        out_specs=pl.BlockSpec((MOD_ROWS, tn), lambda j: (0, j)),
        out_shape=jax.ShapeDtypeStruct((MOD_ROWS, n), F32),
        compiler_params=_params("arbitrary"),
        name="modulation",
    )(cond, mod_w, mod_b.reshape(1, n))
    return out.reshape(MOD_ROWS, N_MOD, D_MODEL)


def _rope_tables(seq):
    t = np.arange(seq)
    rows, cols = t // GRID_W, t % GRID_W
    half = HEAD_DIM // 2
    inv = 1.0 / (ROPE_THETA ** (np.arange(0, half, 2, dtype=np.float64) / half))
    ar = rows[:, None] * inv[None, :]
    ac = cols[:, None] * inv[None, :]
    cos = np.concatenate([np.cos(ar), np.cos(ar), np.cos(ac), np.cos(ac)], axis=1)
    sin = np.concatenate([-np.sin(ar), np.sin(ar), -np.sin(ac), np.sin(ac)], axis=1)
    reps = LANES // HEAD_DIM
    return (jnp.asarray(np.tile(cos, (1, reps)), F32), jnp.asarray(np.tile(sin, (1, reps)), F32))


def _rope(x, cos, sin):
    quarter = HEAD_DIM // 4
    lane = lax.broadcasted_iota(jnp.int32, x.shape, 1)
    lo = (lane % (2 * quarter)) < quarter
    partner = jnp.where(lo, pltpu.roll(x, LANES - quarter, 1), pltpu.roll(x, quarter, 1))
    return x * cos + partner * sin


def _pre_kernel(*refs, outs, rope, sblk):
    n_in = 6 if rope else 4
    x_ref, m_ref, g_ref, w_ref = refs[:4]
    _project(x_ref[...], m_ref, g_ref, w_ref, refs[4:6] if rope else None, outs, refs[n_in:], sblk)


def _project(x, m_ref, g_ref, w_ref, trig, outs, o_refs, sblk):
    rope = trig is not None
    tm = x.shape[0]
    h = (_rms(x, g_ref[0:1, :] * (1.0 + m_ref[1:2, :])) + m_ref[0:1, :]).astype(BF16)
    cost = lambda o: o[2] * {"T": 5.0, "heads": 0.5}.get(o[0], 3.0 if (rope and o[4]) else 1.0)
    ranges = sorted({(o[1], o[2]) for o in outs},
                    key=lambda r: (-sum(cost(o) for o in outs if (o[1], o[2]) == r), r[0]))
    for start, width in ranges:
        p = _dot(h, w_ref[:, start:start + width])
        for (kind, s0, w0, _, roped, scale), o_ref in zip(outs, o_refs):
            if (s0, w0) != (start, width):
                continue
            if kind == "nat":
                for c in range(width // LANES):
                    xc = p[:, c * LANES:(c + 1) * LANES]
                    if rope and roped:
                        xc = _rope(xc, trig[0][...], trig[1][...])
                    if scale is not None:
                        xc = xc * scale
                    o_ref[:, c * LANES:(c + 1) * LANES] = xc.astype(o_ref.dtype)
            elif kind == "T":
                for s in range(tm // sblk):
                    o_ref[s] = p[s * sblk:(s + 1) * sblk, :].T.astype(o_ref.dtype)
            else:
                heads = width // LANES
                for hd in range(heads):
                    o_ref[pl.ds(hd, tm, stride=heads), :] = p[:, hd * LANES:(hd + 1) * LANES]


def _project_specs(m, norm_g, w, outs, *, rows, tm, cond_row, seq, rope):
    n = w.shape[1]
    sblk = min(seq, tm)
    per_seq = seq // sblk
    in_specs = [
        pl.BlockSpec((None, N_MOD, D_MODEL), lambda i: (cond_row(i), 0, 0)),
        pl.BlockSpec((4, D_MODEL), lambda i: (0, 0)),
        _resident((D_MODEL, n)),
    ]
    args = [m, norm_g, w]
    if rope:
        cos, sin = _rope_tables(seq)
        nblk = seq // tm
        in_specs += [pl.BlockSpec((tm, LANES), lambda i: (i % nblk, 0))] * 2
        args += [cos, sin]
    out_specs, out_shape = [], []
    for (kind, _, wd, dt, _, _) in outs:
        if kind == "nat":
            out_specs.append(pl.BlockSpec((tm, wd), lambda i: (i, 0)))
            out_shape.append(jax.ShapeDtypeStruct((rows, wd), dt))
        elif kind == "T":
            out_specs.append(pl.BlockSpec((tm // sblk, wd, sblk), lambda i: (i // per_seq, 0, i % per_seq)))
            out_shape.append(jax.ShapeDtypeStruct((rows // seq, wd, seq), dt))
        else:
            out_specs.append(pl.BlockSpec((tm * (wd // LANES), LANES), lambda i: (i, 0)))
            out_shape.append(jax.ShapeDtypeStruct((rows * (wd // LANES), LANES), dt))
    return in_specs, args, out_specs, out_shape, sblk


def _pre(x, m, norm_g, w, outs, *, tm, cond_row, seq, rope):
    rows = x.shape[0]
    in_specs = [pl.BlockSpec((tm, D_MODEL), lambda i: (i, 0))]
    proj_in, proj_args, out_specs, out_shape, sblk = _project_specs(m, norm_g, w, outs, rows=rows, tm=tm,
                                                                    cond_row=cond_row, seq=seq, rope=rope)
    in_specs += proj_in
    args = [x] + proj_args
    return pl.pallas_call(
        functools.partial(_pre_kernel, outs=outs, rope=rope, sblk=sblk),
        grid=(rows // tm,),
        in_specs=in_specs,
        out_specs=out_specs,
        out_shape=out_shape,
        compiler_params=_params("arbitrary"),
        name="pre_rope" if rope else "pre",
    )(*args)


def _attend(s_list, vt_list, floor=None):
    mx = None
    for s in s_list:
        smax = jnp.max(s, axis=0, keepdims=True)
        mx = smax if mx is None else jnp.maximum(mx, smax)
    if floor is not None:
        mx = jnp.maximum(mx, floor)
    d = vt_list[0].shape[0]
    ov = None
    for s, vt in zip(s_list, vt_list):
        ones = jnp.ones((BF16_ROWS, vt.shape[1]), BF16)
        pv = _dot(jnp.concatenate([vt, ones], axis=0), jnp.exp2(s - mx).astype(BF16))
        ov = pv if ov is None else ov + pv
    return ov[0:d, :], ov[d:d + 1, :], mx


def _pipelined(n, first, second):
    out = []
    nxt = first(0)
    for u in range(n):
        cur = nxt
        if u + 1 < n:
            nxt = first(u + 1)
        out.append(second(u, cur))
    return out


def _side_convert(src_refs, dst_refs):
    for src, dst in zip(src_refs, dst_refs):
        dst[...] = src[...].astype(BF16)


def _mix0_kernel(*refs, seq, tq, n_cache, lam_init, n_side, nseq):
    n_in = 16 if n_cache else 14
    u_ref, q_ref, k_ref, vt_ref = refs[:4]
    ck_ref, cvt_ref = refs[4:6] if n_cache else (None, None)
    x_ref, m_ref, ng_ref, cw_ref, cb_ref, lg_ref, lb_ref, lam_ref, sg_ref, wo_ref = refs[n_in - 10:n_in]
    o_ref = refs[n_in + n_side]
    z_s, zw_s, zsh_s = refs[n_in + 2 * n_side + 1:]
    _side_convert(refs[n_in:n_in + n_side], refs[n_in + n_side + 1:n_in + 2 * n_side + 1])
    qi = pl.program_id(1)
    whole = tq == seq
    win = tq + 2 * CONV_PAD
    span = win - 8
    chunk = 64
    off0 = CONV_PAD - CONV_WIDTH // 2
    la = jnp.sum(lam_ref[0:1, :] * lam_ref[1:2, :], axis=-1, keepdims=True)
    lb = jnp.sum(lam_ref[2:3, :] * lam_ref[3:4, :], axis=-1, keepdims=True)
    lam = jnp.exp(la) - jnp.exp(lb) + lam_init
    sub = LANES // 2
    lane = lax.broadcasted_iota(jnp.int32, (tq, LANES), 1)

    def block(s):
        urows = slice(s * seq, (s + 1) * seq)

        def glu():
            z_s[0:CONV_PAD, :] = jnp.zeros((CONV_PAD, CONV_CH), F32)
            z_s[CONV_PAD + seq:2 * CONV_PAD + seq, :] = jnp.zeros((CONV_PAD, CONV_CH), F32)
            z_s[CONV_PAD:CONV_PAD + seq, :] = u_ref[urows, 0:CONV_CH] * _sigmoid(u_ref[urows, CONV_CH:2 * CONV_CH])

        if whole:
            glu()
            zw = z_s
        else:
            pl.when(qi == 0)(glu)
            zw_s[...] = z_s[pl.ds(pl.multiple_of(qi * tq, tq), win), :]
            zw = zw_s

        for b in range(1, 8):
            zsh_s[b - 1, 0:span, :] = zw[b:b + span, :]
        conv_rows = []
        for c0 in range(0, tq, chunk):
            acc = jnp.zeros((chunk // 8, 8, CONV_CH), F32)
            for j in range(CONV_WIDTH):
                off = j + off0
                a, b = off // 8, off % 8
                if b == 0:
                    zz = zw[c0 + 8 * a:c0 + 8 * a + chunk, :]
                else:
                    zz = zsh_s[b - 1, c0 + 8 * a:c0 + 8 * a + chunk, :]
                acc = acc + zz.reshape(chunk // 8, 8, CONV_CH) * cw_ref[j]
            acc = acc.reshape(chunk, CONV_CH) + cb_ref[0:1, :]
            mu = jnp.mean(acc, axis=-1, keepdims=True)
            xc = acc - mu
            y = xc * lax.rsqrt(jnp.mean(xc * xc, axis=-1, keepdims=True) + EPS) * lg_ref[0:1, :] + lb_ref[0:1, :]
            conv_rows.append((y * _sigmoid(y)).astype(BF16))

        def segments(h):
            cols = slice(h * LANES, (h + 1) * LANES)
            segs = [(k_ref[urows, cols], vt_ref[s, cols, :])]
            if n_cache:
                segs.append((ck_ref[:, cols].astype(BF16), cvt_ref[cols, :].astype(BF16)))
            return segs

        def scores(h):
            qc = q_ref[s * tq:(s + 1) * tq, h * LANES:(h + 1) * LANES]
            zero = jnp.zeros_like(qc)
            qq = jnp.concatenate([jnp.where(lane < sub, qc, zero), jnp.where(lane >= sub, qc, zero)], axis=0)
            return [_dot_nt(kk, qq) for kk, _ in segments(h)]

        def head(h, s_list):
            ov, den, _ = _attend(s_list, [vt for _, vt in segments(h)])
            r = 1.0 / den
            o = ov[:, 0:tq] * r[:, 0:tq] - ov[:, tq:2 * tq] * (lam * r[:, tq:2 * tq])
            return o * lax.rsqrt(jnp.mean(o * o, axis=0, keepdims=True) + EPS) * sg_ref[...] * (1.0 - lam_init)

        heads = _pipelined(DIFF_HEADS, scores, head)
        attn = jnp.concatenate(heads, axis=0).T.astype(BF16)
        return jnp.concatenate([jnp.concatenate(conv_rows, axis=0), attn], axis=1)

    cat = jnp.concatenate([block(s) for s in range(nseq)], axis=0)
    mixed = _dot(cat, wo_ref[...])
    o_ref[...] = x_ref[...] + _rms(mixed, m_ref[2:3, :] * ng_ref[1:2, :])


def _side_specs(side, steps, index):
    in_specs, out_specs, out_shape = [], [], []
    for w in side:
        rows, cols = w.shape
        n_blk = math.gcd(steps, rows // BF16_ROWS)
        per = steps // n_blk
        spec = pl.BlockSpec((rows // n_blk, cols), lambda *ids, per=per: (index(*ids) // per, 0))
        in_specs.append(spec)
        out_specs.append(spec)
        out_shape.append(jax.ShapeDtypeStruct(w.shape, BF16))
    return in_specs, out_specs, out_shape


def _mix0(u, q, k, vt, cache, x, m, norm_g, conv_w, conv_b, ln_g, ln_b, lam, subln_g, w_out,
          *, batch, seq, tq, cond_row, lam_init, side=(), nseq=1):
    nq = seq // tq
    n_cache = 0 if cache is None else cache[0].shape[1]
    win = tq + 2 * CONV_PAD
    steps = (batch // nseq) * nq
    const = lambda shape: pl.BlockSpec(shape, lambda b, i: (0,) * len(shape))
    in_specs = [pl.BlockSpec((nseq * seq, 2 * CONV_CH), lambda b, i: (b, 0)),
                pl.BlockSpec((nseq * tq, DIFF_WIDTH), lambda b, i: (b * nq + i, 0)),
                pl.BlockSpec((nseq * seq, DIFF_WIDTH), lambda b, i: (b, 0)),
                pl.BlockSpec((nseq, DIFF_WIDTH, seq), lambda b, i: (b, 0, 0))]
    args = [u, q, k, vt]
    if n_cache:
        in_specs += [pl.BlockSpec((None, n_cache, DIFF_WIDTH), lambda b, i: (b, 0, 0)),
                     pl.BlockSpec((None, DIFF_WIDTH, n_cache), lambda b, i: (b, 0, 0))]
        args += [cache[0], cache[1]]
    in_specs += [
        pl.BlockSpec((nseq * tq, D_MODEL), lambda b, i: (b * nq + i, 0)),
        pl.BlockSpec((None, N_MOD, D_MODEL), lambda b, i: (cond_row(b * nseq), 0, 0)),
        const((4, D_MODEL)), const((CONV_WIDTH, 8, CONV_CH)), const((8, CONV_CH)), const((8, CONV_CH)),
        const((8, CONV_CH)), const(LAM_BLOCK), const((LANES, 1)), _resident((D_MODEL, D_MODEL)),
    ]
    args += [x, m, norm_g, conv_w, conv_b, ln_g, ln_b, lam, subln_g, w_out]
    side_in, side_out, side_shape = _side_specs(side, steps, lambda b, i: b * nq + i)
    return pl.pallas_call(
        functools.partial(_mix0_kernel, seq=seq, tq=tq, n_cache=n_cache, lam_init=lam_init, n_side=len(side),
                          nseq=nseq),
        grid=(batch // nseq, nq),
        in_specs=in_specs + side_in,
        out_specs=[pl.BlockSpec((nseq * tq, D_MODEL), lambda b, i: (b * nq + i, 0))] + side_out,
        out_shape=[jax.ShapeDtypeStruct((batch * seq, D_MODEL), F32)] + side_shape,
        scratch_shapes=[
            pltpu.VMEM((seq + 2 * CONV_PAD, CONV_CH), F32), pltpu.VMEM((win, CONV_CH), F32),
            pltpu.VMEM((7, win - 8, CONV_CH), F32),
        ],
        compiler_params=_params("arbitrary", "arbitrary"),
        name="mix_conv_diff",
    )(*args, *side)


def _mix1_kernel(*refs, seq, tq, n_cache, nseq):
    if n_cache:
        (q_ref, kp_ref, kc_ref, kn_ref, vtp_ref, vtc_ref, vtn_ref, ck_ref, cvt_ref,
         x_ref, m_ref, ng_ref, sink_ref, wo_ref, o_ref) = refs
    else:
        (q_ref, kc_ref, vtc_ref, x_ref, m_ref, ng_ref, sink_ref, wo_ref, o_ref) = refs
    qi = pl.program_id(1)
    lane = lax.broadcasted_iota(jnp.int32, (tq, LANES), 1)

    def band(base, rows):
        kpos = base + lax.broadcasted_iota(jnp.int32, (rows, tq), 0)
        qpos = qi * tq + lax.broadcasted_iota(jnp.int32, (rows, tq), 1)
        ok = (jnp.abs(qpos - kpos) <= WINDOW) & (kpos >= 0) & (kpos < seq)
        return jnp.concatenate([ok] * GQA_GROUP, axis=1)

    def attention(r0, segs):
        def scores(g):
            b = g % 2
            kcols = slice((g // 2) * LANES, (g // 2 + 1) * LANES)
            keep = (lane < HEAD_DIM) if b == 0 else (lane >= HEAD_DIM)
            qs = []
            for h in range(g * GQA_GROUP, (g + 1) * GQA_GROUP):
                qc = q_ref[r0:r0 + tq, (h // 2) * LANES:(h // 2 + 1) * LANES]
                qa = qc if h % 2 == b else pltpu.roll(qc, HEAD_DIM, 1)
                qs.append(jnp.where(keep, qa, jnp.zeros_like(qa)))
            qg = jnp.concatenate(qs, axis=0)
            s_list = []
            for k_ref, _, ok in segs:
                s = _dot_nt(k_ref[:, kcols].astype(BF16), qg)
                s_list.append(s if ok is None else jnp.where(ok, s, NEG_INF))
            return s_list

        def group(g, s_list):
            vrows = slice(g * HEAD_DIM, (g + 1) * HEAD_DIM)
            sink = jnp.concatenate([jnp.full((1, tq), sink_ref[h] * LOG2E, F32)
                                    for h in range(g * GQA_GROUP, (g + 1) * GQA_GROUP)], axis=1)
            ov, den, mx = _attend(s_list, [vt_ref[vrows, :].astype(BF16) for _, vt_ref, _ in segs], floor=sink)
            ov = ov * (1.0 / (den + jnp.exp2(sink - mx)))
            return jnp.concatenate([ov[:, hh * tq:(hh + 1) * tq] for hh in range(GQA_GROUP)], axis=0)

        groups = _pipelined(GQA_KV_HEADS, scores, group)
        return jnp.concatenate(groups, axis=0).T.astype(BF16)

    if n_cache:
        attn = attention(0, [(kp_ref, vtp_ref, band(qi * tq - WINDOW, WINDOW)),
                             (kc_ref, vtc_ref, band(qi * tq, tq)),
                             (kn_ref, vtn_ref, band(qi * tq + tq, WINDOW)),
                             (ck_ref, cvt_ref, None)])
    else:
        attn = jnp.concatenate([attention(s * tq, [(kc_ref.at[pl.ds(s * tq, tq)], vtc_ref.at[s], None)])
                                for s in range(nseq)], axis=0)
    mixed = _dot(attn, wo_ref[...])
    o_ref[...] = x_ref[...] + _rms(mixed, m_ref[2:3, :] * ng_ref[1:2, :])


def _mix1(q, k, vt, cache, x, m, norm_g, sink, w_out, *, batch, seq, tq, cond_row, nseq=1):
    nq = seq // tq
    n_cache = 0 if cache is None else cache[0].shape[1]
    rows = nseq * tq
    q_spec = pl.BlockSpec((rows, D_MODEL), lambda b, i: (b * nq + i, 0))
    kc_spec = pl.BlockSpec((rows, KV_WIDTH), lambda b, i: (b * nq + i, 0))
    vtc_spec = pl.BlockSpec((None, KV_WIDTH, tq), lambda b, i: (b, 0, i))
    if n_cache:
        nw = seq // WINDOW
        per = tq // WINDOW
        prev = lambda i: jnp.maximum(i * per - 1, 0)
        nxt = lambda i: jnp.minimum(i * per + per, nw - 1)
        in_specs = [q_spec,
                    pl.BlockSpec((WINDOW, KV_WIDTH), lambda b, i: (b * nw + prev(i), 0)), kc_spec,
                    pl.BlockSpec((WINDOW, KV_WIDTH), lambda b, i: (b * nw + nxt(i), 0)),
                    pl.BlockSpec((None, KV_WIDTH, WINDOW), lambda b, i: (b, 0, prev(i))), vtc_spec,
                    pl.BlockSpec((None, KV_WIDTH, WINDOW), lambda b, i: (b, 0, nxt(i))),
                    pl.BlockSpec((None, n_cache, KV_WIDTH), lambda b, i: (b, 0, 0)),
                    pl.BlockSpec((None, KV_WIDTH, n_cache), lambda b, i: (b, 0, 0))]
        args = [q, k, k, k, vt, vt, vt, cache[0], cache[1]]
    else:
        in_specs = [q_spec, kc_spec, pl.BlockSpec((nseq, KV_WIDTH, tq), lambda b, i: (b, 0, 0))]
        args = [q, k, vt]
    in_specs += [
        pl.BlockSpec((rows, D_MODEL), lambda b, i: (b * nq + i, 0)),
        pl.BlockSpec((None, N_MOD, D_MODEL), lambda b, i: (cond_row(b * nseq), 0, 0)),
        pl.BlockSpec((4, D_MODEL), lambda b, i: (0, 0)),
        pl.BlockSpec(memory_space=pltpu.SMEM),
        pl.BlockSpec((D_MODEL, D_MODEL), lambda b, i: (0, 0)),
    ]
    args += [x, m, norm_g, sink, w_out]
    return pl.pallas_call(
        functools.partial(_mix1_kernel, seq=seq, tq=tq, n_cache=n_cache, nseq=nseq),
        grid=(batch // nseq, nq),
        in_specs=in_specs,
        out_specs=pl.BlockSpec((rows, D_MODEL), lambda b, i: (b * nq + i, 0)),
        out_shape=jax.ShapeDtypeStruct((batch * seq, D_MODEL), F32),
        compiler_params=_params("arbitrary", "arbitrary"),
        name="mix_gqa",
    )(*args)


FFN_CHUNKS = (1536, 1280)


def _ffn_kernel(*refs, n_side, proj):
    x_ref, m_ref, ng_ref, wgu_ref, wd_ref = refs[:5]
    n_pin = 0 if proj is None else (5 if proj[1] else 3)
    n_pout = 0 if proj is None else len(proj[0])
    n_in = 5 + n_pin + n_side
    o_ref = refs[n_in]
    _side_convert(refs[5 + n_pin:n_in], refs[n_in + 1 + n_pout:])
    h = (_rms(x_ref[...], ng_ref[2:3, :] * (1.0 + m_ref[4:5, :])) + m_ref[3:4, :]).astype(BF16)
    acc = jnp.zeros(h.shape, F32)
    c0 = 0
    for width in FFN_CHUNKS:
        gate = _dot(h, wgu_ref[:, c0:c0 + width])
        up = _dot(h, wgu_ref[:, FFN_HIDDEN + c0:FFN_HIDDEN + c0 + width])
        act = (gate * _sigmoid(gate) * up).astype(BF16)
        acc = acc + _dot(act, wd_ref[c0:c0 + width, :])
        c0 += width
    y = x_ref[...] + _rms(acc, m_ref[5:6, :] * ng_ref[3:4, :])
    o_ref[...] = y
    if proj is not None:
        outs, rope, sblk = proj
        pm_ref, pg_ref, pw_ref = refs[5:8]
        _project(y, pm_ref, pg_ref, pw_ref, refs[8:10] if rope else None, outs, refs[n_in + 1:n_in + 1 + n_pout], sblk)


def _ffn(x, m, norm_g, w_gu, w_down, *, tm, cond_row, side=(), proj=None):
    rows = x.shape[0]
    side_in, side_out, side_shape = _side_specs(side, rows // tm, lambda i: i)
    proj_in, proj_args, proj_out, proj_shape, kproj = [], [], [], [], None
    if proj is not None:
        pm, pg, pw, outs, seq, rope = proj
        proj_in, proj_args, proj_out, proj_shape, sblk = _project_specs(pm, pg, pw, outs, rows=rows, tm=tm,
                                                                        cond_row=cond_row, seq=seq, rope=rope)
        kproj = (outs, rope, sblk)
    return pl.pallas_call(
        functools.partial(_ffn_kernel, n_side=len(side), proj=kproj),
        grid=(rows // tm,),
        in_specs=[
            pl.BlockSpec((tm, D_MODEL), lambda i: (i, 0)),
            pl.BlockSpec((None, N_MOD, D_MODEL), lambda i: (cond_row(i), 0, 0)),
            pl.BlockSpec((4, D_MODEL), lambda i: (0, 0)),
            _resident((D_MODEL, 2 * FFN_HIDDEN)),
            _resident((FFN_HIDDEN, D_MODEL)),
        ] + proj_in + side_in,
        out_specs=[pl.BlockSpec((tm, D_MODEL), lambda i: (i, 0))] + proj_out + side_out,
        out_shape=[jax.ShapeDtypeStruct((rows, D_MODEL), F32)] + proj_shape + side_shape,
        compiler_params=_params("arbitrary"),
        name="ffn",
    )(x, m, norm_g, w_gu, w_down, *proj_args, *side)


LAM_INIT_0 = 0.8 - 0.6 * math.exp(-0.3 * 0)
CTX_SEQS_PER_STEP = 4
LAM_BLOCK = (32, LANES)


def _trunk(x, m, caches, weights, *, batch, seq, cond_of_batch):
    (norm_g0, w_in, conv_w, conv_b, ln_g, ln_b, lam, subln_g, w_out0, w_gu0, w_down0,
     norm_g1, w_qkv, sink, w_out1, w_gu1, w_down1) = weights
    m0, m1 = m
    ctx = caches is None
    rope = not ctx
    tm = 512
    tm_ffn = 512
    tq = 256
    row_of = lambda t: (lambda i: cond_of_batch((i * t) // seq))

    o0 = 2 * CONV_CH
    ko, vo = o0 + DIFF_WIDTH, o0 + 2 * DIFF_WIDTH
    outs0 = [("nat", 0, o0, F32, False, None), ("nat", o0, DIFF_WIDTH, BF16, True, Q_SCALE),
             ("nat", ko, DIFF_WIDTH, BF16, True, None), ("T", vo, DIFF_WIDTH, BF16, False, None)]
    if ctx:
        outs0 += [("T", ko, DIFF_WIDTH, F32, False, None), ("heads", vo, DIFF_WIDTH, F32, False, None)]
    res = _pre(x, m0, norm_g0, w_in, tuple(outs0), tm=tm, cond_row=row_of(tm), seq=seq, rope=rope)
    u, q, k, vt = res[:4]
    new0 = tuple(res[4:])
    x, *conv0 = _mix0(u, q, k, vt, None if ctx else caches[0], x, m0, norm_g0, conv_w, conv_b,
                      ln_g, ln_b, lam, subln_g, w_out0, batch=batch, seq=seq, tq=tq, cond_row=cond_of_batch,
                      lam_init=LAM_INIT_0, side=(w_gu0, w_down0, w_qkv) if ctx else (),
                      nseq=CTX_SEQS_PER_STEP if ctx else 1)
    if ctx:
        w_gu0, w_down0, w_qkv = conv0

    outs1 = [("nat", 0, Q_WIDTH, BF16, True, Q_SCALE), ("nat", Q_WIDTH, KV_WIDTH, BF16, True, None)]
    if ctx:
        outs1 += [("T", Q_WIDTH + KV_WIDTH, KV_WIDTH, F32, False, None), ("T", Q_WIDTH, KV_WIDTH, F32, False, None)]
    else:
        outs1 += [("T", Q_WIDTH + KV_WIDTH, KV_WIDTH, BF16, False, None)]
    res = _ffn(x, m0, norm_g0, w_gu0, w_down0, tm=tm_ffn, cond_row=row_of(tm_ffn),
               side=(w_out1, w_gu1, w_down1) if ctx else (),
               proj=(m1, norm_g1, w_qkv, tuple(outs1), seq, rope))
    x, q, k, vt = res[:4]
    new1 = (res[4], vt) if ctx else ()
    if ctx:
        w_out1, w_gu1, w_down1 = res[5:]
    x = _mix1(q, k, vt, None if ctx else caches[1], x, m1, norm_g1, sink, w_out1,
              batch=batch, seq=seq, tq=tq, cond_row=cond_of_batch, nseq=CTX_SEQS_PER_STEP if ctx else 1)
    x, = _ffn(x, m1, norm_g1, w_gu1, w_down1, tm=tm_ffn, cond_row=row_of(tm_ffn))
    return x, new0 + new1, (w_gu0, w_down0, w_qkv, w_out1, w_gu1, w_down1)


def kernel(x_prompt, x_sample, cache_k0, cache_v0, cache_k1, cache_v1, c, c_ctx, l0_mod_w, l0_mod_b, l0_norm_g, l0_w_in, l0_conv_w, l0_conv_b, l0_conv_ln_g, l0_conv_ln_b, l0_lambda, l0_subln_g, l0_w_out, l0_w_gu, l0_w_down, l1_mod_w, l1_mod_b, l1_norm_g, l1_w_qkv, l1_sink, l1_w_out, l1_w_gu, l1_w_down):
    batch, seq, d = x_prompt.shape
    dec_batch, dec_seq, _ = x_sample.shape
    n_past = cache_k0.shape[1]

    cond = jnp.concatenate([c_ctx[None, :], c, jnp.zeros((MOD_ROWS - 1 - dec_batch, d), F32)], axis=0)
    m = (_modulation(cond, l0_mod_w, l0_mod_b), _modulation(cond, l1_mod_w, l1_mod_b))

    row = lambda v: jnp.broadcast_to(v.reshape(1, -1), (8, v.shape[0]))
    layer0 = (l0_norm_g, l0_w_in.astype(BF16), jnp.broadcast_to(l0_conv_w, (CONV_WIDTH, 8, CONV_CH)), row(l0_conv_b),
              row(l0_conv_ln_g), row(l0_conv_ln_b),
              jnp.pad(l0_lambda, ((0, LAM_BLOCK[0] - 4), (0, LAM_BLOCK[1] - l0_lambda.shape[1]))),
              l0_subln_g.reshape(LANES, 1),
              l0_w_out.astype(BF16))
    weights = layer0 + (l0_w_gu, l0_w_down, l1_norm_g, l1_w_qkv, l1_sink, l1_w_out, l1_w_gu, l1_w_down)

    y_prompt, (kt0, v0, kt1, vt1), bf16_w = _trunk(x_prompt.reshape(batch * seq, d), m, None, weights,
                                                   batch=batch, seq=seq, cond_of_batch=lambda b: 0)
    w_gu0, w_down0, w_qkv, w_out1, w_gu1, w_down1 = bf16_w
    weights = layer0 + (w_gu0, w_down0, l1_norm_g, w_qkv, l1_sink, w_out1, w_gu1, w_down1)
    caches = ((cache_k0.reshape(dec_batch, n_past, -1),
               jnp.swapaxes(cache_v0.reshape(dec_batch, n_past, -1), 1, 2)),
              (cache_k1.reshape(dec_batch, n_past, -1),
               jnp.swapaxes(cache_v1.reshape(dec_batch, n_past, -1), 1, 2)))
    y_sample, _, _ = _trunk(x_sample.reshape(dec_batch * dec_seq, d), m, caches, weights,
                            batch=dec_batch, seq=dec_seq, cond_of_batch=lambda b: 1 + b)
    new_k0 = kt0.reshape(batch, DIFF_HEADS, 2, HEAD_DIM, seq).transpose(0, 4, 1, 2, 3)
    new_v0 = v0.reshape(batch, seq, DIFF_HEADS, 2 * HEAD_DIM)
    new_k1 = kt1.reshape(batch, GQA_KV_HEADS, HEAD_DIM, seq).transpose(0, 3, 1, 2)
    new_v1 = vt1.reshape(batch, GQA_KV_HEADS, HEAD_DIM, seq).transpose(0, 3, 1, 2)
    return (y_prompt.reshape(batch, seq, d), y_sample.reshape(dec_batch, dec_seq, d),
            new_k0, new_v0, new_k1, new_v1)
```

```python
import functools
import math

import numpy as np
import jax
import jax.numpy as jnp
from jax import lax
from jax.experimental import pallas as pl
from jax.experimental.pallas import tpu as pltpu

D_MODEL = 1024
GRID_W = 64
ROPE_THETA = 10000.0
EPS = 1e-6
NEG_INF = -1e30
LOG2E = math.log2(math.e)
CONV_CH = 512
CONV_WIDTH = 31
CONV_PAD = 16
DIFF_HEADS = 4
DIFF_WIDTH = 512
HEAD_DIM = 64
GQA_HEADS = 16
GQA_KV_HEADS = 4
GQA_GROUP = 4
Q_WIDTH = GQA_HEADS * HEAD_DIM
KV_WIDTH = GQA_KV_HEADS * HEAD_DIM
Q_SCALE = LOG2E * HEAD_DIM ** -0.5
WINDOW = 128
FFN_HIDDEN = 2816
N_MOD = 6
LANES = 128
BF16_ROWS = 16
MOD_ROWS = 8
VMEM_LIMIT = 56 * 1024 * 1024

BF16 = jnp.bfloat16
F32 = jnp.float32


def _sigmoid(x):
    return 0.5 * jnp.tanh(0.5 * x) + 0.5


def _rms(x, g):
    return x * lax.rsqrt(jnp.mean(x * x, axis=-1, keepdims=True) + EPS) * g


def _dot(a, b):
    return jnp.dot(a, b, preferred_element_type=F32)


def _dot_nt(a, b):
    return lax.dot_general(a, b, (((1,), (1,)), ((), ())), preferred_element_type=F32)


def _params(*sem):
    return pltpu.CompilerParams(dimension_semantics=sem, vmem_limit_bytes=VMEM_LIMIT)


def _resident(shape):
    return pl.BlockSpec(shape, lambda *_: (0,) * len(shape), pipeline_mode=pl.Buffered(1))


def _mod_kernel(c_ref, w_ref, b_ref, o_ref):
    c = c_ref[...]
    s = (c * _sigmoid(c)).astype(BF16)
    o_ref[...] = _dot(s, w_ref[...].astype(BF16)) + b_ref[...]


def _modulation(cond, mod_w, mod_b):
    n = mod_w.shape[1]
    tn = 1536
    out = pl.pallas_call(
        _mod_kernel,
        grid=(n // tn,),
        in_specs=[
            pl.BlockSpec((MOD_ROWS, D_MODEL), lambda j: (0, 0)),
            pl.BlockSpec((D_MODEL, tn), lambda j: (0, j)),
            pl.BlockSpec((1, tn), lambda j: (0, j)),
        ],
        out_specs=pl.BlockSpec((MOD_ROWS, tn), lambda j: (0, j)),
        out_shape=jax.ShapeDtypeStruct((MOD_ROWS, n), F32),
        compiler_params=_params("arbitrary"),
        name="modulation",
    )(cond, mod_w, mod_b.reshape(1, n))
    return out.reshape(MOD_ROWS, N_MOD, D_MODEL)


def _rope_tables(seq):
    t = np.arange(seq)
    rows, cols = t // GRID_W, t % GRID_W
    half = HEAD_DIM // 2
    inv = 1.0 / (ROPE_THETA ** (np.arange(0, half, 2, dtype=np.float64) / half))
    ar = rows[:, None] * inv[None, :]
    ac = cols[:, None] * inv[None, :]
    cos = np.concatenate([np.cos(ar), np.cos(ar), np.cos(ac), np.cos(ac)], axis=1)
    sin = np.concatenate([-np.sin(ar), np.sin(ar), -np.sin(ac), np.sin(ac)], axis=1)
    reps = LANES // HEAD_DIM
    return (jnp.asarray(np.tile(cos, (1, reps)), F32), jnp.asarray(np.tile(sin, (1, reps)), F32))


def _rope(x, cos, sin):
    quarter = HEAD_DIM // 4
    lane = lax.broadcasted_iota(jnp.int32, x.shape, 1)
    lo = (lane % (2 * quarter)) < quarter
    partner = jnp.where(lo, pltpu.roll(x, LANES - quarter, 1), pltpu.roll(x, quarter, 1))
    return x * cos + partner * sin


def _pre_kernel(*refs, outs, rope, sblk):
    n_in = 6 if rope else 4
    x_ref, m_ref, g_ref, w_ref = refs[:4]
    _project(x_ref[...], m_ref, g_ref, w_ref, refs[4:6] if rope else None, outs, refs[n_in:], sblk)


def _project(x, m_ref, g_ref, w_ref, trig, outs, o_refs, sblk):
    rope = trig is not None
    tm = x.shape[0]
    h = (_rms(x, g_ref[0:1, :] * (1.0 + m_ref[1:2, :])) + m_ref[0:1, :]).astype(BF16)
    cost = lambda o: o[2] * {"T": 5.0, "heads": 0.5, "glu": 1.2}.get(o[0], 3.0 if (rope and o[4]) else 1.0)
    ranges = sorted({(o[1], o[2]) for o in outs},
                    key=lambda r: (-sum(cost(o) for o in outs if (o[1], o[2]) == r), r[0]))
    for start, width in ranges:
        p = _dot(h, w_ref[:, start:start + width])
        for (kind, s0, w0, _, roped, scale), o_ref in zip(outs, o_refs):
            if (s0, w0) != (start, width):
                continue
            if kind == "nat":
                for c in range(width // LANES):
                    xc = p[:, c * LANES:(c + 1) * LANES]
                    if rope and roped:
                        xc = _rope(xc, trig[0][...], trig[1][...])
                    if scale is not None:
                        xc = xc * scale
                    o_ref[:, c * LANES:(c + 1) * LANES] = xc.astype(o_ref.dtype)
            elif kind == "T":
                for s in range(tm // sblk):
                    o_ref[s] = p[s * sblk:(s + 1) * sblk, :].T.astype(o_ref.dtype)
            elif kind == "glu":
                o_ref[...] = p[:, 0:width // 2] * _sigmoid(p[:, width // 2:width])
            else:
                heads = width // LANES
                for hd in range(heads):
                    o_ref[pl.ds(hd, tm, stride=heads), :] = p[:, hd * LANES:(hd + 1) * LANES]


def _project_specs(m, norm_g, w, outs, *, rows, tm, cond_row, seq, rope):
    n = w.shape[1]
    sblk = min(seq, tm)
    per_seq = seq // sblk
    in_specs = [
        pl.BlockSpec((None, N_MOD, D_MODEL), lambda i: (cond_row(i), 0, 0)),
        pl.BlockSpec((4, D_MODEL), lambda i: (0, 0)),
        _resident((D_MODEL, n)),
    ]
    args = [m, norm_g, w]
    if rope:
        cos, sin = _rope_tables(seq)
        nblk = seq // tm
        in_specs += [pl.BlockSpec((tm, LANES), lambda i: (i % nblk, 0))] * 2
        args += [cos, sin]
    out_specs, out_shape = [], []
    for (kind, _, wd, dt, _, _) in outs:
        if kind == "nat":
            out_specs.append(pl.BlockSpec((tm, wd), lambda i: (i, 0)))
            out_shape.append(jax.ShapeDtypeStruct((rows, wd), dt))
        elif kind == "T":
            out_specs.append(pl.BlockSpec((tm // sblk, wd, sblk), lambda i: (i // per_seq, 0, i % per_seq)))
            out_shape.append(jax.ShapeDtypeStruct((rows // seq, wd, seq), dt))
        elif kind == "glu":
            out_specs.append(pl.BlockSpec((tm, wd // 2), lambda i: (i, 0)))
            out_shape.append(jax.ShapeDtypeStruct((rows, wd // 2), dt))
        else:
            out_specs.append(pl.BlockSpec((tm * (wd // LANES), LANES), lambda i: (i, 0)))
            out_shape.append(jax.ShapeDtypeStruct((rows * (wd // LANES), LANES), dt))
    return in_specs, args, out_specs, out_shape, sblk


def _pre(x, m, norm_g, w, outs, *, tm, cond_row, seq, rope):
    rows = x.shape[0]
    in_specs = [pl.BlockSpec((tm, D_MODEL), lambda i: (i, 0))]
    proj_in, proj_args, out_specs, out_shape, sblk = _project_specs(m, norm_g, w, outs, rows=rows, tm=tm,
                                                                    cond_row=cond_row, seq=seq, rope=rope)
    in_specs += proj_in
    args = [x] + proj_args
    return pl.pallas_call(
        functools.partial(_pre_kernel, outs=outs, rope=rope, sblk=sblk),
        grid=(rows // tm,),
        in_specs=in_specs,
        out_specs=out_specs,
        out_shape=out_shape,
        compiler_params=_params("arbitrary"),
        name="pre_rope" if rope else "pre",
    )(*args)


def _attend(s_list, vt_list, floor=None):
    mx = None
    for s in s_list:
        smax = jnp.max(s, axis=0, keepdims=True)
        mx = smax if mx is None else jnp.maximum(mx, smax)
    if floor is not None:
        mx = jnp.maximum(mx, floor)
    d = vt_list[0].shape[0]
    ov = None
    for s, vt in zip(s_list, vt_list):
        ones = jnp.ones((BF16_ROWS, vt.shape[1]), BF16)
        pv = _dot(jnp.concatenate([vt, ones], axis=0), jnp.exp2(s - mx).astype(BF16))
        ov = pv if ov is None else ov + pv
    return ov[0:d, :], ov[d:d + 1, :], mx


def _pipelined(n, first, second):
    out = []
    nxt = first(0)
    for u in range(n):
        cur = nxt
        if u + 1 < n:
            nxt = first(u + 1)
        out.append(second(u, cur))
    return out


def _side_convert(src_refs, dst_refs):
    for src, dst in zip(src_refs, dst_refs):
        dst[...] = src[...].astype(BF16)


def _mix0_kernel(*refs, seq, tq, n_cache, lam_init, n_side, nseq):
    n_in = 16 if n_cache else 14
    z_ref, q_ref, k_ref, vt_ref = refs[:4]
    ck_ref, cvt_ref = refs[4:6] if n_cache else (None, None)
    x_ref, m_ref, ng_ref, cw_ref, cb_ref, lg_ref, lb_ref, lam_ref, sg_ref, wo_ref = refs[n_in - 10:n_in]
    o_ref = refs[n_in + n_side]
    z_s, zw_s, zsh_s = refs[n_in + 2 * n_side + 1:]
    _side_convert(refs[n_in:n_in + n_side], refs[n_in + n_side + 1:n_in + 2 * n_side + 1])
    qi = pl.program_id(1)
    whole = tq == seq
    win = tq + 2 * CONV_PAD
    span = win - 8
    chunk = 64
    off0 = CONV_PAD - CONV_WIDTH // 2
    la = jnp.sum(lam_ref[0:1, :] * lam_ref[1:2, :], axis=-1, keepdims=True)
    lb = jnp.sum(lam_ref[2:3, :] * lam_ref[3:4, :], axis=-1, keepdims=True)
    lam = jnp.exp(la) - jnp.exp(lb) + lam_init
    sub = LANES // 2
    lane = lax.broadcasted_iota(jnp.int32, (tq, LANES), 1)

    def block(s):
        urows = slice(s * seq, (s + 1) * seq)

        def pad_sequence():
            z_s[0:CONV_PAD, :] = jnp.zeros((CONV_PAD, CONV_CH), F32)
            z_s[CONV_PAD + seq:2 * CONV_PAD + seq, :] = jnp.zeros((CONV_PAD, CONV_CH), F32)
            z_s[CONV_PAD:CONV_PAD + seq, :] = z_ref[urows, :]

        if whole:
            pad_sequence()
            zw = z_s
        else:
            pl.when(qi == 0)(pad_sequence)
            zw_s[...] = z_s[pl.ds(pl.multiple_of(qi * tq, tq), win), :]
            zw = zw_s

        for b in range(1, 8):
            zsh_s[b - 1, 0:span, :] = zw[b:b + span, :]
        conv_rows = []
        for c0 in range(0, tq, chunk):
            acc = jnp.zeros((chunk // 8, 8, CONV_CH), F32)
            for j in range(CONV_WIDTH):
                off = j + off0
                a, b = off // 8, off % 8
                if b == 0:
                    zz = zw[c0 + 8 * a:c0 + 8 * a + chunk, :]
                else:
                    zz = zsh_s[b - 1, c0 + 8 * a:c0 + 8 * a + chunk, :]
                acc = acc + zz.reshape(chunk // 8, 8, CONV_CH) * cw_ref[j]
            acc = acc.reshape(chunk, CONV_CH) + cb_ref[...]
            mu = jnp.mean(acc, axis=-1, keepdims=True)
            xc = acc - mu
            y = xc * lax.rsqrt(jnp.mean(xc * xc, axis=-1, keepdims=True) + EPS) * lg_ref[...] + lb_ref[...]
            conv_rows.append((y * _sigmoid(y)).astype(BF16))

        def segments(h):
            cols = slice(h * LANES, (h + 1) * LANES)
            segs = [(k_ref[urows, cols], vt_ref[s, cols, :])]
            if n_cache:
                segs.append((ck_ref[:, cols].astype(BF16), cvt_ref[cols, :].astype(BF16)))
            return segs

        def scores(h):
            qc = q_ref[s * tq:(s + 1) * tq, h * LANES:(h + 1) * LANES]
            zero = jnp.zeros_like(qc)
            qq = jnp.concatenate([jnp.where(lane < sub, qc, zero), jnp.where(lane >= sub, qc, zero)], axis=0)
            return [_dot_nt(kk, qq) for kk, _ in segments(h)]

        def head(h, s_list):
            ov, den, _ = _attend(s_list, [vt for _, vt in segments(h)])
            r = 1.0 / den
            o = ov[:, 0:tq] * r[:, 0:tq] - ov[:, tq:2 * tq] * (lam * r[:, tq:2 * tq])
            return o * lax.rsqrt(jnp.mean(o * o, axis=0, keepdims=True) + EPS) * sg_ref[...] * (1.0 - lam_init)

        heads = _pipelined(DIFF_HEADS, scores, head)
        attn = jnp.concatenate(heads, axis=0).T.astype(BF16)
        return jnp.concatenate([jnp.concatenate(conv_rows, axis=0), attn], axis=1)

    cat = jnp.concatenate([block(s) for s in range(nseq)], axis=0)
    mixed = _dot(cat, wo_ref[...])
    o_ref[...] = x_ref[...] + _rms(mixed, m_ref[2:3, :] * ng_ref[1:2, :])


def _side_specs(side, steps, index):
    in_specs, out_specs, out_shape = [], [], []
    for w in side:
        rows, cols = w.shape
        n_blk = math.gcd(steps, rows // BF16_ROWS)
        per = steps // n_blk
        spec = pl.BlockSpec((rows // n_blk, cols), lambda *ids, per=per: (index(*ids) // per, 0))
        in_specs.append(spec)
        out_specs.append(spec)
        out_shape.append(jax.ShapeDtypeStruct(w.shape, BF16))
    return in_specs, out_specs, out_shape


def _mix0(u, q, k, vt, cache, x, m, norm_g, conv_w, conv_b, ln_g, ln_b, lam, subln_g, w_out,
          *, batch, seq, tq, cond_row, lam_init, side=(), nseq=1):
    nq = seq // tq
    n_cache = 0 if cache is None else cache[0].shape[1]
    win = tq + 2 * CONV_PAD
    steps = (batch // nseq) * nq
    const = lambda shape: pl.BlockSpec(shape, lambda b, i: (0,) * len(shape))
    in_specs = [pl.BlockSpec((nseq * seq, CONV_CH), lambda b, i: (b, 0)),
                pl.BlockSpec((nseq * tq, DIFF_WIDTH), lambda b, i: (b * nq + i, 0)),
                pl.BlockSpec((nseq * seq, DIFF_WIDTH), lambda b, i: (b, 0)),
                pl.BlockSpec((nseq, DIFF_WIDTH, seq), lambda b, i: (b, 0, 0))]
    args = [u, q, k, vt]
    if n_cache:
        in_specs += [pl.BlockSpec((None, n_cache, DIFF_WIDTH), lambda b, i: (b, 0, 0)),
                     pl.BlockSpec((None, DIFF_WIDTH, n_cache), lambda b, i: (b, 0, 0))]
        args += [cache[0], cache[1]]
    in_specs += [
        pl.BlockSpec((nseq * tq, D_MODEL), lambda b, i: (b * nq + i, 0)),
        pl.BlockSpec((None, N_MOD, D_MODEL), lambda b, i: (cond_row(b * nseq), 0, 0)),
        const((4, D_MODEL)), const((CONV_WIDTH, 8, CONV_CH)), const((1, CONV_CH)), const((1, CONV_CH)),
        const((1, CONV_CH)), const((4, LANES // 2)), const((LANES, 1)), _resident((D_MODEL, D_MODEL)),
    ]
    args += [x, m, norm_g, conv_w, conv_b, ln_g, ln_b, lam, subln_g, w_out]
    side_in, side_out, side_shape = _side_specs(side, steps, lambda b, i: b * nq + i)
    return pl.pallas_call(
        functools.partial(_mix0_kernel, seq=seq, tq=tq, n_cache=n_cache, lam_init=lam_init, n_side=len(side),
                          nseq=nseq),
        grid=(batch // nseq, nq),
        in_specs=in_specs + side_in,
        out_specs=[pl.BlockSpec((nseq * tq, D_MODEL), lambda b, i: (b * nq + i, 0))] + side_out,
        out_shape=[jax.ShapeDtypeStruct((batch * seq, D_MODEL), F32)] + side_shape,
        scratch_shapes=[
            pltpu.VMEM((seq + 2 * CONV_PAD, CONV_CH), F32), pltpu.VMEM((win, CONV_CH), F32),
            pltpu.VMEM((7, win - 8, CONV_CH), F32),
        ],
        compiler_params=_params("arbitrary", "arbitrary"),
        name="mix_conv_diff",
    )(*args, *side)


def _mix1_kernel(*refs, seq, tq, n_cache, nseq):
    if n_cache:
        (q_ref, kp_ref, kc_ref, kn_ref, vtp_ref, vtc_ref, vtn_ref, ck_ref, cvt_ref,
         x_ref, m_ref, ng_ref, sink_ref, wo_ref, o_ref) = refs
    else:
        (q_ref, kc_ref, vtc_ref, x_ref, m_ref, ng_ref, sink_ref, wo_ref, o_ref) = refs
    qi = pl.program_id(1)
    lane = lax.broadcasted_iota(jnp.int32, (tq, LANES), 1)

    def band(base, rows):
        kpos = base + lax.broadcasted_iota(jnp.int32, (rows, tq), 0)
        qpos = qi * tq + lax.broadcasted_iota(jnp.int32, (rows, tq), 1)
        ok = (jnp.abs(qpos - kpos) <= WINDOW) & (kpos >= 0) & (kpos < seq)
        return jnp.concatenate([ok] * GQA_GROUP, axis=1)

    def attention(r0, segs):
        def scores(g):
            b = g % 2
            kcols = slice((g // 2) * LANES, (g // 2 + 1) * LANES)
            keep = (lane < HEAD_DIM) if b == 0 else (lane >= HEAD_DIM)
            qs = []
            for h in range(g * GQA_GROUP, (g + 1) * GQA_GROUP):
                qc = q_ref[r0:r0 + tq, (h // 2) * LANES:(h // 2 + 1) * LANES]
                qa = qc if h % 2 == b else pltpu.roll(qc, HEAD_DIM, 1)
                qs.append(jnp.where(keep, qa, jnp.zeros_like(qa)))
            qg = jnp.concatenate(qs, axis=0)
            s_list = []
            for k_ref, _, ok in segs:
                s = _dot_nt(k_ref[:, kcols].astype(BF16), qg)
                s_list.append(s if ok is None else jnp.where(ok, s, NEG_INF))
            return s_list

        def group(g, s_list):
            vrows = slice(g * HEAD_DIM, (g + 1) * HEAD_DIM)
            sink = jnp.concatenate([jnp.full((1, tq), sink_ref[h] * LOG2E, F32)
                                    for h in range(g * GQA_GROUP, (g + 1) * GQA_GROUP)], axis=1)
            ov, den, mx = _attend(s_list, [vt_ref[vrows, :].astype(BF16) for _, vt_ref, _ in segs], floor=sink)
            ov = ov * (1.0 / (den + jnp.exp2(sink - mx)))
            return jnp.concatenate([ov[:, hh * tq:(hh + 1) * tq] for hh in range(GQA_GROUP)], axis=0)

        groups = _pipelined(GQA_KV_HEADS, scores, group)
        return jnp.concatenate(groups, axis=0).T.astype(BF16)

    if n_cache:
        attn = attention(0, [(kp_ref, vtp_ref, band(qi * tq - WINDOW, WINDOW)),
                             (kc_ref, vtc_ref, band(qi * tq, tq)),
                             (kn_ref, vtn_ref, band(qi * tq + tq, WINDOW)),
                             (ck_ref, cvt_ref, None)])
    else:
        attn = jnp.concatenate([attention(s * tq, [(kc_ref.at[pl.ds(s * tq, tq)], vtc_ref.at[s], None)])
                                for s in range(nseq)], axis=0)
    mixed = _dot(attn, wo_ref[...])
    o_ref[...] = x_ref[...] + _rms(mixed, m_ref[2:3, :] * ng_ref[1:2, :])


def _mix1(q, k, vt, cache, x, m, norm_g, sink, w_out, *, batch, seq, tq, cond_row, nseq=1):
    nq = seq // tq
    n_cache = 0 if cache is None else cache[0].shape[1]
    rows = nseq * tq
    q_spec = pl.BlockSpec((rows, D_MODEL), lambda b, i: (b * nq + i, 0))
    kc_spec = pl.BlockSpec((rows, KV_WIDTH), lambda b, i: (b * nq + i, 0))
    vtc_spec = pl.BlockSpec((None, KV_WIDTH, tq), lambda b, i: (b, 0, i))
    if n_cache:
        nw = seq // WINDOW
        per = tq // WINDOW
        prev = lambda i: jnp.maximum(i * per - 1, 0)
        nxt = lambda i: jnp.minimum(i * per + per, nw - 1)
        in_specs = [q_spec,
                    pl.BlockSpec((WINDOW, KV_WIDTH), lambda b, i: (b * nw + prev(i), 0)), kc_spec,
                    pl.BlockSpec((WINDOW, KV_WIDTH), lambda b, i: (b * nw + nxt(i), 0)),
                    pl.BlockSpec((None, KV_WIDTH, WINDOW), lambda b, i: (b, 0, prev(i))), vtc_spec,
                    pl.BlockSpec((None, KV_WIDTH, WINDOW), lambda b, i: (b, 0, nxt(i))),
                    pl.BlockSpec((None, n_cache, KV_WIDTH), lambda b, i: (b, 0, 0)),
                    pl.BlockSpec((None, KV_WIDTH, n_cache), lambda b, i: (b, 0, 0))]
        args = [q, k, k, k, vt, vt, vt, cache[0], cache[1]]
    else:
        in_specs = [q_spec, kc_spec, pl.BlockSpec((nseq, KV_WIDTH, tq), lambda b, i: (b, 0, 0))]
        args = [q, k, vt]
    in_specs += [
        pl.BlockSpec((rows, D_MODEL), lambda b, i: (b * nq + i, 0)),
        pl.BlockSpec((None, N_MOD, D_MODEL), lambda b, i: (cond_row(b * nseq), 0, 0)),
        pl.BlockSpec((4, D_MODEL), lambda b, i: (0, 0)),
        pl.BlockSpec(memory_space=pltpu.SMEM),
        pl.BlockSpec((D_MODEL, D_MODEL), lambda b, i: (0, 0)),
    ]
    args += [x, m, norm_g, sink, w_out]
    return pl.pallas_call(
        functools.partial(_mix1_kernel, seq=seq, tq=tq, n_cache=n_cache, nseq=nseq),
        grid=(batch // nseq, nq),
        in_specs=in_specs,
        out_specs=pl.BlockSpec((rows, D_MODEL), lambda b, i: (b * nq + i, 0)),
        out_shape=jax.ShapeDtypeStruct((batch * seq, D_MODEL), F32),
        compiler_params=_params("arbitrary", "arbitrary"),
        name="mix_gqa",
    )(*args)


FFN_CHUNKS = (1536, 1280)


def _ffn_kernel(*refs, n_side, proj):
    x_ref, m_ref, ng_ref, wgu_ref, wd_ref = refs[:5]
    n_pin = 0 if proj is None else (5 if proj[1] else 3)
    n_pout = 0 if proj is None else len(proj[0])
    n_in = 5 + n_pin + n_side
    o_ref = refs[n_in]
    _side_convert(refs[5 + n_pin:n_in], refs[n_in + 1 + n_pout:])
    h = (_rms(x_ref[...], ng_ref[2:3, :] * (1.0 + m_ref[4:5, :])) + m_ref[3:4, :]).astype(BF16)
    acc = jnp.zeros(h.shape, F32)
    c0 = 0
    for width in FFN_CHUNKS:
        gate = _dot(h, wgu_ref[:, c0:c0 + width])
        up = _dot(h, wgu_ref[:, FFN_HIDDEN + c0:FFN_HIDDEN + c0 + width])
        act = (gate * _sigmoid(gate) * up).astype(BF16)
        acc = acc + _dot(act, wd_ref[c0:c0 + width, :])
        c0 += width
    y = x_ref[...] + _rms(acc, m_ref[5:6, :] * ng_ref[3:4, :])
    o_ref[...] = y
    if proj is not None:
        outs, rope, sblk = proj
        pm_ref, pg_ref, pw_ref = refs[5:8]
        _project(y, pm_ref, pg_ref, pw_ref, refs[8:10] if rope else None, outs, refs[n_in + 1:n_in + 1 + n_pout], sblk)


def _ffn(x, m, norm_g, w_gu, w_down, *, tm, cond_row, side=(), proj=None):
    rows = x.shape[0]
    side_in, side_out, side_shape = _side_specs(side, rows // tm, lambda i: i)
    proj_in, proj_args, proj_out, proj_shape, kproj = [], [], [], [], None
    if proj is not None:
        pm, pg, pw, outs, seq, rope = proj
        proj_in, proj_args, proj_out, proj_shape, sblk = _project_specs(pm, pg, pw, outs, rows=rows, tm=tm,
                                                                        cond_row=cond_row, seq=seq, rope=rope)
        kproj = (outs, rope, sblk)
    return pl.pallas_call(
        functools.partial(_ffn_kernel, n_side=len(side), proj=kproj),
        grid=(rows // tm,),
        in_specs=[
            pl.BlockSpec((tm, D_MODEL), lambda i: (i, 0)),
            pl.BlockSpec((None, N_MOD, D_MODEL), lambda i: (cond_row(i), 0, 0)),
            pl.BlockSpec((4, D_MODEL), lambda i: (0, 0)),
            _resident((D_MODEL, 2 * FFN_HIDDEN)),
            _resident((FFN_HIDDEN, D_MODEL)),
        ] + proj_in + side_in,
        out_specs=[pl.BlockSpec((tm, D_MODEL), lambda i: (i, 0))] + proj_out + side_out,
        out_shape=[jax.ShapeDtypeStruct((rows, D_MODEL), F32)] + proj_shape + side_shape,
        compiler_params=_params("arbitrary"),
        name="ffn",
    )(x, m, norm_g, w_gu, w_down, *proj_args, *side)


LAM_INIT_0 = 0.8 - 0.6 * math.exp(-0.3 * 0)
CTX_SEQS_PER_STEP = 4


def _trunk(x, m, caches, weights, *, batch, seq, cond_of_batch):
    (norm_g0, w_in, conv_w, conv_b, ln_g, ln_b, lam, subln_g, w_out0, w_gu0, w_down0,
     norm_g1, w_qkv, sink, w_out1, w_gu1, w_down1) = weights
    m0, m1 = m
    ctx = caches is None
    rope = not ctx
    tm = 512
    tm_ffn = 512
    tq = 256
    row_of = lambda t: (lambda i: cond_of_batch((i * t) // seq))

    o0 = 2 * CONV_CH
    ko, vo = o0 + DIFF_WIDTH, o0 + 2 * DIFF_WIDTH
    outs0 = [("glu", 0, o0, F32, False, None), ("nat", o0, DIFF_WIDTH, BF16, True, Q_SCALE),
             ("nat", ko, DIFF_WIDTH, BF16, True, None), ("T", vo, DIFF_WIDTH, BF16, False, None)]
    if ctx:
        outs0 += [("T", ko, DIFF_WIDTH, F32, False, None), ("heads", vo, DIFF_WIDTH, F32, False, None)]
    res = _pre(x, m0, norm_g0, w_in, tuple(outs0), tm=tm, cond_row=row_of(tm), seq=seq, rope=rope)
    u, q, k, vt = res[:4]
    new0 = tuple(res[4:])
    x, *conv0 = _mix0(u, q, k, vt, None if ctx else caches[0], x, m0, norm_g0, conv_w, conv_b,
                      ln_g, ln_b, lam, subln_g, w_out0, batch=batch, seq=seq, tq=tq, cond_row=cond_of_batch,
                      lam_init=LAM_INIT_0, side=(w_gu0, w_down0, w_qkv) if ctx else (),
                      nseq=CTX_SEQS_PER_STEP if ctx else 1)
    if ctx:
        w_gu0, w_down0, w_qkv = conv0

    outs1 = [("nat", 0, Q_WIDTH, BF16, True, Q_SCALE), ("nat", Q_WIDTH, KV_WIDTH, BF16, True, None)]
    if ctx:
        outs1 += [("T", Q_WIDTH + KV_WIDTH, KV_WIDTH, F32, False, None), ("T", Q_WIDTH, KV_WIDTH, F32, False, None)]
    else:
        outs1 += [("T", Q_WIDTH + KV_WIDTH, KV_WIDTH, BF16, False, None)]
    res = _ffn(x, m0, norm_g0, w_gu0, w_down0, tm=tm_ffn, cond_row=row_of(tm_ffn),
               side=(w_out1, w_gu1, w_down1) if ctx else (),
               proj=(m1, norm_g1, w_qkv, tuple(outs1), seq, rope))
    x, q, k, vt = res[:4]
    new1 = (res[4], vt) if ctx else ()
    if ctx:
        w_out1, w_gu1, w_down1 = res[5:]
    x = _mix1(q, k, vt, None if ctx else caches[1], x, m1, norm_g1, sink, w_out1,
              batch=batch, seq=seq, tq=tq, cond_row=cond_of_batch, nseq=CTX_SEQS_PER_STEP if ctx else 1)
    x, = _ffn(x, m1, norm_g1, w_gu1, w_down1, tm=tm_ffn, cond_row=row_of(tm_ffn))
    return x, new0 + new1, (w_gu0, w_down0, w_qkv, w_out1, w_gu1, w_down1)


def kernel(x_prompt, x_sample, cache_k0, cache_v0, cache_k1, cache_v1, c, c_ctx, l0_mod_w, l0_mod_b, l0_norm_g, l0_w_in, l0_conv_w, l0_conv_b, l0_conv_ln_g, l0_conv_ln_b, l0_lambda, l0_subln_g, l0_w_out, l0_w_gu, l0_w_down, l1_mod_w, l1_mod_b, l1_norm_g, l1_w_qkv, l1_sink, l1_w_out, l1_w_gu, l1_w_down):
    batch, seq, d = x_prompt.shape
    dec_batch, dec_seq, _ = x_sample.shape
    n_past = cache_k0.shape[1]

    cond = jnp.concatenate([c_ctx[None, :], c, jnp.zeros((MOD_ROWS - 1 - dec_batch, d), F32)], axis=0)
    m = (_modulation(cond, l0_mod_w, l0_mod_b), _modulation(cond, l1_mod_w, l1_mod_b))

    row = lambda v: v.reshape(1, -1)
    layer0 = (l0_norm_g, l0_w_in.astype(BF16), jnp.broadcast_to(l0_conv_w, (CONV_WIDTH, 8, CONV_CH)), row(l0_conv_b),
              row(l0_conv_ln_g), row(l0_conv_ln_b), l0_lambda, l0_subln_g.reshape(LANES, 1),
              l0_w_out.astype(BF16))
    weights = layer0 + (l0_w_gu, l0_w_down, l1_norm_g, l1_w_qkv, l1_sink, l1_w_out, l1_w_gu, l1_w_down)

    y_prompt, (kt0, v0, kt1, vt1), bf16_w = _trunk(x_prompt.reshape(batch * seq, d), m, None, weights,
                                                   batch=batch, seq=seq, cond_of_batch=lambda b: 0)
    w_gu0, w_down0, w_qkv, w_out1, w_gu1, w_down1 = bf16_w
    weights = layer0 + (w_gu0, w_down0, l1_norm_g, w_qkv, l1_sink, w_out1, w_gu1, w_down1)
    caches = ((cache_k0.reshape(dec_batch, n_past, -1),
               jnp.swapaxes(cache_v0.reshape(dec_batch, n_past, -1), 1, 2)),
              (cache_k1.reshape(dec_batch, n_past, -1),
               jnp.swapaxes(cache_v1.reshape(dec_batch, n_past, -1), 1, 2)))
    y_sample, _, _ = _trunk(x_sample.reshape(dec_batch * dec_seq, d), m, caches, weights,
                            batch=dec_batch, seq=dec_seq, cond_of_batch=lambda b: 1 + b)
    new_k0 = kt0.reshape(batch, DIFF_HEADS, 2, HEAD_DIM, seq).transpose(0, 4, 1, 2, 3)
    new_v0 = v0.reshape(batch, seq, DIFF_HEADS, 2 * HEAD_DIM)
    new_k1 = kt1.reshape(batch, GQA_KV_HEADS, HEAD_DIM, seq).transpose(0, 3, 1, 2)
    new_v1 = vt1.reshape(batch, GQA_KV_HEADS, HEAD_DIM, seq).transpose(0, 3, 1, 2)
    return (y_prompt.reshape(batch, seq, d), y_sample.reshape(dec_batch, dec_seq, d),
            new_k0, new_v0, new_k1, new_v1)
```

```python
import functools
import math

import numpy as np
import jax
import jax.numpy as jnp
from jax import lax
from jax.experimental import pallas as pl
from jax.experimental.pallas import tpu as pltpu

D_MODEL = 1024
GRID_W = 64
ROPE_THETA = 10000.0
EPS = 1e-6
NEG_INF = -1e30
LOG2E = math.log2(math.e)
CONV_CH = 512
CONV_WIDTH = 31
CONV_PAD = 16
DIFF_HEADS = 4
DIFF_WIDTH = 512
HEAD_DIM = 64
GQA_HEADS = 16
GQA_KV_HEADS = 4
GQA_GROUP = 4
Q_WIDTH = GQA_HEADS * HEAD_DIM
KV_WIDTH = GQA_KV_HEADS * HEAD_DIM
Q_SCALE = LOG2E * HEAD_DIM ** -0.5
WINDOW = 128
FFN_HIDDEN = 2816
N_MOD = 6
LANES = 128
BF16_ROWS = 16
MOD_ROWS = 8
VMEM_LIMIT = 56 * 1024 * 1024

BF16 = jnp.bfloat16
F32 = jnp.float32


def _gated(a, g):
    return (0.5 * a) * (jnp.tanh(0.5 * g) + 1.0)


def _silu(x):
    h = 0.5 * x
    return h * (jnp.tanh(h) + 1.0)


def _rms(x, g):
    return x * lax.rsqrt(jnp.mean(x * x, axis=-1, keepdims=True) + EPS) * g


def _dot(a, b):
    return jnp.dot(a, b, preferred_element_type=F32)


def _dot_nt(a, b):
    return lax.dot_general(a, b, (((1,), (1,)), ((), ())), preferred_element_type=F32)


def _params(*sem):
    return pltpu.CompilerParams(dimension_semantics=sem, vmem_limit_bytes=VMEM_LIMIT)


def _resident(shape):
    return pl.BlockSpec(shape, lambda *_: (0,) * len(shape), pipeline_mode=pl.Buffered(1))


def _mod_kernel(c_ref, w_ref, b_ref, o_ref, *, n_rows):
    st = _silu(c_ref[...]).T
    w = w_ref[...]
    rows = [jnp.sum(w * st[:, r:r + 1], axis=0, keepdims=True) for r in range(n_rows)]
    rows += [jnp.zeros_like(rows[0])] * (MOD_ROWS - n_rows)
    o_ref[...] = jnp.concatenate(rows, axis=0) + b_ref[...]


def _modulation(cond, mod_w, mod_b, n_rows):
    n = mod_w.shape[1]
    tn = 1536
    out = pl.pallas_call(
        functools.partial(_mod_kernel, n_rows=n_rows),
        grid=(n // tn,),
        in_specs=[
            pl.BlockSpec((MOD_ROWS, D_MODEL), lambda j: (0, 0)),
            pl.BlockSpec((D_MODEL, tn), lambda j: (0, j)),
            pl.BlockSpec((1, tn), lambda j: (0, j)),
        ],
        out_specs=pl.BlockSpec((MOD_ROWS, tn), lambda j: (0, j)),
        out_shape=jax.ShapeDtypeStruct((MOD_ROWS, n), F32),
        compiler_params=_params("arbitrary"),
        name="modulation",
    )(cond, mod_w, mod_b.reshape(1, n))
    return out.reshape(MOD_ROWS, N_MOD, D_MODEL)


def _rope_tables(seq):
    t = np.arange(seq)
    rows, cols = t // GRID_W, t % GRID_W
    half = HEAD_DIM // 2
    inv = 1.0 / (ROPE_THETA ** (np.arange(0, half, 2, dtype=np.float64) / half))
    ar = rows[:, None] * inv[None, :]
    ac = cols[:, None] * inv[None, :]
    cos = np.concatenate([np.cos(ar), np.cos(ar), np.cos(ac), np.cos(ac)], axis=1)
    sin = np.concatenate([-np.sin(ar), np.sin(ar), -np.sin(ac), np.sin(ac)], axis=1)
    reps = LANES // HEAD_DIM
    return (jnp.asarray(np.tile(cos, (1, reps)), F32), jnp.asarray(np.tile(sin, (1, reps)), F32))


def _rope(x, cos, sin):
    quarter = HEAD_DIM // 4
    lane = lax.broadcasted_iota(jnp.int32, x.shape, 1)
    lo = (lane % (2 * quarter)) < quarter
    partner = jnp.where(lo, pltpu.roll(x, LANES - quarter, 1), pltpu.roll(x, quarter, 1))
    return x * cos + partner * sin


def _pre_kernel(*refs, outs, rope, sblk):
    n_in = 6 if rope else 4
    x_ref, m_ref, g_ref, w_ref = refs[:4]
    _project(x_ref[...], m_ref, g_ref, w_ref, refs[4:6] if rope else None, outs, refs[n_in:], sblk)


def _project(x, m_ref, g_ref, w_ref, trig, outs, o_refs, sblk):
    rope = trig is not None
    tm = x.shape[0]
    h = (_rms(x, g_ref[0:1, :] * (1.0 + m_ref[1:2, :])) + m_ref[0:1, :]).astype(BF16)
    cost = lambda o: o[2] * {"T": 5.0, "heads": 0.5, "glu": 1.2}.get(o[0], 3.0 if (rope and o[4]) else 1.0)
    ranges = sorted({(o[1], o[2]) for o in outs},
                    key=lambda r: (-sum(cost(o) for o in outs if (o[1], o[2]) == r), r[0]))
    for start, width in ranges:
        p = _dot(h, w_ref[:, start:start + width])
        for (kind, s0, w0, _, roped, scale), o_ref in zip(outs, o_refs):
            if (s0, w0) != (start, width):
                continue
            if kind == "nat":
                for c in range(width // LANES):
                    xc = p[:, c * LANES:(c + 1) * LANES]
                    if rope and roped:
                        xc = _rope(xc, trig[0][...], trig[1][...])
                    if scale is not None:
                        xc = xc * scale
                    o_ref[:, c * LANES:(c + 1) * LANES] = xc.astype(o_ref.dtype)
            elif kind == "T":
                for s in range(tm // sblk):
                    o_ref[s] = p[s * sblk:(s + 1) * sblk, :].T.astype(o_ref.dtype)
            elif kind == "glu":
                o_ref[...] = _gated(p[:, 0:width // 2], p[:, width // 2:width])
            else:
                heads = width // LANES
                for hd in range(heads):
                    o_ref[pl.ds(hd, tm, stride=heads), :] = p[:, hd * LANES:(hd + 1) * LANES]


def _project_specs(m, norm_g, w, outs, *, rows, tm, cond_row, seq, rope):
    n = w.shape[1]
    sblk = min(seq, tm)
    per_seq = seq // sblk
    in_specs = [
        pl.BlockSpec((None, N_MOD, D_MODEL), lambda i: (cond_row(i), 0, 0)),
        pl.BlockSpec((4, D_MODEL), lambda i: (0, 0)),
        _resident((D_MODEL, n)),
    ]
    args = [m, norm_g, w]
    if rope:
        cos, sin = _rope_tables(seq)
        nblk = seq // tm
        in_specs += [pl.BlockSpec((tm, LANES), lambda i: (i % nblk, 0))] * 2
        args += [cos, sin]
    out_specs, out_shape = [], []
    for (kind, _, wd, dt, _, _) in outs:
        if kind == "nat":
            out_specs.append(pl.BlockSpec((tm, wd), lambda i: (i, 0)))
            out_shape.append(jax.ShapeDtypeStruct((rows, wd), dt))
        elif kind == "T":
            out_specs.append(pl.BlockSpec((tm // sblk, wd, sblk), lambda i: (i // per_seq, 0, i % per_seq)))
            out_shape.append(jax.ShapeDtypeStruct((rows // seq, wd, seq), dt))
        elif kind == "glu":
            out_specs.append(pl.BlockSpec((tm, wd // 2), lambda i: (i, 0)))
            out_shape.append(jax.ShapeDtypeStruct((rows, wd // 2), dt))
        else:
            out_specs.append(pl.BlockSpec((tm * (wd // LANES), LANES), lambda i: (i, 0)))
            out_shape.append(jax.ShapeDtypeStruct((rows * (wd // LANES), LANES), dt))
    return in_specs, args, out_specs, out_shape, sblk


def _pre(x, m, norm_g, w, outs, *, tm, cond_row, seq, rope):
    rows = x.shape[0]
    in_specs = [pl.BlockSpec((tm, D_MODEL), lambda i: (i, 0))]
    proj_in, proj_args, out_specs, out_shape, sblk = _project_specs(m, norm_g, w, outs, rows=rows, tm=tm,
                                                                    cond_row=cond_row, seq=seq, rope=rope)
    in_specs += proj_in
    args = [x] + proj_args
    return pl.pallas_call(
        functools.partial(_pre_kernel, outs=outs, rope=rope, sblk=sblk),
        grid=(rows // tm,),
        in_specs=in_specs,
        out_specs=out_specs,
        out_shape=out_shape,
        compiler_params=_params("arbitrary"),
        name="pre_rope" if rope else "pre",
    )(*args)


def _attend(s_list, vt_list, floor=None):
    mx = None
    for s in s_list:
        smax = jnp.max(s, axis=0, keepdims=True)
        mx = smax if mx is None else jnp.maximum(mx, smax)
    if floor is not None:
        mx = jnp.maximum(mx, floor)
    d = vt_list[0].shape[0]
    ov = None
    for s, vt in zip(s_list, vt_list):
        ones = jnp.ones((BF16_ROWS, vt.shape[1]), BF16)
        pv = _dot(jnp.concatenate([vt, ones], axis=0), jnp.exp2(s - mx).astype(BF16))
        ov = pv if ov is None else ov + pv
    return ov[0:d, :], ov[d:d + 1, :], mx


def _pipelined(n, first, second):
    out = []
    nxt = first(0)
    for u in range(n):
        cur = nxt
        if u + 1 < n:
            nxt = first(u + 1)
        out.append(second(u, cur))
    return out


def _side_convert(src_refs, dst_refs):
    for src, dst in zip(src_refs, dst_refs):
        dst[...] = src[...].astype(BF16)


def _mix0_kernel(*refs, seq, tq, n_cache, lam_init, n_side, nseq):
    n_in = 16 if n_cache else 14
    z_ref, q_ref, k_ref, vt_ref = refs[:4]
    ck_ref, cvt_ref = refs[4:6] if n_cache else (None, None)
    x_ref, m_ref, ng_ref, cw_ref, cb_ref, lg_ref, lb_ref, lam_ref, sg_ref, wo_ref = refs[n_in - 10:n_in]
    o_ref = refs[n_in + n_side]
    z_s, zw_s, zsh_s = refs[n_in + 2 * n_side + 1:]
    _side_convert(refs[n_in:n_in + n_side], refs[n_in + n_side + 1:n_in + 2 * n_side + 1])
    qi = pl.program_id(1)
    whole = tq == seq
    win = tq + 2 * CONV_PAD
    span = win - 8
    chunk = 64
    off0 = CONV_PAD - CONV_WIDTH // 2
    la = jnp.sum(lam_ref[0:1, :] * lam_ref[1:2, :], axis=-1, keepdims=True)
    lb = jnp.sum(lam_ref[2:3, :] * lam_ref[3:4, :], axis=-1, keepdims=True)
    lam = jnp.exp(la) - jnp.exp(lb) + lam_init
    sub = LANES // 2
    lane = lax.broadcasted_iota(jnp.int32, (tq, LANES), 1)

    def block(s):
        urows = slice(s * seq, (s + 1) * seq)

        def pad_sequence():
            z_s[0:CONV_PAD, :] = jnp.zeros((CONV_PAD, CONV_CH), F32)
            z_s[CONV_PAD + seq:2 * CONV_PAD + seq, :] = jnp.zeros((CONV_PAD, CONV_CH), F32)
            z_s[CONV_PAD:CONV_PAD + seq, :] = z_ref[urows, :]

        if whole:
            pad_sequence()
            zw = z_s
        else:
            pl.when(qi == 0)(pad_sequence)
            zw_s[...] = z_s[pl.ds(pl.multiple_of(qi * tq, tq), win), :]
            zw = zw_s

        for b in range(1, 8):
            zsh_s[b - 1, 0:span, :] = zw[b:b + span, :]
        conv_rows = []
        for c0 in range(0, tq, chunk):
            acc = jnp.zeros((chunk // 8, 8, CONV_CH), F32)
            for j in range(CONV_WIDTH):
                off = j + off0
                a, b = off // 8, off % 8
                if b == 0:
                    zz = zw[c0 + 8 * a:c0 + 8 * a + chunk, :]
                else:
                    zz = zsh_s[b - 1, c0 + 8 * a:c0 + 8 * a + chunk, :]
                acc = acc + zz.reshape(chunk // 8, 8, CONV_CH) * cw_ref[j]
            acc = acc.reshape(chunk, CONV_CH) + cb_ref[...]
            mu = jnp.mean(acc, axis=-1, keepdims=True)
            xc = acc - mu
            y = xc * lax.rsqrt(jnp.mean(xc * xc, axis=-1, keepdims=True) + EPS) * lg_ref[...] + lb_ref[...]
            conv_rows.append(_silu(y).astype(BF16))

        def segments(h):
            cols = slice(h * LANES, (h + 1) * LANES)
            segs = [(k_ref[urows, cols], vt_ref[s, cols, :])]
            if n_cache:
                segs.append((ck_ref[:, cols].astype(BF16), cvt_ref[cols, :].astype(BF16)))
            return segs

        def scores(h):
            qc = q_ref[s * tq:(s + 1) * tq, h * LANES:(h + 1) * LANES]
            zero = jnp.zeros_like(qc)
            qq = jnp.concatenate([jnp.where(lane < sub, qc, zero), jnp.where(lane >= sub, qc, zero)], axis=0)
            return [_dot_nt(kk, qq) for kk, _ in segments(h)]

        def head(h, s_list):
            ov, den, _ = _attend(s_list, [vt for _, vt in segments(h)])
            r = 1.0 / den
            o = ov[:, 0:tq] * r[:, 0:tq] - ov[:, tq:2 * tq] * (lam * r[:, tq:2 * tq])
            return o * lax.rsqrt(jnp.mean(o * o, axis=0, keepdims=True) + EPS) * sg_ref[...] * (1.0 - lam_init)

        heads = _pipelined(DIFF_HEADS, scores, head)
        attn = jnp.concatenate(heads, axis=0).T.astype(BF16)
        return jnp.concatenate([jnp.concatenate(conv_rows, axis=0), attn], axis=1)

    cat = jnp.concatenate([block(s) for s in range(nseq)], axis=0)
    mixed = _dot(cat, wo_ref[...])
    o_ref[...] = x_ref[...] + _rms(mixed, m_ref[2:3, :] * ng_ref[1:2, :])


def _side_specs(side, steps, index):
    in_specs, out_specs, out_shape = [], [], []
    for w in side:
        rows, cols = w.shape
        n_blk = math.gcd(steps, rows // BF16_ROWS)
        per = steps // n_blk
        spec = pl.BlockSpec((rows // n_blk, cols), lambda *ids, per=per: (index(*ids) // per, 0))
        in_specs.append(spec)
        out_specs.append(spec)
        out_shape.append(jax.ShapeDtypeStruct(w.shape, BF16))
    return in_specs, out_specs, out_shape


def _mix0(u, q, k, vt, cache, x, m, norm_g, conv_w, conv_b, ln_g, ln_b, lam, subln_g, w_out,
          *, batch, seq, tq, cond_row, lam_init, side=(), nseq=1):
    nq = seq // tq
    n_cache = 0 if cache is None else cache[0].shape[1]
    win = tq + 2 * CONV_PAD
    steps = (batch // nseq) * nq
    const = lambda shape: pl.BlockSpec(shape, lambda b, i: (0,) * len(shape))
    in_specs = [pl.BlockSpec((nseq * seq, CONV_CH), lambda b, i: (b, 0)),
                pl.BlockSpec((nseq * tq, DIFF_WIDTH), lambda b, i: (b * nq + i, 0)),
                pl.BlockSpec((nseq * seq, DIFF_WIDTH), lambda b, i: (b, 0)),
                pl.BlockSpec((nseq, DIFF_WIDTH, seq), lambda b, i: (b, 0, 0))]
    args = [u, q, k, vt]
    if n_cache:
        in_specs += [pl.BlockSpec((None, n_cache, DIFF_WIDTH), lambda b, i: (b, 0, 0)),
                     pl.BlockSpec((None, DIFF_WIDTH, n_cache), lambda b, i: (b, 0, 0))]
        args += [cache[0], cache[1]]
    in_specs += [
        pl.BlockSpec((nseq * tq, D_MODEL), lambda b, i: (b * nq + i, 0)),
        pl.BlockSpec((None, N_MOD, D_MODEL), lambda b, i: (cond_row(b * nseq), 0, 0)),
        const((4, D_MODEL)), const((CONV_WIDTH, 8, CONV_CH)), const((1, CONV_CH)), const((1, CONV_CH)),
        const((1, CONV_CH)), const((4, LANES // 2)), const((LANES, 1)), _resident((D_MODEL, D_MODEL)),
    ]
    args += [x, m, norm_g, conv_w, conv_b, ln_g, ln_b, lam, subln_g, w_out]
    side_in, side_out, side_shape = _side_specs(side, steps, lambda b, i: b * nq + i)
    return pl.pallas_call(
        functools.partial(_mix0_kernel, seq=seq, tq=tq, n_cache=n_cache, lam_init=lam_init, n_side=len(side),
                          nseq=nseq),
        grid=(batch // nseq, nq),
        in_specs=in_specs + side_in,
        out_specs=[pl.BlockSpec((nseq * tq, D_MODEL), lambda b, i: (b * nq + i, 0))] + side_out,
        out_shape=[jax.ShapeDtypeStruct((batch * seq, D_MODEL), F32)] + side_shape,
        scratch_shapes=[
            pltpu.VMEM((seq + 2 * CONV_PAD, CONV_CH), F32), pltpu.VMEM((win, CONV_CH), F32),
            pltpu.VMEM((7, win - 8, CONV_CH), F32),
        ],
        compiler_params=_params("arbitrary", "arbitrary"),
        name="mix_conv_diff",
    )(*args, *side)


def _mix1_kernel(*refs, seq, tq, n_cache, nseq):
    if n_cache:
        (q_ref, kp_ref, kc_ref, kn_ref, vtp_ref, vtc_ref, vtn_ref, ck_ref, cvt_ref,
         x_ref, m_ref, ng_ref, sink_ref, wo_ref, o_ref) = refs
    else:
        (q_ref, kc_ref, vtc_ref, x_ref, m_ref, ng_ref, sink_ref, wo_ref, o_ref) = refs
    qi = pl.program_id(1)
    lane = lax.broadcasted_iota(jnp.int32, (tq, LANES), 1)

    def band(base, rows):
        kpos = base + lax.broadcasted_iota(jnp.int32, (rows, tq), 0)
        qpos = qi * tq + lax.broadcasted_iota(jnp.int32, (rows, tq), 1)
        ok = (jnp.abs(qpos - kpos) <= WINDOW) & (kpos >= 0) & (kpos < seq)
        return jnp.concatenate([ok] * GQA_GROUP, axis=1)

    def attention(r0, segs):
        def scores(g):
            b = g % 2
            kcols = slice((g // 2) * LANES, (g // 2 + 1) * LANES)
            keep = (lane < HEAD_DIM) if b == 0 else (lane >= HEAD_DIM)
            qs = []
            for h in range(g * GQA_GROUP, (g + 1) * GQA_GROUP):
                qc = q_ref[r0:r0 + tq, (h // 2) * LANES:(h // 2 + 1) * LANES]
                qa = qc if h % 2 == b else pltpu.roll(qc, HEAD_DIM, 1)
                qs.append(jnp.where(keep, qa, jnp.zeros_like(qa)))
            qg = jnp.concatenate(qs, axis=0)
            s_list = []
            for k_ref, _, ok in segs:
                s = _dot_nt(k_ref[:, kcols].astype(BF16), qg)
                s_list.append(s if ok is None else jnp.where(ok, s, NEG_INF))
            return s_list

        def group(g, s_list):
            vrows = slice(g * HEAD_DIM, (g + 1) * HEAD_DIM)
            sink = jnp.concatenate([jnp.full((1, tq), sink_ref[h] * LOG2E, F32)
                                    for h in range(g * GQA_GROUP, (g + 1) * GQA_GROUP)], axis=1)
            ov, den, mx = _attend(s_list, [vt_ref[vrows, :].astype(BF16) for _, vt_ref, _ in segs], floor=sink)
            ov = ov * (1.0 / (den + jnp.exp2(sink - mx)))
            return jnp.concatenate([ov[:, hh * tq:(hh + 1) * tq] for hh in range(GQA_GROUP)], axis=0)

        groups = _pipelined(GQA_KV_HEADS, scores, group)
        return jnp.concatenate(groups, axis=0).T.astype(BF16)

    if n_cache:
        attn = attention(0, [(kp_ref, vtp_ref, band(qi * tq - WINDOW, WINDOW)),
                             (kc_ref, vtc_ref, band(qi * tq, tq)),
                             (kn_ref, vtn_ref, band(qi * tq + tq, WINDOW)),
                             (ck_ref, cvt_ref, None)])
    else:
        attn = jnp.concatenate([attention(s * tq, [(kc_ref.at[pl.ds(s * tq, tq)], vtc_ref.at[s], None)])
                                for s in range(nseq)], axis=0)
    mixed = _dot(attn, wo_ref[...])
    o_ref[...] = x_ref[...] + _rms(mixed, m_ref[2:3, :] * ng_ref[1:2, :])


def _mix1(q, k, vt, cache, x, m, norm_g, sink, w_out, *, batch, seq, tq, cond_row, nseq=1):
    nq = seq // tq
    n_cache = 0 if cache is None else cache[0].shape[1]
    rows = nseq * tq
    q_spec = pl.BlockSpec((rows, D_MODEL), lambda b, i: (b * nq + i, 0))
    kc_spec = pl.BlockSpec((rows, KV_WIDTH), lambda b, i: (b * nq + i, 0))
    vtc_spec = pl.BlockSpec((None, KV_WIDTH, tq), lambda b, i: (b, 0, i))
    if n_cache:
        nw = seq // WINDOW
        per = tq // WINDOW
        prev = lambda i: jnp.maximum(i * per - 1, 0)
        nxt = lambda i: jnp.minimum(i * per + per, nw - 1)
        in_specs = [q_spec,
                    pl.BlockSpec((WINDOW, KV_WIDTH), lambda b, i: (b * nw + prev(i), 0)), kc_spec,
                    pl.BlockSpec((WINDOW, KV_WIDTH), lambda b, i: (b * nw + nxt(i), 0)),
                    pl.BlockSpec((None, KV_WIDTH, WINDOW), lambda b, i: (b, 0, prev(i))), vtc_spec,
                    pl.BlockSpec((None, KV_WIDTH, WINDOW), lambda b, i: (b, 0, nxt(i))),
                    pl.BlockSpec((None, n_cache, KV_WIDTH), lambda b, i: (b, 0, 0)),
                    pl.BlockSpec((None, KV_WIDTH, n_cache), lambda b, i: (b, 0, 0))]
        args = [q, k, k, k, vt, vt, vt, cache[0], cache[1]]
    else:
        in_specs = [q_spec, kc_spec, pl.BlockSpec((nseq, KV_WIDTH, tq), lambda b, i: (b, 0, 0))]
        args = [q, k, vt]
    in_specs += [
        pl.BlockSpec((rows, D_MODEL), lambda b, i: (b * nq + i, 0)),
        pl.BlockSpec((None, N_MOD, D_MODEL), lambda b, i: (cond_row(b * nseq), 0, 0)),
        pl.BlockSpec((4, D_MODEL), lambda b, i: (0, 0)),
        pl.BlockSpec(memory_space=pltpu.SMEM),
        pl.BlockSpec((D_MODEL, D_MODEL), lambda b, i: (0, 0)),
    ]
    args += [x, m, norm_g, sink, w_out]
    return pl.pallas_call(
        functools.partial(_mix1_kernel, seq=seq, tq=tq, n_cache=n_cache, nseq=nseq),
        grid=(batch // nseq, nq),
        in_specs=in_specs,
        out_specs=pl.BlockSpec((rows, D_MODEL), lambda b, i: (b * nq + i, 0)),
        out_shape=jax.ShapeDtypeStruct((batch * seq, D_MODEL), F32),
        compiler_params=_params("arbitrary", "arbitrary"),
        name="mix_gqa",
    )(*args)


FFN_CHUNKS = (1536, 1280)


def _ffn_kernel(*refs, n_side, proj):
    x_ref, m_ref, ng_ref, wgu_ref, wd_ref = refs[:5]
    n_pin = 0 if proj is None else (5 if proj[1] else 3)
    n_pout = 0 if proj is None else len(proj[0])
    n_in = 5 + n_pin + n_side
    o_ref = refs[n_in]
    _side_convert(refs[5 + n_pin:n_in], refs[n_in + 1 + n_pout:])
    h = (_rms(x_ref[...], ng_ref[2:3, :] * (1.0 + m_ref[4:5, :])) + m_ref[3:4, :]).astype(BF16)
    acc = jnp.zeros(h.shape, F32)
    c0 = 0
    for width in FFN_CHUNKS:
        gate = _dot(h, wgu_ref[:, c0:c0 + width])
        up = _dot(h, wgu_ref[:, FFN_HIDDEN + c0:FFN_HIDDEN + c0 + width])
        act = (_silu(gate) * up).astype(BF16)
        acc = acc + _dot(act, wd_ref[c0:c0 + width, :])
        c0 += width
    y = x_ref[...] + _rms(acc, m_ref[5:6, :] * ng_ref[3:4, :])
    o_ref[...] = y
    if proj is not None:
        outs, rope, sblk = proj
        pm_ref, pg_ref, pw_ref = refs[5:8]
        _project(y, pm_ref, pg_ref, pw_ref, refs[8:10] if rope else None, outs, refs[n_in + 1:n_in + 1 + n_pout], sblk)


def _ffn(x, m, norm_g, w_gu, w_down, *, tm, cond_row, side=(), proj=None):
    rows = x.shape[0]
    side_in, side_out, side_shape = _side_specs(side, rows // tm, lambda i: i)
    proj_in, proj_args, proj_out, proj_shape, kproj = [], [], [], [], None
    if proj is not None:
        pm, pg, pw, outs, seq, rope = proj
        proj_in, proj_args, proj_out, proj_shape, sblk = _project_specs(pm, pg, pw, outs, rows=rows, tm=tm,
                                                                        cond_row=cond_row, seq=seq, rope=rope)
        kproj = (outs, rope, sblk)
    return pl.pallas_call(
        functools.partial(_ffn_kernel, n_side=len(side), proj=kproj),
        grid=(rows // tm,),
        in_specs=[
            pl.BlockSpec((tm, D_MODEL), lambda i: (i, 0)),
            pl.BlockSpec((None, N_MOD, D_MODEL), lambda i: (cond_row(i), 0, 0)),
            pl.BlockSpec((4, D_MODEL), lambda i: (0, 0)),
            _resident((D_MODEL, 2 * FFN_HIDDEN)),
            _resident((FFN_HIDDEN, D_MODEL)),
        ] + proj_in + side_in,
        out_specs=[pl.BlockSpec((tm, D_MODEL), lambda i: (i, 0))] + proj_out + side_out,
        out_shape=[jax.ShapeDtypeStruct((rows, D_MODEL), F32)] + proj_shape + side_shape,
        compiler_params=_params("arbitrary"),
        name="ffn",
    )(x, m, norm_g, w_gu, w_down, *proj_args, *side)


LAM_INIT_0 = 0.8 - 0.6 * math.exp(-0.3 * 0)
CTX_SEQS_PER_STEP = 4


def _trunk(x, m, caches, weights, *, batch, seq, cond_of_batch):
    (norm_g0, w_in, conv_w, conv_b, ln_g, ln_b, lam, subln_g, w_out0, w_gu0, w_down0,
     norm_g1, w_qkv, sink, w_out1, w_gu1, w_down1) = weights
    m0, m1 = m
    ctx = caches is None
    rope = not ctx
    tm = 512
    tm_ffn = 512
    tq = 256
    row_of = lambda t: (lambda i: cond_of_batch((i * t) // seq))

    o0 = 2 * CONV_CH
    ko, vo = o0 + DIFF_WIDTH, o0 + 2 * DIFF_WIDTH
    outs0 = [("glu", 0, o0, F32, False, None), ("nat", o0, DIFF_WIDTH, BF16, True, Q_SCALE),
             ("nat", ko, DIFF_WIDTH, BF16, True, None), ("T", vo, DIFF_WIDTH, BF16, False, None)]
    if ctx:
        outs0 += [("T", ko, DIFF_WIDTH, F32, False, None), ("heads", vo, DIFF_WIDTH, F32, False, None)]
    res = _pre(x, m0, norm_g0, w_in, tuple(outs0), tm=tm, cond_row=row_of(tm), seq=seq, rope=rope)
    u, q, k, vt = res[:4]
    new0 = tuple(res[4:])
    x, *conv0 = _mix0(u, q, k, vt, None if ctx else caches[0], x, m0, norm_g0, conv_w, conv_b,
                      ln_g, ln_b, lam, subln_g, w_out0, batch=batch, seq=seq, tq=tq, cond_row=cond_of_batch,
                      lam_init=LAM_INIT_0, side=(w_gu0, w_down0, w_qkv) if ctx else (),
                      nseq=CTX_SEQS_PER_STEP if ctx else 1)
    if ctx:
        w_gu0, w_down0, w_qkv = conv0

    outs1 = [("nat", 0, Q_WIDTH, BF16, True, Q_SCALE), ("nat", Q_WIDTH, KV_WIDTH, BF16, True, None)]
    if ctx:
        outs1 += [("T", Q_WIDTH + KV_WIDTH, KV_WIDTH, F32, False, None), ("T", Q_WIDTH, KV_WIDTH, F32, False, None)]
    else:
        outs1 += [("T", Q_WIDTH + KV_WIDTH, KV_WIDTH, BF16, False, None)]
    res = _ffn(x, m0, norm_g0, w_gu0, w_down0, tm=tm_ffn, cond_row=row_of(tm_ffn),
               side=(w_out1, w_gu1, w_down1) if ctx else (),
               proj=(m1, norm_g1, w_qkv, tuple(outs1), seq, rope))
    x, q, k, vt = res[:4]
    new1 = (res[4], vt) if ctx else ()
    if ctx:
        w_out1, w_gu1, w_down1 = res[5:]
    x = _mix1(q, k, vt, None if ctx else caches[1], x, m1, norm_g1, sink, w_out1,
              batch=batch, seq=seq, tq=tq, cond_row=cond_of_batch, nseq=CTX_SEQS_PER_STEP if ctx else 1)
    x, = _ffn(x, m1, norm_g1, w_gu1, w_down1, tm=tm_ffn, cond_row=row_of(tm_ffn))
    return x, new0 + new1, (w_gu0, w_down0, w_qkv, w_out1, w_gu1, w_down1)


def kernel(x_prompt, x_sample, cache_k0, cache_v0, cache_k1, cache_v1, c, c_ctx, l0_mod_w, l0_mod_b, l0_norm_g, l0_w_in, l0_conv_w, l0_conv_b, l0_conv_ln_g, l0_conv_ln_b, l0_lambda, l0_subln_g, l0_w_out, l0_w_gu, l0_w_down, l1_mod_w, l1_mod_b, l1_norm_g, l1_w_qkv, l1_sink, l1_w_out, l1_w_gu, l1_w_down):
    batch, seq, d = x_prompt.shape
    dec_batch, dec_seq, _ = x_sample.shape
    n_past = cache_k0.shape[1]

    cond = jnp.concatenate([c_ctx[None, :], c, jnp.zeros((MOD_ROWS - 1 - dec_batch, d), F32)], axis=0)
    m = (_modulation(cond, l0_mod_w, l0_mod_b, 1 + dec_batch), _modulation(cond, l1_mod_w, l1_mod_b, 1 + dec_batch))

    row = lambda v: v.reshape(1, -1)
    layer0 = (l0_norm_g, l0_w_in.astype(BF16), jnp.broadcast_to(l0_conv_w, (CONV_WIDTH, 8, CONV_CH)), row(l0_conv_b),
              row(l0_conv_ln_g), row(l0_conv_ln_b), l0_lambda, l0_subln_g.reshape(LANES, 1),
              l0_w_out.astype(BF16))
    weights = layer0 + (l0_w_gu, l0_w_down, l1_norm_g, l1_w_qkv, l1_sink, l1_w_out, l1_w_gu, l1_w_down)

    y_prompt, (kt0, v0, kt1, vt1), bf16_w = _trunk(x_prompt.reshape(batch * seq, d), m, None, weights,
                                                   batch=batch, seq=seq, cond_of_batch=lambda b: 0)
    w_gu0, w_down0, w_qkv, w_out1, w_gu1, w_down1 = bf16_w
    weights = layer0 + (w_gu0, w_down0, l1_norm_g, w_qkv, l1_sink, w_out1, w_gu1, w_down1)
    caches = ((cache_k0.reshape(dec_batch, n_past, -1),
               jnp.swapaxes(cache_v0.reshape(dec_batch, n_past, -1), 1, 2)),
              (cache_k1.reshape(dec_batch, n_past, -1),
               jnp.swapaxes(cache_v1.reshape(dec_batch, n_past, -1), 1, 2)))
    y_sample, _, _ = _trunk(x_sample.reshape(dec_batch * dec_seq, d), m, caches, weights,
                            batch=dec_batch, seq=dec_seq, cond_of_batch=lambda b: 1 + b)
    new_k0 = kt0.reshape(batch, DIFF_HEADS, 2, HEAD_DIM, seq).transpose(0, 4, 1, 2, 3)
    new_v0 = v0.reshape(batch, seq, DIFF_HEADS, 2 * HEAD_DIM)
    new_k1 = kt1.reshape(batch, GQA_KV_HEADS, HEAD_DIM, seq).transpose(0, 3, 1, 2)
    new_v1 = vt1.reshape(batch, GQA_KV_HEADS, HEAD_DIM, seq).transpose(0, 3, 1, 2)
    return (y_prompt.reshape(batch, seq, d), y_sample.reshape(dec_batch, dec_seq, d),
            new_k0, new_v0, new_k1, new_v1)
```

```python
import functools
import math

import numpy as np
import jax
import jax.numpy as jnp
from jax import lax
from jax.experimental import pallas as pl
from jax.experimental.pallas import tpu as pltpu

D_MODEL = 1024
GRID_W = 64
ROPE_THETA = 10000.0
EPS = 1e-6
NEG_INF = -1e30
LOG2E = math.log2(math.e)
CONV_CH = 512
CONV_WIDTH = 31
CONV_PAD = 16
CONV_ROWS = 64
DIFF_HEADS = 4
DIFF_WIDTH = 512
HEAD_DIM = 64
GQA_HEADS = 16
GQA_KV_HEADS = 4
GQA_GROUP = 4
Q_WIDTH = GQA_HEADS * HEAD_DIM
KV_WIDTH = GQA_KV_HEADS * HEAD_DIM
Q_SCALE = LOG2E * HEAD_DIM ** -0.5
WINDOW = 128
FFN_HIDDEN = 2816
N_MOD = 6
LANES = 128
BF16_ROWS = 16
MOD_ROWS = 8
VMEM_LIMIT = 56 * 1024 * 1024

BF16 = jnp.bfloat16
F32 = jnp.float32


def _gated(a, g):
    return (0.5 * a) * (jnp.tanh(0.5 * g) + 1.0)


def _silu(x):
    h = 0.5 * x
    return h * (jnp.tanh(h) + 1.0)


def _rms(x, g):
    return x * lax.rsqrt(jnp.mean(x * x, axis=-1, keepdims=True) + EPS) * g


def _dot(a, b):
    return jnp.dot(a, b, preferred_element_type=F32)


def _dot_nt(a, b):
    return lax.dot_general(a, b, (((1,), (1,)), ((), ())), preferred_element_type=F32)


def _params(*sem):
    return pltpu.CompilerParams(dimension_semantics=sem, vmem_limit_bytes=VMEM_LIMIT)


def _resident(shape):
    return pl.BlockSpec(shape, lambda *_: (0,) * len(shape), pipeline_mode=pl.Buffered(1))


def _mod_kernel(c_ref, w_ref, b_ref, o_ref):
    s = _silu(c_ref[...]).astype(BF16)
    o_ref[...] = _dot(s, w_ref[...].astype(BF16)) + b_ref[...]


def _modulation(cond, mod_w, mod_b):
    n = mod_w.shape[1]
    tn = 1536
    out = pl.pallas_call(
        _mod_kernel,
        grid=(n // tn,),
        in_specs=[
            pl.BlockSpec((MOD_ROWS, D_MODEL), lambda j: (0, 0)),
            pl.BlockSpec((D_MODEL, tn), lambda j: (0, j)),
            pl.BlockSpec((1, tn), lambda j: (0, j)),
        ],
        out_specs=pl.BlockSpec((MOD_ROWS, tn), lambda j: (0, j)),
        out_shape=jax.ShapeDtypeStruct((MOD_ROWS, n), F32),
        compiler_params=_params("arbitrary"),
        name="modulation",
    )(cond, mod_w, mod_b.reshape(1, n))
    return out.reshape(MOD_ROWS, N_MOD, D_MODEL)


def _rope_tables(seq):
    t = np.arange(seq)
    rows, cols = t // GRID_W, t % GRID_W
    half = HEAD_DIM // 2
    inv = 1.0 / (ROPE_THETA ** (np.arange(0, half, 2, dtype=np.float64) / half))
    ar = rows[:, None] * inv[None, :]
    ac = cols[:, None] * inv[None, :]
    cos = np.concatenate([np.cos(ar), np.cos(ar), np.cos(ac), np.cos(ac)], axis=1)
    sin = np.concatenate([-np.sin(ar), np.sin(ar), -np.sin(ac), np.sin(ac)], axis=1)
    reps = LANES // HEAD_DIM
    return (jnp.asarray(np.tile(cos, (1, reps)), F32), jnp.asarray(np.tile(sin, (1, reps)), F32))


def _rope(x, cos, sin):
    quarter = HEAD_DIM // 4
    lane = lax.broadcasted_iota(jnp.int32, x.shape, 1)
    lo = (lane % (2 * quarter)) < quarter
    partner = jnp.where(lo, pltpu.roll(x, LANES - quarter, 1), pltpu.roll(x, quarter, 1))
    return x * cos + partner * sin


def _pre_kernel(*refs, outs, rope, sblk):
    n_in = 6 if rope else 4
    x_ref, m_ref, g_ref, w_ref = refs[:4]
    _project(x_ref[...], m_ref, g_ref, w_ref, refs[4:6] if rope else None, outs, refs[n_in:], sblk)


def _project(x, m_ref, g_ref, w_ref, trig, outs, o_refs, sblk):
    rope = trig is not None
    tm = x.shape[0]
    h = (_rms(x, g_ref[0:1, :] * (1.0 + m_ref[1:2, :])) + m_ref[0:1, :]).astype(BF16)
    cost = lambda o: o[2] * {"T": 5.0, "heads": 0.5, "glu": 1.2}.get(o[0], 3.0 if (rope and o[4]) else 1.0)
    ranges = sorted({(o[1], o[2]) for o in outs},
                    key=lambda r: (-sum(cost(o) for o in outs if (o[1], o[2]) == r), r[0]))
    for start, width in ranges:
        p = _dot(h, w_ref[:, start:start + width])
        for (kind, s0, w0, _, roped, scale), o_ref in zip(outs, o_refs):
            if (s0, w0) != (start, width):
                continue
            if kind == "nat":
                for c in range(width // LANES):
                    xc = p[:, c * LANES:(c + 1) * LANES]
                    if rope and roped:
                        xc = _rope(xc, trig[0][...], trig[1][...])
                    if scale is not None:
                        xc = xc * scale
                    o_ref[:, c * LANES:(c + 1) * LANES] = xc.astype(o_ref.dtype)
            elif kind == "T":
                for s in range(tm // sblk):
                    o_ref[s] = p[s * sblk:(s + 1) * sblk, :].T.astype(o_ref.dtype)
            elif kind == "glu":
                o_ref[...] = _gated(p[:, 0:width // 2], p[:, width // 2:width])
            else:
                heads = width // LANES
                for hd in range(heads):
                    o_ref[pl.ds(hd, tm, stride=heads), :] = p[:, hd * LANES:(hd + 1) * LANES]


def _project_specs(m, norm_g, w, outs, *, rows, tm, cond_row, seq, rope):
    n = w.shape[1]
    sblk = min(seq, tm)
    per_seq = seq // sblk
    in_specs = [
        pl.BlockSpec((None, N_MOD, D_MODEL), lambda i: (cond_row(i), 0, 0)),
        pl.BlockSpec((4, D_MODEL), lambda i: (0, 0)),
        _resident((D_MODEL, n)),
    ]
    args = [m, norm_g, w]
    if rope:
        cos, sin = _rope_tables(seq)
        nblk = seq // tm
        in_specs += [pl.BlockSpec((tm, LANES), lambda i: (i % nblk, 0))] * 2
        args += [cos, sin]
    out_specs, out_shape = [], []
    for (kind, _, wd, dt, _, _) in outs:
        if kind == "nat":
            out_specs.append(pl.BlockSpec((tm, wd), lambda i: (i, 0)))
            out_shape.append(jax.ShapeDtypeStruct((rows, wd), dt))
        elif kind == "T":
            out_specs.append(pl.BlockSpec((tm // sblk, wd, sblk), lambda i: (i // per_seq, 0, i % per_seq)))
            out_shape.append(jax.ShapeDtypeStruct((rows // seq, wd, seq), dt))
        elif kind == "glu":
            out_specs.append(pl.BlockSpec((tm, wd // 2), lambda i: (i, 0)))
            out_shape.append(jax.ShapeDtypeStruct((rows, wd // 2), dt))
        else:
            out_specs.append(pl.BlockSpec((tm * (wd // LANES), LANES), lambda i: (i, 0)))
            out_shape.append(jax.ShapeDtypeStruct((rows * (wd // LANES), LANES), dt))
    return in_specs, args, out_specs, out_shape, sblk


def _pre(x, m, norm_g, w, outs, *, tm, cond_row, seq, rope):
    rows = x.shape[0]
    in_specs = [pl.BlockSpec((tm, D_MODEL), lambda i: (i, 0))]
    proj_in, proj_args, out_specs, out_shape, sblk = _project_specs(m, norm_g, w, outs, rows=rows, tm=tm,
                                                                    cond_row=cond_row, seq=seq, rope=rope)
    in_specs += proj_in
    args = [x] + proj_args
    return pl.pallas_call(
        functools.partial(_pre_kernel, outs=outs, rope=rope, sblk=sblk),
        grid=(rows // tm,),
        in_specs=in_specs,
        out_specs=out_specs,
        out_shape=out_shape,
        compiler_params=_params("arbitrary"),
        name="pre_rope" if rope else "pre",
    )(*args)


def _attend(s_list, vt_list, floor=None):
    mx = None
    for s in s_list:
        smax = jnp.max(s, axis=0, keepdims=True)
        mx = smax if mx is None else jnp.maximum(mx, smax)
    if floor is not None:
        mx = jnp.maximum(mx, floor)
    d = vt_list[0].shape[0]
    ov = None
    for s, vt in zip(s_list, vt_list):
        ones = jnp.ones((BF16_ROWS, vt.shape[1]), BF16)
        pv = _dot(jnp.concatenate([vt, ones], axis=0), jnp.exp2(s - mx).astype(BF16))
        ov = pv if ov is None else ov + pv
    return ov[0:d, :], ov[d:d + 1, :], mx


def _pipelined(n, first, second):
    out = []
    nxt = first(0)
    for u in range(n):
        cur = nxt
        if u + 1 < n:
            nxt = first(u + 1)
        out.append(second(u, cur))
    return out


def _side_convert(src_refs, dst_refs):
    for src, dst in zip(src_refs, dst_refs):
        dst[...] = src[...].astype(BF16)


def _mix0_kernel(*refs, seq, tq, n_cache, lam_init, n_side, nseq):
    n_in = 16 if n_cache else 14
    z_ref, q_ref, k_ref, vt_ref = refs[:4]
    ck_ref, cvt_ref = refs[4:6] if n_cache else (None, None)
    x_ref, m_ref, ng_ref, cw_ref, cb_ref, lg_ref, lb_ref, lam_ref, sg_ref, wo_ref = refs[n_in - 10:n_in]
    o_ref = refs[n_in + n_side]
    z_s, zw_s, zsh_s = refs[n_in + 2 * n_side + 1:]
    _side_convert(refs[n_in:n_in + n_side], refs[n_in + n_side + 1:n_in + 2 * n_side + 1])
    qi = pl.program_id(1)
    whole = tq == seq
    win = tq + 2 * CONV_PAD
    span = win - 8
    chunk = CONV_ROWS
    off0 = CONV_PAD - CONV_WIDTH // 2
    la = jnp.sum(lam_ref[0:1, :] * lam_ref[1:2, :], axis=-1, keepdims=True)
    lb = jnp.sum(lam_ref[2:3, :] * lam_ref[3:4, :], axis=-1, keepdims=True)
    lam = jnp.exp(la) - jnp.exp(lb) + lam_init
    sub = LANES // 2
    lane = lax.broadcasted_iota(jnp.int32, (tq, LANES), 1)

    def block(s):
        urows = slice(s * seq, (s + 1) * seq)

        def pad_sequence():
            z_s[0:CONV_PAD, :] = jnp.zeros((CONV_PAD, CONV_CH), F32)
            z_s[CONV_PAD + seq:2 * CONV_PAD + seq, :] = jnp.zeros((CONV_PAD, CONV_CH), F32)
            z_s[CONV_PAD:CONV_PAD + seq, :] = z_ref[urows, :]

        if whole:
            pad_sequence()
            zw = z_s
        else:
            pl.when(qi == 0)(pad_sequence)
            zw_s[...] = z_s[pl.ds(pl.multiple_of(qi * tq, tq), win), :]
            zw = zw_s

        for b in range(1, 8):
            zsh_s[b - 1, 0:span, :] = zw[b:b + span, :]
        conv_rows = []
        for c0 in range(0, tq, chunk):
            acc = jnp.zeros((chunk // 8, 8, CONV_CH), F32)
            for j in range(CONV_WIDTH):
                off = j + off0
                a, b = off // 8, off % 8
                if b == 0:
                    zz = zw[c0 + 8 * a:c0 + 8 * a + chunk, :]
                else:
                    zz = zsh_s[b - 1, c0 + 8 * a:c0 + 8 * a + chunk, :]
                acc = acc + zz.reshape(chunk // 8, 8, CONV_CH) * cw_ref[j]
            acc = acc.reshape(chunk, CONV_CH) + cb_ref[...]
            mu = jnp.mean(acc, axis=-1, keepdims=True)
            xc = acc - mu
            y = xc * lax.rsqrt(jnp.mean(xc * xc, axis=-1, keepdims=True) + EPS) * lg_ref[...] + lb_ref[...]
            conv_rows.append(_silu(y).astype(BF16))

        def segments(h):
            cols = slice(h * LANES, (h + 1) * LANES)
            segs = [(k_ref[urows, cols], vt_ref[s, cols, :])]
            if n_cache:
                segs.append((ck_ref[:, cols].astype(BF16), cvt_ref[cols, :].astype(BF16)))
            return segs

        def scores(h):
            qc = q_ref[s * tq:(s + 1) * tq, h * LANES:(h + 1) * LANES]
            zero = jnp.zeros_like(qc)
            qq = jnp.concatenate([jnp.where(lane < sub, qc, zero), jnp.where(lane >= sub, qc, zero)], axis=0)
            return [_dot_nt(kk, qq) for kk, _ in segments(h)]

        def head(h, s_list):
            ov, den, _ = _attend(s_list, [vt for _, vt in segments(h)])
            r = 1.0 / den
            o = ov[:, 0:tq] * r[:, 0:tq] - ov[:, tq:2 * tq] * (lam * r[:, tq:2 * tq])
            return o * lax.rsqrt(jnp.mean(o * o, axis=0, keepdims=True) + EPS) * sg_ref[...] * (1.0 - lam_init)

        heads = _pipelined(DIFF_HEADS, scores, head)
        attn = jnp.concatenate(heads, axis=0).T.astype(BF16)
        return jnp.concatenate([jnp.concatenate(conv_rows, axis=0), attn], axis=1)

    cat = jnp.concatenate([block(s) for s in range(nseq)], axis=0)
    mixed = _dot(cat, wo_ref[...])
    o_ref[...] = x_ref[...] + _rms(mixed, m_ref[2:3, :] * ng_ref[1:2, :])


def _side_specs(side, steps, index):
    in_specs, out_specs, out_shape = [], [], []
    for w in side:
        rows, cols = w.shape
        n_blk = math.gcd(steps, rows // BF16_ROWS)
        per = steps // n_blk
        spec = pl.BlockSpec((rows // n_blk, cols), lambda *ids, per=per: (index(*ids) // per, 0))
        in_specs.append(spec)
        out_specs.append(spec)
        out_shape.append(jax.ShapeDtypeStruct(w.shape, BF16))
    return in_specs, out_specs, out_shape


def _mix0(u, q, k, vt, cache, x, m, norm_g, conv_w, conv_b, ln_g, ln_b, lam, subln_g, w_out,
          *, batch, seq, tq, cond_row, lam_init, side=(), nseq=1):
    nq = seq // tq
    n_cache = 0 if cache is None else cache[0].shape[1]
    win = tq + 2 * CONV_PAD
    assert tq % CONV_ROWS == 0
    steps = (batch // nseq) * nq
    const = lambda shape: pl.BlockSpec(shape, lambda b, i: (0,) * len(shape))
    in_specs = [pl.BlockSpec((nseq * seq, CONV_CH), lambda b, i: (b, 0)),
                pl.BlockSpec((nseq * tq, DIFF_WIDTH), lambda b, i: (b * nq + i, 0)),
                pl.BlockSpec((nseq * seq, DIFF_WIDTH), lambda b, i: (b, 0)),
                pl.BlockSpec((nseq, DIFF_WIDTH, seq), lambda b, i: (b, 0, 0))]
    args = [u, q, k, vt]
    if n_cache:
        in_specs += [pl.BlockSpec((None, n_cache, DIFF_WIDTH), lambda b, i: (b, 0, 0)),
                     pl.BlockSpec((None, DIFF_WIDTH, n_cache), lambda b, i: (b, 0, 0))]
        args += [cache[0], cache[1]]
    in_specs += [
        pl.BlockSpec((nseq * tq, D_MODEL), lambda b, i: (b * nq + i, 0)),
        pl.BlockSpec((None, N_MOD, D_MODEL), lambda b, i: (cond_row(b * nseq), 0, 0)),
        const((4, D_MODEL)), const((CONV_WIDTH, 8, CONV_CH)), const((1, CONV_CH)), const((1, CONV_CH)),
        const((1, CONV_CH)), const((4, LANES // 2)), const((LANES, 1)), _resident((D_MODEL, D_MODEL)),
    ]
    args += [x, m, norm_g, conv_w, conv_b, ln_g, ln_b, lam, subln_g, w_out]
    side_in, side_out, side_shape = _side_specs(side, steps, lambda b, i: b * nq + i)
    return pl.pallas_call(
        functools.partial(_mix0_kernel, seq=seq, tq=tq, n_cache=n_cache, lam_init=lam_init, n_side=len(side),
                          nseq=nseq),
        grid=(batch // nseq, nq),
        in_specs=in_specs + side_in,
        out_specs=[pl.BlockSpec((nseq * tq, D_MODEL), lambda b, i: (b * nq + i, 0))] + side_out,
        out_shape=[jax.ShapeDtypeStruct((batch * seq, D_MODEL), F32)] + side_shape,
        scratch_shapes=[
            pltpu.VMEM((seq + 2 * CONV_PAD, CONV_CH), F32), pltpu.VMEM((win, CONV_CH), F32),
            pltpu.VMEM((7, win - 8, CONV_CH), F32),
        ],
        compiler_params=_params("arbitrary", "arbitrary"),
        name="mix_conv_diff",
    )(*args, *side)


def _mix1_kernel(*refs, seq, tq, n_cache, nseq):
    if n_cache:
        (q_ref, kp_ref, kc_ref, kn_ref, vtp_ref, vtc_ref, vtn_ref, ck_ref, cvt_ref,
         x_ref, m_ref, ng_ref, sink_ref, wo_ref, o_ref) = refs
    else:
        (q_ref, kc_ref, vtc_ref, x_ref, m_ref, ng_ref, sink_ref, wo_ref, o_ref) = refs
    qi = pl.program_id(1)
    lane = lax.broadcasted_iota(jnp.int32, (tq, LANES), 1)

    def band(base, rows):
        kpos = base + lax.broadcasted_iota(jnp.int32, (rows, tq), 0)
        qpos = qi * tq + lax.broadcasted_iota(jnp.int32, (rows, tq), 1)
        ok = (jnp.abs(qpos - kpos) <= WINDOW) & (kpos >= 0) & (kpos < seq)
        return jnp.concatenate([ok] * GQA_GROUP, axis=1)

    def attention(r0, segs):
        def scores(g):
            b = g % 2
            kcols = slice((g // 2) * LANES, (g // 2 + 1) * LANES)
            keep = (lane < HEAD_DIM) if b == 0 else (lane >= HEAD_DIM)
            qs = []
            for h in range(g * GQA_GROUP, (g + 1) * GQA_GROUP):
                qc = q_ref[r0:r0 + tq, (h // 2) * LANES:(h // 2 + 1) * LANES]
                qa = qc if h % 2 == b else pltpu.roll(qc, HEAD_DIM, 1)
                qs.append(jnp.where(keep, qa, jnp.zeros_like(qa)))
            qg = jnp.concatenate(qs, axis=0)
            s_list = []
            for k_ref, _, ok in segs:
                s = _dot_nt(k_ref[:, kcols].astype(BF16), qg)
                s_list.append(s if ok is None else jnp.where(ok, s, NEG_INF))
            return s_list

        def group(g, s_list):
            vrows = slice(g * HEAD_DIM, (g + 1) * HEAD_DIM)
            sink = jnp.concatenate([jnp.full((1, tq), sink_ref[h] * LOG2E, F32)
                                    for h in range(g * GQA_GROUP, (g + 1) * GQA_GROUP)], axis=1)
            ov, den, mx = _attend(s_list, [vt_ref[vrows, :].astype(BF16) for _, vt_ref, _ in segs], floor=sink)
            ov = ov * (1.0 / (den + jnp.exp2(sink - mx)))
            return jnp.concatenate([ov[:, hh * tq:(hh + 1) * tq] for hh in range(GQA_GROUP)], axis=0)

        groups = _pipelined(GQA_KV_HEADS, scores, group)
        return jnp.concatenate(groups, axis=0).T.astype(BF16)

    if n_cache:
        attn = attention(0, [(kp_ref, vtp_ref, band(qi * tq - WINDOW, WINDOW)),
                             (kc_ref, vtc_ref, band(qi * tq, tq)),
                             (kn_ref, vtn_ref, band(qi * tq + tq, WINDOW)),
                             (ck_ref, cvt_ref, None)])
    else:
        attn = jnp.concatenate([attention(s * tq, [(kc_ref.at[pl.ds(s * tq, tq)], vtc_ref.at[s], None)])
                                for s in range(nseq)], axis=0)
    mixed = _dot(attn, wo_ref[...])
    o_ref[...] = x_ref[...] + _rms(mixed, m_ref[2:3, :] * ng_ref[1:2, :])


def _mix1(q, k, vt, cache, x, m, norm_g, sink, w_out, *, batch, seq, tq, cond_row, nseq=1):
    nq = seq // tq
    n_cache = 0 if cache is None else cache[0].shape[1]
    rows = nseq * tq
    q_spec = pl.BlockSpec((rows, D_MODEL), lambda b, i: (b * nq + i, 0))
    kc_spec = pl.BlockSpec((rows, KV_WIDTH), lambda b, i: (b * nq + i, 0))
    vtc_spec = pl.BlockSpec((None, KV_WIDTH, tq), lambda b, i: (b, 0, i))
    if n_cache:
        nw = seq // WINDOW
        per = tq // WINDOW
        prev = lambda i: jnp.maximum(i * per - 1, 0)
        nxt = lambda i: jnp.minimum(i * per + per, nw - 1)
        in_specs = [q_spec,
                    pl.BlockSpec((WINDOW, KV_WIDTH), lambda b, i: (b * nw + prev(i), 0)), kc_spec,
                    pl.BlockSpec((WINDOW, KV_WIDTH), lambda b, i: (b * nw + nxt(i), 0)),
                    pl.BlockSpec((None, KV_WIDTH, WINDOW), lambda b, i: (b, 0, prev(i))), vtc_spec,
                    pl.BlockSpec((None, KV_WIDTH, WINDOW), lambda b, i: (b, 0, nxt(i))),
                    pl.BlockSpec((None, n_cache, KV_WIDTH), lambda b, i: (b, 0, 0)),
                    pl.BlockSpec((None, KV_WIDTH, n_cache), lambda b, i: (b, 0, 0))]
        args = [q, k, k, k, vt, vt, vt, cache[0], cache[1]]
    else:
        in_specs = [q_spec, kc_spec, pl.BlockSpec((nseq, KV_WIDTH, tq), lambda b, i: (b, 0, 0))]
        args = [q, k, vt]
    in_specs += [
        pl.BlockSpec((rows, D_MODEL), lambda b, i: (b * nq + i, 0)),
        pl.BlockSpec((None, N_MOD, D_MODEL), lambda b, i: (cond_row(b * nseq), 0, 0)),
        pl.BlockSpec((4, D_MODEL), lambda b, i: (0, 0)),
        pl.BlockSpec(memory_space=pltpu.SMEM),
        pl.BlockSpec((D_MODEL, D_MODEL), lambda b, i: (0, 0)),
    ]
    args += [x, m, norm_g, sink, w_out]
    return pl.pallas_call(
        functools.partial(_mix1_kernel, seq=seq, tq=tq, n_cache=n_cache, nseq=nseq),
        grid=(batch // nseq, nq),
        in_specs=in_specs,
        out_specs=pl.BlockSpec((rows, D_MODEL), lambda b, i: (b * nq + i, 0)),
        out_shape=jax.ShapeDtypeStruct((batch * seq, D_MODEL), F32),
        compiler_params=_params("arbitrary", "arbitrary"),
        name="mix_gqa",
    )(*args)


FFN_CHUNKS = (1536, 1280)


def _ffn_kernel(*refs, n_side, proj):
    x_ref, m_ref, ng_ref, wgu_ref, wd_ref = refs[:5]
    n_pin = 0 if proj is None else (5 if proj[1] else 3)
    n_pout = 0 if proj is None else len(proj[0])
    n_in = 5 + n_pin + n_side
    o_ref = refs[n_in]
    _side_convert(refs[5 + n_pin:n_in], refs[n_in + 1 + n_pout:])
    h = (_rms(x_ref[...], ng_ref[2:3, :] * (1.0 + m_ref[4:5, :])) + m_ref[3:4, :]).astype(BF16)
    acc = jnp.zeros(h.shape, F32)
    c0 = 0
    for width in FFN_CHUNKS:
        gate = _dot(h, wgu_ref[:, c0:c0 + width])
        up = _dot(h, wgu_ref[:, FFN_HIDDEN + c0:FFN_HIDDEN + c0 + width])
        act = (_silu(gate) * up).astype(BF16)
        acc = acc + _dot(act, wd_ref[c0:c0 + width, :])
        c0 += width
    y = x_ref[...] + _rms(acc, m_ref[5:6, :] * ng_ref[3:4, :])
    o_ref[...] = y
    if proj is not None:
        outs, rope, sblk = proj
        pm_ref, pg_ref, pw_ref = refs[5:8]
        _project(y, pm_ref, pg_ref, pw_ref, refs[8:10] if rope else None, outs, refs[n_in + 1:n_in + 1 + n_pout], sblk)


def _ffn(x, m, norm_g, w_gu, w_down, *, tm, cond_row, side=(), proj=None):
    rows = x.shape[0]
    side_in, side_out, side_shape = _side_specs(side, rows // tm, lambda i: i)
    proj_in, proj_args, proj_out, proj_shape, kproj = [], [], [], [], None
    if proj is not None:
        pm, pg, pw, outs, seq, rope = proj
        proj_in, proj_args, proj_out, proj_shape, sblk = _project_specs(pm, pg, pw, outs, rows=rows, tm=tm,
                                                                        cond_row=cond_row, seq=seq, rope=rope)
        kproj = (outs, rope, sblk)
    return pl.pallas_call(
        functools.partial(_ffn_kernel, n_side=len(side), proj=kproj),
        grid=(rows // tm,),
        in_specs=[
            pl.BlockSpec((tm, D_MODEL), lambda i: (i, 0)),
            pl.BlockSpec((None, N_MOD, D_MODEL), lambda i: (cond_row(i), 0, 0)),
            pl.BlockSpec((4, D_MODEL), lambda i: (0, 0)),
            _resident((D_MODEL, 2 * FFN_HIDDEN)),
            _resident((FFN_HIDDEN, D_MODEL)),
        ] + proj_in + side_in,
        out_specs=[pl.BlockSpec((tm, D_MODEL), lambda i: (i, 0))] + proj_out + side_out,
        out_shape=[jax.ShapeDtypeStruct((rows, D_MODEL), F32)] + proj_shape + side_shape,
        compiler_params=_params("arbitrary"),
        name="ffn",
    )(x, m, norm_g, w_gu, w_down, *proj_args, *side)


LAM_INIT_0 = 0.8 - 0.6 * math.exp(-0.3 * 0)
CTX_SEQS_PER_STEP = 4


def _trunk(x, m, caches, weights, *, batch, seq, cond_of_batch):
    (norm_g0, w_in, conv_w, conv_b, ln_g, ln_b, lam, subln_g, w_out0, w_gu0, w_down0,
     norm_g1, w_qkv, sink, w_out1, w_gu1, w_down1) = weights
    m0, m1 = m
    ctx = caches is None
    rope = not ctx
    tm = 512
    tm_ffn = 512
    tq = 256
    row_of = lambda t: (lambda i: cond_of_batch((i * t) // seq))

    o0 = 2 * CONV_CH
    ko, vo = o0 + DIFF_WIDTH, o0 + 2 * DIFF_WIDTH
    outs0 = [("glu", 0, o0, F32, False, None), ("nat", o0, DIFF_WIDTH, BF16, True, Q_SCALE),
             ("nat", ko, DIFF_WIDTH, BF16, True, None), ("T", vo, DIFF_WIDTH, BF16, False, None)]
    if ctx:
        outs0 += [("T", ko, DIFF_WIDTH, F32, False, None), ("heads", vo, DIFF_WIDTH, F32, False, None)]
    res = _pre(x, m0, norm_g0, w_in, tuple(outs0), tm=tm, cond_row=row_of(tm), seq=seq, rope=rope)
    u, q, k, vt = res[:4]
    new0 = tuple(res[4:])
    x, *conv0 = _mix0(u, q, k, vt, None if ctx else caches[0], x, m0, norm_g0, conv_w, conv_b,
                      ln_g, ln_b, lam, subln_g, w_out0, batch=batch, seq=seq, tq=tq, cond_row=cond_of_batch,
                      lam_init=LAM_INIT_0, side=(w_gu0, w_down0, w_qkv) if ctx else (),
                      nseq=CTX_SEQS_PER_STEP if ctx else 1)
    if ctx:
        w_gu0, w_down0, w_qkv = conv0

    outs1 = [("nat", 0, Q_WIDTH, BF16, True, Q_SCALE), ("nat", Q_WIDTH, KV_WIDTH, BF16, True, None)]
    if ctx:
        outs1 += [("T", Q_WIDTH + KV_WIDTH, KV_WIDTH, F32, False, None), ("T", Q_WIDTH, KV_WIDTH, F32, False, None)]
    else:
        outs1 += [("T", Q_WIDTH + KV_WIDTH, KV_WIDTH, BF16, False, None)]
    res = _ffn(x, m0, norm_g0, w_gu0, w_down0, tm=tm_ffn, cond_row=row_of(tm_ffn),
               side=(w_out1, w_gu1, w_down1) if ctx else (),
               proj=(m1, norm_g1, w_qkv, tuple(outs1), seq, rope))
    x, q, k, vt = res[:4]
    new1 = (res[4], vt) if ctx else ()
    if ctx:
        w_out1, w_gu1, w_down1 = res[5:]
    x = _mix1(q, k, vt, None if ctx else caches[1], x, m1, norm_g1, sink, w_out1,
              batch=batch, seq=seq, tq=tq, cond_row=cond_of_batch, nseq=CTX_SEQS_PER_STEP if ctx else 1)
    x, = _ffn(x, m1, norm_g1, w_gu1, w_down1, tm=tm_ffn, cond_row=row_of(tm_ffn))
    return x, new0 + new1, (w_gu0, w_down0, w_qkv, w_out1, w_gu1, w_down1)


def kernel(x_prompt, x_sample, cache_k0, cache_v0, cache_k1, cache_v1, c, c_ctx, l0_mod_w, l0_mod_b, l0_norm_g, l0_w_in, l0_conv_w, l0_conv_b, l0_conv_ln_g, l0_conv_ln_b, l0_lambda, l0_subln_g, l0_w_out, l0_w_gu, l0_w_down, l1_mod_w, l1_mod_b, l1_norm_g, l1_w_qkv, l1_sink, l1_w_out, l1_w_gu, l1_w_down):
    batch, seq, d = x_prompt.shape
    dec_batch, dec_seq, _ = x_sample.shape
    n_past = cache_k0.shape[1]

    cond = jnp.concatenate([c_ctx[None, :], c, jnp.zeros((MOD_ROWS - 1 - dec_batch, d), F32)], axis=0)
    m = (_modulation(cond, l0_mod_w, l0_mod_b), _modulation(cond, l1_mod_w, l1_mod_b))

    row = lambda v: v.reshape(1, -1)
    layer0 = (l0_norm_g, l0_w_in.astype(BF16), jnp.broadcast_to(l0_conv_w, (CONV_WIDTH, 8, CONV_CH)), row(l0_conv_b),
              row(l0_conv_ln_g), row(l0_conv_ln_b), l0_lambda, l0_subln_g.reshape(LANES, 1),
              l0_w_out.astype(BF16))
    weights = layer0 + (l0_w_gu, l0_w_down, l1_norm_g, l1_w_qkv, l1_sink, l1_w_out, l1_w_gu, l1_w_down)

    y_prompt, (kt0, v0, kt1, vt1), bf16_w = _trunk(x_prompt.reshape(batch * seq, d), m, None, weights,
                                                   batch=batch, seq=seq, cond_of_batch=lambda b: 0)
    w_gu0, w_down0, w_qkv, w_out1, w_gu1, w_down1 = bf16_w
    weights = layer0 + (w_gu0, w_down0, l1_norm_g, w_qkv, l1_sink, w_out1, w_gu1, w_down1)
    caches = ((cache_k0.reshape(dec_batch, n_past, -1),
               jnp.swapaxes(cache_v0.reshape(dec_batch, n_past, -1), 1, 2)),
              (cache_k1.reshape(dec_batch, n_past, -1),
               jnp.swapaxes(cache_v1.reshape(dec_batch, n_past, -1), 1, 2)))
    y_sample, _, _ = _trunk(x_sample.reshape(dec_batch * dec_seq, d), m, caches, weights,
                            batch=dec_batch, seq=dec_seq, cond_of_batch=lambda b: 1 + b)
    new_k0 = kt0.reshape(batch, DIFF_HEADS, 2, HEAD_DIM, seq).transpose(0, 4, 1, 2, 3)
    new_v0 = v0.reshape(batch, seq, DIFF_HEADS, 2 * HEAD_DIM)
    new_k1 = kt1.reshape(batch, GQA_KV_HEADS, HEAD_DIM, seq).transpose(0, 3, 1, 2)
    new_v1 = vt1.reshape(batch, GQA_KV_HEADS, HEAD_DIM, seq).transpose(0, 3, 1, 2)
    return (y_prompt.reshape(batch, seq, d), y_sample.reshape(dec_batch, dec_seq, d),
            new_k0, new_v0, new_k1, new_v1)
```

```python
import functools
import math

import numpy as np
import jax
import jax.numpy as jnp
from jax import lax
from jax.experimental import pallas as pl
from jax.experimental.pallas import tpu as pltpu

D_MODEL = 1024
GRID_W = 64
ROPE_THETA = 10000.0
EPS = 1e-6
NEG_INF = -1e30
LOG2E = math.log2(math.e)
CONV_CH = 512
CONV_WIDTH = 31
CONV_PAD = 16
CONV_ROWS = 64
DIFF_HEADS = 4
DIFF_WIDTH = 512
HEAD_DIM = 64
GQA_HEADS = 16
GQA_KV_HEADS = 4
GQA_GROUP = 4
Q_WIDTH = GQA_HEADS * HEAD_DIM
KV_WIDTH = GQA_KV_HEADS * HEAD_DIM
Q_SCALE = LOG2E * HEAD_DIM ** -0.5
WINDOW = 128
FFN_HIDDEN = 2816
N_MOD = 6
LANES = 128
BF16_ROWS = 16
MOD_ROWS = 8
VMEM_LIMIT = 56 * 1024 * 1024

BF16 = jnp.bfloat16
F32 = jnp.float32


def _gated(a, g):
    return (0.5 * a) * (jnp.tanh(0.5 * g) + 1.0)


def _silu(x):
    h = 0.5 * x
    return h * (jnp.tanh(h) + 1.0)


def _rms(x, g):
    return x * lax.rsqrt(jnp.mean(x * x, axis=-1, keepdims=True) + EPS) * g


def _dot(a, b):
    return jnp.dot(a, b, preferred_element_type=F32)


def _dot_nt(a, b):
    return lax.dot_general(a, b, (((1,), (1,)), ((), ())), preferred_element_type=F32)


def _params(*sem):
    return pltpu.CompilerParams(dimension_semantics=sem, vmem_limit_bytes=VMEM_LIMIT)


def _resident(shape):
    return pl.BlockSpec(shape, lambda *_: (0,) * len(shape), pipeline_mode=pl.Buffered(1))


def _mod_kernel(c_ref, w_ref, b_ref, o_ref):
    s = _silu(c_ref[...]).astype(BF16)
    o_ref[...] = _dot(s, w_ref[...].astype(BF16)) + b_ref[...]


def _modulation(cond, mod_w, mod_b):
    n = mod_w.shape[1]
    tn = 1536
    out = pl.pallas_call(
        _mod_kernel,
        grid=(n // tn,),
        in_specs=[
            pl.BlockSpec((MOD_ROWS, D_MODEL), lambda j: (0, 0)),
            pl.BlockSpec((D_MODEL, tn), lambda j: (0, j)),
            pl.BlockSpec((1, tn), lambda j: (0, j)),
        ],
        out_specs=pl.BlockSpec((MOD_ROWS, tn), lambda j: (0, j)),
        out_shape=jax.ShapeDtypeStruct((MOD_ROWS, n), F32),
        compiler_params=_params("arbitrary"),
        name="modulation",
    )(cond, mod_w, mod_b.reshape(1, n))
    return out.reshape(MOD_ROWS, N_MOD, D_MODEL)


def _rope_tables(seq):
    t = np.arange(seq)
    rows, cols = t // GRID_W, t % GRID_W
    half = HEAD_DIM // 2
    inv = 1.0 / (ROPE_THETA ** (np.arange(0, half, 2, dtype=np.float64) / half))
    ar = rows[:, None] * inv[None, :]
    ac = cols[:, None] * inv[None, :]
    cos = np.concatenate([np.cos(ar), np.cos(ar), np.cos(ac), np.cos(ac)], axis=1)
    sin = np.concatenate([-np.sin(ar), np.sin(ar), -np.sin(ac), np.sin(ac)], axis=1)
    reps = LANES // HEAD_DIM
    return (jnp.asarray(np.tile(cos, (1, reps)), F32), jnp.asarray(np.tile(sin, (1, reps)), F32))


def _rope(x, cos, sin):
    quarter = HEAD_DIM // 4
    lane = lax.broadcasted_iota(jnp.int32, x.shape, 1)
    lo = (lane % (2 * quarter)) < quarter
    partner = jnp.where(lo, pltpu.roll(x, LANES - quarter, 1), pltpu.roll(x, quarter, 1))
    return x * cos + partner * sin


def _pre_kernel(*refs, outs, rope, sblk):
    n_in = 6 if rope else 4
    x_ref, m_ref, g_ref, w_ref = refs[:4]
    _project(x_ref[...], m_ref, g_ref, w_ref, refs[4:6] if rope else None, outs, refs[n_in:], sblk)


def _project(x, m_ref, g_ref, w_ref, trig, outs, o_refs, sblk):
    rope = trig is not None
    tm = x.shape[0]
    h = (_rms(x, g_ref[0:1, :] * (1.0 + m_ref[1:2, :])) + m_ref[0:1, :]).astype(BF16)
    cost = lambda o: o[2] * {"T": 5.0, "heads": 0.5, "glu": 1.2}.get(o[0], 3.0 if (rope and o[4]) else 1.0)
    ranges = sorted({(o[1], o[2]) for o in outs},
                    key=lambda r: (-sum(cost(o) for o in outs if (o[1], o[2]) == r), r[0]))
    for start, width in ranges:
        p = _dot(h, w_ref[:, start:start + width])
        for (kind, s0, w0, _, roped, scale), o_ref in zip(outs, o_refs):
            if (s0, w0) != (start, width):
                continue
            if kind == "nat":
                for c in range(width // LANES):
                    xc = p[:, c * LANES:(c + 1) * LANES]
                    if rope and roped:
                        xc = _rope(xc, trig[0][...], trig[1][...])
                    if scale is not None:
                        xc = xc * scale
                    o_ref[:, c * LANES:(c + 1) * LANES] = xc.astype(o_ref.dtype)
            elif kind == "T":
                for s in range(tm // sblk):
                    o_ref[s] = p[s * sblk:(s + 1) * sblk, :].T.astype(o_ref.dtype)
            elif kind == "glu":
                o_ref[...] = _gated(p[:, 0:width // 2], p[:, width // 2:width])
            else:
                heads = width // LANES
                for hd in range(heads):
                    o_ref[pl.ds(hd, tm, stride=heads), :] = p[:, hd * LANES:(hd + 1) * LANES]


def _project_specs(m, norm_g, w, outs, *, rows, tm, cond_row, seq, rope):
    n = w.shape[1]
    sblk = min(seq, tm)
    per_seq = seq // sblk
    in_specs = [
        pl.BlockSpec((None, N_MOD, D_MODEL), lambda i: (cond_row(i), 0, 0)),
        pl.BlockSpec((4, D_MODEL), lambda i: (0, 0)),
        _resident((D_MODEL, n)),
    ]
    args = [m, norm_g, w]
    if rope:
        cos, sin = _rope_tables(seq)
        nblk = seq // tm
        in_specs += [pl.BlockSpec((tm, LANES), lambda i: (i % nblk, 0))] * 2
        args += [cos, sin]
    out_specs, out_shape = [], []
    for (kind, _, wd, dt, _, _) in outs:
        if kind == "nat":
            out_specs.append(pl.BlockSpec((tm, wd), lambda i: (i, 0)))
            out_shape.append(jax.ShapeDtypeStruct((rows, wd), dt))
        elif kind == "T":
            out_specs.append(pl.BlockSpec((tm // sblk, wd, sblk), lambda i: (i // per_seq, 0, i % per_seq)))
            out_shape.append(jax.ShapeDtypeStruct((rows // seq, wd, seq), dt))
        elif kind == "glu":
            out_specs.append(pl.BlockSpec((tm, wd // 2), lambda i: (i, 0)))
            out_shape.append(jax.ShapeDtypeStruct((rows, wd // 2), dt))
        else:
            out_specs.append(pl.BlockSpec((tm * (wd // LANES), LANES), lambda i: (i, 0)))
            out_shape.append(jax.ShapeDtypeStruct((rows * (wd // LANES), LANES), dt))
    return in_specs, args, out_specs, out_shape, sblk


def _pre(x, m, norm_g, w, outs, *, tm, cond_row, seq, rope):
    rows = x.shape[0]
    in_specs = [pl.BlockSpec((tm, D_MODEL), lambda i: (i, 0))]
    proj_in, proj_args, out_specs, out_shape, sblk = _project_specs(m, norm_g, w, outs, rows=rows, tm=tm,
                                                                    cond_row=cond_row, seq=seq, rope=rope)
    in_specs += proj_in
    args = [x] + proj_args
    return pl.pallas_call(
        functools.partial(_pre_kernel, outs=outs, rope=rope, sblk=sblk),
        grid=(rows // tm,),
        in_specs=in_specs,
        out_specs=out_specs,
        out_shape=out_shape,
        compiler_params=_params("arbitrary"),
        name="pre_rope" if rope else "pre",
    )(*args)


def _attend(s_list, vt_list, floor=None):
    mx = None
    for s in s_list:
        smax = jnp.max(s, axis=0, keepdims=True)
        mx = smax if mx is None else jnp.maximum(mx, smax)
    if floor is not None:
        mx = jnp.maximum(mx, floor)
    d = vt_list[0].shape[0]
    ov = None
    for s, vt in zip(s_list, vt_list):
        ones = jnp.ones((BF16_ROWS, vt.shape[1]), BF16)
        pv = _dot(jnp.concatenate([vt, ones], axis=0), jnp.exp2(s - mx).astype(BF16))
        ov = pv if ov is None else ov + pv
    return ov[0:d, :], ov[d:d + 1, :], mx


def _pipelined(n, first, second):
    out = []
    nxt = first(0)
    for u in range(n):
        cur = nxt
        if u + 1 < n:
            nxt = first(u + 1)
        out.append(second(u, cur))
    return out


def _side_convert(src_refs, dst_refs):
    for src, dst in zip(src_refs, dst_refs):
        dst[...] = src[...].astype(BF16)


def _mix0_kernel(*refs, seq, tq, n_cache, lam_init, n_side, nseq):
    n_in = 16 if n_cache else 14
    z_ref, q_ref, k_ref, vt_ref = refs[:4]
    ck_ref, cvt_ref = refs[4:6] if n_cache else (None, None)
    x_ref, m_ref, ng_ref, cw_ref, cb_ref, lg_ref, lb_ref, lam_ref, sg_ref, wo_ref = refs[n_in - 10:n_in]
    o_ref = refs[n_in + n_side]
    z_s, zw_s, zsh_s = refs[n_in + 2 * n_side + 1:]
    _side_convert(refs[n_in:n_in + n_side], refs[n_in + n_side + 1:n_in + 2 * n_side + 1])
    qi = pl.program_id(1)
    whole = tq == seq
    win = tq + 2 * CONV_PAD
    span = win - 8
    chunk = CONV_ROWS
    off0 = CONV_PAD - CONV_WIDTH // 2
    la = jnp.sum(lam_ref[0:1, :] * lam_ref[1:2, :], axis=-1, keepdims=True)
    lb = jnp.sum(lam_ref[2:3, :] * lam_ref[3:4, :], axis=-1, keepdims=True)
    lam = jnp.exp(la) - jnp.exp(lb) + lam_init
    sub = LANES // 2
    lane = lax.broadcasted_iota(jnp.int32, (tq, LANES), 1)

    def block(s):
        urows = slice(s * seq, (s + 1) * seq)

        def pad_sequence():
            z_s[0:CONV_PAD, :] = jnp.zeros((CONV_PAD, CONV_CH), F32)
            z_s[CONV_PAD + seq:2 * CONV_PAD + seq, :] = jnp.zeros((CONV_PAD, CONV_CH), F32)
            z_s[CONV_PAD:CONV_PAD + seq, :] = z_ref[urows, :]

        if whole:
            pad_sequence()
            zw = z_s
        else:
            pl.when(qi == 0)(pad_sequence)
            zw_s[...] = z_s[pl.ds(pl.multiple_of(qi * tq, tq), win), :]
            zw = zw_s

        for b in range(1, 8):
            zsh_s[b - 1, 0:span, :] = zw[b:b + span, :]
        conv_rows = []
        for c0 in range(0, tq, chunk):
            acc = jnp.zeros((chunk // 8, 8, CONV_CH), F32)
            for j in range(CONV_WIDTH):
                off = j + off0
                a, b = off // 8, off % 8
                if b == 0:
                    zz = zw[c0 + 8 * a:c0 + 8 * a + chunk, :]
                else:
                    zz = zsh_s[b - 1, c0 + 8 * a:c0 + 8 * a + chunk, :]
                acc = acc + zz.reshape(chunk // 8, 8, CONV_CH) * cw_ref[j]
            acc = acc.reshape(chunk, CONV_CH) + cb_ref[...]
            mu = jnp.mean(acc, axis=-1, keepdims=True)
            xc = acc - mu
            y = xc * lax.rsqrt(jnp.mean(xc * xc, axis=-1, keepdims=True) + EPS) * lg_ref[...] + lb_ref[...]
            conv_rows.append(_silu(y).astype(BF16))

        def segments(h):
            cols = slice(h * LANES, (h + 1) * LANES)
            segs = [(k_ref[urows, cols], vt_ref[s, cols, :])]
            if n_cache:
                segs.append((ck_ref[:, cols].astype(BF16), cvt_ref[cols, :].astype(BF16)))
            return segs

        def scores(h):
            qc = q_ref[s * tq:(s + 1) * tq, h * LANES:(h + 1) * LANES]
            zero = jnp.zeros_like(qc)
            qq = jnp.concatenate([jnp.where(lane < sub, qc, zero), jnp.where(lane >= sub, qc, zero)], axis=0)
            return [_dot_nt(kk, qq) for kk, _ in segments(h)]

        def head(h, s_list):
            ov, den, _ = _attend(s_list, [vt for _, vt in segments(h)])
            r = 1.0 / den
            o = ov[:, 0:tq] * r[:, 0:tq] - ov[:, tq:2 * tq] * (lam * r[:, tq:2 * tq])
            return o * lax.rsqrt(jnp.mean(o * o, axis=0, keepdims=True) + EPS) * sg_ref[...] * (1.0 - lam_init)

        heads = _pipelined(DIFF_HEADS, scores, head)
        attn = jnp.concatenate(heads, axis=0).T.astype(BF16)
        return jnp.concatenate([jnp.concatenate(conv_rows, axis=0), attn], axis=1)

    cat = jnp.concatenate([block(s) for s in range(nseq)], axis=0)
    mixed = _dot(cat, wo_ref[...])
    o_ref[...] = x_ref[...] + _rms(mixed, m_ref[2:3, :] * ng_ref[1:2, :])


def _side_specs(side, steps, index):
    in_specs, out_specs, out_shape = [], [], []
    for w in side:
        rows, cols = w.shape
        n_blk = math.gcd(steps, rows // BF16_ROWS)
        per = steps // n_blk
        spec = pl.BlockSpec((rows // n_blk, cols), lambda *ids, per=per: (index(*ids) // per, 0))
        in_specs.append(spec)
        out_specs.append(spec)
        out_shape.append(jax.ShapeDtypeStruct(w.shape, BF16))
    return in_specs, out_specs, out_shape


def _mix0(u, q, k, vt, cache, x, m, norm_g, conv_w, conv_b, ln_g, ln_b, lam, subln_g, w_out,
          *, batch, seq, tq, cond_row, lam_init, side=(), nseq=1):
    nq = seq // tq
    n_cache = 0 if cache is None else cache[0].shape[1]
    win = tq + 2 * CONV_PAD
    assert tq % CONV_ROWS == 0
    steps = (batch // nseq) * nq
    const = lambda shape: pl.BlockSpec(shape, lambda b, i: (0,) * len(shape))
    in_specs = [pl.BlockSpec((nseq * seq, CONV_CH), lambda b, i: (b, 0)),
                pl.BlockSpec((nseq * tq, DIFF_WIDTH), lambda b, i: (b * nq + i, 0)),
                pl.BlockSpec((nseq * seq, DIFF_WIDTH), lambda b, i: (b, 0)),
                pl.BlockSpec((nseq, DIFF_WIDTH, seq), lambda b, i: (b, 0, 0))]
    args = [u, q, k, vt]
    if n_cache:
        in_specs += [pl.BlockSpec((None, n_cache, DIFF_WIDTH), lambda b, i: (b, 0, 0)),
                     pl.BlockSpec((None, DIFF_WIDTH, n_cache), lambda b, i: (b, 0, 0))]
        args += [cache[0], cache[1]]
    in_specs += [
        pl.BlockSpec((nseq * tq, D_MODEL), lambda b, i: (b * nq + i, 0)),
        pl.BlockSpec((None, N_MOD, D_MODEL), lambda b, i: (cond_row(b * nseq), 0, 0)),
        const((4, D_MODEL)), const((CONV_WIDTH, 8, CONV_CH)), const((1, CONV_CH)), const((1, CONV_CH)),
        const((1, CONV_CH)), const((4, LANES // 2)), const((LANES, 1)), _resident((D_MODEL, D_MODEL)),
    ]
    args += [x, m, norm_g, conv_w, conv_b, ln_g, ln_b, lam, subln_g, w_out]
    side_in, side_out, side_shape = _side_specs(side, steps, lambda b, i: b * nq + i)
    return pl.pallas_call(
        functools.partial(_mix0_kernel, seq=seq, tq=tq, n_cache=n_cache, lam_init=lam_init, n_side=len(side),
                          nseq=nseq),
        grid=(batch // nseq, nq),
        in_specs=in_specs + side_in,
        out_specs=[pl.BlockSpec((nseq * tq, D_MODEL), lambda b, i: (b * nq + i, 0))] + side_out,
        out_shape=[jax.ShapeDtypeStruct((batch * seq, D_MODEL), F32)] + side_shape,
        scratch_shapes=[
            pltpu.VMEM((seq + 2 * CONV_PAD, CONV_CH), F32), pltpu.VMEM((win, CONV_CH), F32),
            pltpu.VMEM((7, win - 8, CONV_CH), F32),
        ],
        compiler_params=_params("arbitrary", "arbitrary"),
        name="mix_conv_diff",
    )(*args, *side)


def _mix1_kernel(*refs, seq, tq, n_cache, nseq):
    if n_cache:
        (q_ref, kp_ref, kc_ref, kn_ref, vtp_ref, vtc_ref, vtn_ref, ck_ref, cvt_ref,
         x_ref, m_ref, ng_ref, sink_ref, wo_ref, o_ref) = refs
    else:
        (q_ref, kc_ref, vtc_ref, x_ref, m_ref, ng_ref, sink_ref, wo_ref, o_ref) = refs
    qi = pl.program_id(1)
    lane = lax.broadcasted_iota(jnp.int32, (tq, LANES), 1)

    def band(base, rows):
        kpos = base + lax.broadcasted_iota(jnp.int32, (rows, tq), 0)
        qpos = qi * tq + lax.broadcasted_iota(jnp.int32, (rows, tq), 1)
        ok = (jnp.abs(qpos - kpos) <= WINDOW) & (kpos >= 0) & (kpos < seq)
        return jnp.concatenate([ok] * GQA_GROUP, axis=1)

    def attention(r0, segs):
        def scores(g):
            b = g % 2
            kcols = slice((g // 2) * LANES, (g // 2 + 1) * LANES)
            keep = (lane < HEAD_DIM) if b == 0 else (lane >= HEAD_DIM)
            qs = []
            for h in range(g * GQA_GROUP, (g + 1) * GQA_GROUP):
                qc = q_ref[r0:r0 + tq, (h // 2) * LANES:(h // 2 + 1) * LANES]
                qa = qc if h % 2 == b else pltpu.roll(qc, HEAD_DIM, 1)
                qs.append(jnp.where(keep, qa, jnp.zeros_like(qa)))
            qg = jnp.concatenate(qs, axis=0)
            s_list = []
            for k_ref, _, ok in segs:
                s = _dot_nt(k_ref[:, kcols].astype(BF16), qg)
                s_list.append(s if ok is None else jnp.where(ok, s, NEG_INF))
            return s_list

        def group(g, s_list):
            vrows = slice(g * HEAD_DIM, (g + 1) * HEAD_DIM)
            sink = jnp.concatenate([jnp.full((1, tq), sink_ref[h] * LOG2E, F32)
                                    for h in range(g * GQA_GROUP, (g + 1) * GQA_GROUP)], axis=1)
            ov, den, mx = _attend(s_list, [vt_ref[vrows, :].astype(BF16) for _, vt_ref, _ in segs], floor=sink)
            ov = ov * (1.0 / (den + jnp.exp2(sink - mx)))
            return jnp.concatenate([ov[:, hh * tq:(hh + 1) * tq] for hh in range(GQA_GROUP)], axis=0)

        groups = _pipelined(GQA_KV_HEADS, scores, group)
        return jnp.concatenate(groups, axis=0).T.astype(BF16)

    if n_cache:
        attn = attention(0, [(kp_ref, vtp_ref, band(qi * tq - WINDOW, WINDOW)),
                             (kc_ref, vtc_ref, band(qi * tq, tq)),
                             (kn_ref, vtn_ref, band(qi * tq + tq, WINDOW)),
                             (ck_ref, cvt_ref, None)])
    else:
        attn = jnp.concatenate([attention(s * tq, [(kc_ref.at[pl.ds(s * tq, tq)], vtc_ref.at[s], None)])
                                for s in range(nseq)], axis=0)
    mixed = _dot(attn, wo_ref[...])
    o_ref[...] = x_ref[...] + _rms(mixed, m_ref[2:3, :] * ng_ref[1:2, :])


def _mix1(q, k, vt, cache, x, m, norm_g, sink, w_out, *, batch, seq, tq, cond_row, nseq=1):
    nq = seq // tq
    n_cache = 0 if cache is None else cache[0].shape[1]
    rows = nseq * tq
    q_spec = pl.BlockSpec((rows, D_MODEL), lambda b, i: (b * nq + i, 0))
    kc_spec = pl.BlockSpec((rows, KV_WIDTH), lambda b, i: (b * nq + i, 0))
    vtc_spec = pl.BlockSpec((None, KV_WIDTH, tq), lambda b, i: (b, 0, i))
    if n_cache:
        nw = seq // WINDOW
        per = tq // WINDOW
        prev = lambda i: jnp.maximum(i * per - 1, 0)
        nxt = lambda i: jnp.minimum(i * per + per, nw - 1)
        in_specs = [q_spec,
                    pl.BlockSpec((WINDOW, KV_WIDTH), lambda b, i: (b * nw + prev(i), 0)), kc_spec,
                    pl.BlockSpec((WINDOW, KV_WIDTH), lambda b, i: (b * nw + nxt(i), 0)),
                    pl.BlockSpec((None, KV_WIDTH, WINDOW), lambda b, i: (b, 0, prev(i))), vtc_spec,
                    pl.BlockSpec((None, KV_WIDTH, WINDOW), lambda b, i: (b, 0, nxt(i))),
                    pl.BlockSpec((None, n_cache, KV_WIDTH), lambda b, i: (b, 0, 0)),
                    pl.BlockSpec((None, KV_WIDTH, n_cache), lambda b, i: (b, 0, 0))]
        args = [q, k, k, k, vt, vt, vt, cache[0], cache[1]]
    else:
        in_specs = [q_spec, kc_spec, pl.BlockSpec((nseq, KV_WIDTH, tq), lambda b, i: (b, 0, 0))]
        args = [q, k, vt]
    in_specs += [
        pl.BlockSpec((rows, D_MODEL), lambda b, i: (b * nq + i, 0)),
        pl.BlockSpec((None, N_MOD, D_MODEL), lambda b, i: (cond_row(b * nseq), 0, 0)),
        pl.BlockSpec((4, D_MODEL), lambda b, i: (0, 0)),
        pl.BlockSpec(memory_space=pltpu.SMEM),
        pl.BlockSpec((D_MODEL, D_MODEL), lambda b, i: (0, 0)),
    ]
    args += [x, m, norm_g, sink, w_out]
    return pl.pallas_call(
        functools.partial(_mix1_kernel, seq=seq, tq=tq, n_cache=n_cache, nseq=nseq),
        grid=(batch // nseq, nq),
        in_specs=in_specs,
        out_specs=pl.BlockSpec((rows, D_MODEL), lambda b, i: (b * nq + i, 0)),
        out_shape=jax.ShapeDtypeStruct((batch * seq, D_MODEL), F32),
        compiler_params=_params("arbitrary", "arbitrary"),
        name="mix_gqa",
    )(*args)


FFN_CHUNKS = (1536, 1280)


def _ffn_kernel(*refs, n_side, proj):
    x_ref, m_ref, ng_ref, wgu_ref, wd_ref = refs[:5]
    n_pin = 0 if proj is None else (5 if proj[1] else 3)
    n_pout = 0 if proj is None else len(proj[0])
    n_in = 5 + n_pin + n_side
    o_ref = refs[n_in]
    _side_convert(refs[5 + n_pin:n_in], refs[n_in + 1 + n_pout:])
    h = (_rms(x_ref[...], ng_ref[2:3, :] * (1.0 + m_ref[4:5, :])) + m_ref[3:4, :]).astype(BF16)
    acc = jnp.zeros(h.shape, F32)
    c0 = 0
    for width in FFN_CHUNKS:
        gate = _dot(h, wgu_ref[:, c0:c0 + width])
        up = _dot(h, wgu_ref[:, FFN_HIDDEN + c0:FFN_HIDDEN + c0 + width])
        act = (_silu(gate) * up).astype(BF16)
        acc = acc + _dot(act, wd_ref[c0:c0 + width, :])
        c0 += width
    y = x_ref[...] + _rms(acc, m_ref[5:6, :] * ng_ref[3:4, :])
    o_ref[...] = y
    if proj is not None:
        outs, rope, sblk = proj
        pm_ref, pg_ref, pw_ref = refs[5:8]
        _project(y, pm_ref, pg_ref, pw_ref, refs[8:10] if rope else None, outs, refs[n_in + 1:n_in + 1 + n_pout], sblk)


def _ffn(x, m, norm_g, w_gu, w_down, *, tm, cond_row, side=(), proj=None):
    rows = x.shape[0]
    side_in, side_out, side_shape = _side_specs(side, rows // tm, lambda i: i)
    proj_in, proj_args, proj_out, proj_shape, kproj = [], [], [], [], None
    if proj is not None:
        pm, pg, pw, outs, seq, rope = proj
        proj_in, proj_args, proj_out, proj_shape, sblk = _project_specs(pm, pg, pw, outs, rows=rows, tm=tm,
                                                                        cond_row=cond_row, seq=seq, rope=rope)
        kproj = (outs, rope, sblk)
    return pl.pallas_call(
        functools.partial(_ffn_kernel, n_side=len(side), proj=kproj),
        grid=(rows // tm,),
        in_specs=[
            pl.BlockSpec((tm, D_MODEL), lambda i: (i, 0)),
            pl.BlockSpec((None, N_MOD, D_MODEL), lambda i: (cond_row(i), 0, 0)),
            pl.BlockSpec((4, D_MODEL), lambda i: (0, 0)),
            _resident((D_MODEL, 2 * FFN_HIDDEN)),
            _resident((FFN_HIDDEN, D_MODEL)),
        ] + proj_in + side_in,
        out_specs=[pl.BlockSpec((tm, D_MODEL), lambda i: (i, 0))] + proj_out + side_out,
        out_shape=[jax.ShapeDtypeStruct((rows, D_MODEL), F32)] + proj_shape + side_shape,
        compiler_params=_params("arbitrary"),
        name="ffn",
    )(x, m, norm_g, w_gu, w_down, *proj_args, *side)


LAM_INIT_0 = 0.8 - 0.6 * math.exp(-0.3 * 0)
CTX_SEQS_PER_STEP = 4


def _trunk(x, m, caches, weights, *, batch, seq, cond_of_batch):
    (norm_g0, w_in, conv_w, conv_b, ln_g, ln_b, lam, subln_g, w_out0, w_gu0, w_down0,
     norm_g1, w_qkv, sink, w_out1, w_gu1, w_down1) = weights
    m0, m1 = m
    ctx = caches is None
    rope = not ctx
    tm = 512
    tm_ffn = 512
    tq = 256
    row_of = lambda t: (lambda i: cond_of_batch((i * t) // seq))

    o0 = 2 * CONV_CH
    ko, vo = o0 + DIFF_WIDTH, o0 + 2 * DIFF_WIDTH
    outs0 = [("glu", 0, o0, F32, False, None), ("nat", o0, DIFF_WIDTH, BF16, True, Q_SCALE),
             ("nat", ko, DIFF_WIDTH, BF16, True, None), ("T", vo, DIFF_WIDTH, BF16, False, None)]
    if ctx:
        outs0 += [("T", ko, DIFF_WIDTH, F32, False, None), ("heads", vo, DIFF_WIDTH, F32, False, None)]
    res = _pre(x, m0, norm_g0, w_in, tuple(outs0), tm=tm, cond_row=row_of(tm), seq=seq, rope=rope)
    u, q, k, vt = res[:4]
    new0 = tuple(res[4:])
    x, *conv0 = _mix0(u, q, k, vt, None if ctx else caches[0], x, m0, norm_g0, conv_w, conv_b,
                      ln_g, ln_b, lam, subln_g, w_out0, batch=batch, seq=seq, tq=tq, cond_row=cond_of_batch,
                      lam_init=LAM_INIT_0, side=(w_gu0, w_down0, w_qkv) if ctx else (),
                      nseq=CTX_SEQS_PER_STEP if ctx else 1)
    if ctx:
        w_gu0, w_down0, w_qkv = conv0

    outs1 = [("nat", 0, Q_WIDTH, BF16, True, Q_SCALE), ("nat", Q_WIDTH, KV_WIDTH, BF16, True, None)]
    if ctx:
        outs1 += [("T", Q_WIDTH + KV_WIDTH, KV_WIDTH, F32, False, None), ("T", Q_WIDTH, KV_WIDTH, F32, False, None)]
    else:
        outs1 += [("T", Q_WIDTH + KV_WIDTH, KV_WIDTH, BF16, False, None)]
    res = _ffn(x, m0, norm_g0, w_gu0, w_down0, tm=tm_ffn, cond_row=row_of(tm_ffn),
               side=(w_out1, w_gu1, w_down1) if ctx else (),
               proj=(m1, norm_g1, w_qkv, tuple(outs1), seq, rope))
    x, q, k, vt = res[:4]
    new1 = (res[4], vt) if ctx else ()
    if ctx:
        w_out1, w_gu1, w_down1 = res[5:]
    x = _mix1(q, k, vt, None if ctx else caches[1], x, m1, norm_g1, sink, w_out1,
              batch=batch, seq=seq, tq=tq, cond_row=cond_of_batch, nseq=CTX_SEQS_PER_STEP if ctx else 1)
    x, = _ffn(x, m1, norm_g1, w_gu1, w_down1, tm=tm_ffn, cond_row=row_of(tm_ffn))
    return x, new0 + new1, (w_gu0, w_down0, w_qkv, w_out1, w_gu1, w_down1)


def kernel(x_prompt, x_sample, cache_k0, cache_v0, cache_k1, cache_v1, c, c_ctx, l0_mod_w, l0_mod_b, l0_norm_g, l0_w_in, l0_conv_w, l0_conv_b, l0_conv_ln_g, l0_conv_ln_b, l0_lambda, l0_subln_g, l0_w_out, l0_w_gu, l0_w_down, l1_mod_w, l1_mod_b, l1_norm_g, l1_w_qkv, l1_sink, l1_w_out, l1_w_gu, l1_w_down):
    batch, seq, d = x_prompt.shape
    dec_batch, dec_seq, _ = x_sample.shape
    n_past = cache_k0.shape[1]
    assert d == D_MODEL and seq == 256 and batch % CTX_SEQS_PER_STEP == 0 and (batch * seq) % 512 == 0
    assert dec_seq % 512 == 0 and dec_seq % GRID_W == 0 and n_past % LANES == 0 and 1 + dec_batch <= MOD_ROWS

    cond = jnp.concatenate([c_ctx[None, :], c, jnp.zeros((MOD_ROWS - 1 - dec_batch, d), F32)], axis=0)
    m = (_modulation(cond, l0_mod_w, l0_mod_b), _modulation(cond, l1_mod_w, l1_mod_b))

    row = lambda v: v.reshape(1, -1)
    layer0 = (l0_norm_g, l0_w_in.astype(BF16), jnp.broadcast_to(l0_conv_w, (CONV_WIDTH, 8, CONV_CH)), row(l0_conv_b),
              row(l0_conv_ln_g), row(l0_conv_ln_b), l0_lambda, l0_subln_g.reshape(LANES, 1),
              l0_w_out.astype(BF16))
    weights = layer0 + (l0_w_gu, l0_w_down, l1_norm_g, l1_w_qkv, l1_sink, l1_w_out, l1_w_gu, l1_w_down)

    y_prompt, (kt0, v0, kt1, vt1), bf16_w = _trunk(x_prompt.reshape(batch * seq, d), m, None, weights,
                                                   batch=batch, seq=seq, cond_of_batch=lambda b: 0)
    w_gu0, w_down0, w_qkv, w_out1, w_gu1, w_down1 = bf16_w
    weights = layer0 + (w_gu0, w_down0, l1_norm_g, w_qkv, l1_sink, w_out1, w_gu1, w_down1)
    caches = ((cache_k0.reshape(dec_batch, n_past, -1),
               jnp.swapaxes(cache_v0.reshape(dec_batch, n_past, -1), 1, 2)),
              (cache_k1.reshape(dec_batch, n_past, -1),
               jnp.swapaxes(cache_v1.reshape(dec_batch, n_past, -1), 1, 2)))
    y_sample, _, _ = _trunk(x_sample.reshape(dec_batch * dec_seq, d), m, caches, weights,
                            batch=dec_batch, seq=dec_seq, cond_of_batch=lambda b: 1 + b)
    new_k0 = kt0.reshape(batch, DIFF_HEADS, 2, HEAD_DIM, seq).transpose(0, 4, 1, 2, 3)
    new_v0 = v0.reshape(batch, seq, DIFF_HEADS, 2 * HEAD_DIM)
    new_k1 = kt1.reshape(batch, GQA_KV_HEADS, HEAD_DIM, seq).transpose(0, 3, 1, 2)
    new_v1 = vt1.reshape(batch, GQA_KV_HEADS, HEAD_DIM, seq).transpose(0, 3, 1, 2)
    return (y_prompt.reshape(batch, seq, d), y_sample.reshape(dec_batch, dec_seq, d),
            new_k0, new_v0, new_k1, new_v1)
```

```python
import functools
import math

import numpy as np
import jax
import jax.numpy as jnp
from jax import lax
from jax.experimental import pallas as pl
from jax.experimental.pallas import tpu as pltpu

D_MODEL = 1024
GRID_W = 64
ROPE_THETA = 10000.0
EPS = 1e-6
NEG_INF = -1e30
LOG2E = math.log2(math.e)
CONV_CH = 512
CONV_WIDTH = 31
CONV_PAD = 16
CONV_ROWS = 64
DIFF_HEADS = 4
DIFF_WIDTH = 512
HEAD_DIM = 64
GQA_HEADS = 16
GQA_KV_HEADS = 4
GQA_GROUP = 4
Q_WIDTH = GQA_HEADS * HEAD_DIM
KV_WIDTH = GQA_KV_HEADS * HEAD_DIM
Q_SCALE = LOG2E * HEAD_DIM ** -0.5
WINDOW = 128
FFN_HIDDEN = 2816
N_MOD = 6
LANES = 128
BF16_ROWS = 16
MOD_ROWS = 8
VMEM_LIMIT = 56 * 1024 * 1024

BF16 = jnp.bfloat16
F32 = jnp.float32


def _gated(a, g):
    return (0.5 * a) * (jnp.tanh(0.5 * g) + 1.0)


def _silu(x):
    h = 0.5 * x
    return h * (jnp.tanh(h) + 1.0)


def _rms(x, g):
    return x * lax.rsqrt(jnp.mean(x * x, axis=-1, keepdims=True) + EPS) * g


def _dot(a, b):
    return jnp.dot(a, b, preferred_element_type=F32)


def _dot_nt(a, b):
    return lax.dot_general(a, b, (((1,), (1,)), ((), ())), preferred_element_type=F32)


def _params(*sem):
    return pltpu.CompilerParams(dimension_semantics=sem, vmem_limit_bytes=VMEM_LIMIT)


def _resident(shape):
    return pl.BlockSpec(shape, lambda *_: (0,) * len(shape), pipeline_mode=pl.Buffered(1))


def _mod_kernel(c_ref, w_ref, b_ref, o_ref):
    s = _silu(c_ref[...]).astype(BF16)
    o_ref[...] = _dot(s, w_ref[...].astype(BF16)) + b_ref[...]


def _modulation(cond, mod_w, mod_b):
    n = mod_w.shape[1]
    tn = 1536
    out = pl.pallas_call(
        _mod_kernel,
        grid=(n // tn,),
        in_specs=[
            pl.BlockSpec((MOD_ROWS, D_MODEL), lambda j: (0, 0)),
            pl.BlockSpec((D_MODEL, tn), lambda j: (0, j)),
            pl.BlockSpec((1, tn), lambda j: (0, j)),
        ],
        out_specs=pl.BlockSpec((MOD_ROWS, tn), lambda j: (0, j)),
        out_shape=jax.ShapeDtypeStruct((MOD_ROWS, n), F32),
        compiler_params=_params("arbitrary"),
        name="modulation",
    )(cond, mod_w, mod_b.reshape(1, n))
    return out.reshape(MOD_ROWS, N_MOD, D_MODEL)


def _rope_tables(seq):
    t = np.arange(seq)
    rows, cols = t // GRID_W, t % GRID_W
    half = HEAD_DIM // 2
    inv = 1.0 / (ROPE_THETA ** (np.arange(0, half, 2, dtype=np.float64) / half))
    ar = rows[:, None] * inv[None, :]
    ac = cols[:, None] * inv[None, :]
    cos = np.concatenate([np.cos(ar), np.cos(ar), np.cos(ac), np.cos(ac)], axis=1)
    sin = np.concatenate([-np.sin(ar), np.sin(ar), -np.sin(ac), np.sin(ac)], axis=1)
    reps = LANES // HEAD_DIM
    return (jnp.asarray(np.tile(cos, (1, reps)), F32), jnp.asarray(np.tile(sin, (1, reps)), F32))


def _rope(x, cos, sin):
    quarter = HEAD_DIM // 4
    lane = lax.broadcasted_iota(jnp.int32, x.shape, 1)
    lo = (lane % (2 * quarter)) < quarter
    partner = jnp.where(lo, pltpu.roll(x, LANES - quarter, 1), pltpu.roll(x, quarter, 1))
    return x * cos + partner * sin


def _pre_kernel(*refs, outs, rope, sblk):
    n_in = 6 if rope else 4
    x_ref, m_ref, g_ref, w_ref = refs[:4]
    _project(x_ref[...], m_ref, g_ref, w_ref, refs[4:6] if rope else None, outs, refs[n_in:], sblk)


def _project(x, m_ref, g_ref, w_ref, trig, outs, o_refs, sblk):
    rope = trig is not None
    tm = x.shape[0]
    h = (_rms(x, g_ref[0:1, :] * (1.0 + m_ref[1:2, :])) + m_ref[0:1, :]).astype(BF16)
    cost = lambda o: o[2] * {"T": 5.0, "heads": 0.5, "glu": 1.2}.get(o[0], 3.0 if (rope and o[4]) else 1.0)
    ranges = sorted({(o[1], o[2]) for o in outs},
                    key=lambda r: (-sum(cost(o) for o in outs if (o[1], o[2]) == r), r[0]))
    for start, width in ranges:
        p = _dot(h, w_ref[:, start:start + width])
        for (kind, s0, w0, _, roped, scale), o_ref in zip(outs, o_refs):
            if (s0, w0) != (start, width):
                continue
            if kind == "nat":
                for c in range(width // LANES):
                    xc = p[:, c * LANES:(c + 1) * LANES]
                    if rope and roped:
                        xc = _rope(xc, trig[0][...], trig[1][...])
                    if scale is not None:
                        xc = xc * scale
                    o_ref[:, c * LANES:(c + 1) * LANES] = xc.astype(o_ref.dtype)
            elif kind == "T":
                for s in range(tm // sblk):
                    o_ref[s] = p[s * sblk:(s + 1) * sblk, :].T.astype(o_ref.dtype)
            elif kind == "glu":
                o_ref[...] = _gated(p[:, 0:width // 2], p[:, width // 2:width])
            else:
                heads = width // LANES
                for hd in range(heads):
                    o_ref[pl.ds(hd, tm, stride=heads), :] = p[:, hd * LANES:(hd + 1) * LANES]


def _project_specs(m, norm_g, w, outs, *, rows, tm, cond_row, seq, rope):
    n = w.shape[1]
    sblk = min(seq, tm)
    per_seq = seq // sblk
    in_specs = [
        pl.BlockSpec((None, N_MOD, D_MODEL), lambda i: (cond_row(i), 0, 0)),
        pl.BlockSpec((4, D_MODEL), lambda i: (0, 0)),
        _resident((D_MODEL, n)),
    ]
    args = [m, norm_g, w]
    if rope:
        cos, sin = _rope_tables(seq)
        nblk = seq // tm
        in_specs += [pl.BlockSpec((tm, LANES), lambda i: (i % nblk, 0))] * 2
        args += [cos, sin]
    out_specs, out_shape = [], []
    for (kind, _, wd, dt, _, _) in outs:
        if kind == "nat":
            out_specs.append(pl.BlockSpec((tm, wd), lambda i: (i, 0)))
            out_shape.append(jax.ShapeDtypeStruct((rows, wd), dt))
        elif kind == "T":
            out_specs.append(pl.BlockSpec((tm // sblk, wd, sblk), lambda i: (i // per_seq, 0, i % per_seq)))
            out_shape.append(jax.ShapeDtypeStruct((rows // seq, wd, seq), dt))
        elif kind == "glu":
            out_specs.append(pl.BlockSpec((tm, wd // 2), lambda i: (i, 0)))
            out_shape.append(jax.ShapeDtypeStruct((rows, wd // 2), dt))
        else:
            out_specs.append(pl.BlockSpec((tm * (wd // LANES), LANES), lambda i: (i, 0)))
            out_shape.append(jax.ShapeDtypeStruct((rows * (wd // LANES), LANES), dt))
    return in_specs, args, out_specs, out_shape, sblk


def _pre(x, m, norm_g, w, outs, *, tm, cond_row, seq, rope):
    rows = x.shape[0]
    in_specs = [pl.BlockSpec((tm, D_MODEL), lambda i: (i, 0))]
    proj_in, proj_args, out_specs, out_shape, sblk = _project_specs(m, norm_g, w, outs, rows=rows, tm=tm,
                                                                    cond_row=cond_row, seq=seq, rope=rope)
    in_specs += proj_in
    args = [x] + proj_args
    return pl.pallas_call(
        functools.partial(_pre_kernel, outs=outs, rope=rope, sblk=sblk),
        grid=(rows // tm,),
        in_specs=in_specs,
        out_specs=out_specs,
        out_shape=out_shape,
        compiler_params=_params("arbitrary"),
        name="pre_rope" if rope else "pre",
    )(*args)


def _attend(s_list, vt_list, floor=None):
    mx = None
    for s in s_list:
        smax = jnp.max(s, axis=0, keepdims=True)
        mx = smax if mx is None else jnp.maximum(mx, smax)
    if floor is not None:
        mx = jnp.maximum(mx, floor)
    d = vt_list[0].shape[0]
    ov = None
    for s, vt in zip(s_list, vt_list):
        ones = jnp.ones((BF16_ROWS, vt.shape[1]), BF16)
        pv = _dot(jnp.concatenate([vt, ones], axis=0), jnp.exp2(s - mx).astype(BF16))
        ov = pv if ov is None else ov + pv
    return ov[0:d, :], ov[d:d + 1, :], mx


def _pipelined(n, first, second):
    out = []
    nxt = first(0)
    for u in range(n):
        cur = nxt
        if u + 1 < n:
            nxt = first(u + 1)
        out.append(second(u, cur))
    return out


def _side_convert(src_refs, dst_refs):
    for src, dst in zip(src_refs, dst_refs):
        dst[...] = src[...].astype(BF16)


def _mix0_kernel(*refs, seq, tq, n_cache, lam_init, n_side, nseq):
    n_in = 16 if n_cache else 14
    z_ref, q_ref, k_ref, vt_ref = refs[:4]
    ck_ref, cvt_ref = refs[4:6] if n_cache else (None, None)
    x_ref, m_ref, ng_ref, cw_ref, cb_ref, lg_ref, lb_ref, lam_ref, sg_ref, wo_ref = refs[n_in - 10:n_in]
    o_ref = refs[n_in + n_side]
    z_s, zw_s, zsh_s = refs[n_in + 2 * n_side + 1:]
    _side_convert(refs[n_in:n_in + n_side], refs[n_in + n_side + 1:n_in + 2 * n_side + 1])
    qi = pl.program_id(1)
    whole = tq == seq
    win = tq + 2 * CONV_PAD
    span = win - 8
    chunk = CONV_ROWS
    off0 = CONV_PAD - CONV_WIDTH // 2
    la = jnp.sum(lam_ref[0:1, :] * lam_ref[1:2, :], axis=-1, keepdims=True)
    lb = jnp.sum(lam_ref[2:3, :] * lam_ref[3:4, :], axis=-1, keepdims=True)
    lam = jnp.exp(la) - jnp.exp(lb) + lam_init
    sub = LANES // 2
    lane = lax.broadcasted_iota(jnp.int32, (tq, LANES), 1)

    def block(s):
        urows = slice(s * seq, (s + 1) * seq)

        def pad_sequence():
            z_s[0:CONV_PAD, :] = jnp.zeros((CONV_PAD, CONV_CH), F32)
            z_s[CONV_PAD + seq:2 * CONV_PAD + seq, :] = jnp.zeros((CONV_PAD, CONV_CH), F32)
            z_s[CONV_PAD:CONV_PAD + seq, :] = z_ref[urows, :]

        if whole:
            pad_sequence()
            zw = z_s
        else:
            pl.when(qi == 0)(pad_sequence)
            zw_s[...] = z_s[pl.ds(pl.multiple_of(qi * tq, tq), win), :]
            zw = zw_s

        for b in range(1, 8):
            zsh_s[b - 1, 0:span, :] = zw[b:b + span, :]
        conv_rows = []
        for c0 in range(0, tq, chunk):
            acc = jnp.zeros((chunk // 8, 8, CONV_CH), F32)
            for j in range(CONV_WIDTH):
                off = j + off0
                a, b = off // 8, off % 8
                if b == 0:
                    zz = zw[c0 + 8 * a:c0 + 8 * a + chunk, :]
                else:
                    zz = zsh_s[b - 1, c0 + 8 * a:c0 + 8 * a + chunk, :]
                acc = acc + zz.reshape(chunk // 8, 8, CONV_CH) * cw_ref[j]
            acc = acc.reshape(chunk, CONV_CH) + cb_ref[...]
            mu = jnp.mean(acc, axis=-1, keepdims=True)
            xc = acc - mu
            y = xc * lax.rsqrt(jnp.mean(xc * xc, axis=-1, keepdims=True) + EPS) * lg_ref[...] + lb_ref[...]
            conv_rows.append(_silu(y).astype(BF16))

        def segments(h):
            cols = slice(h * LANES, (h + 1) * LANES)
            segs = [(k_ref[urows, cols], vt_ref[s, cols, :])]
            if n_cache:
                segs.append((ck_ref[:, cols].astype(BF16), cvt_ref[cols, :].astype(BF16)))
            return segs

        def scores(h):
            qc = q_ref[s * tq:(s + 1) * tq, h * LANES:(h + 1) * LANES]
            zero = jnp.zeros_like(qc)
            qq = jnp.concatenate([jnp.where(lane < sub, qc, zero), jnp.where(lane >= sub, qc, zero)], axis=0)
            return [_dot_nt(kk, qq) for kk, _ in segments(h)]

        def head(h, s_list):
            ov, den, _ = _attend(s_list, [vt for _, vt in segments(h)])
            r = 1.0 / den
            o = ov[:, 0:tq] * r[:, 0:tq] - ov[:, tq:2 * tq] * (lam * r[:, tq:2 * tq])
            return o * lax.rsqrt(jnp.mean(o * o, axis=0, keepdims=True) + EPS) * sg_ref[...] * (1.0 - lam_init)

        heads = _pipelined(DIFF_HEADS, scores, head)
        attn = jnp.concatenate(heads, axis=0).T.astype(BF16)
        return jnp.concatenate([jnp.concatenate(conv_rows, axis=0), attn], axis=1)

    cat = jnp.concatenate([block(s) for s in range(nseq)], axis=0)
    mixed = _dot(cat, wo_ref[...])
    o_ref[...] = x_ref[...] + _rms(mixed, m_ref[2:3, :] * ng_ref[1:2, :])


def _side_specs(side, steps, index):
    in_specs, out_specs, out_shape = [], [], []
    for w in side:
        rows, cols = w.shape
        n_blk = math.gcd(steps, rows // BF16_ROWS)
        per = steps // n_blk
        spec = pl.BlockSpec((rows // n_blk, cols), lambda *ids, per=per: (index(*ids) // per, 0))
        in_specs.append(spec)
        out_specs.append(spec)
        out_shape.append(jax.ShapeDtypeStruct(w.shape, BF16))
    return in_specs, out_specs, out_shape


def _mix0(u, q, k, vt, cache, x, m, norm_g, conv_w, conv_b, ln_g, ln_b, lam, subln_g, w_out,
          *, batch, seq, tq, cond_row, lam_init, side=(), nseq=1):
    nq = seq // tq
    n_cache = 0 if cache is None else cache[0].shape[1]
    win = tq + 2 * CONV_PAD
    assert tq % CONV_ROWS == 0
    steps = (batch // nseq) * nq
    const = lambda shape: pl.BlockSpec(shape, lambda b, i: (0,) * len(shape))
    in_specs = [pl.BlockSpec((nseq * seq, CONV_CH), lambda b, i: (b, 0)),
                pl.BlockSpec((nseq * tq, DIFF_WIDTH), lambda b, i: (b * nq + i, 0)),
                pl.BlockSpec((nseq * seq, DIFF_WIDTH), lambda b, i: (b, 0)),
                pl.BlockSpec((nseq, DIFF_WIDTH, seq), lambda b, i: (b, 0, 0))]
    args = [u, q, k, vt]
    if n_cache:
        in_specs += [pl.BlockSpec((None, n_cache, DIFF_WIDTH), lambda b, i: (b, 0, 0)),
                     pl.BlockSpec((None, DIFF_WIDTH, n_cache), lambda b, i: (b, 0, 0))]
        args += [cache[0], cache[1]]
    in_specs += [
        pl.BlockSpec((nseq * tq, D_MODEL), lambda b, i: (b * nq + i, 0)),
        pl.BlockSpec((None, N_MOD, D_MODEL), lambda b, i: (cond_row(b * nseq), 0, 0)),
        const((4, D_MODEL)), const((CONV_WIDTH, 8, CONV_CH)), const((1, CONV_CH)), const((1, CONV_CH)),
        const((1, CONV_CH)), const((4, LANES // 2)), const((LANES, 1)), _resident((D_MODEL, D_MODEL)),
    ]
    args += [x, m, norm_g, conv_w, conv_b, ln_g, ln_b, lam, subln_g, w_out]
    side_in, side_out, side_shape = _side_specs(side, steps, lambda b, i: b * nq + i)
    return pl.pallas_call(
        functools.partial(_mix0_kernel, seq=seq, tq=tq, n_cache=n_cache, lam_init=lam_init, n_side=len(side),
                          nseq=nseq),
        grid=(batch // nseq, nq),
        in_specs=in_specs + side_in,
        out_specs=[pl.BlockSpec((nseq * tq, D_MODEL), lambda b, i: (b * nq + i, 0))] + side_out,
        out_shape=[jax.ShapeDtypeStruct((batch * seq, D_MODEL), F32)] + side_shape,
        scratch_shapes=[
            pltpu.VMEM((seq + 2 * CONV_PAD, CONV_CH), F32), pltpu.VMEM((win, CONV_CH), F32),
            pltpu.VMEM((7, win - 8, CONV_CH), F32),
        ],
        compiler_params=_params("arbitrary", "arbitrary"),
        name="mix_conv_diff",
    )(*args, *side)


def _mix1_kernel(*refs, seq, tq, n_cache, nseq):
    if n_cache:
        (q_ref, kp_ref, kc_ref, kn_ref, vtp_ref, vtc_ref, vtn_ref, ck_ref, cvt_ref,
         x_ref, m_ref, ng_ref, sink_ref, wo_ref, o_ref) = refs
    else:
        (q_ref, kc_ref, vtc_ref, x_ref, m_ref, ng_ref, sink_ref, wo_ref, o_ref) = refs
    qi = pl.program_id(1)
    lane = lax.broadcasted_iota(jnp.int32, (tq, LANES), 1)

    def band(base, rows):
        kpos = base + lax.broadcasted_iota(jnp.int32, (rows, tq), 0)
        qpos = qi * tq + lax.broadcasted_iota(jnp.int32, (rows, tq), 1)
        ok = (jnp.abs(qpos - kpos) <= WINDOW) & (kpos >= 0) & (kpos < seq)
        return jnp.concatenate([ok] * GQA_GROUP, axis=1)

    def attention(blocks):
        units = [(r0, segs, g) for r0, segs in blocks for g in range(GQA_KV_HEADS)]

        def scores(u):
            r0, segs, g = units[u]
            b = g % 2
            kcols = slice((g // 2) * LANES, (g // 2 + 1) * LANES)
            keep = (lane < HEAD_DIM) if b == 0 else (lane >= HEAD_DIM)
            qs = []
            for h in range(g * GQA_GROUP, (g + 1) * GQA_GROUP):
                qc = q_ref[r0:r0 + tq, (h // 2) * LANES:(h // 2 + 1) * LANES]
                qa = qc if h % 2 == b else pltpu.roll(qc, HEAD_DIM, 1)
                qs.append(jnp.where(keep, qa, jnp.zeros_like(qa)))
            qg = jnp.concatenate(qs, axis=0)
            s_list = []
            for k_ref, _, ok in segs:
                s = _dot_nt(k_ref[:, kcols].astype(BF16), qg)
                s_list.append(s if ok is None else jnp.where(ok, s, NEG_INF))
            return s_list

        def group(u, s_list):
            _, segs, g = units[u]
            vrows = slice(g * HEAD_DIM, (g + 1) * HEAD_DIM)
            sink = jnp.concatenate([jnp.full((1, tq), sink_ref[h] * LOG2E, F32)
                                    for h in range(g * GQA_GROUP, (g + 1) * GQA_GROUP)], axis=1)
            ov, den, mx = _attend(s_list, [vt_ref[vrows, :].astype(BF16) for _, vt_ref, _ in segs], floor=sink)
            ov = ov * (1.0 / (den + jnp.exp2(sink - mx)))
            return jnp.concatenate([ov[:, hh * tq:(hh + 1) * tq] for hh in range(GQA_GROUP)], axis=0)

        groups = _pipelined(len(units), scores, group)
        return [jnp.concatenate(groups[i:i + GQA_KV_HEADS], axis=0).T.astype(BF16)
                for i in range(0, len(units), GQA_KV_HEADS)]

    if n_cache:
        blocks = [(0, [(kp_ref, vtp_ref, band(qi * tq - WINDOW, WINDOW)),
                       (kc_ref, vtc_ref, band(qi * tq, tq)),
                       (kn_ref, vtn_ref, band(qi * tq + tq, WINDOW)),
                       (ck_ref, cvt_ref, None)])]
    else:
        blocks = [(s * tq, [(kc_ref.at[pl.ds(s * tq, tq)], vtc_ref.at[s], None)]) for s in range(nseq)]
    attn = jnp.concatenate(attention(blocks), axis=0)
    mixed = _dot(attn, wo_ref[...])
    o_ref[...] = x_ref[...] + _rms(mixed, m_ref[2:3, :] * ng_ref[1:2, :])


def _mix1(q, k, vt, cache, x, m, norm_g, sink, w_out, *, batch, seq, tq, cond_row, nseq=1):
    nq = seq // tq
    n_cache = 0 if cache is None else cache[0].shape[1]
    rows = nseq * tq
    q_spec = pl.BlockSpec((rows, D_MODEL), lambda b, i: (b * nq + i, 0))
    kc_spec = pl.BlockSpec((rows, KV_WIDTH), lambda b, i: (b * nq + i, 0))
    vtc_spec = pl.BlockSpec((None, KV_WIDTH, tq), lambda b, i: (b, 0, i))
    if n_cache:
        nw = seq // WINDOW
        per = tq // WINDOW
        prev = lambda i: jnp.maximum(i * per - 1, 0)
        nxt = lambda i: jnp.minimum(i * per + per, nw - 1)
        in_specs = [q_spec,
                    pl.BlockSpec((WINDOW, KV_WIDTH), lambda b, i: (b * nw + prev(i), 0)), kc_spec,
                    pl.BlockSpec((WINDOW, KV_WIDTH), lambda b, i: (b * nw + nxt(i), 0)),
                    pl.BlockSpec((None, KV_WIDTH, WINDOW), lambda b, i: (b, 0, prev(i))), vtc_spec,
                    pl.BlockSpec((None, KV_WIDTH, WINDOW), lambda b, i: (b, 0, nxt(i))),
                    pl.BlockSpec((None, n_cache, KV_WIDTH), lambda b, i: (b, 0, 0)),
                    pl.BlockSpec((None, KV_WIDTH, n_cache), lambda b, i: (b, 0, 0))]
        args = [q, k, k, k, vt, vt, vt, cache[0], cache[1]]
    else:
        in_specs = [q_spec, kc_spec, pl.BlockSpec((nseq, KV_WIDTH, tq), lambda b, i: (b, 0, 0))]
        args = [q, k, vt]
    in_specs += [
        pl.BlockSpec((rows, D_MODEL), lambda b, i: (b * nq + i, 0)),
        pl.BlockSpec((None, N_MOD, D_MODEL), lambda b, i: (cond_row(b * nseq), 0, 0)),
        pl.BlockSpec((4, D_MODEL), lambda b, i: (0, 0)),
        pl.BlockSpec(memory_space=pltpu.SMEM),
        pl.BlockSpec((D_MODEL, D_MODEL), lambda b, i: (0, 0)),
    ]
    args += [x, m, norm_g, sink, w_out]
    return pl.pallas_call(
        functools.partial(_mix1_kernel, seq=seq, tq=tq, n_cache=n_cache, nseq=nseq),
        grid=(batch // nseq, nq),
        in_specs=in_specs,
        out_specs=pl.BlockSpec((rows, D_MODEL), lambda b, i: (b * nq + i, 0)),
        out_shape=jax.ShapeDtypeStruct((batch * seq, D_MODEL), F32),
        compiler_params=_params("arbitrary", "arbitrary"),
        name="mix_gqa",
    )(*args)


FFN_CHUNKS = (1536, 1280)


def _ffn_kernel(*refs, n_side, proj):
    x_ref, m_ref, ng_ref, wgu_ref, wd_ref = refs[:5]
    n_pin = 0 if proj is None else (5 if proj[1] else 3)
    n_pout = 0 if proj is None else len(proj[0])
    n_in = 5 + n_pin + n_side
    o_ref = refs[n_in]
    _side_convert(refs[5 + n_pin:n_in], refs[n_in + 1 + n_pout:])
    h = (_rms(x_ref[...], ng_ref[2:3, :] * (1.0 + m_ref[4:5, :])) + m_ref[3:4, :]).astype(BF16)
    acc = jnp.zeros(h.shape, F32)
    c0 = 0
    for width in FFN_CHUNKS:
        gate = _dot(h, wgu_ref[:, c0:c0 + width])
        up = _dot(h, wgu_ref[:, FFN_HIDDEN + c0:FFN_HIDDEN + c0 + width])
        act = (_silu(gate) * up).astype(BF16)
        acc = acc + _dot(act, wd_ref[c0:c0 + width, :])
        c0 += width
    y = x_ref[...] + _rms(acc, m_ref[5:6, :] * ng_ref[3:4, :])
    o_ref[...] = y
    if proj is not None:
        outs, rope, sblk = proj
        pm_ref, pg_ref, pw_ref = refs[5:8]
        _project(y, pm_ref, pg_ref, pw_ref, refs[8:10] if rope else None, outs, refs[n_in + 1:n_in + 1 + n_pout], sblk)


def _ffn(x, m, norm_g, w_gu, w_down, *, tm, cond_row, side=(), proj=None):
    rows = x.shape[0]
    side_in, side_out, side_shape = _side_specs(side, rows // tm, lambda i: i)
    proj_in, proj_args, proj_out, proj_shape, kproj = [], [], [], [], None
    if proj is not None:
        pm, pg, pw, outs, seq, rope = proj
        proj_in, proj_args, proj_out, proj_shape, sblk = _project_specs(pm, pg, pw, outs, rows=rows, tm=tm,
                                                                        cond_row=cond_row, seq=seq, rope=rope)
        kproj = (outs, rope, sblk)
    return pl.pallas_call(
        functools.partial(_ffn_kernel, n_side=len(side), proj=kproj),
        grid=(rows // tm,),
        in_specs=[
            pl.BlockSpec((tm, D_MODEL), lambda i: (i, 0)),
            pl.BlockSpec((None, N_MOD, D_MODEL), lambda i: (cond_row(i), 0, 0)),
            pl.BlockSpec((4, D_MODEL), lambda i: (0, 0)),
            _resident((D_MODEL, 2 * FFN_HIDDEN)),
            _resident((FFN_HIDDEN, D_MODEL)),
        ] + proj_in + side_in,
        out_specs=[pl.BlockSpec((tm, D_MODEL), lambda i: (i, 0))] + proj_out + side_out,
        out_shape=[jax.ShapeDtypeStruct((rows, D_MODEL), F32)] + proj_shape + side_shape,
        compiler_params=_params("arbitrary"),
        name="ffn",
    )(x, m, norm_g, w_gu, w_down, *proj_args, *side)


LAM_INIT_0 = 0.8 - 0.6 * math.exp(-0.3 * 0)
CTX_SEQS_PER_STEP = 4


def _trunk(x, m, caches, weights, *, batch, seq, cond_of_batch):
    (norm_g0, w_in, conv_w, conv_b, ln_g, ln_b, lam, subln_g, w_out0, w_gu0, w_down0,
     norm_g1, w_qkv, sink, w_out1, w_gu1, w_down1) = weights
    m0, m1 = m
    ctx = caches is None
    rope = not ctx
    tm = 1024 if ctx else 512
    tm_ffn = 512
    tq = 256
    row_of = lambda t: (lambda i: cond_of_batch((i * t) // seq))

    o0 = 2 * CONV_CH
    ko, vo = o0 + DIFF_WIDTH, o0 + 2 * DIFF_WIDTH
    outs0 = [("glu", 0, o0, F32, False, None), ("nat", o0, DIFF_WIDTH, BF16, True, Q_SCALE),
             ("nat", ko, DIFF_WIDTH, BF16, True, None), ("T", vo, DIFF_WIDTH, BF16, False, None)]
    if ctx:
        outs0 += [("T", ko, DIFF_WIDTH, F32, False, None), ("heads", vo, DIFF_WIDTH, F32, False, None)]
    res = _pre(x, m0, norm_g0, w_in, tuple(outs0), tm=tm, cond_row=row_of(tm), seq=seq, rope=rope)
    u, q, k, vt = res[:4]
    new0 = tuple(res[4:])
    x, *conv0 = _mix0(u, q, k, vt, None if ctx else caches[0], x, m0, norm_g0, conv_w, conv_b,
                      ln_g, ln_b, lam, subln_g, w_out0, batch=batch, seq=seq, tq=tq, cond_row=cond_of_batch,
                      lam_init=LAM_INIT_0, side=(w_gu0, w_down0, w_qkv) if ctx else (),
                      nseq=CTX_SEQS_PER_STEP if ctx else 1)
    if ctx:
        w_gu0, w_down0, w_qkv = conv0

    outs1 = [("nat", 0, Q_WIDTH, BF16, True, Q_SCALE), ("nat", Q_WIDTH, KV_WIDTH, BF16, True, None)]
    if ctx:
        outs1 += [("T", Q_WIDTH + KV_WIDTH, KV_WIDTH, F32, False, None), ("T", Q_WIDTH, KV_WIDTH, F32, False, None)]
    else:
        outs1 += [("T", Q_WIDTH + KV_WIDTH, KV_WIDTH, BF16, False, None)]
    res = _ffn(x, m0, norm_g0, w_gu0, w_down0, tm=tm_ffn, cond_row=row_of(tm_ffn),
               side=(w_out1, w_gu1, w_down1) if ctx else (),
               proj=(m1, norm_g1, w_qkv, tuple(outs1), seq, rope))
    x, q, k, vt = res[:4]
    new1 = (res[4], vt) if ctx else ()
    if ctx:
        w_out1, w_gu1, w_down1 = res[5:]
    x = _mix1(q, k, vt, None if ctx else caches[1], x, m1, norm_g1, sink, w_out1,
              batch=batch, seq=seq, tq=tq, cond_row=cond_of_batch, nseq=CTX_SEQS_PER_STEP if ctx else 1)
    x, = _ffn(x, m1, norm_g1, w_gu1, w_down1, tm=tm_ffn, cond_row=row_of(tm_ffn))
    return x, new0 + new1, (w_gu0, w_down0, w_qkv, w_out1, w_gu1, w_down1)


def kernel(x_prompt, x_sample, cache_k0, cache_v0, cache_k1, cache_v1, c, c_ctx, l0_mod_w, l0_mod_b, l0_norm_g, l0_w_in, l0_conv_w, l0_conv_b, l0_conv_ln_g, l0_conv_ln_b, l0_lambda, l0_subln_g, l0_w_out, l0_w_gu, l0_w_down, l1_mod_w, l1_mod_b, l1_norm_g, l1_w_qkv, l1_sink, l1_w_out, l1_w_gu, l1_w_down):
    batch, seq, d = x_prompt.shape
    dec_batch, dec_seq, _ = x_sample.shape
    n_past = cache_k0.shape[1]
    assert d == D_MODEL and seq == 256 and batch % CTX_SEQS_PER_STEP == 0 and (batch * seq) % 512 == 0
    assert dec_seq % 512 == 0 and dec_seq % GRID_W == 0 and n_past % LANES == 0 and 1 + dec_batch <= MOD_ROWS

    cond = jnp.concatenate([c_ctx[None, :], c, jnp.zeros((MOD_ROWS - 1 - dec_batch, d), F32)], axis=0)
    m = (_modulation(cond, l0_mod_w, l0_mod_b), _modulation(cond, l1_mod_w, l1_mod_b))

    row = lambda v: v.reshape(1, -1)
    layer0 = (l0_norm_g, l0_w_in.astype(BF16), jnp.broadcast_to(l0_conv_w, (CONV_WIDTH, 8, CONV_CH)), row(l0_conv_b),
              row(l0_conv_ln_g), row(l0_conv_ln_b), l0_lambda, l0_subln_g.reshape(LANES, 1),
              l0_w_out.astype(BF16))
    weights = layer0 + (l0_w_gu, l0_w_down, l1_norm_g, l1_w_qkv, l1_sink, l1_w_out, l1_w_gu, l1_w_down)

    y_prompt, (kt0, v0, kt1, vt1), bf16_w = _trunk(x_prompt.reshape(batch * seq, d), m, None, weights,
                                                   batch=batch, seq=seq, cond_of_batch=lambda b: 0)
    w_gu0, w_down0, w_qkv, w_out1, w_gu1, w_down1 = bf16_w
    weights = layer0 + (w_gu0, w_down0, l1_norm_g, w_qkv, l1_sink, w_out1, w_gu1, w_down1)
    caches = ((cache_k0.reshape(dec_batch, n_past, -1),
               jnp.swapaxes(cache_v0.reshape(dec_batch, n_past, -1), 1, 2)),
              (cache_k1.reshape(dec_batch, n_past, -1),
               jnp.swapaxes(cache_v1.reshape(dec_batch, n_past, -1), 1, 2)))
    y_sample, _, _ = _trunk(x_sample.reshape(dec_batch * dec_seq, d), m, caches, weights,
                            batch=dec_batch, seq=dec_seq, cond_of_batch=lambda b: 1 + b)
    new_k0 = kt0.reshape(batch, DIFF_HEADS, 2, HEAD_DIM, seq).transpose(0, 4, 1, 2, 3)
    new_v0 = v0.reshape(batch, seq, DIFF_HEADS, 2 * HEAD_DIM)
    new_k1 = kt1.reshape(batch, GQA_KV_HEADS, HEAD_DIM, seq).transpose(0, 3, 1, 2)
    new_v1 = vt1.reshape(batch, GQA_KV_HEADS, HEAD_DIM, seq).transpose(0, 3, 1, 2)
    return (y_prompt.reshape(batch, seq, d), y_sample.reshape(dec_batch, dec_seq, d),
            new_k0, new_v0, new_k1, new_v1)
```

```python
import functools
import math

import numpy as np
import jax
import jax.numpy as jnp
from jax import lax
from jax.experimental import pallas as pl
from jax.experimental.pallas import tpu as pltpu

D_MODEL = 1024
GRID_W = 64
ROPE_THETA = 10000.0
EPS = 1e-6
NEG_INF = -1e30
LOG2E = math.log2(math.e)
CONV_CH = 512
CONV_WIDTH = 31
CONV_PAD = 16
CONV_ROWS = 64
DIFF_HEADS = 4
DIFF_WIDTH = 512
HEAD_DIM = 64
GQA_HEADS = 16
GQA_KV_HEADS = 4
GQA_GROUP = 4
Q_WIDTH = GQA_HEADS * HEAD_DIM
KV_WIDTH = GQA_KV_HEADS * HEAD_DIM
Q_SCALE = LOG2E * HEAD_DIM ** -0.5
WINDOW = 128
FFN_HIDDEN = 2816
N_MOD = 6
LANES = 128
BF16_ROWS = 16
MOD_ROWS = 8
VMEM_LIMIT = 56 * 1024 * 1024

BF16 = jnp.bfloat16
F32 = jnp.float32


def _gated(a, g):
    return (0.5 * a) * (jnp.tanh(0.5 * g) + 1.0)


def _silu(x):
    h = 0.5 * x
    return h * (jnp.tanh(h) + 1.0)


def _rms(x, g):
    return x * lax.rsqrt(jnp.mean(x * x, axis=-1, keepdims=True) + EPS) * g


def _dot(a, b):
    return jnp.dot(a, b, preferred_element_type=F32)


def _dot_nt(a, b):
    return lax.dot_general(a, b, (((1,), (1,)), ((), ())), preferred_element_type=F32)


def _params(*sem):
    return pltpu.CompilerParams(dimension_semantics=sem, vmem_limit_bytes=VMEM_LIMIT)


def _resident(shape):
    return pl.BlockSpec(shape, lambda *_: (0,) * len(shape), pipeline_mode=pl.Buffered(1))


def _mod_kernel(c_ref, w_ref, b_ref, o_ref):
    s = _silu(c_ref[...]).astype(BF16)
    o_ref[...] = _dot(s, w_ref[...].astype(BF16)) + b_ref[...]


def _modulation(cond, mod_w, mod_b):
    n = mod_w.shape[1]
    tn = 1536
    out = pl.pallas_call(
        _mod_kernel,
        grid=(n // tn,),
        in_specs=[
            pl.BlockSpec((MOD_ROWS, D_MODEL), lambda j: (0, 0)),
            pl.BlockSpec((D_MODEL, tn), lambda j: (0, j)),
            pl.BlockSpec((1, tn), lambda j: (0, j)),
        ],
        out_specs=pl.BlockSpec((MOD_ROWS, tn), lambda j: (0, j)),
        out_shape=jax.ShapeDtypeStruct((MOD_ROWS, n), F32),
        compiler_params=_params("arbitrary"),
        name="modulation",
    )(cond, mod_w, mod_b.reshape(1, n))
    return out.reshape(MOD_ROWS, N_MOD, D_MODEL)


def _rope_tables(seq):
    t = np.arange(seq)
    rows, cols = t // GRID_W, t % GRID_W
    half = HEAD_DIM // 2
    inv = 1.0 / (ROPE_THETA ** (np.arange(0, half, 2, dtype=np.float64) / half))
    ar = rows[:, None] * inv[None, :]
    ac = cols[:, None] * inv[None, :]
    cos = np.concatenate([np.cos(ar), np.cos(ar), np.cos(ac), np.cos(ac)], axis=1)
    sin = np.concatenate([-np.sin(ar), np.sin(ar), -np.sin(ac), np.sin(ac)], axis=1)
    reps = LANES // HEAD_DIM
    return (jnp.asarray(np.tile(cos, (1, reps)), F32), jnp.asarray(np.tile(sin, (1, reps)), F32))


def _rope(x, cos, sin):
    quarter = HEAD_DIM // 4
    lane = lax.broadcasted_iota(jnp.int32, x.shape, 1)
    lo = (lane % (2 * quarter)) < quarter
    partner = jnp.where(lo, pltpu.roll(x, LANES - quarter, 1), pltpu.roll(x, quarter, 1))
    return x * cos + partner * sin


def _pre_kernel(*refs, outs, rope, sblk):
    n_in = 6 if rope else 4
    x_ref, m_ref, g_ref, w_ref = refs[:4]
    _project(x_ref[...], m_ref, g_ref, w_ref, refs[4:6] if rope else None, outs, refs[n_in:], sblk)


def _project(x, m_ref, g_ref, w_ref, trig, outs, o_refs, sblk):
    rope = trig is not None
    tm = x.shape[0]
    h = (_rms(x, g_ref[0:1, :] * (1.0 + m_ref[1:2, :])) + m_ref[0:1, :]).astype(BF16)
    cost = lambda o: o[2] * {"T": 5.0, "heads": 0.5, "glu": 1.2}.get(o[0], 3.0 if (rope and o[4]) else 1.0)
    ranges = sorted({(o[1], o[2]) for o in outs},
                    key=lambda r: (-sum(cost(o) for o in outs if (o[1], o[2]) == r), r[0]))
    for start, width in ranges:
        p = _dot(h, w_ref[:, start:start + width])
        for (kind, s0, w0, _, roped, scale), o_ref in zip(outs, o_refs):
            if (s0, w0) != (start, width):
                continue
            if kind == "nat":
                for c in range(width // LANES):
                    xc = p[:, c * LANES:(c + 1) * LANES]
                    if rope and roped:
                        xc = _rope(xc, trig[0][...], trig[1][...])
                    if scale is not None:
                        xc = xc * scale
                    o_ref[:, c * LANES:(c + 1) * LANES] = xc.astype(o_ref.dtype)
            elif kind == "T":
                for s in range(tm // sblk):
                    o_ref[s] = p[s * sblk:(s + 1) * sblk, :].T.astype(o_ref.dtype)
            elif kind == "glu":
                o_ref[...] = _gated(p[:, 0:width // 2], p[:, width // 2:width])
            else:
                heads = width // LANES
                for hd in range(heads):
                    o_ref[pl.ds(hd, tm, stride=heads), :] = p[:, hd * LANES:(hd + 1) * LANES]


def _project_specs(m, norm_g, w, outs, *, rows, tm, cond_row, seq, rope):
    n = w.shape[1]
    sblk = min(seq, tm)
    per_seq = seq // sblk
    in_specs = [
        pl.BlockSpec((None, N_MOD, D_MODEL), lambda i: (cond_row(i), 0, 0)),
        pl.BlockSpec((4, D_MODEL), lambda i: (0, 0)),
        _resident((D_MODEL, n)),
    ]
    args = [m, norm_g, w]
    if rope:
        cos, sin = _rope_tables(seq)
        nblk = seq // tm
        in_specs += [pl.BlockSpec((tm, LANES), lambda i: (i % nblk, 0))] * 2
        args += [cos, sin]
    out_specs, out_shape = [], []
    for (kind, _, wd, dt, _, _) in outs:
        if kind == "nat":
            out_specs.append(pl.BlockSpec((tm, wd), lambda i: (i, 0)))
            out_shape.append(jax.ShapeDtypeStruct((rows, wd), dt))
        elif kind == "T":
            out_specs.append(pl.BlockSpec((tm // sblk, wd, sblk), lambda i: (i // per_seq, 0, i % per_seq)))
            out_shape.append(jax.ShapeDtypeStruct((rows // seq, wd, seq), dt))
        elif kind == "glu":
            out_specs.append(pl.BlockSpec((tm, wd // 2), lambda i: (i, 0)))
            out_shape.append(jax.ShapeDtypeStruct((rows, wd // 2), dt))
        else:
            out_specs.append(pl.BlockSpec((tm * (wd // LANES), LANES), lambda i: (i, 0)))
            out_shape.append(jax.ShapeDtypeStruct((rows * (wd // LANES), LANES), dt))
    return in_specs, args, out_specs, out_shape, sblk


def _pre(x, m, norm_g, w, outs, *, tm, cond_row, seq, rope):
    rows = x.shape[0]
    in_specs = [pl.BlockSpec((tm, D_MODEL), lambda i: (i, 0))]
    proj_in, proj_args, out_specs, out_shape, sblk = _project_specs(m, norm_g, w, outs, rows=rows, tm=tm,
                                                                    cond_row=cond_row, seq=seq, rope=rope)
    in_specs += proj_in
    args = [x] + proj_args
    return pl.pallas_call(
        functools.partial(_pre_kernel, outs=outs, rope=rope, sblk=sblk),
        grid=(rows // tm,),
        in_specs=in_specs,
        out_specs=out_specs,
        out_shape=out_shape,
        compiler_params=_params("arbitrary"),
        name="pre_rope" if rope else "pre",
    )(*args)


def _attend(s_list, vt_list, floor=None):
    mx = None
    for s in s_list:
        smax = jnp.max(s, axis=0, keepdims=True)
        mx = smax if mx is None else jnp.maximum(mx, smax)
    if floor is not None:
        mx = jnp.maximum(mx, floor)
    d = vt_list[0].shape[0]
    ov = None
    for s, vt in zip(s_list, vt_list):
        ones = jnp.ones((BF16_ROWS, vt.shape[1]), BF16)
        pv = _dot(jnp.concatenate([vt, ones], axis=0), jnp.exp2(s - mx).astype(BF16))
        ov = pv if ov is None else ov + pv
    return ov[0:d, :], ov[d:d + 1, :], mx


def _pipelined(n, first, second):
    out = []
    nxt = first(0)
    for u in range(n):
        cur = nxt
        if u + 1 < n:
            nxt = first(u + 1)
        out.append(second(u, cur))
    return out


def _side_convert(src_refs, dst_refs):
    for src, dst in zip(src_refs, dst_refs):
        dst[...] = src[...].astype(BF16)


def _mix0_kernel(*refs, seq, tq, n_cache, lam_init, n_side, nseq):
    n_in = 16 if n_cache else 14
    z_ref, q_ref, k_ref, vt_ref = refs[:4]
    ck_ref, cvt_ref = refs[4:6] if n_cache else (None, None)
    x_ref, m_ref, ng_ref, cw_ref, cb_ref, lg_ref, lb_ref, lam_ref, sg_ref, wo_ref = refs[n_in - 10:n_in]
    o_ref = refs[n_in + n_side]
    z_s, zw_s, zsh_s = refs[n_in + 2 * n_side + 1:]
    _side_convert(refs[n_in:n_in + n_side], refs[n_in + n_side + 1:n_in + 2 * n_side + 1])
    qi = pl.program_id(1)
    whole = tq == seq
    win = tq + 2 * CONV_PAD
    span = win - 8
    chunk = CONV_ROWS
    off0 = CONV_PAD - CONV_WIDTH // 2
    la = jnp.sum(lam_ref[0:1, :] * lam_ref[1:2, :], axis=-1, keepdims=True)
    lb = jnp.sum(lam_ref[2:3, :] * lam_ref[3:4, :], axis=-1, keepdims=True)
    lam = jnp.exp(la) - jnp.exp(lb) + lam_init
    sub = LANES // 2
    lane = lax.broadcasted_iota(jnp.int32, (tq, LANES), 1)

    def block(s):
        urows = slice(s * seq, (s + 1) * seq)

        def pad_sequence():
            z_s[0:CONV_PAD, :] = jnp.zeros((CONV_PAD, CONV_CH), F32)
            z_s[CONV_PAD + seq:2 * CONV_PAD + seq, :] = jnp.zeros((CONV_PAD, CONV_CH), F32)
            z_s[CONV_PAD:CONV_PAD + seq, :] = z_ref[urows, :]

        if whole:
            pad_sequence()
            zw = z_s
        else:
            pl.when(qi == 0)(pad_sequence)
            zw_s[...] = z_s[pl.ds(pl.multiple_of(qi * tq, tq), win), :]
            zw = zw_s

        for b in range(1, 8):
            zsh_s[b - 1, 0:span, :] = zw[b:b + span, :]
        conv_rows = []
        for c0 in range(0, tq, chunk):
            acc = jnp.zeros((chunk // 8, 8, CONV_CH), F32)
            for j in range(CONV_WIDTH):
                off = j + off0
                a, b = off // 8, off % 8
                if b == 0:
                    zz = zw[c0 + 8 * a:c0 + 8 * a + chunk, :]
                else:
                    zz = zsh_s[b - 1, c0 + 8 * a:c0 + 8 * a + chunk, :]
                acc = acc + zz.reshape(chunk // 8, 8, CONV_CH) * cw_ref[j]
            acc = acc.reshape(chunk, CONV_CH) + cb_ref[...]
            mu = jnp.mean(acc, axis=-1, keepdims=True)
            xc = acc - mu
            y = xc * lax.rsqrt(jnp.mean(xc * xc, axis=-1, keepdims=True) + EPS) * lg_ref[...] + lb_ref[...]
            conv_rows.append(_silu(y).astype(BF16))

        def segments(h):
            cols = slice(h * LANES, (h + 1) * LANES)
            segs = [(k_ref[urows, cols], vt_ref[s, cols, :])]
            if n_cache:
                segs.append((ck_ref[:, cols].astype(BF16), cvt_ref[cols, :].astype(BF16)))
            return segs

        def scores(h):
            qc = q_ref[s * tq:(s + 1) * tq, h * LANES:(h + 1) * LANES]
            zero = jnp.zeros_like(qc)
            qq = jnp.concatenate([jnp.where(lane < sub, qc, zero), jnp.where(lane >= sub, qc, zero)], axis=0)
            return [_dot_nt(kk, qq) for kk, _ in segments(h)]

        def head(h, s_list):
            ov, den, _ = _attend(s_list, [vt for _, vt in segments(h)])
            r = 1.0 / den
            o = ov[:, 0:tq] * r[:, 0:tq] - ov[:, tq:2 * tq] * (lam * r[:, tq:2 * tq])
            return o * lax.rsqrt(jnp.mean(o * o, axis=0, keepdims=True) + EPS) * sg_ref[...] * (1.0 - lam_init)

        heads = _pipelined(DIFF_HEADS, scores, head)
        attn = jnp.concatenate(heads, axis=0).T.astype(BF16)
        return jnp.concatenate([jnp.concatenate(conv_rows, axis=0), attn], axis=1)

    cat = jnp.concatenate([block(s) for s in range(nseq)], axis=0)
    mixed = _dot(cat, wo_ref[...])
    o_ref[...] = x_ref[...] + _rms(mixed, m_ref[2:3, :] * ng_ref[1:2, :])


def _side_specs(side, steps, index):
    in_specs, out_specs, out_shape = [], [], []
    for w in side:
        rows, cols = w.shape
        n_blk = math.gcd(steps, rows // BF16_ROWS)
        per = steps // n_blk
        spec = pl.BlockSpec((rows // n_blk, cols), lambda *ids, per=per: (index(*ids) // per, 0))
        in_specs.append(spec)
        out_specs.append(spec)
        out_shape.append(jax.ShapeDtypeStruct(w.shape, BF16))
    return in_specs, out_specs, out_shape


def _mix0(u, q, k, vt, cache, x, m, norm_g, conv_w, conv_b, ln_g, ln_b, lam, subln_g, w_out,
          *, batch, seq, tq, cond_row, lam_init, side=(), nseq=1):
    nq = seq // tq
    n_cache = 0 if cache is None else cache[0].shape[1]
    win = tq + 2 * CONV_PAD
    assert tq % CONV_ROWS == 0
    steps = (batch // nseq) * nq
    const = lambda shape: pl.BlockSpec(shape, lambda b, i: (0,) * len(shape))
    in_specs = [pl.BlockSpec((nseq * seq, CONV_CH), lambda b, i: (b, 0)),
                pl.BlockSpec((nseq * tq, DIFF_WIDTH), lambda b, i: (b * nq + i, 0)),
                pl.BlockSpec((nseq * seq, DIFF_WIDTH), lambda b, i: (b, 0)),
                pl.BlockSpec((nseq, DIFF_WIDTH, seq), lambda b, i: (b, 0, 0))]
    args = [u, q, k, vt]
    if n_cache:
        in_specs += [pl.BlockSpec((None, n_cache, DIFF_WIDTH), lambda b, i: (b, 0, 0)),
                     pl.BlockSpec((None, DIFF_WIDTH, n_cache), lambda b, i: (b, 0, 0))]
        args += [cache[0], cache[1]]
    in_specs += [
        pl.BlockSpec((nseq * tq, D_MODEL), lambda b, i: (b * nq + i, 0)),
        pl.BlockSpec((None, N_MOD, D_MODEL), lambda b, i: (cond_row(b * nseq), 0, 0)),
        const((4, D_MODEL)), const((CONV_WIDTH, 8, CONV_CH)), const((1, CONV_CH)), const((1, CONV_CH)),
        const((1, CONV_CH)), const((4, LANES // 2)), const((LANES, 1)), _resident((D_MODEL, D_MODEL)),
    ]
    args += [x, m, norm_g, conv_w, conv_b, ln_g, ln_b, lam, subln_g, w_out]
    side_in, side_out, side_shape = _side_specs(side, steps, lambda b, i: b * nq + i)
    return pl.pallas_call(
        functools.partial(_mix0_kernel, seq=seq, tq=tq, n_cache=n_cache, lam_init=lam_init, n_side=len(side),
                          nseq=nseq),
        grid=(batch // nseq, nq),
        in_specs=in_specs + side_in,
        out_specs=[pl.BlockSpec((nseq * tq, D_MODEL), lambda b, i: (b * nq + i, 0))] + side_out,
        out_shape=[jax.ShapeDtypeStruct((batch * seq, D_MODEL), F32)] + side_shape,
        scratch_shapes=[
            pltpu.VMEM((seq + 2 * CONV_PAD, CONV_CH), F32), pltpu.VMEM((win, CONV_CH), F32),
            pltpu.VMEM((7, win - 8, CONV_CH), F32),
        ],
        compiler_params=_params("arbitrary", "arbitrary"),
        name="mix_conv_diff",
    )(*args, *side)


def _mix1_kernel(*refs, seq, tq, n_cache, nseq):
    if n_cache:
        (q_ref, kp_ref, kc_ref, kn_ref, vtp_ref, vtc_ref, vtn_ref, ck_ref, cvt_ref,
         x_ref, m_ref, ng_ref, sink_ref, wo_ref, o_ref) = refs
    else:
        (q_ref, kc_ref, vtc_ref, x_ref, m_ref, ng_ref, sink_ref, wo_ref, o_ref) = refs
    qi = pl.program_id(1)
    lane = lax.broadcasted_iota(jnp.int32, (tq, LANES), 1)

    def band(base, rows):
        kpos = base + lax.broadcasted_iota(jnp.int32, (rows, tq), 0)
        qpos = qi * tq + lax.broadcasted_iota(jnp.int32, (rows, tq), 1)
        ok = (jnp.abs(qpos - kpos) <= WINDOW) & (kpos >= 0) & (kpos < seq)
        return jnp.concatenate([ok] * GQA_GROUP, axis=1)

    def attention(blocks):
        units = [(r0, segs, g) for r0, segs in blocks for g in range(GQA_KV_HEADS)]

        def scores(u):
            r0, segs, g = units[u]
            b = g % 2
            kcols = slice((g // 2) * LANES, (g // 2 + 1) * LANES)
            keep = (lane < HEAD_DIM) if b == 0 else (lane >= HEAD_DIM)
            qs = []
            for h in range(g * GQA_GROUP, (g + 1) * GQA_GROUP):
                qc = q_ref[r0:r0 + tq, (h // 2) * LANES:(h // 2 + 1) * LANES]
                qa = qc if h % 2 == b else pltpu.roll(qc, HEAD_DIM, 1)
                qs.append(jnp.where(keep, qa, jnp.zeros_like(qa)))
            qg = jnp.concatenate(qs, axis=0)
            s_list = []
            for k_ref, _, ok in segs:
                s = _dot_nt(k_ref[:, kcols].astype(BF16), qg)
                s_list.append(s if ok is None else jnp.where(ok, s, NEG_INF))
            return s_list

        def group(u, s_list):
            _, segs, g = units[u]
            vrows = slice(g * HEAD_DIM, (g + 1) * HEAD_DIM)
            sink = jnp.concatenate([jnp.full((1, tq), sink_ref[h] * LOG2E, F32)
                                    for h in range(g * GQA_GROUP, (g + 1) * GQA_GROUP)], axis=1)
            ov, den, mx = _attend(s_list, [vt_ref[vrows, :].astype(BF16) for _, vt_ref, _ in segs], floor=sink)
            ov = ov * (1.0 / (den + jnp.exp2(sink - mx)))
            return jnp.concatenate([ov[:, hh * tq:(hh + 1) * tq] for hh in range(GQA_GROUP)], axis=0)

        groups = _pipelined(len(units), scores, group)
        return [jnp.concatenate(groups[i:i + GQA_KV_HEADS], axis=0).T.astype(BF16)
                for i in range(0, len(units), GQA_KV_HEADS)]

    if n_cache:
        blocks = [(0, [(kp_ref, vtp_ref, band(qi * tq - WINDOW, WINDOW)),
                       (kc_ref, vtc_ref, band(qi * tq, tq)),
                       (kn_ref, vtn_ref, band(qi * tq + tq, WINDOW)),
                       (ck_ref, cvt_ref, None)])]
    else:
        blocks = [(s * tq, [(kc_ref.at[pl.ds(s * tq, tq)], vtc_ref.at[s], None)]) for s in range(nseq)]
    attn = jnp.concatenate(attention(blocks), axis=0)
    mixed = _dot(attn, wo_ref[...])
    o_ref[...] = x_ref[...] + _rms(mixed, m_ref[2:3, :] * ng_ref[1:2, :])


def _mix1(q, k, vt, cache, x, m, norm_g, sink, w_out, *, batch, seq, tq, cond_row, nseq=1):
    nq = seq // tq
    n_cache = 0 if cache is None else cache[0].shape[1]
    rows = nseq * tq
    q_spec = pl.BlockSpec((rows, D_MODEL), lambda b, i: (b * nq + i, 0))
    kc_spec = pl.BlockSpec((rows, KV_WIDTH), lambda b, i: (b * nq + i, 0))
    vtc_spec = pl.BlockSpec((None, KV_WIDTH, tq), lambda b, i: (b, 0, i))
    if n_cache:
        nw = seq // WINDOW
        per = tq // WINDOW
        prev = lambda i: jnp.maximum(i * per - 1, 0)
        nxt = lambda i: jnp.minimum(i * per + per, nw - 1)
        in_specs = [q_spec,
                    pl.BlockSpec((WINDOW, KV_WIDTH), lambda b, i: (b * nw + prev(i), 0)), kc_spec,
                    pl.BlockSpec((WINDOW, KV_WIDTH), lambda b, i: (b * nw + nxt(i), 0)),
                    pl.BlockSpec((None, KV_WIDTH, WINDOW), lambda b, i: (b, 0, prev(i))), vtc_spec,
                    pl.BlockSpec((None, KV_WIDTH, WINDOW), lambda b, i: (b, 0, nxt(i))),
                    pl.BlockSpec((None, n_cache, KV_WIDTH), lambda b, i: (b, 0, 0)),
                    pl.BlockSpec((None, KV_WIDTH, n_cache), lambda b, i: (b, 0, 0))]
        args = [q, k, k, k, vt, vt, vt, cache[0], cache[1]]
    else:
        in_specs = [q_spec, kc_spec, pl.BlockSpec((nseq, KV_WIDTH, tq), lambda b, i: (b, 0, 0))]
        args = [q, k, vt]
    in_specs += [
        pl.BlockSpec((rows, D_MODEL), lambda b, i: (b * nq + i, 0)),
        pl.BlockSpec((None, N_MOD, D_MODEL), lambda b, i: (cond_row(b * nseq), 0, 0)),
        pl.BlockSpec((4, D_MODEL), lambda b, i: (0, 0)),
        pl.BlockSpec(memory_space=pltpu.SMEM),
        pl.BlockSpec((D_MODEL, D_MODEL), lambda b, i: (0, 0)),
    ]
    args += [x, m, norm_g, sink, w_out]
    return pl.pallas_call(
        functools.partial(_mix1_kernel, seq=seq, tq=tq, n_cache=n_cache, nseq=nseq),
        grid=(batch // nseq, nq),
        in_specs=in_specs,
        out_specs=pl.BlockSpec((rows, D_MODEL), lambda b, i: (b * nq + i, 0)),
        out_shape=jax.ShapeDtypeStruct((batch * seq, D_MODEL), F32),
        compiler_params=_params("arbitrary", "arbitrary"),
        name="mix_gqa",
    )(*args)


FFN_CHUNKS = (1536, 1280)


def _ffn_kernel(*refs, n_side, proj, sub):
    x_ref, m_ref, ng_ref, wgu_ref, wd_ref = refs[:5]
    n_pin = 0 if proj is None else (5 if proj[1] else 3)
    n_pout = 0 if proj is None else len(proj[0])
    n_in = 5 + n_pin + n_side
    o_ref = refs[n_in]
    _side_convert(refs[5 + n_pin:n_in], refs[n_in + 1 + n_pout:])
    for r0 in range(0, x_ref.shape[0], sub):
        rows = slice(r0, r0 + sub)
        h = (_rms(x_ref[rows, :], ng_ref[2:3, :] * (1.0 + m_ref[4:5, :])) + m_ref[3:4, :]).astype(BF16)
        acc = jnp.zeros(h.shape, F32)
        c0 = 0
        for width in FFN_CHUNKS:
            gate = _dot(h, wgu_ref[:, c0:c0 + width])
            up = _dot(h, wgu_ref[:, FFN_HIDDEN + c0:FFN_HIDDEN + c0 + width])
            act = (_silu(gate) * up).astype(BF16)
            acc = acc + _dot(act, wd_ref[c0:c0 + width, :])
            c0 += width
        o_ref[rows, :] = x_ref[rows, :] + _rms(acc, m_ref[5:6, :] * ng_ref[3:4, :])
    if proj is not None:
        outs, rope, sblk = proj
        pm_ref, pg_ref, pw_ref = refs[5:8]
        _project(o_ref[...], pm_ref, pg_ref, pw_ref, refs[8:10] if rope else None, outs,
                 refs[n_in + 1:n_in + 1 + n_pout], sblk)


def _ffn(x, m, norm_g, w_gu, w_down, *, tm, sub, cond_row, side=(), proj=None):
    rows = x.shape[0]
    side_in, side_out, side_shape = _side_specs(side, rows // tm, lambda i: i)
    proj_in, proj_args, proj_out, proj_shape, kproj = [], [], [], [], None
    if proj is not None:
        pm, pg, pw, outs, seq, rope = proj
        proj_in, proj_args, proj_out, proj_shape, sblk = _project_specs(pm, pg, pw, outs, rows=rows, tm=tm,
                                                                        cond_row=cond_row, seq=seq, rope=rope)
        kproj = (outs, rope, sblk)
    return pl.pallas_call(
        functools.partial(_ffn_kernel, n_side=len(side), proj=kproj, sub=sub),
        grid=(rows // tm,),
        in_specs=[
            pl.BlockSpec((tm, D_MODEL), lambda i: (i, 0)),
            pl.BlockSpec((None, N_MOD, D_MODEL), lambda i: (cond_row(i), 0, 0)),
            pl.BlockSpec((4, D_MODEL), lambda i: (0, 0)),
            _resident((D_MODEL, 2 * FFN_HIDDEN)),
            _resident((FFN_HIDDEN, D_MODEL)),
        ] + proj_in + side_in,
        out_specs=[pl.BlockSpec((tm, D_MODEL), lambda i: (i, 0))] + proj_out + side_out,
        out_shape=[jax.ShapeDtypeStruct((rows, D_MODEL), F32)] + proj_shape + side_shape,
        compiler_params=_params("arbitrary"),
        name="ffn",
    )(x, m, norm_g, w_gu, w_down, *proj_args, *side)


LAM_INIT_0 = 0.8 - 0.6 * math.exp(-0.3 * 0)
CTX_SEQS_PER_STEP = 4


def _trunk(x, m, caches, weights, *, batch, seq, cond_of_batch):
    (norm_g0, w_in, conv_w, conv_b, ln_g, ln_b, lam, subln_g, w_out0, w_gu0, w_down0,
     norm_g1, w_qkv, sink, w_out1, w_gu1, w_down1) = weights
    m0, m1 = m
    ctx = caches is None
    rope = not ctx
    tm = 1024 if ctx else 512
    tm_ffn = 512
    tm_ffn1 = 1024 if ctx else 512
    ffn_sub = 256
    tq = 256
    row_of = lambda t: (lambda i: cond_of_batch((i * t) // seq))

    o0 = 2 * CONV_CH
    ko, vo = o0 + DIFF_WIDTH, o0 + 2 * DIFF_WIDTH
    outs0 = [("glu", 0, o0, F32, False, None), ("nat", o0, DIFF_WIDTH, BF16, True, Q_SCALE),
             ("nat", ko, DIFF_WIDTH, BF16, True, None), ("T", vo, DIFF_WIDTH, BF16, False, None)]
    if ctx:
        outs0 += [("T", ko, DIFF_WIDTH, F32, False, None), ("heads", vo, DIFF_WIDTH, F32, False, None)]
    res = _pre(x, m0, norm_g0, w_in, tuple(outs0), tm=tm, cond_row=row_of(tm), seq=seq, rope=rope)
    u, q, k, vt = res[:4]
    new0 = tuple(res[4:])
    x, *conv0 = _mix0(u, q, k, vt, None if ctx else caches[0], x, m0, norm_g0, conv_w, conv_b,
                      ln_g, ln_b, lam, subln_g, w_out0, batch=batch, seq=seq, tq=tq, cond_row=cond_of_batch,
                      lam_init=LAM_INIT_0, side=(w_gu0, w_down0, w_qkv) if ctx else (),
                      nseq=CTX_SEQS_PER_STEP if ctx else 1)
    if ctx:
        w_gu0, w_down0, w_qkv = conv0

    outs1 = [("nat", 0, Q_WIDTH, BF16, True, Q_SCALE), ("nat", Q_WIDTH, KV_WIDTH, BF16, True, None)]
    if ctx:
        outs1 += [("T", Q_WIDTH + KV_WIDTH, KV_WIDTH, F32, False, None), ("T", Q_WIDTH, KV_WIDTH, F32, False, None)]
    else:
        outs1 += [("T", Q_WIDTH + KV_WIDTH, KV_WIDTH, BF16, False, None)]
    res = _ffn(x, m0, norm_g0, w_gu0, w_down0, tm=tm_ffn, sub=ffn_sub, cond_row=row_of(tm_ffn),
               side=(w_out1, w_gu1, w_down1) if ctx else (),
               proj=(m1, norm_g1, w_qkv, tuple(outs1), seq, rope))
    x, q, k, vt = res[:4]
    new1 = (res[4], vt) if ctx else ()
    if ctx:
        w_out1, w_gu1, w_down1 = res[5:]
    x = _mix1(q, k, vt, None if ctx else caches[1], x, m1, norm_g1, sink, w_out1,
              batch=batch, seq=seq, tq=tq, cond_row=cond_of_batch, nseq=CTX_SEQS_PER_STEP if ctx else 1)
    x, = _ffn(x, m1, norm_g1, w_gu1, w_down1, tm=tm_ffn1, sub=ffn_sub, cond_row=row_of(tm_ffn1))
    return x, new0 + new1, (w_gu0, w_down0, w_qkv, w_out1, w_gu1, w_down1)


def kernel(x_prompt, x_sample, cache_k0, cache_v0, cache_k1, cache_v1, c, c_ctx, l0_mod_w, l0_mod_b, l0_norm_g, l0_w_in, l0_conv_w, l0_conv_b, l0_conv_ln_g, l0_conv_ln_b, l0_lambda, l0_subln_g, l0_w_out, l0_w_gu, l0_w_down, l1_mod_w, l1_mod_b, l1_norm_g, l1_w_qkv, l1_sink, l1_w_out, l1_w_gu, l1_w_down):
    batch, seq, d = x_prompt.shape
    dec_batch, dec_seq, _ = x_sample.shape
    n_past = cache_k0.shape[1]
    assert d == D_MODEL and seq == 256 and batch % CTX_SEQS_PER_STEP == 0 and (batch * seq) % 1024 == 0
    assert dec_seq % 512 == 0 and dec_seq % GRID_W == 0 and n_past % LANES == 0 and 1 + dec_batch <= MOD_ROWS

    cond = jnp.concatenate([c_ctx[None, :], c, jnp.zeros((MOD_ROWS - 1 - dec_batch, d), F32)], axis=0)
    m = (_modulation(cond, l0_mod_w, l0_mod_b), _modulation(cond, l1_mod_w, l1_mod_b))

    row = lambda v: v.reshape(1, -1)
    layer0 = (l0_norm_g, l0_w_in.astype(BF16), jnp.broadcast_to(l0_conv_w, (CONV_WIDTH, 8, CONV_CH)), row(l0_conv_b),
              row(l0_conv_ln_g), row(l0_conv_ln_b), l0_lambda, l0_subln_g.reshape(LANES, 1),
              l0_w_out.astype(BF16))
    weights = layer0 + (l0_w_gu, l0_w_down, l1_norm_g, l1_w_qkv, l1_sink, l1_w_out, l1_w_gu, l1_w_down)

    y_prompt, (kt0, v0, kt1, vt1), bf16_w = _trunk(x_prompt.reshape(batch * seq, d), m, None, weights,
                                                   batch=batch, seq=seq, cond_of_batch=lambda b: 0)
    w_gu0, w_down0, w_qkv, w_out1, w_gu1, w_down1 = bf16_w
    weights = layer0 + (w_gu0, w_down0, l1_norm_g, w_qkv, l1_sink, w_out1, w_gu1, w_down1)
    caches = ((cache_k0.reshape(dec_batch, n_past, -1),
               jnp.swapaxes(cache_v0.reshape(dec_batch, n_past, -1), 1, 2)),
              (cache_k1.reshape(dec_batch, n_past, -1),
               jnp.swapaxes(cache_v1.reshape(dec_batch, n_past, -1), 1, 2)))
    y_sample, _, _ = _trunk(x_sample.reshape(dec_batch * dec_seq, d), m, caches, weights,
                            batch=dec_batch, seq=dec_seq, cond_of_batch=lambda b: 1 + b)
    new_k0 = kt0.reshape(batch, DIFF_HEADS, 2, HEAD_DIM, seq).transpose(0, 4, 1, 2, 3)
    new_v0 = v0.reshape(batch, seq, DIFF_HEADS, 2 * HEAD_DIM)
    new_k1 = kt1.reshape(batch, GQA_KV_HEADS, HEAD_DIM, seq).transpose(0, 3, 1, 2)
    new_v1 = vt1.reshape(batch, GQA_KV_HEADS, HEAD_DIM, seq).transpose(0, 3, 1, 2)
    return (y_prompt.reshape(batch, seq, d), y_sample.reshape(dec_batch, dec_seq, d),
            new_k0, new_v0, new_k1, new_v1)
```

```python
import functools
import math

import numpy as np
import jax
import jax.numpy as jnp
from jax import lax
from jax.experimental import pallas as pl
from jax.experimental.pallas import tpu as pltpu

D_MODEL = 1024
GRID_W = 64
ROPE_THETA = 10000.0
EPS = 1e-6
NEG_INF = -1e30
LOG2E = math.log2(math.e)
CONV_CH = 512
CONV_WIDTH = 31
CONV_PAD = 16
CONV_ROWS = 64
DIFF_HEADS = 4
DIFF_WIDTH = 512
HEAD_DIM = 64
GQA_HEADS = 16
GQA_KV_HEADS = 4
GQA_GROUP = 4
Q_WIDTH = GQA_HEADS * HEAD_DIM
KV_WIDTH = GQA_KV_HEADS * HEAD_DIM
Q_SCALE = LOG2E * HEAD_DIM ** -0.5
WINDOW = 128
FFN_HIDDEN = 2816
N_MOD = 6
LANES = 128
BF16_ROWS = 16
MOD_ROWS = 8
VMEM_LIMIT = 56 * 1024 * 1024

BF16 = jnp.bfloat16
F32 = jnp.float32


def _gated(a, g):
    return (0.5 * a) * (jnp.tanh(0.5 * g) + 1.0)


def _silu(x):
    h = 0.5 * x
    return h * (jnp.tanh(h) + 1.0)


def _rms(x, g):
    return x * lax.rsqrt(jnp.mean(x * x, axis=-1, keepdims=True) + EPS) * g


def _dot(a, b):
    return jnp.dot(a, b, preferred_element_type=F32)


def _dot_nt(a, b):
    return lax.dot_general(a, b, (((1,), (1,)), ((), ())), preferred_element_type=F32)


def _params(*sem):
    return pltpu.CompilerParams(dimension_semantics=sem, vmem_limit_bytes=VMEM_LIMIT)


def _resident(shape):
    return pl.BlockSpec(shape, lambda *_: (0,) * len(shape), pipeline_mode=pl.Buffered(1))


def _mod_kernel(c_ref, w_ref, b_ref, o_ref):
    s = _silu(c_ref[...]).astype(BF16)
    o_ref[...] = _dot(s, w_ref[...].astype(BF16)) + b_ref[...]


def _modulation(cond, mod_w, mod_b):
    n = mod_w.shape[1]
    tn = 1536
    out = pl.pallas_call(
        _mod_kernel,
        grid=(n // tn,),
        in_specs=[
            pl.BlockSpec((MOD_ROWS, D_MODEL), lambda j: (0, 0)),
            pl.BlockSpec((D_MODEL, tn), lambda j: (0, j)),
            pl.BlockSpec((1, tn), lambda j: (0, j)),
        ],
        out_specs=pl.BlockSpec((MOD_ROWS, tn), lambda j: (0, j)),
        out_shape=jax.ShapeDtypeStruct((MOD_ROWS, n), F32),
        compiler_params=_params("arbitrary"),
        name="modulation",
    )(cond, mod_w, mod_b.reshape(1, n))
    return out.reshape(MOD_ROWS, N_MOD, D_MODEL)


def _rope_tables(seq):
    t = np.arange(seq)
    rows, cols = t // GRID_W, t % GRID_W
    half = HEAD_DIM // 2
    inv = 1.0 / (ROPE_THETA ** (np.arange(0, half, 2, dtype=np.float64) / half))
    ar = rows[:, None] * inv[None, :]
    ac = cols[:, None] * inv[None, :]
    cos = np.concatenate([np.cos(ar), np.cos(ar), np.cos(ac), np.cos(ac)], axis=1)
    sin = np.concatenate([-np.sin(ar), np.sin(ar), -np.sin(ac), np.sin(ac)], axis=1)
    reps = LANES // HEAD_DIM
    return (jnp.asarray(np.tile(cos, (1, reps)), F32), jnp.asarray(np.tile(sin, (1, reps)), F32))


def _rope(x, cos, sin):
    quarter = HEAD_DIM // 4
    lane = lax.broadcasted_iota(jnp.int32, x.shape, 1)
    lo = (lane % (2 * quarter)) < quarter
    partner = jnp.where(lo, pltpu.roll(x, LANES - quarter, 1), pltpu.roll(x, quarter, 1))
    return x * cos + partner * sin


def _pre_kernel(*refs, outs, rope, sblk):
    n_in = 6 if rope else 4
    x_ref, m_ref, g_ref, w_ref = refs[:4]
    _project(x_ref[...], m_ref, g_ref, w_ref, refs[4:6] if rope else None, outs, refs[n_in:], sblk)


def _project(x, m_ref, g_ref, w_ref, trig, outs, o_refs, sblk):
    rope = trig is not None
    tm = x.shape[0]
    h = (_rms(x, g_ref[0:1, :] * (1.0 + m_ref[1:2, :])) + m_ref[0:1, :]).astype(BF16)
    cost = lambda o: o[2] * {"T": 5.0, "heads": 0.5, "glu": 1.2}.get(o[0], 3.0 if (rope and o[4]) else 1.0)
    ranges = sorted({(o[1], o[2]) for o in outs},
                    key=lambda r: (-sum(cost(o) for o in outs if (o[1], o[2]) == r), r[0]))
    for start, width in ranges:
        p = _dot(h, w_ref[:, start:start + width])
        for (kind, s0, w0, _, roped, scale), o_ref in zip(outs, o_refs):
            if (s0, w0) != (start, width):
                continue
            if kind == "nat":
                for c in range(width // LANES):
                    xc = p[:, c * LANES:(c + 1) * LANES]
                    if rope and roped:
                        xc = _rope(xc, trig[0][...], trig[1][...])
                    if scale is not None:
                        xc = xc * scale
                    o_ref[:, c * LANES:(c + 1) * LANES] = xc.astype(o_ref.dtype)
            elif kind == "T":
                for s in range(tm // sblk):
                    o_ref[s] = p[s * sblk:(s + 1) * sblk, :].T.astype(o_ref.dtype)
            elif kind == "glu":
                o_ref[...] = _gated(p[:, 0:width // 2], p[:, width // 2:width])
            else:
                heads = width // LANES
                for hd in range(heads):
                    o_ref[pl.ds(hd, tm, stride=heads), :] = p[:, hd * LANES:(hd + 1) * LANES]


def _project_specs(m, norm_g, w, outs, *, rows, tm, cond_row, seq, rope):
    n = w.shape[1]
    sblk = min(seq, tm)
    per_seq = seq // sblk
    in_specs = [
        pl.BlockSpec((None, N_MOD, D_MODEL), lambda i: (cond_row(i), 0, 0)),
        pl.BlockSpec((4, D_MODEL), lambda i: (0, 0)),
        _resident((D_MODEL, n)),
    ]
    args = [m, norm_g, w]
    if rope:
        cos, sin = _rope_tables(seq)
        nblk = seq // tm
        in_specs += [pl.BlockSpec((tm, LANES), lambda i: (i % nblk, 0))] * 2
        args += [cos, sin]
    out_specs, out_shape = [], []
    for (kind, _, wd, dt, _, _) in outs:
        if kind == "nat":
            out_specs.append(pl.BlockSpec((tm, wd), lambda i: (i, 0)))
            out_shape.append(jax.ShapeDtypeStruct((rows, wd), dt))
        elif kind == "T":
            out_specs.append(pl.BlockSpec((tm // sblk, wd, sblk), lambda i: (i // per_seq, 0, i % per_seq)))
            out_shape.append(jax.ShapeDtypeStruct((rows // seq, wd, seq), dt))
        elif kind == "glu":
            out_specs.append(pl.BlockSpec((tm, wd // 2), lambda i: (i, 0)))
            out_shape.append(jax.ShapeDtypeStruct((rows, wd // 2), dt))
        else:
            out_specs.append(pl.BlockSpec((tm * (wd // LANES), LANES), lambda i: (i, 0)))
            out_shape.append(jax.ShapeDtypeStruct((rows * (wd // LANES), LANES), dt))
    return in_specs, args, out_specs, out_shape, sblk


def _pre(x, m, norm_g, w, outs, *, tm, cond_row, seq, rope):
    rows = x.shape[0]
    in_specs = [pl.BlockSpec((tm, D_MODEL), lambda i: (i, 0))]
    proj_in, proj_args, out_specs, out_shape, sblk = _project_specs(m, norm_g, w, outs, rows=rows, tm=tm,
                                                                    cond_row=cond_row, seq=seq, rope=rope)
    in_specs += proj_in
    args = [x] + proj_args
    return pl.pallas_call(
        functools.partial(_pre_kernel, outs=outs, rope=rope, sblk=sblk),
        grid=(rows // tm,),
        in_specs=in_specs,
        out_specs=out_specs,
        out_shape=out_shape,
        compiler_params=_params("arbitrary"),
        name="pre_rope" if rope else "pre",
    )(*args)


def _attend(s_list, vt_list, floor=None):
    mx = None
    for s in s_list:
        smax = jnp.max(s, axis=0, keepdims=True)
        mx = smax if mx is None else jnp.maximum(mx, smax)
    if floor is not None:
        mx = jnp.maximum(mx, floor)
    d = vt_list[0].shape[0]
    ov = None
    for s, vt in zip(s_list, vt_list):
        ones = jnp.ones((BF16_ROWS, vt.shape[1]), BF16)
        pv = _dot(jnp.concatenate([vt, ones], axis=0), jnp.exp2(s - mx).astype(BF16))
        ov = pv if ov is None else ov + pv
    return ov[0:d, :], ov[d:d + 1, :], mx


def _pipelined(n, first, second):
    out = []
    nxt = first(0)
    for u in range(n):
        cur = nxt
        if u + 1 < n:
            nxt = first(u + 1)
        out.append(second(u, cur))
    return out


def _side_convert(src_refs, dst_refs):
    for src, dst in zip(src_refs, dst_refs):
        dst[...] = src[...].astype(BF16)


def _mix0_kernel(*refs, seq, tq, n_cache, lam_init, n_side, nseq):
    n_in = 16 if n_cache else 14
    z_ref, q_ref, k_ref, vt_ref = refs[:4]
    ck_ref, cvt_ref = refs[4:6] if n_cache else (None, None)
    x_ref, m_ref, ng_ref, cw_ref, cb_ref, lg_ref, lb_ref, lam_ref, sg_ref, wo_ref = refs[n_in - 10:n_in]
    o_ref = refs[n_in + n_side]
    z_s, zw_s, zsh_s = refs[n_in + 2 * n_side + 1:]
    _side_convert(refs[n_in:n_in + n_side], refs[n_in + n_side + 1:n_in + 2 * n_side + 1])
    qi = pl.program_id(1)
    whole = tq == seq
    win = tq + 2 * CONV_PAD
    span = win - 8
    chunk = CONV_ROWS
    off0 = CONV_PAD - CONV_WIDTH // 2
    la = jnp.sum(lam_ref[0:1, :] * lam_ref[1:2, :], axis=-1, keepdims=True)
    lb = jnp.sum(lam_ref[2:3, :] * lam_ref[3:4, :], axis=-1, keepdims=True)
    lam = jnp.exp(la) - jnp.exp(lb) + lam_init
    sub = LANES // 2
    lane = lax.broadcasted_iota(jnp.int32, (tq, LANES), 1)

    def block(s):
        urows = slice(s * seq, (s + 1) * seq)

        def pad_sequence():
            z_s[0:CONV_PAD, :] = jnp.zeros((CONV_PAD, CONV_CH), F32)
            z_s[CONV_PAD + seq:2 * CONV_PAD + seq, :] = jnp.zeros((CONV_PAD, CONV_CH), F32)
            z_s[CONV_PAD:CONV_PAD + seq, :] = z_ref[urows, :]

        if whole:
            pad_sequence()
            zw = z_s
        else:
            pl.when(qi == 0)(pad_sequence)
            zw_s[...] = z_s[pl.ds(pl.multiple_of(qi * tq, tq), win), :]
            zw = zw_s

        for b in range(1, 8):
            zsh_s[b - 1, 0:span, :] = zw[b:b + span, :]
        conv_rows = []
        for c0 in range(0, tq, chunk):
            acc = jnp.zeros((chunk // 8, 8, CONV_CH), F32)
            for j in range(CONV_WIDTH):
                off = j + off0
                a, b = off // 8, off % 8
                if b == 0:
                    zz = zw[c0 + 8 * a:c0 + 8 * a + chunk, :]
                else:
                    zz = zsh_s[b - 1, c0 + 8 * a:c0 + 8 * a + chunk, :]
                acc = acc + zz.reshape(chunk // 8, 8, CONV_CH) * cw_ref[j]
            acc = acc.reshape(chunk, CONV_CH) + cb_ref[...]
            mu = jnp.mean(acc, axis=-1, keepdims=True)
            xc = acc - mu
            y = xc * lax.rsqrt(jnp.mean(xc * xc, axis=-1, keepdims=True) + EPS) * lg_ref[...] + lb_ref[...]
            conv_rows.append(_silu(y).astype(BF16))

        def segments(h):
            cols = slice(h * LANES, (h + 1) * LANES)
            segs = [(k_ref[urows, cols], vt_ref[s, cols, :])]
            if n_cache:
                segs.append((ck_ref[:, cols].astype(BF16), cvt_ref[cols, :].astype(BF16)))
            return segs

        def scores(h):
            qc = q_ref[s * tq:(s + 1) * tq, h * LANES:(h + 1) * LANES]
            zero = jnp.zeros_like(qc)
            qq = jnp.concatenate([jnp.where(lane < sub, qc, zero), jnp.where(lane >= sub, qc, zero)], axis=0)
            return [_dot_nt(kk, qq) for kk, _ in segments(h)]

        def head(h, s_list):
            ov, den, _ = _attend(s_list, [vt for _, vt in segments(h)])
            r = 1.0 / den
            o = ov[:, 0:tq] * r[:, 0:tq] - ov[:, tq:2 * tq] * (lam * r[:, tq:2 * tq])
            return o * lax.rsqrt(jnp.mean(o * o, axis=0, keepdims=True) + EPS) * sg_ref[...] * (1.0 - lam_init)

        heads = _pipelined(DIFF_HEADS, scores, head)
        attn = jnp.concatenate(heads, axis=0).T.astype(BF16)
        cat = jnp.concatenate([jnp.concatenate(conv_rows, axis=0), attn], axis=1)
        rows = slice(s * tq, (s + 1) * tq)
        mixed = _dot(cat, wo_ref[...])
        o_ref[rows, :] = x_ref[rows, :] + _rms(mixed, m_ref[2:3, :] * ng_ref[1:2, :])

    for s in range(nseq):
        block(s)


def _side_specs(side, steps, index):
    in_specs, out_specs, out_shape = [], [], []
    for w in side:
        rows, cols = w.shape
        n_blk = math.gcd(steps, rows // BF16_ROWS)
        per = steps // n_blk
        spec = pl.BlockSpec((rows // n_blk, cols), lambda *ids, per=per: (index(*ids) // per, 0))
        in_specs.append(spec)
        out_specs.append(spec)
        out_shape.append(jax.ShapeDtypeStruct(w.shape, BF16))
    return in_specs, out_specs, out_shape


def _mix0(u, q, k, vt, cache, x, m, norm_g, conv_w, conv_b, ln_g, ln_b, lam, subln_g, w_out,
          *, batch, seq, tq, cond_row, lam_init, side=(), nseq=1):
    nq = seq // tq
    n_cache = 0 if cache is None else cache[0].shape[1]
    win = tq + 2 * CONV_PAD
    assert tq % CONV_ROWS == 0
    steps = (batch // nseq) * nq
    const = lambda shape: pl.BlockSpec(shape, lambda b, i: (0,) * len(shape))
    in_specs = [pl.BlockSpec((nseq * seq, CONV_CH), lambda b, i: (b, 0)),
                pl.BlockSpec((nseq * tq, DIFF_WIDTH), lambda b, i: (b * nq + i, 0)),
                pl.BlockSpec((nseq * seq, DIFF_WIDTH), lambda b, i: (b, 0)),
                pl.BlockSpec((nseq, DIFF_WIDTH, seq), lambda b, i: (b, 0, 0))]
    args = [u, q, k, vt]
    if n_cache:
        in_specs += [pl.BlockSpec((None, n_cache, DIFF_WIDTH), lambda b, i: (b, 0, 0)),
                     pl.BlockSpec((None, DIFF_WIDTH, n_cache), lambda b, i: (b, 0, 0))]
        args += [cache[0], cache[1]]
    in_specs += [
        pl.BlockSpec((nseq * tq, D_MODEL), lambda b, i: (b * nq + i, 0)),
        pl.BlockSpec((None, N_MOD, D_MODEL), lambda b, i: (cond_row(b * nseq), 0, 0)),
        const((4, D_MODEL)), const((CONV_WIDTH, 8, CONV_CH)), const((1, CONV_CH)), const((1, CONV_CH)),
        const((1, CONV_CH)), const((4, LANES // 2)), const((LANES, 1)), _resident((D_MODEL, D_MODEL)),
    ]
    args += [x, m, norm_g, conv_w, conv_b, ln_g, ln_b, lam, subln_g, w_out]
    side_in, side_out, side_shape = _side_specs(side, steps, lambda b, i: b * nq + i)
    return pl.pallas_call(
        functools.partial(_mix0_kernel, seq=seq, tq=tq, n_cache=n_cache, lam_init=lam_init, n_side=len(side),
                          nseq=nseq),
        grid=(batch // nseq, nq),
        in_specs=in_specs + side_in,
        out_specs=[pl.BlockSpec((nseq * tq, D_MODEL), lambda b, i: (b * nq + i, 0))] + side_out,
        out_shape=[jax.ShapeDtypeStruct((batch * seq, D_MODEL), F32)] + side_shape,
        scratch_shapes=[
            pltpu.VMEM((seq + 2 * CONV_PAD, CONV_CH), F32), pltpu.VMEM((win, CONV_CH), F32),
            pltpu.VMEM((7, win - 8, CONV_CH), F32),
        ],
        compiler_params=_params("arbitrary", "arbitrary"),
        name="mix_conv_diff",
    )(*args, *side)


def _mix1_kernel(*refs, seq, tq, n_cache, nseq):
    if n_cache:
        (q_ref, kp_ref, kc_ref, kn_ref, vtp_ref, vtc_ref, vtn_ref, ck_ref, cvt_ref,
         x_ref, m_ref, ng_ref, sink_ref, wo_ref, o_ref) = refs
    else:
        (q_ref, kc_ref, vtc_ref, x_ref, m_ref, ng_ref, sink_ref, wo_ref, o_ref) = refs
    qi = pl.program_id(1)
    lane = lax.broadcasted_iota(jnp.int32, (tq, LANES), 1)

    def band(base, rows):
        kpos = base + lax.broadcasted_iota(jnp.int32, (rows, tq), 0)
        qpos = qi * tq + lax.broadcasted_iota(jnp.int32, (rows, tq), 1)
        ok = (jnp.abs(qpos - kpos) <= WINDOW) & (kpos >= 0) & (kpos < seq)
        return jnp.concatenate([ok] * GQA_GROUP, axis=1)

    def attention(blocks):
        units = [(r0, segs, g) for r0, segs in blocks for g in range(GQA_KV_HEADS)]

        def scores(u):
            r0, segs, g = units[u]
            b = g % 2
            kcols = slice((g // 2) * LANES, (g // 2 + 1) * LANES)
            keep = (lane < HEAD_DIM) if b == 0 else (lane >= HEAD_DIM)
            qs = []
            for h in range(g * GQA_GROUP, (g + 1) * GQA_GROUP):
                qc = q_ref[r0:r0 + tq, (h // 2) * LANES:(h // 2 + 1) * LANES]
                qa = qc if h % 2 == b else pltpu.roll(qc, HEAD_DIM, 1)
                qs.append(jnp.where(keep, qa, jnp.zeros_like(qa)))
            qg = jnp.concatenate(qs, axis=0)
            s_list = []
            for k_ref, _, ok in segs:
                s = _dot_nt(k_ref[:, kcols].astype(BF16), qg)
                s_list.append(s if ok is None else jnp.where(ok, s, NEG_INF))
            return s_list

        def group(u, s_list):
            _, segs, g = units[u]
            vrows = slice(g * HEAD_DIM, (g + 1) * HEAD_DIM)
            sink = jnp.concatenate([jnp.full((1, tq), sink_ref[h] * LOG2E, F32)
                                    for h in range(g * GQA_GROUP, (g + 1) * GQA_GROUP)], axis=1)
            ov, den, mx = _attend(s_list, [vt_ref[vrows, :].astype(BF16) for _, vt_ref, _ in segs], floor=sink)
            ov = ov * (1.0 / (den + jnp.exp2(sink - mx)))
            return jnp.concatenate([ov[:, hh * tq:(hh + 1) * tq] for hh in range(GQA_GROUP)], axis=0)

        nxt = scores(0)
        groups = []
        for u in range(len(units)):
            cur = nxt
            if u + 1 < len(units):
                nxt = scores(u + 1)
            groups.append(group(u, cur))
            if len(groups) == GQA_KV_HEADS:
                rows = slice(units[u][0], units[u][0] + tq)
                mixed = _dot(jnp.concatenate(groups, axis=0).T.astype(BF16), wo_ref[...])
                o_ref[rows, :] = x_ref[rows, :] + _rms(mixed, m_ref[2:3, :] * ng_ref[1:2, :])
                groups = []

    if n_cache:
        blocks = [(0, [(kp_ref, vtp_ref, band(qi * tq - WINDOW, WINDOW)),
                       (kc_ref, vtc_ref, band(qi * tq, tq)),
                       (kn_ref, vtn_ref, band(qi * tq + tq, WINDOW)),
                       (ck_ref, cvt_ref, None)])]
    else:
        blocks = [(s * tq, [(kc_ref.at[pl.ds(s * tq, tq)], vtc_ref.at[s], None)]) for s in range(nseq)]
    attention(blocks)


def _mix1(q, k, vt, cache, x, m, norm_g, sink, w_out, *, batch, seq, tq, cond_row, nseq=1):
    nq = seq // tq
    n_cache = 0 if cache is None else cache[0].shape[1]
    rows = nseq * tq
    q_spec = pl.BlockSpec((rows, D_MODEL), lambda b, i: (b * nq + i, 0))
    kc_spec = pl.BlockSpec((rows, KV_WIDTH), lambda b, i: (b * nq + i, 0))
    vtc_spec = pl.BlockSpec((None, KV_WIDTH, tq), lambda b, i: (b, 0, i))
    if n_cache:
        nw = seq // WINDOW
        per = tq // WINDOW
        prev = lambda i: jnp.maximum(i * per - 1, 0)
        nxt = lambda i: jnp.minimum(i * per + per, nw - 1)
        in_specs = [q_spec,
                    pl.BlockSpec((WINDOW, KV_WIDTH), lambda b, i: (b * nw + prev(i), 0)), kc_spec,
                    pl.BlockSpec((WINDOW, KV_WIDTH), lambda b, i: (b * nw + nxt(i), 0)),
                    pl.BlockSpec((None, KV_WIDTH, WINDOW), lambda b, i: (b, 0, prev(i))), vtc_spec,
                    pl.BlockSpec((None, KV_WIDTH, WINDOW), lambda b, i: (b, 0, nxt(i))),
                    pl.BlockSpec((None, n_cache, KV_WIDTH), lambda b, i: (b, 0, 0)),
                    pl.BlockSpec((None, KV_WIDTH, n_cache), lambda b, i: (b, 0, 0))]
        args = [q, k, k, k, vt, vt, vt, cache[0], cache[1]]
    else:
        in_specs = [q_spec, kc_spec, pl.BlockSpec((nseq, KV_WIDTH, tq), lambda b, i: (b, 0, 0))]
        args = [q, k, vt]
    in_specs += [
        pl.BlockSpec((rows, D_MODEL), lambda b, i: (b * nq + i, 0)),
        pl.BlockSpec((None, N_MOD, D_MODEL), lambda b, i: (cond_row(b * nseq), 0, 0)),
        pl.BlockSpec((4, D_MODEL), lambda b, i: (0, 0)),
        pl.BlockSpec(memory_space=pltpu.SMEM),
        pl.BlockSpec((D_MODEL, D_MODEL), lambda b, i: (0, 0)),
    ]
    args += [x, m, norm_g, sink, w_out]
    return pl.pallas_call(
        functools.partial(_mix1_kernel, seq=seq, tq=tq, n_cache=n_cache, nseq=nseq),
        grid=(batch // nseq, nq),
        in_specs=in_specs,
        out_specs=pl.BlockSpec((rows, D_MODEL), lambda b, i: (b * nq + i, 0)),
        out_shape=jax.ShapeDtypeStruct((batch * seq, D_MODEL), F32),
        compiler_params=_params("arbitrary", "arbitrary"),
        name="mix_gqa",
    )(*args)


FFN_CHUNKS = (1536, 1280)


def _ffn_kernel(*refs, n_side, proj, sub):
    x_ref, m_ref, ng_ref, wgu_ref, wd_ref = refs[:5]
    n_pin = 0 if proj is None else (5 if proj[1] else 3)
    n_pout = 0 if proj is None else len(proj[0])
    n_in = 5 + n_pin + n_side
    o_ref = refs[n_in]
    _side_convert(refs[5 + n_pin:n_in], refs[n_in + 1 + n_pout:])
    for r0 in range(0, x_ref.shape[0], sub):
        rows = slice(r0, r0 + sub)
        h = (_rms(x_ref[rows, :], ng_ref[2:3, :] * (1.0 + m_ref[4:5, :])) + m_ref[3:4, :]).astype(BF16)
        acc = jnp.zeros(h.shape, F32)
        c0 = 0
        for width in FFN_CHUNKS:
            gate = _dot(h, wgu_ref[:, c0:c0 + width])
            up = _dot(h, wgu_ref[:, FFN_HIDDEN + c0:FFN_HIDDEN + c0 + width])
            act = (_silu(gate) * up).astype(BF16)
            acc = acc + _dot(act, wd_ref[c0:c0 + width, :])
            c0 += width
        o_ref[rows, :] = x_ref[rows, :] + _rms(acc, m_ref[5:6, :] * ng_ref[3:4, :])
    if proj is not None:
        outs, rope, sblk = proj
        pm_ref, pg_ref, pw_ref = refs[5:8]
        _project(o_ref[...], pm_ref, pg_ref, pw_ref, refs[8:10] if rope else None, outs,
                 refs[n_in + 1:n_in + 1 + n_pout], sblk)


def _ffn(x, m, norm_g, w_gu, w_down, *, tm, sub, cond_row, side=(), proj=None):
    rows = x.shape[0]
    side_in, side_out, side_shape = _side_specs(side, rows // tm, lambda i: i)
    proj_in, proj_args, proj_out, proj_shape, kproj = [], [], [], [], None
    if proj is not None:
        pm, pg, pw, outs, seq, rope = proj
        proj_in, proj_args, proj_out, proj_shape, sblk = _project_specs(pm, pg, pw, outs, rows=rows, tm=tm,
                                                                        cond_row=cond_row, seq=seq, rope=rope)
        kproj = (outs, rope, sblk)
    return pl.pallas_call(
        functools.partial(_ffn_kernel, n_side=len(side), proj=kproj, sub=sub),
        grid=(rows // tm,),
        in_specs=[
            pl.BlockSpec((tm, D_MODEL), lambda i: (i, 0)),
            pl.BlockSpec((None, N_MOD, D_MODEL), lambda i: (cond_row(i), 0, 0)),
            pl.BlockSpec((4, D_MODEL), lambda i: (0, 0)),
            _resident((D_MODEL, 2 * FFN_HIDDEN)),
            _resident((FFN_HIDDEN, D_MODEL)),
        ] + proj_in + side_in,
        out_specs=[pl.BlockSpec((tm, D_MODEL), lambda i: (i, 0))] + proj_out + side_out,
        out_shape=[jax.ShapeDtypeStruct((rows, D_MODEL), F32)] + proj_shape + side_shape,
        compiler_params=_params("arbitrary"),
        name="ffn",
    )(x, m, norm_g, w_gu, w_down, *proj_args, *side)


LAM_INIT_0 = 0.8 - 0.6 * math.exp(-0.3 * 0)
CTX_SEQS_PER_STEP = 4


def _trunk(x, m, caches, weights, *, batch, seq, cond_of_batch):
    (norm_g0, w_in, conv_w, conv_b, ln_g, ln_b, lam, subln_g, w_out0, w_gu0, w_down0,
     norm_g1, w_qkv, sink, w_out1, w_gu1, w_down1) = weights
    m0, m1 = m
    ctx = caches is None
    rope = not ctx
    tm = 1024 if ctx else 512
    tm_ffn = 512
    tm_ffn1 = 1024 if ctx else 512
    ffn_sub = 256
    tq = 256
    row_of = lambda t: (lambda i: cond_of_batch((i * t) // seq))

    o0 = 2 * CONV_CH
    ko, vo = o0 + DIFF_WIDTH, o0 + 2 * DIFF_WIDTH
    outs0 = [("glu", 0, o0, F32, False, None), ("nat", o0, DIFF_WIDTH, BF16, True, Q_SCALE),
             ("nat", ko, DIFF_WIDTH, BF16, True, None), ("T", vo, DIFF_WIDTH, BF16, False, None)]
    if ctx:
        outs0 += [("T", ko, DIFF_WIDTH, F32, False, None), ("heads", vo, DIFF_WIDTH, F32, False, None)]
    res = _pre(x, m0, norm_g0, w_in, tuple(outs0), tm=tm, cond_row=row_of(tm), seq=seq, rope=rope)
    u, q, k, vt = res[:4]
    new0 = tuple(res[4:])
    x, *conv0 = _mix0(u, q, k, vt, None if ctx else caches[0], x, m0, norm_g0, conv_w, conv_b,
                      ln_g, ln_b, lam, subln_g, w_out0, batch=batch, seq=seq, tq=tq, cond_row=cond_of_batch,
                      lam_init=LAM_INIT_0, side=(w_gu0, w_down0, w_qkv) if ctx else (),
                      nseq=CTX_SEQS_PER_STEP if ctx else 1)
    if ctx:
        w_gu0, w_down0, w_qkv = conv0

    outs1 = [("nat", 0, Q_WIDTH, BF16, True, Q_SCALE), ("nat", Q_WIDTH, KV_WIDTH, BF16, True, None)]
    if ctx:
        outs1 += [("T", Q_WIDTH + KV_WIDTH, KV_WIDTH, F32, False, None), ("T", Q_WIDTH, KV_WIDTH, F32, False, None)]
    else:
        outs1 += [("T", Q_WIDTH + KV_WIDTH, KV_WIDTH, BF16, False, None)]
    res = _ffn(x, m0, norm_g0, w_gu0, w_down0, tm=tm_ffn, sub=ffn_sub, cond_row=row_of(tm_ffn),
               side=(w_out1, w_gu1, w_down1) if ctx else (),
               proj=(m1, norm_g1, w_qkv, tuple(outs1), seq, rope))
    x, q, k, vt = res[:4]
    new1 = (res[4], vt) if ctx else ()
    if ctx:
        w_out1, w_gu1, w_down1 = res[5:]
    x = _mix1(q, k, vt, None if ctx else caches[1], x, m1, norm_g1, sink, w_out1,
              batch=batch, seq=seq, tq=tq, cond_row=cond_of_batch, nseq=CTX_SEQS_PER_STEP if ctx else 1)
    x, = _ffn(x, m1, norm_g1, w_gu1, w_down1, tm=tm_ffn1, sub=ffn_sub, cond_row=row_of(tm_ffn1))
    return x, new0 + new1, (w_gu0, w_down0, w_qkv, w_out1, w_gu1, w_down1)


def kernel(x_prompt, x_sample, cache_k0, cache_v0, cache_k1, cache_v1, c, c_ctx, l0_mod_w, l0_mod_b, l0_norm_g, l0_w_in, l0_conv_w, l0_conv_b, l0_conv_ln_g, l0_conv_ln_b, l0_lambda, l0_subln_g, l0_w_out, l0_w_gu, l0_w_down, l1_mod_w, l1_mod_b, l1_norm_g, l1_w_qkv, l1_sink, l1_w_out, l1_w_gu, l1_w_down):
    batch, seq, d = x_prompt.shape
    dec_batch, dec_seq, _ = x_sample.shape
    n_past = cache_k0.shape[1]
    assert d == D_MODEL and seq == 256 and batch % CTX_SEQS_PER_STEP == 0 and (batch * seq) % 1024 == 0
    assert dec_seq % 512 == 0 and dec_seq % GRID_W == 0 and n_past % LANES == 0 and 1 + dec_batch <= MOD_ROWS

    cond = jnp.concatenate([c_ctx[None, :], c, jnp.zeros((MOD_ROWS - 1 - dec_batch, d), F32)], axis=0)
    m = (_modulation(cond, l0_mod_w, l0_mod_b), _modulation(cond, l1_mod_w, l1_mod_b))

    row = lambda v: v.reshape(1, -1)
    layer0 = (l0_norm_g, l0_w_in.astype(BF16), jnp.broadcast_to(l0_conv_w, (CONV_WIDTH, 8, CONV_CH)), row(l0_conv_b),
              row(l0_conv_ln_g), row(l0_conv_ln_b), l0_lambda, l0_subln_g.reshape(LANES, 1),
              l0_w_out.astype(BF16))
    weights = layer0 + (l0_w_gu, l0_w_down, l1_norm_g, l1_w_qkv, l1_sink, l1_w_out, l1_w_gu, l1_w_down)

    y_prompt, (kt0, v0, kt1, vt1), bf16_w = _trunk(x_prompt.reshape(batch * seq, d), m, None, weights,
                                                   batch=batch, seq=seq, cond_of_batch=lambda b: 0)
    w_gu0, w_down0, w_qkv, w_out1, w_gu1, w_down1 = bf16_w
    weights = layer0 + (w_gu0, w_down0, l1_norm_g, w_qkv, l1_sink, w_out1, w_gu1, w_down1)
    caches = ((cache_k0.reshape(dec_batch, n_past, -1),
               jnp.swapaxes(cache_v0.reshape(dec_batch, n_past, -1), 1, 2)),
              (cache_k1.reshape(dec_batch, n_past, -1),
               jnp.swapaxes(cache_v1.reshape(dec_batch, n_past, -1), 1, 2)))
    y_sample, _, _ = _trunk(x_sample.reshape(dec_batch * dec_seq, d), m, caches, weights,
                            batch=dec_batch, seq=dec_seq, cond_of_batch=lambda b: 1 + b)
    new_k0 = kt0.reshape(batch, DIFF_HEADS, 2, HEAD_DIM, seq).transpose(0, 4, 1, 2, 3)
    new_v0 = v0.reshape(batch, seq, DIFF_HEADS, 2 * HEAD_DIM)
    new_k1 = kt1.reshape(batch, GQA_KV_HEADS, HEAD_DIM, seq).transpose(0, 3, 1, 2)
    new_v1 = vt1.reshape(batch, GQA_KV_HEADS, HEAD_DIM, seq).transpose(0, 3, 1, 2)
    return (y_prompt.reshape(batch, seq, d), y_sample.reshape(dec_batch, dec_seq, d),
            new_k0, new_v0, new_k1, new_v1)
```

```python
import functools
import math

import numpy as np
import jax
import jax.numpy as jnp
from jax import lax
from jax.experimental import pallas as pl
from jax.experimental.pallas import tpu as pltpu

D_MODEL = 1024
GRID_W = 64
ROPE_THETA = 10000.0
EPS = 1e-6
NEG_INF = -1e30
LOG2E = math.log2(math.e)
CONV_CH = 512
CONV_WIDTH = 31
CONV_PAD = 16
CONV_ROWS = 64
DIFF_HEADS = 4
DIFF_WIDTH = 512
HEAD_DIM = 64
GQA_HEADS = 16
GQA_KV_HEADS = 4
GQA_GROUP = 4
Q_WIDTH = GQA_HEADS * HEAD_DIM
KV_WIDTH = GQA_KV_HEADS * HEAD_DIM
Q_SCALE = LOG2E * HEAD_DIM ** -0.5
WINDOW = 128
FFN_HIDDEN = 2816
N_MOD = 6
LANES = 128
BF16_ROWS = 16
MOD_ROWS = 8
VMEM_LIMIT = 56 * 1024 * 1024

BF16 = jnp.bfloat16
F32 = jnp.float32


def _gated(a, g):
    return (0.5 * a) * (jnp.tanh(0.5 * g) + 1.0)


def _silu(x):
    h = 0.5 * x
    return h * (jnp.tanh(h) + 1.0)


def _rms(x, g):
    return x * lax.rsqrt(jnp.mean(x * x, axis=-1, keepdims=True) + EPS) * g


def _dot(a, b):
    return jnp.dot(a, b, preferred_element_type=F32)


def _dot_nt(a, b):
    return lax.dot_general(a, b, (((1,), (1,)), ((), ())), preferred_element_type=F32)


def _params(*sem):
    return pltpu.CompilerParams(dimension_semantics=sem, vmem_limit_bytes=VMEM_LIMIT)


def _resident(shape):
    return pl.BlockSpec(shape, lambda *_: (0,) * len(shape), pipeline_mode=pl.Buffered(1))


MOD_TILE = 1536
MOD_BUFS = 3


def _mod_kernel(c_ref, w0_hbm, b0_ref, w1_hbm, b1_ref, o0_ref, o1_ref, wbuf, sem):
    n = o0_ref.shape[1]
    tiles = [(w, b, o, c0) for w, b, o in ((w0_hbm, b0_ref, o0_ref), (w1_hbm, b1_ref, o1_ref))
             for c0 in range(0, n, MOD_TILE)]

    def copy(i):
        w_hbm, _, _, c0 = tiles[i]
        slot = i % MOD_BUFS
        return pltpu.make_async_copy(w_hbm.at[:, pl.ds(c0, MOD_TILE)], wbuf.at[slot], sem.at[slot])

    for i in range(min(MOD_BUFS, len(tiles))):
        copy(i).start()
    s = _silu(c_ref[...]).astype(BF16)
    for i, (_, b_ref, o_ref, c0) in enumerate(tiles):
        copy(i).wait()
        o_ref[:, c0:c0 + MOD_TILE] = _dot(s, wbuf[i % MOD_BUFS].astype(BF16)) + b_ref[:, c0:c0 + MOD_TILE]
        if i + MOD_BUFS < len(tiles):
            copy(i + MOD_BUFS).start()


def _modulation(cond, mod_w0, mod_b0, mod_w1, mod_b1):
    n = mod_w0.shape[1]
    assert n % MOD_TILE == 0 and mod_w1.shape == mod_w0.shape
    vmem = pl.BlockSpec(memory_space=pltpu.VMEM)
    hbm = pl.BlockSpec(memory_space=pl.ANY)
    out = pl.pallas_call(
        _mod_kernel,
        in_specs=[vmem, hbm, vmem, hbm, vmem],
        out_specs=[vmem, vmem],
        out_shape=[jax.ShapeDtypeStruct((MOD_ROWS, n), F32)] * 2,
        scratch_shapes=[pltpu.VMEM((MOD_BUFS, D_MODEL, MOD_TILE), F32), pltpu.SemaphoreType.DMA((MOD_BUFS,))],
        compiler_params=pltpu.CompilerParams(vmem_limit_bytes=VMEM_LIMIT),
        name="modulation",
    )(cond, mod_w0, mod_b0.reshape(1, n), mod_w1, mod_b1.reshape(1, n))
    return tuple(o.reshape(MOD_ROWS, N_MOD, D_MODEL) for o in out)


def _rope_tables(seq):
    t = np.arange(seq)
    rows, cols = t // GRID_W, t % GRID_W
    half = HEAD_DIM // 2
    inv = 1.0 / (ROPE_THETA ** (np.arange(0, half, 2, dtype=np.float64) / half))
    ar = rows[:, None] * inv[None, :]
    ac = cols[:, None] * inv[None, :]
    cos = np.concatenate([np.cos(ar), np.cos(ar), np.cos(ac), np.cos(ac)], axis=1)
    sin = np.concatenate([-np.sin(ar), np.sin(ar), -np.sin(ac), np.sin(ac)], axis=1)
    reps = LANES // HEAD_DIM
    return (jnp.asarray(np.tile(cos, (1, reps)), F32), jnp.asarray(np.tile(sin, (1, reps)), F32))


def _rope(x, cos, sin):
    quarter = HEAD_DIM // 4
    lane = lax.broadcasted_iota(jnp.int32, x.shape, 1)
    lo = (lane % (2 * quarter)) < quarter
    partner = jnp.where(lo, pltpu.roll(x, LANES - quarter, 1), pltpu.roll(x, quarter, 1))
    return x * cos + partner * sin


def _pre_kernel(*refs, outs, rope, sblk):
    n_in = 6 if rope else 4
    x_ref, m_ref, g_ref, w_ref = refs[:4]
    _project(x_ref[...], m_ref, g_ref, w_ref, refs[4:6] if rope else None, outs, refs[n_in:], sblk)


def _project(x, m_ref, g_ref, w_ref, trig, outs, o_refs, sblk):
    rope = trig is not None
    tm = x.shape[0]
    h = (_rms(x, g_ref[0:1, :] * (1.0 + m_ref[1:2, :])) + m_ref[0:1, :]).astype(BF16)
    cost = lambda o: o[2] * {"T": 5.0, "heads": 0.5, "glu": 1.2}.get(o[0], 3.0 if (rope and o[4]) else 1.0)
    ranges = sorted({(o[1], o[2]) for o in outs},
                    key=lambda r: (-sum(cost(o) for o in outs if (o[1], o[2]) == r), r[0]))
    for start, width in ranges:
        p = _dot(h, w_ref[:, start:start + width])
        for (kind, s0, w0, _, roped, scale), o_ref in zip(outs, o_refs):
            if (s0, w0) != (start, width):
                continue
            if kind == "nat":
                for c in range(width // LANES):
                    xc = p[:, c * LANES:(c + 1) * LANES]
                    if rope and roped:
                        xc = _rope(xc, trig[0][...], trig[1][...])
                    if scale is not None:
                        xc = xc * scale
                    o_ref[:, c * LANES:(c + 1) * LANES] = xc.astype(o_ref.dtype)
            elif kind == "T":
                for s in range(tm // sblk):
                    o_ref[s] = p[s * sblk:(s + 1) * sblk, :].T.astype(o_ref.dtype)
            elif kind == "glu":
                o_ref[...] = _gated(p[:, 0:width // 2], p[:, width // 2:width])
            else:
                heads = width // LANES
                for hd in range(heads):
                    o_ref[pl.ds(hd, tm, stride=heads), :] = p[:, hd * LANES:(hd + 1) * LANES]


def _project_specs(m, norm_g, w, outs, *, rows, tm, cond_row, seq, rope):
    n = w.shape[1]
    sblk = min(seq, tm)
    per_seq = seq // sblk
    in_specs = [
        pl.BlockSpec((None, N_MOD, D_MODEL), lambda i: (cond_row(i), 0, 0)),
        pl.BlockSpec((4, D_MODEL), lambda i: (0, 0)),
        _resident((D_MODEL, n)),
    ]
    args = [m, norm_g, w]
    if rope:
        cos, sin = _rope_tables(seq)
        nblk = seq // tm
        in_specs += [pl.BlockSpec((tm, LANES), lambda i: (i % nblk, 0))] * 2
        args += [cos, sin]
    out_specs, out_shape = [], []
    for (kind, _, wd, dt, _, _) in outs:
        if kind == "nat":
            out_specs.append(pl.BlockSpec((tm, wd), lambda i: (i, 0)))
            out_shape.append(jax.ShapeDtypeStruct((rows, wd), dt))
        elif kind == "T":
            out_specs.append(pl.BlockSpec((tm // sblk, wd, sblk), lambda i: (i // per_seq, 0, i % per_seq)))
            out_shape.append(jax.ShapeDtypeStruct((rows // seq, wd, seq), dt))
        elif kind == "glu":
            out_specs.append(pl.BlockSpec((tm, wd // 2), lambda i: (i, 0)))
            out_shape.append(jax.ShapeDtypeStruct((rows, wd // 2), dt))
        else:
            out_specs.append(pl.BlockSpec((tm * (wd // LANES), LANES), lambda i: (i, 0)))
            out_shape.append(jax.ShapeDtypeStruct((rows * (wd // LANES), LANES), dt))
    return in_specs, args, out_specs, out_shape, sblk


def _pre(x, m, norm_g, w, outs, *, tm, cond_row, seq, rope):
    rows = x.shape[0]
    in_specs = [pl.BlockSpec((tm, D_MODEL), lambda i: (i, 0))]
    proj_in, proj_args, out_specs, out_shape, sblk = _project_specs(m, norm_g, w, outs, rows=rows, tm=tm,
                                                                    cond_row=cond_row, seq=seq, rope=rope)
    in_specs += proj_in
    args = [x] + proj_args
    return pl.pallas_call(
        functools.partial(_pre_kernel, outs=outs, rope=rope, sblk=sblk),
        grid=(rows // tm,),
        in_specs=in_specs,
        out_specs=out_specs,
        out_shape=out_shape,
        compiler_params=_params("arbitrary"),
        name="pre_rope" if rope else "pre",
    )(*args)


def _attend(s_list, vt_list, floor=None):
    mx = None
    for s in s_list:
        smax = jnp.max(s, axis=0, keepdims=True)
        mx = smax if mx is None else jnp.maximum(mx, smax)
    if floor is not None:
        mx = jnp.maximum(mx, floor)
    d = vt_list[0].shape[0]
    ov = None
    for s, vt in zip(s_list, vt_list):
        ones = jnp.ones((BF16_ROWS, vt.shape[1]), BF16)
        pv = _dot(jnp.concatenate([vt, ones], axis=0), jnp.exp2(s - mx).astype(BF16))
        ov = pv if ov is None else ov + pv
    return ov[0:d, :], ov[d:d + 1, :], mx


def _pipelined(n, first, second):
    out = []
    nxt = first(0)
    for u in range(n):
        cur = nxt
        if u + 1 < n:
            nxt = first(u + 1)
        out.append(second(u, cur))
    return out


def _side_convert(src_refs, dst_refs):
    for src, dst in zip(src_refs, dst_refs):
        dst[...] = src[...].astype(BF16)


def _mix0_kernel(*refs, seq, tq, n_cache, lam_init, n_side, nseq):
    n_in = 16 if n_cache else 14
    z_ref, q_ref, k_ref, vt_ref = refs[:4]
    ck_ref, cvt_ref = refs[4:6] if n_cache else (None, None)
    x_ref, m_ref, ng_ref, cw_ref, cb_ref, lg_ref, lb_ref, lam_ref, sg_ref, wo_ref = refs[n_in - 10:n_in]
    o_ref = refs[n_in + n_side]
    z_s, zw_s, zsh_s = refs[n_in + 2 * n_side + 1:]
    _side_convert(refs[n_in:n_in + n_side], refs[n_in + n_side + 1:n_in + 2 * n_side + 1])
    qi = pl.program_id(1)
    whole = tq == seq
    win = tq + 2 * CONV_PAD
    span = win - 8
    chunk = CONV_ROWS
    off0 = CONV_PAD - CONV_WIDTH // 2
    la = jnp.sum(lam_ref[0:1, :] * lam_ref[1:2, :], axis=-1, keepdims=True)
    lb = jnp.sum(lam_ref[2:3, :] * lam_ref[3:4, :], axis=-1, keepdims=True)
    lam = jnp.exp(la) - jnp.exp(lb) + lam_init
    sub = LANES // 2
    lane = lax.broadcasted_iota(jnp.int32, (tq, LANES), 1)

    def block(s):
        urows = slice(s * seq, (s + 1) * seq)

        def pad_sequence():
            z_s[0:CONV_PAD, :] = jnp.zeros((CONV_PAD, CONV_CH), F32)
            z_s[CONV_PAD + seq:2 * CONV_PAD + seq, :] = jnp.zeros((CONV_PAD, CONV_CH), F32)
            z_s[CONV_PAD:CONV_PAD + seq, :] = z_ref[urows, :]

        if whole:
            pad_sequence()
            zw = z_s
        else:
            pl.when(qi == 0)(pad_sequence)
            zw_s[...] = z_s[pl.ds(pl.multiple_of(qi * tq, tq), win), :]
            zw = zw_s

        for b in range(1, 8):
            zsh_s[b - 1, 0:span, :] = zw[b:b + span, :]
        conv_rows = []
        for c0 in range(0, tq, chunk):
            acc = jnp.zeros((chunk // 8, 8, CONV_CH), F32)
            for j in range(CONV_WIDTH):
                off = j + off0
                a, b = off // 8, off % 8
                if b == 0:
                    zz = zw[c0 + 8 * a:c0 + 8 * a + chunk, :]
                else:
                    zz = zsh_s[b - 1, c0 + 8 * a:c0 + 8 * a + chunk, :]
                acc = acc + zz.reshape(chunk // 8, 8, CONV_CH) * cw_ref[j]
            acc = acc.reshape(chunk, CONV_CH) + cb_ref[...]
            mu = jnp.mean(acc, axis=-1, keepdims=True)
            xc = acc - mu
            y = xc * lax.rsqrt(jnp.mean(xc * xc, axis=-1, keepdims=True) + EPS) * lg_ref[...] + lb_ref[...]
            conv_rows.append(_silu(y).astype(BF16))

        def segments(h):
            cols = slice(h * LANES, (h + 1) * LANES)
            segs = [(k_ref[urows, cols], vt_ref[s, cols, :])]
            if n_cache:
                segs.append((ck_ref[:, cols].astype(BF16), cvt_ref[cols, :].astype(BF16)))
            return segs

        def scores(h):
            qc = q_ref[s * tq:(s + 1) * tq, h * LANES:(h + 1) * LANES]
            zero = jnp.zeros_like(qc)
            qq = jnp.concatenate([jnp.where(lane < sub, qc, zero), jnp.where(lane >= sub, qc, zero)], axis=0)
            return [_dot_nt(kk, qq) for kk, _ in segments(h)]

        def head(h, s_list):
            ov, den, _ = _attend(s_list, [vt for _, vt in segments(h)])
            r = 1.0 / den
            o = ov[:, 0:tq] * r[:, 0:tq] - ov[:, tq:2 * tq] * (lam * r[:, tq:2 * tq])
            return o * lax.rsqrt(jnp.mean(o * o, axis=0, keepdims=True) + EPS) * sg_ref[...] * (1.0 - lam_init)

        heads = _pipelined(DIFF_HEADS, scores, head)
        attn = jnp.concatenate(heads, axis=0).T.astype(BF16)
        return jnp.concatenate([jnp.concatenate(conv_rows, axis=0), attn], axis=1)

    cat = jnp.concatenate([block(s) for s in range(nseq)], axis=0)
    mixed = _dot(cat, wo_ref[...])
    o_ref[...] = x_ref[...] + _rms(mixed, m_ref[2:3, :] * ng_ref[1:2, :])


def _side_specs(side, steps, index):
    in_specs, out_specs, out_shape = [], [], []
    for w in side:
        rows, cols = w.shape
        n_blk = math.gcd(steps, rows // BF16_ROWS)
        per = steps // n_blk
        spec = pl.BlockSpec((rows // n_blk, cols), lambda *ids, per=per: (index(*ids) // per, 0))
        in_specs.append(spec)
        out_specs.append(spec)
        out_shape.append(jax.ShapeDtypeStruct(w.shape, BF16))
    return in_specs, out_specs, out_shape


def _mix0(u, q, k, vt, cache, x, m, norm_g, conv_w, conv_b, ln_g, ln_b, lam, subln_g, w_out,
          *, batch, seq, tq, cond_row, lam_init, side=(), nseq=1):
    nq = seq // tq
    n_cache = 0 if cache is None else cache[0].shape[1]
    win = tq + 2 * CONV_PAD
    assert tq % CONV_ROWS == 0
    steps = (batch // nseq) * nq
    const = lambda shape: pl.BlockSpec(shape, lambda b, i: (0,) * len(shape))
    in_specs = [pl.BlockSpec((nseq * seq, CONV_CH), lambda b, i: (b, 0)),
                pl.BlockSpec((nseq * tq, DIFF_WIDTH), lambda b, i: (b * nq + i, 0)),
                pl.BlockSpec((nseq * seq, DIFF_WIDTH), lambda b, i: (b, 0)),
                pl.BlockSpec((nseq, DIFF_WIDTH, seq), lambda b, i: (b, 0, 0))]
    args = [u, q, k, vt]
    if n_cache:
        in_specs += [pl.BlockSpec((None, n_cache, DIFF_WIDTH), lambda b, i: (b, 0, 0)),
                     pl.BlockSpec((None, DIFF_WIDTH, n_cache), lambda b, i: (b, 0, 0))]
        args += [cache[0], cache[1]]
    in_specs += [
        pl.BlockSpec((nseq * tq, D_MODEL), lambda b, i: (b * nq + i, 0)),
        pl.BlockSpec((None, N_MOD, D_MODEL), lambda b, i: (cond_row(b * nseq), 0, 0)),
        const((4, D_MODEL)), const((CONV_WIDTH, 8, CONV_CH)), const((1, CONV_CH)), const((1, CONV_CH)),
        const((1, CONV_CH)), const((4, LANES // 2)), const((LANES, 1)), _resident((D_MODEL, D_MODEL)),
    ]
    args += [x, m, norm_g, conv_w, conv_b, ln_g, ln_b, lam, subln_g, w_out]
    side_in, side_out, side_shape = _side_specs(side, steps, lambda b, i: b * nq + i)
    return pl.pallas_call(
        functools.partial(_mix0_kernel, seq=seq, tq=tq, n_cache=n_cache, lam_init=lam_init, n_side=len(side),
                          nseq=nseq),
        grid=(batch // nseq, nq),
        in_specs=in_specs + side_in,
        out_specs=[pl.BlockSpec((nseq * tq, D_MODEL), lambda b, i: (b * nq + i, 0))] + side_out,
        out_shape=[jax.ShapeDtypeStruct((batch * seq, D_MODEL), F32)] + side_shape,
        scratch_shapes=[
            pltpu.VMEM((seq + 2 * CONV_PAD, CONV_CH), F32), pltpu.VMEM((win, CONV_CH), F32),
            pltpu.VMEM((7, win - 8, CONV_CH), F32),
        ],
        compiler_params=_params("arbitrary", "arbitrary"),
        name="mix_conv_diff",
    )(*args, *side)


def _mix1_kernel(*refs, seq, tq, n_cache, nseq):
    if n_cache:
        (q_ref, kp_ref, kc_ref, kn_ref, vtp_ref, vtc_ref, vtn_ref, ck_ref, cvt_ref,
         x_ref, m_ref, ng_ref, sink_ref, wo_ref, o_ref) = refs
    else:
        (q_ref, kc_ref, vtc_ref, x_ref, m_ref, ng_ref, sink_ref, wo_ref, o_ref) = refs
    qi = pl.program_id(1)
    lane = lax.broadcasted_iota(jnp.int32, (tq, LANES), 1)

    def band(base, rows):
        kpos = base + lax.broadcasted_iota(jnp.int32, (rows, tq), 0)
        qpos = qi * tq + lax.broadcasted_iota(jnp.int32, (rows, tq), 1)
        ok = (jnp.abs(qpos - kpos) <= WINDOW) & (kpos >= 0) & (kpos < seq)
        return jnp.concatenate([ok] * GQA_GROUP, axis=1)

    def attention(blocks):
        units = [(r0, segs, g) for r0, segs in blocks for g in range(GQA_KV_HEADS)]

        def scores(u):
            r0, segs, g = units[u]
            b = g % 2
            kcols = slice((g // 2) * LANES, (g // 2 + 1) * LANES)
            keep = (lane < HEAD_DIM) if b == 0 else (lane >= HEAD_DIM)
            qs = []
            for h in range(g * GQA_GROUP, (g + 1) * GQA_GROUP):
                qc = q_ref[r0:r0 + tq, (h // 2) * LANES:(h // 2 + 1) * LANES]
                qa = qc if h % 2 == b else pltpu.roll(qc, HEAD_DIM, 1)
                qs.append(jnp.where(keep, qa, jnp.zeros_like(qa)))
            qg = jnp.concatenate(qs, axis=0)
            s_list = []
            for k_ref, _, ok in segs:
                s = _dot_nt(k_ref[:, kcols].astype(BF16), qg)
                s_list.append(s if ok is None else jnp.where(ok, s, NEG_INF))
            return s_list

        def group(u, s_list):
            _, segs, g = units[u]
            vrows = slice(g * HEAD_DIM, (g + 1) * HEAD_DIM)
            sink = jnp.concatenate([jnp.full((1, tq), sink_ref[h] * LOG2E, F32)
                                    for h in range(g * GQA_GROUP, (g + 1) * GQA_GROUP)], axis=1)
            ov, den, mx = _attend(s_list, [vt_ref[vrows, :].astype(BF16) for _, vt_ref, _ in segs], floor=sink)
            ov = ov * (1.0 / (den + jnp.exp2(sink - mx)))
            return jnp.concatenate([ov[:, hh * tq:(hh + 1) * tq] for hh in range(GQA_GROUP)], axis=0)

        groups = _pipelined(len(units), scores, group)
        return [jnp.concatenate(groups[i:i + GQA_KV_HEADS], axis=0).T.astype(BF16)
                for i in range(0, len(units), GQA_KV_HEADS)]

    if n_cache:
        blocks = [(0, [(kp_ref, vtp_ref, band(qi * tq - WINDOW, WINDOW)),
                       (kc_ref, vtc_ref, band(qi * tq, tq)),
                       (kn_ref, vtn_ref, band(qi * tq + tq, WINDOW)),
                       (ck_ref, cvt_ref, None)])]
    else:
        blocks = [(s * tq, [(kc_ref.at[pl.ds(s * tq, tq)], vtc_ref.at[s], None)]) for s in range(nseq)]
    attn = jnp.concatenate(attention(blocks), axis=0)
    mixed = _dot(attn, wo_ref[...])
    o_ref[...] = x_ref[...] + _rms(mixed, m_ref[2:3, :] * ng_ref[1:2, :])


def _mix1(q, k, vt, cache, x, m, norm_g, sink, w_out, *, batch, seq, tq, cond_row, nseq=1):
    nq = seq // tq
    n_cache = 0 if cache is None else cache[0].shape[1]
    rows = nseq * tq
    q_spec = pl.BlockSpec((rows, D_MODEL), lambda b, i: (b * nq + i, 0))
    kc_spec = pl.BlockSpec((rows, KV_WIDTH), lambda b, i: (b * nq + i, 0))
    vtc_spec = pl.BlockSpec((None, KV_WIDTH, tq), lambda b, i: (b, 0, i))
    if n_cache:
        nw = seq // WINDOW
        per = tq // WINDOW
        prev = lambda i: jnp.maximum(i * per - 1, 0)
        nxt = lambda i: jnp.minimum(i * per + per, nw - 1)
        in_specs = [q_spec,
                    pl.BlockSpec((WINDOW, KV_WIDTH), lambda b, i: (b * nw + prev(i), 0)), kc_spec,
                    pl.BlockSpec((WINDOW, KV_WIDTH), lambda b, i: (b * nw + nxt(i), 0)),
                    pl.BlockSpec((None, KV_WIDTH, WINDOW), lambda b, i: (b, 0, prev(i))), vtc_spec,
                    pl.BlockSpec((None, KV_WIDTH, WINDOW), lambda b, i: (b, 0, nxt(i))),
                    pl.BlockSpec((None, n_cache, KV_WIDTH), lambda b, i: (b, 0, 0)),
                    pl.BlockSpec((None, KV_WIDTH, n_cache), lambda b, i: (b, 0, 0))]
        args = [q, k, k, k, vt, vt, vt, cache[0], cache[1]]
    else:
        in_specs = [q_spec, kc_spec, pl.BlockSpec((nseq, KV_WIDTH, tq), lambda b, i: (b, 0, 0))]
        args = [q, k, vt]
    in_specs += [
        pl.BlockSpec((rows, D_MODEL), lambda b, i: (b * nq + i, 0)),
        pl.BlockSpec((None, N_MOD, D_MODEL), lambda b, i: (cond_row(b * nseq), 0, 0)),
        pl.BlockSpec((4, D_MODEL), lambda b, i: (0, 0)),
        pl.BlockSpec(memory_space=pltpu.SMEM),
        pl.BlockSpec((D_MODEL, D_MODEL), lambda b, i: (0, 0)),
    ]
    args += [x, m, norm_g, sink, w_out]
    return pl.pallas_call(
        functools.partial(_mix1_kernel, seq=seq, tq=tq, n_cache=n_cache, nseq=nseq),
        grid=(batch // nseq, nq),
        in_specs=in_specs,
        out_specs=pl.BlockSpec((rows, D_MODEL), lambda b, i: (b * nq + i, 0)),
        out_shape=jax.ShapeDtypeStruct((batch * seq, D_MODEL), F32),
        compiler_params=_params("arbitrary", "arbitrary"),
        name="mix_gqa",
    )(*args)


FFN_CHUNKS = (1536, 1280)


def _ffn_kernel(*refs, n_side, proj, sub):
    x_ref, m_ref, ng_ref, wgu_ref, wd_ref = refs[:5]
    n_pin = 0 if proj is None else (5 if proj[1] else 3)
    n_pout = 0 if proj is None else len(proj[0])
    n_in = 5 + n_pin + n_side
    o_ref = refs[n_in]
    _side_convert(refs[5 + n_pin:n_in], refs[n_in + 1 + n_pout:])
    for r0 in range(0, x_ref.shape[0], sub):
        rows = slice(r0, r0 + sub)
        h = (_rms(x_ref[rows, :], ng_ref[2:3, :] * (1.0 + m_ref[4:5, :])) + m_ref[3:4, :]).astype(BF16)
        acc = jnp.zeros(h.shape, F32)
        c0 = 0
        for width in FFN_CHUNKS:
            gate = _dot(h, wgu_ref[:, c0:c0 + width])
            up = _dot(h, wgu_ref[:, FFN_HIDDEN + c0:FFN_HIDDEN + c0 + width])
            act = (_silu(gate) * up).astype(BF16)
            acc = acc + _dot(act, wd_ref[c0:c0 + width, :])
            c0 += width
        o_ref[rows, :] = x_ref[rows, :] + _rms(acc, m_ref[5:6, :] * ng_ref[3:4, :])
    if proj is not None:
        outs, rope, sblk = proj
        pm_ref, pg_ref, pw_ref = refs[5:8]
        _project(o_ref[...], pm_ref, pg_ref, pw_ref, refs[8:10] if rope else None, outs,
                 refs[n_in + 1:n_in + 1 + n_pout], sblk)


def _ffn(x, m, norm_g, w_gu, w_down, *, tm, sub, cond_row, side=(), proj=None):
    rows = x.shape[0]
    side_in, side_out, side_shape = _side_specs(side, rows // tm, lambda i: i)
    proj_in, proj_args, proj_out, proj_shape, kproj = [], [], [], [], None
    if proj is not None:
        pm, pg, pw, outs, seq, rope = proj
        proj_in, proj_args, proj_out, proj_shape, sblk = _project_specs(pm, pg, pw, outs, rows=rows, tm=tm,
                                                                        cond_row=cond_row, seq=seq, rope=rope)
        kproj = (outs, rope, sblk)
    return pl.pallas_call(
        functools.partial(_ffn_kernel, n_side=len(side), proj=kproj, sub=sub),
        grid=(rows // tm,),
        in_specs=[
            pl.BlockSpec((tm, D_MODEL), lambda i: (i, 0)),
            pl.BlockSpec((None, N_MOD, D_MODEL), lambda i: (cond_row(i), 0, 0)),
            pl.BlockSpec((4, D_MODEL), lambda i: (0, 0)),
            _resident((D_MODEL, 2 * FFN_HIDDEN)),
            _resident((FFN_HIDDEN, D_MODEL)),
        ] + proj_in + side_in,
        out_specs=[pl.BlockSpec((tm, D_MODEL), lambda i: (i, 0))] + proj_out + side_out,
        out_shape=[jax.ShapeDtypeStruct((rows, D_MODEL), F32)] + proj_shape + side_shape,
        compiler_params=_params("arbitrary"),
        name="ffn",
    )(x, m, norm_g, w_gu, w_down, *proj_args, *side)


LAM_INIT_0 = 0.8 - 0.6 * math.exp(-0.3 * 0)
CTX_SEQS_PER_STEP = 4


def _trunk(x, m, caches, weights, *, batch, seq, cond_of_batch):
    (norm_g0, w_in, conv_w, conv_b, ln_g, ln_b, lam, subln_g, w_out0, w_gu0, w_down0,
     norm_g1, w_qkv, sink, w_out1, w_gu1, w_down1) = weights
    m0, m1 = m
    ctx = caches is None
    rope = not ctx
    tm = 1024 if ctx else 512
    tm_ffn = 512
    tm_ffn1 = 1024 if ctx else 512
    ffn_sub = 256
    tq = 256
    row_of = lambda t: (lambda i: cond_of_batch((i * t) // seq))

    o0 = 2 * CONV_CH
    ko, vo = o0 + DIFF_WIDTH, o0 + 2 * DIFF_WIDTH
    outs0 = [("glu", 0, o0, F32, False, None), ("nat", o0, DIFF_WIDTH, BF16, True, Q_SCALE),
             ("nat", ko, DIFF_WIDTH, BF16, True, None), ("T", vo, DIFF_WIDTH, BF16, False, None)]
    if ctx:
        outs0 += [("T", ko, DIFF_WIDTH, F32, False, None), ("heads", vo, DIFF_WIDTH, F32, False, None)]
    res = _pre(x, m0, norm_g0, w_in, tuple(outs0), tm=tm, cond_row=row_of(tm), seq=seq, rope=rope)
    u, q, k, vt = res[:4]
    new0 = tuple(res[4:])
    x, *conv0 = _mix0(u, q, k, vt, None if ctx else caches[0], x, m0, norm_g0, conv_w, conv_b,
                      ln_g, ln_b, lam, subln_g, w_out0, batch=batch, seq=seq, tq=tq, cond_row=cond_of_batch,
                      lam_init=LAM_INIT_0, side=(w_gu0, w_down0, w_qkv) if ctx else (),
                      nseq=CTX_SEQS_PER_STEP if ctx else 1)
    if ctx:
        w_gu0, w_down0, w_qkv = conv0

    outs1 = [("nat", 0, Q_WIDTH, BF16, True, Q_SCALE), ("nat", Q_WIDTH, KV_WIDTH, BF16, True, None)]
    if ctx:
        outs1 += [("T", Q_WIDTH + KV_WIDTH, KV_WIDTH, F32, False, None), ("T", Q_WIDTH, KV_WIDTH, F32, False, None)]
    else:
        outs1 += [("T", Q_WIDTH + KV_WIDTH, KV_WIDTH, BF16, False, None)]
    res = _ffn(x, m0, norm_g0, w_gu0, w_down0, tm=tm_ffn, sub=ffn_sub, cond_row=row_of(tm_ffn),
               side=(w_out1, w_gu1, w_down1) if ctx else (),
               proj=(m1, norm_g1, w_qkv, tuple(outs1), seq, rope))
    x, q, k, vt = res[:4]
    new1 = (res[4], vt) if ctx else ()
    if ctx:
        w_out1, w_gu1, w_down1 = res[5:]
    x = _mix1(q, k, vt, None if ctx else caches[1], x, m1, norm_g1, sink, w_out1,
              batch=batch, seq=seq, tq=tq, cond_row=cond_of_batch, nseq=CTX_SEQS_PER_STEP if ctx else 1)
    x, = _ffn(x, m1, norm_g1, w_gu1, w_down1, tm=tm_ffn1, sub=ffn_sub, cond_row=row_of(tm_ffn1))
    return x, new0 + new1, (w_gu0, w_down0, w_qkv, w_out1, w_gu1, w_down1)


def kernel(x_prompt, x_sample, cache_k0, cache_v0, cache_k1, cache_v1, c, c_ctx, l0_mod_w, l0_mod_b, l0_norm_g, l0_w_in, l0_conv_w, l0_conv_b, l0_conv_ln_g, l0_conv_ln_b, l0_lambda, l0_subln_g, l0_w_out, l0_w_gu, l0_w_down, l1_mod_w, l1_mod_b, l1_norm_g, l1_w_qkv, l1_sink, l1_w_out, l1_w_gu, l1_w_down):
    batch, seq, d = x_prompt.shape
    dec_batch, dec_seq, _ = x_sample.shape
    n_past = cache_k0.shape[1]
    assert d == D_MODEL and seq == 256 and batch % CTX_SEQS_PER_STEP == 0 and (batch * seq) % 1024 == 0
    assert dec_seq % 512 == 0 and dec_seq % GRID_W == 0 and n_past % LANES == 0 and 1 + dec_batch <= MOD_ROWS

    cond = jnp.concatenate([c_ctx[None, :], c, jnp.zeros((MOD_ROWS - 1 - dec_batch, d), F32)], axis=0)
    m = _modulation(cond, l0_mod_w, l0_mod_b, l1_mod_w, l1_mod_b)

    row = lambda v: v.reshape(1, -1)
    layer0 = (l0_norm_g, l0_w_in.astype(BF16), jnp.broadcast_to(l0_conv_w, (CONV_WIDTH, 8, CONV_CH)), row(l0_conv_b),
              row(l0_conv_ln_g), row(l0_conv_ln_b), l0_lambda, l0_subln_g.reshape(LANES, 1),
              l0_w_out.astype(BF16))
    weights = layer0 + (l0_w_gu, l0_w_down, l1_norm_g, l1_w_qkv, l1_sink, l1_w_out, l1_w_gu, l1_w_down)

    y_prompt, (kt0, v0, kt1, vt1), bf16_w = _trunk(x_prompt.reshape(batch * seq, d), m, None, weights,
                                                   batch=batch, seq=seq, cond_of_batch=lambda b: 0)
    w_gu0, w_down0, w_qkv, w_out1, w_gu1, w_down1 = bf16_w
    weights = layer0 + (w_gu0, w_down0, l1_norm_g, w_qkv, l1_sink, w_out1, w_gu1, w_down1)
    caches = ((cache_k0.reshape(dec_batch, n_past, -1),
               jnp.swapaxes(cache_v0.reshape(dec_batch, n_past, -1), 1, 2)),
              (cache_k1.reshape(dec_batch, n_past, -1),
               jnp.swapaxes(cache_v1.reshape(dec_batch, n_past, -1), 1, 2)))
    y_sample, _, _ = _trunk(x_sample.reshape(dec_batch * dec_seq, d), m, caches, weights,
                            batch=dec_batch, seq=dec_seq, cond_of_batch=lambda b: 1 + b)
    new_k0 = kt0.reshape(batch, DIFF_HEADS, 2, HEAD_DIM, seq).transpose(0, 4, 1, 2, 3)
    new_v0 = v0.reshape(batch, seq, DIFF_HEADS, 2 * HEAD_DIM)
    new_k1 = kt1.reshape(batch, GQA_KV_HEADS, HEAD_DIM, seq).transpose(0, 3, 1, 2)
    new_v1 = vt1.reshape(batch, GQA_KV_HEADS, HEAD_DIM, seq).transpose(0, 3, 1, 2)
    return (y_prompt.reshape(batch, seq, d), y_sample.reshape(dec_batch, dec_seq, d),
            new_k0, new_v0, new_k1, new_v1)
```

```python
import functools
import math

import numpy as np
import jax
import jax.numpy as jnp
from jax import lax
from jax.experimental import pallas as pl
from jax.experimental.pallas import tpu as pltpu

D_MODEL = 1024
GRID_W = 64
ROPE_THETA = 10000.0
EPS = 1e-6
NEG_INF = -1e30
LOG2E = math.log2(math.e)
CONV_CH = 512
CONV_WIDTH = 31
CONV_PAD = 16
CONV_ROWS = 64
DIFF_HEADS = 4
DIFF_WIDTH = 512
HEAD_DIM = 64
GQA_HEADS = 16
GQA_KV_HEADS = 4
GQA_GROUP = 4
Q_WIDTH = GQA_HEADS * HEAD_DIM
KV_WIDTH = GQA_KV_HEADS * HEAD_DIM
Q_SCALE = LOG2E * HEAD_DIM ** -0.5
WINDOW = 128
FFN_HIDDEN = 2816
N_MOD = 6
LANES = 128
BF16_ROWS = 16
MOD_ROWS = 8
VMEM_LIMIT = 56 * 1024 * 1024

BF16 = jnp.bfloat16
F32 = jnp.float32


def _gated(a, g):
    return (0.5 * a) * (jnp.tanh(0.5 * g) + 1.0)


def _silu(x):
    h = 0.5 * x
    return h * (jnp.tanh(h) + 1.0)


def _rms(x, g):
    return x * lax.rsqrt(jnp.mean(x * x, axis=-1, keepdims=True) + EPS) * g


def _dot(a, b):
    return jnp.dot(a, b, preferred_element_type=F32)


def _dot_nt(a, b):
    return lax.dot_general(a, b, (((1,), (1,)), ((), ())), preferred_element_type=F32)


def _params(*sem):
    return pltpu.CompilerParams(dimension_semantics=sem, vmem_limit_bytes=VMEM_LIMIT)


def _resident(shape):
    return pl.BlockSpec(shape, lambda *_: (0,) * len(shape), pipeline_mode=pl.Buffered(1))


MOD_TILE = 768
MOD_BUFS = 4


def _mod_kernel(c_ref, w0_hbm, b0_ref, w1_hbm, b1_ref, o0_ref, o1_ref, wbuf, sem):
    n = o0_ref.shape[1]
    tiles = [(w, b, o, c0) for w, b, o in ((w0_hbm, b0_ref, o0_ref), (w1_hbm, b1_ref, o1_ref))
             for c0 in range(0, n, MOD_TILE)]

    def copy(i):
        w_hbm, _, _, c0 = tiles[i]
        slot = i % MOD_BUFS
        return pltpu.make_async_copy(w_hbm.at[:, pl.ds(c0, MOD_TILE)], wbuf.at[slot], sem.at[slot])

    for i in range(min(MOD_BUFS, len(tiles))):
        copy(i).start()
    s = _silu(c_ref[...]).astype(BF16)
    for i, (_, b_ref, o_ref, c0) in enumerate(tiles):
        copy(i).wait()
        o_ref[:, c0:c0 + MOD_TILE] = _dot(s, wbuf[i % MOD_BUFS].astype(BF16)) + b_ref[:, c0:c0 + MOD_TILE]
        if i + MOD_BUFS < len(tiles):
            copy(i + MOD_BUFS).start()


def _modulation(cond, mod_w0, mod_b0, mod_w1, mod_b1):
    n = mod_w0.shape[1]
    assert n % MOD_TILE == 0 and mod_w1.shape == mod_w0.shape
    vmem = pl.BlockSpec(memory_space=pltpu.VMEM)
    hbm = pl.BlockSpec(memory_space=pl.ANY)
    out = pl.pallas_call(
        _mod_kernel,
        in_specs=[vmem, hbm, vmem, hbm, vmem],
        out_specs=[vmem, vmem],
        out_shape=[jax.ShapeDtypeStruct((MOD_ROWS, n), F32)] * 2,
        scratch_shapes=[pltpu.VMEM((MOD_BUFS, D_MODEL, MOD_TILE), F32), pltpu.SemaphoreType.DMA((MOD_BUFS,))],
        compiler_params=pltpu.CompilerParams(vmem_limit_bytes=VMEM_LIMIT),
        name="modulation",
    )(cond, mod_w0, mod_b0.reshape(1, n), mod_w1, mod_b1.reshape(1, n))
    return tuple(o.reshape(MOD_ROWS, N_MOD, D_MODEL) for o in out)


def _rope_tables(seq):
    t = np.arange(seq)
    rows, cols = t // GRID_W, t % GRID_W
    half = HEAD_DIM // 2
    inv = 1.0 / (ROPE_THETA ** (np.arange(0, half, 2, dtype=np.float64) / half))
    ar = rows[:, None] * inv[None, :]
    ac = cols[:, None] * inv[None, :]
    cos = np.concatenate([np.cos(ar), np.cos(ar), np.cos(ac), np.cos(ac)], axis=1)
    sin = np.concatenate([-np.sin(ar), np.sin(ar), -np.sin(ac), np.sin(ac)], axis=1)
    reps = LANES // HEAD_DIM
    return (jnp.asarray(np.tile(cos, (1, reps)), F32), jnp.asarray(np.tile(sin, (1, reps)), F32))


def _rope(x, cos, sin):
    quarter = HEAD_DIM // 4
    lane = lax.broadcasted_iota(jnp.int32, x.shape, 1)
    lo = (lane % (2 * quarter)) < quarter
    partner = jnp.where(lo, pltpu.roll(x, LANES - quarter, 1), pltpu.roll(x, quarter, 1))
    return x * cos + partner * sin


def _pre_kernel(*refs, outs, rope, sblk):
    n_in = 6 if rope else 4
    x_ref, m_ref, g_ref, w_ref = refs[:4]
    _project(x_ref[...], m_ref, g_ref, w_ref, refs[4:6] if rope else None, outs, refs[n_in:], sblk)


def _project(x, m_ref, g_ref, w_ref, trig, outs, o_refs, sblk):
    rope = trig is not None
    tm = x.shape[0]
    h = (_rms(x, g_ref[0:1, :] * (1.0 + m_ref[1:2, :])) + m_ref[0:1, :]).astype(BF16)
    cost = lambda o: o[2] * {"T": 5.0, "heads": 0.5, "glu": 1.2}.get(o[0], 3.0 if (rope and o[4]) else 1.0)
    ranges = sorted({(o[1], o[2]) for o in outs},
                    key=lambda r: (-sum(cost(o) for o in outs if (o[1], o[2]) == r), r[0]))
    for start, width in ranges:
        p = _dot(h, w_ref[:, start:start + width])
        for (kind, s0, w0, _, roped, scale), o_ref in zip(outs, o_refs):
            if (s0, w0) != (start, width):
                continue
            if kind == "nat":
                for c in range(width // LANES):
                    xc = p[:, c * LANES:(c + 1) * LANES]
                    if rope and roped:
                        xc = _rope(xc, trig[0][...], trig[1][...])
                    if scale is not None:
                        xc = xc * scale
                    o_ref[:, c * LANES:(c + 1) * LANES] = xc.astype(o_ref.dtype)
            elif kind == "T":
                for s in range(tm // sblk):
                    o_ref[s] = p[s * sblk:(s + 1) * sblk, :].T.astype(o_ref.dtype)
            elif kind == "glu":
                o_ref[...] = _gated(p[:, 0:width // 2], p[:, width // 2:width])
            else:
                heads = width // LANES
                for hd in range(heads):
                    o_ref[pl.ds(hd, tm, stride=heads), :] = p[:, hd * LANES:(hd + 1) * LANES]


def _project_specs(m, norm_g, w, outs, *, rows, tm, cond_row, seq, rope):
    n = w.shape[1]
    sblk = min(seq, tm)
    per_seq = seq // sblk
    in_specs = [
        pl.BlockSpec((None, N_MOD, D_MODEL), lambda i: (cond_row(i), 0, 0)),
        pl.BlockSpec((4, D_MODEL), lambda i: (0, 0)),
        _resident((D_MODEL, n)),
    ]
    args = [m, norm_g, w]
    if rope:
        cos, sin = _rope_tables(seq)
        nblk = seq // tm
        in_specs += [pl.BlockSpec((tm, LANES), lambda i: (i % nblk, 0))] * 2
        args += [cos, sin]
    out_specs, out_shape = [], []
    for (kind, _, wd, dt, _, _) in outs:
        if kind == "nat":
            out_specs.append(pl.BlockSpec((tm, wd), lambda i: (i, 0)))
            out_shape.append(jax.ShapeDtypeStruct((rows, wd), dt))
        elif kind == "T":
            out_specs.append(pl.BlockSpec((tm // sblk, wd, sblk), lambda i: (i // per_seq, 0, i % per_seq)))
            out_shape.append(jax.ShapeDtypeStruct((rows // seq, wd, seq), dt))
        elif kind == "glu":
            out_specs.append(pl.BlockSpec((tm, wd // 2), lambda i: (i, 0)))
            out_shape.append(jax.ShapeDtypeStruct((rows, wd // 2), dt))
        else:
            out_specs.append(pl.BlockSpec((tm * (wd // LANES), LANES), lambda i: (i, 0)))
            out_shape.append(jax.ShapeDtypeStruct((rows * (wd // LANES), LANES), dt))
    return in_specs, args, out_specs, out_shape, sblk


def _pre(x, m, norm_g, w, outs, *, tm, cond_row, seq, rope):
    rows = x.shape[0]
    in_specs = [pl.BlockSpec((tm, D_MODEL), lambda i: (i, 0))]
    proj_in, proj_args, out_specs, out_shape, sblk = _project_specs(m, norm_g, w, outs, rows=rows, tm=tm,
                                                                    cond_row=cond_row, seq=seq, rope=rope)
    in_specs += proj_in
    args = [x] + proj_args
    return pl.pallas_call(
        functools.partial(_pre_kernel, outs=outs, rope=rope, sblk=sblk),
        grid=(rows // tm,),
        in_specs=in_specs,
        out_specs=out_specs,
        out_shape=out_shape,
        compiler_params=_params("arbitrary"),
        name="pre_rope" if rope else "pre",
    )(*args)


def _attend(s_list, vt_list, floor=None):
    mx = None
    for s in s_list:
        smax = jnp.max(s, axis=0, keepdims=True)
        mx = smax if mx is None else jnp.maximum(mx, smax)
    if floor is not None:
        mx = jnp.maximum(mx, floor)
    d = vt_list[0].shape[0]
    ov = None
    for s, vt in zip(s_list, vt_list):
        ones = jnp.ones((BF16_ROWS, vt.shape[1]), BF16)
        pv = _dot(jnp.concatenate([vt, ones], axis=0), jnp.exp2(s - mx).astype(BF16))
        ov = pv if ov is None else ov + pv
    return ov[0:d, :], ov[d:d + 1, :], mx


def _pipelined(n, first, second):
    out = []
    nxt = first(0)
    for u in range(n):
        cur = nxt
        if u + 1 < n:
            nxt = first(u + 1)
        out.append(second(u, cur))
    return out


def _side_convert(src_refs, dst_refs):
    for src, dst in zip(src_refs, dst_refs):
        dst[...] = src[...].astype(BF16)


def _mix0_kernel(*refs, seq, tq, n_cache, lam_init, n_side, nseq):
    n_in = 16 if n_cache else 14
    z_ref, q_ref, k_ref, vt_ref = refs[:4]
    ck_ref, cvt_ref = refs[4:6] if n_cache else (None, None)
    x_ref, m_ref, ng_ref, cw_ref, cb_ref, lg_ref, lb_ref, lam_ref, sg_ref, wo_ref = refs[n_in - 10:n_in]
    o_ref = refs[n_in + n_side]
    z_s, zw_s, zsh_s = refs[n_in + 2 * n_side + 1:]
    _side_convert(refs[n_in:n_in + n_side], refs[n_in + n_side + 1:n_in + 2 * n_side + 1])
    qi = pl.program_id(1)
    whole = tq == seq
    win = tq + 2 * CONV_PAD
    span = win - 8
    chunk = CONV_ROWS
    off0 = CONV_PAD - CONV_WIDTH // 2
    la = jnp.sum(lam_ref[0:1, :] * lam_ref[1:2, :], axis=-1, keepdims=True)
    lb = jnp.sum(lam_ref[2:3, :] * lam_ref[3:4, :], axis=-1, keepdims=True)
    lam = jnp.exp(la) - jnp.exp(lb) + lam_init
    sub = LANES // 2
    lane = lax.broadcasted_iota(jnp.int32, (tq, LANES), 1)

    def block(s):
        urows = slice(s * seq, (s + 1) * seq)

        def pad_sequence():
            z_s[0:CONV_PAD, :] = jnp.zeros((CONV_PAD, CONV_CH), F32)
            z_s[CONV_PAD + seq:2 * CONV_PAD + seq, :] = jnp.zeros((CONV_PAD, CONV_CH), F32)
            z_s[CONV_PAD:CONV_PAD + seq, :] = z_ref[urows, :]

        if whole:
            pad_sequence()
            zw = z_s
        else:
            pl.when(qi == 0)(pad_sequence)
            zw_s[...] = z_s[pl.ds(pl.multiple_of(qi * tq, tq), win), :]
            zw = zw_s

        for b in range(1, 8):
            zsh_s[b - 1, 0:span, :] = zw[b:b + span, :]
        conv_rows = []
        for c0 in range(0, tq, chunk):
            acc = jnp.zeros((chunk // 8, 8, CONV_CH), F32)
            for j in range(CONV_WIDTH):
                off = j + off0
                a, b = off // 8, off % 8
                if b == 0:
                    zz = zw[c0 + 8 * a:c0 + 8 * a + chunk, :]
                else:
                    zz = zsh_s[b - 1, c0 + 8 * a:c0 + 8 * a + chunk, :]
                acc = acc + zz.reshape(chunk // 8, 8, CONV_CH) * cw_ref[j]
            acc = acc.reshape(chunk, CONV_CH) + cb_ref[...]
            mu = jnp.mean(acc, axis=-1, keepdims=True)
            xc = acc - mu
            y = xc * lax.rsqrt(jnp.mean(xc * xc, axis=-1, keepdims=True) + EPS) * lg_ref[...] + lb_ref[...]
            conv_rows.append(_silu(y).astype(BF16))

        def segments(h):
            cols = slice(h * LANES, (h + 1) * LANES)
            segs = [(k_ref[urows, cols], vt_ref[s, cols, :])]
            if n_cache:
                segs.append((ck_ref[:, cols].astype(BF16), cvt_ref[cols, :].astype(BF16)))
            return segs

        def scores(h):
            qc = q_ref[s * tq:(s + 1) * tq, h * LANES:(h + 1) * LANES]
            zero = jnp.zeros_like(qc)
            qq = jnp.concatenate([jnp.where(lane < sub, qc, zero), jnp.where(lane >= sub, qc, zero)], axis=0)
            return [_dot_nt(kk, qq) for kk, _ in segments(h)]

        def head(h, s_list):
            ov, den, _ = _attend(s_list, [vt for _, vt in segments(h)])
            r = 1.0 / den
            o = ov[:, 0:tq] * r[:, 0:tq] - ov[:, tq:2 * tq] * (lam * r[:, tq:2 * tq])
            return o * lax.rsqrt(jnp.mean(o * o, axis=0, keepdims=True) + EPS) * sg_ref[...] * (1.0 - lam_init)

        heads = _pipelined(DIFF_HEADS, scores, head)
        attn = jnp.concatenate(heads, axis=0).T.astype(BF16)
        return jnp.concatenate([jnp.concatenate(conv_rows, axis=0), attn], axis=1)

    cat = jnp.concatenate([block(s) for s in range(nseq)], axis=0)
    mixed = _dot(cat, wo_ref[...])
    o_ref[...] = x_ref[...] + _rms(mixed, m_ref[2:3, :] * ng_ref[1:2, :])


def _side_specs(side, steps, index):
    in_specs, out_specs, out_shape = [], [], []
    for w in side:
        rows, cols = w.shape
        n_blk = math.gcd(steps, rows // BF16_ROWS)
        per = steps // n_blk
        spec = pl.BlockSpec((rows // n_blk, cols), lambda *ids, per=per: (index(*ids) // per, 0))
        in_specs.append(spec)
        out_specs.append(spec)
        out_shape.append(jax.ShapeDtypeStruct(w.shape, BF16))
    return in_specs, out_specs, out_shape


def _mix0(u, q, k, vt, cache, x, m, norm_g, conv_w, conv_b, ln_g, ln_b, lam, subln_g, w_out,
          *, batch, seq, tq, cond_row, lam_init, side=(), nseq=1):
    nq = seq // tq
    n_cache = 0 if cache is None else cache[0].shape[1]
    win = tq + 2 * CONV_PAD
    assert tq % CONV_ROWS == 0
    steps = (batch // nseq) * nq
    const = lambda shape: pl.BlockSpec(shape, lambda b, i: (0,) * len(shape))
    in_specs = [pl.BlockSpec((nseq * seq, CONV_CH), lambda b, i: (b, 0)),
                pl.BlockSpec((nseq * tq, DIFF_WIDTH), lambda b, i: (b * nq + i, 0)),
                pl.BlockSpec((nseq * seq, DIFF_WIDTH), lambda b, i: (b, 0)),
                pl.BlockSpec((nseq, DIFF_WIDTH, seq), lambda b, i: (b, 0, 0))]
    args = [u, q, k, vt]
    if n_cache:
        in_specs += [pl.BlockSpec((None, n_cache, DIFF_WIDTH), lambda b, i: (b, 0, 0)),
                     pl.BlockSpec((None, DIFF_WIDTH, n_cache), lambda b, i: (b, 0, 0))]
        args += [cache[0], cache[1]]
    in_specs += [
        pl.BlockSpec((nseq * tq, D_MODEL), lambda b, i: (b * nq + i, 0)),
        pl.BlockSpec((None, N_MOD, D_MODEL), lambda b, i: (cond_row(b * nseq), 0, 0)),
        const((4, D_MODEL)), const((CONV_WIDTH, 8, CONV_CH)), const((1, CONV_CH)), const((1, CONV_CH)),
        const((1, CONV_CH)), const((4, LANES // 2)), const((LANES, 1)), _resident((D_MODEL, D_MODEL)),
    ]
    args += [x, m, norm_g, conv_w, conv_b, ln_g, ln_b, lam, subln_g, w_out]
    side_in, side_out, side_shape = _side_specs(side, steps, lambda b, i: b * nq + i)
    return pl.pallas_call(
        functools.partial(_mix0_kernel, seq=seq, tq=tq, n_cache=n_cache, lam_init=lam_init, n_side=len(side),
                          nseq=nseq),
        grid=(batch // nseq, nq),
        in_specs=in_specs + side_in,
        out_specs=[pl.BlockSpec((nseq * tq, D_MODEL), lambda b, i: (b * nq + i, 0))] + side_out,
        out_shape=[jax.ShapeDtypeStruct((batch * seq, D_MODEL), F32)] + side_shape,
        scratch_shapes=[
            pltpu.VMEM((seq + 2 * CONV_PAD, CONV_CH), F32), pltpu.VMEM((win, CONV_CH), F32),
            pltpu.VMEM((7, win - 8, CONV_CH), F32),
        ],
        compiler_params=_params("arbitrary", "arbitrary"),
        name="mix_conv_diff",
    )(*args, *side)


def _mix1_kernel(*refs, seq, tq, n_cache, nseq):
    if n_cache:
        (q_ref, kp_ref, kc_ref, kn_ref, vtp_ref, vtc_ref, vtn_ref, ck_ref, cvt_ref,
         x_ref, m_ref, ng_ref, sink_ref, wo_ref, o_ref) = refs
    else:
        (q_ref, kc_ref, vtc_ref, x_ref, m_ref, ng_ref, sink_ref, wo_ref, o_ref) = refs
    qi = pl.program_id(1)
    lane = lax.broadcasted_iota(jnp.int32, (tq, LANES), 1)

    def band(base, rows):
        kpos = base + lax.broadcasted_iota(jnp.int32, (rows, tq), 0)
        qpos = qi * tq + lax.broadcasted_iota(jnp.int32, (rows, tq), 1)
        ok = (jnp.abs(qpos - kpos) <= WINDOW) & (kpos >= 0) & (kpos < seq)
        return jnp.concatenate([ok] * GQA_GROUP, axis=1)

    def attention(blocks):
        units = [(r0, segs, g) for r0, segs in blocks for g in range(GQA_KV_HEADS)]

        def scores(u):
            r0, segs, g = units[u]
            b = g % 2
            kcols = slice((g // 2) * LANES, (g // 2 + 1) * LANES)
            keep = (lane < HEAD_DIM) if b == 0 else (lane >= HEAD_DIM)
            qs = []
            for h in range(g * GQA_GROUP, (g + 1) * GQA_GROUP):
                qc = q_ref[r0:r0 + tq, (h // 2) * LANES:(h // 2 + 1) * LANES]
                qa = qc if h % 2 == b else pltpu.roll(qc, HEAD_DIM, 1)
                qs.append(jnp.where(keep, qa, jnp.zeros_like(qa)))
            qg = jnp.concatenate(qs, axis=0)
            s_list = []
            for k_ref, _, ok in segs:
                s = _dot_nt(k_ref[:, kcols].astype(BF16), qg)
                s_list.append(s if ok is None else jnp.where(ok, s, NEG_INF))
            return s_list

        def group(u, s_list):
            _, segs, g = units[u]
            vrows = slice(g * HEAD_DIM, (g + 1) * HEAD_DIM)
            sink = jnp.concatenate([jnp.full((1, tq), sink_ref[h] * LOG2E, F32)
                                    for h in range(g * GQA_GROUP, (g + 1) * GQA_GROUP)], axis=1)
            ov, den, mx = _attend(s_list, [vt_ref[vrows, :].astype(BF16) for _, vt_ref, _ in segs], floor=sink)
            ov = ov * (1.0 / (den + jnp.exp2(sink - mx)))
            return jnp.concatenate([ov[:, hh * tq:(hh + 1) * tq] for hh in range(GQA_GROUP)], axis=0)

        groups = _pipelined(len(units), scores, group)
        return [jnp.concatenate(groups[i:i + GQA_KV_HEADS], axis=0).T.astype(BF16)
                for i in range(0, len(units), GQA_KV_HEADS)]

    if n_cache:
        blocks = [(0, [(kp_ref, vtp_ref, band(qi * tq - WINDOW, WINDOW)),
                       (kc_ref, vtc_ref, band(qi * tq, tq)),
                       (kn_ref, vtn_ref, band(qi * tq + tq, WINDOW)),
                       (ck_ref, cvt_ref, None)])]
    else:
        blocks = [(s * tq, [(kc_ref.at[pl.ds(s * tq, tq)], vtc_ref.at[s], None)]) for s in range(nseq)]
    attn = jnp.concatenate(attention(blocks), axis=0)
    mixed = _dot(attn, wo_ref[...])
    o_ref[...] = x_ref[...] + _rms(mixed, m_ref[2:3, :] * ng_ref[1:2, :])


def _mix1(q, k, vt, cache, x, m, norm_g, sink, w_out, *, batch, seq, tq, cond_row, nseq=1):
    nq = seq // tq
    n_cache = 0 if cache is None else cache[0].shape[1]
    rows = nseq * tq
    q_spec = pl.BlockSpec((rows, D_MODEL), lambda b, i: (b * nq + i, 0))
    kc_spec = pl.BlockSpec((rows, KV_WIDTH), lambda b, i: (b * nq + i, 0))
    vtc_spec = pl.BlockSpec((None, KV_WIDTH, tq), lambda b, i: (b, 0, i))
    if n_cache:
        nw = seq // WINDOW
        per = tq // WINDOW
        prev = lambda i: jnp.maximum(i * per - 1, 0)
        nxt = lambda i: jnp.minimum(i * per + per, nw - 1)
        in_specs = [q_spec,
                    pl.BlockSpec((WINDOW, KV_WIDTH), lambda b, i: (b * nw + prev(i), 0)), kc_spec,
                    pl.BlockSpec((WINDOW, KV_WIDTH), lambda b, i: (b * nw + nxt(i), 0)),
                    pl.BlockSpec((None, KV_WIDTH, WINDOW), lambda b, i: (b, 0, prev(i))), vtc_spec,
                    pl.BlockSpec((None, KV_WIDTH, WINDOW), lambda b, i: (b, 0, nxt(i))),
                    pl.BlockSpec((None, n_cache, KV_WIDTH), lambda b, i: (b, 0, 0)),
                    pl.BlockSpec((None, KV_WIDTH, n_cache), lambda b, i: (b, 0, 0))]
        args = [q, k, k, k, vt, vt, vt, cache[0], cache[1]]
    else:
        in_specs = [q_spec, kc_spec, pl.BlockSpec((nseq, KV_WIDTH, tq), lambda b, i: (b, 0, 0))]
        args = [q, k, vt]
    in_specs += [
        pl.BlockSpec((rows, D_MODEL), lambda b, i: (b * nq + i, 0)),
        pl.BlockSpec((None, N_MOD, D_MODEL), lambda b, i: (cond_row(b * nseq), 0, 0)),
        pl.BlockSpec((4, D_MODEL), lambda b, i: (0, 0)),
        pl.BlockSpec(memory_space=pltpu.SMEM),
        pl.BlockSpec((D_MODEL, D_MODEL), lambda b, i: (0, 0)),
    ]
    args += [x, m, norm_g, sink, w_out]
    return pl.pallas_call(
        functools.partial(_mix1_kernel, seq=seq, tq=tq, n_cache=n_cache, nseq=nseq),
        grid=(batch // nseq, nq),
        in_specs=in_specs,
        out_specs=pl.BlockSpec((rows, D_MODEL), lambda b, i: (b * nq + i, 0)),
        out_shape=jax.ShapeDtypeStruct((batch * seq, D_MODEL), F32),
        compiler_params=_params("arbitrary", "arbitrary"),
        name="mix_gqa",
    )(*args)


FFN_CHUNKS = (1536, 1280)


def _ffn_kernel(*refs, n_side, proj, sub):
    x_ref, m_ref, ng_ref, wgu_ref, wd_ref = refs[:5]
    n_pin = 0 if proj is None else (5 if proj[1] else 3)
    n_pout = 0 if proj is None else len(proj[0])
    n_in = 5 + n_pin + n_side
    o_ref = refs[n_in]
    _side_convert(refs[5 + n_pin:n_in], refs[n_in + 1 + n_pout:])
    for r0 in range(0, x_ref.shape[0], sub):
        rows = slice(r0, r0 + sub)
        h = (_rms(x_ref[rows, :], ng_ref[2:3, :] * (1.0 + m_ref[4:5, :])) + m_ref[3:4, :]).astype(BF16)
        acc = jnp.zeros(h.shape, F32)
        c0 = 0
        for width in FFN_CHUNKS:
            gate = _dot(h, wgu_ref[:, c0:c0 + width])
            up = _dot(h, wgu_ref[:, FFN_HIDDEN + c0:FFN_HIDDEN + c0 + width])
            act = (_silu(gate) * up).astype(BF16)
            acc = acc + _dot(act, wd_ref[c0:c0 + width, :])
            c0 += width
        o_ref[rows, :] = x_ref[rows, :] + _rms(acc, m_ref[5:6, :] * ng_ref[3:4, :])
    if proj is not None:
        outs, rope, sblk = proj
        pm_ref, pg_ref, pw_ref = refs[5:8]
        _project(o_ref[...], pm_ref, pg_ref, pw_ref, refs[8:10] if rope else None, outs,
                 refs[n_in + 1:n_in + 1 + n_pout], sblk)


def _ffn(x, m, norm_g, w_gu, w_down, *, tm, sub, cond_row, side=(), proj=None):
    rows = x.shape[0]
    side_in, side_out, side_shape = _side_specs(side, rows // tm, lambda i: i)
    proj_in, proj_args, proj_out, proj_shape, kproj = [], [], [], [], None
    if proj is not None:
        pm, pg, pw, outs, seq, rope = proj
        proj_in, proj_args, proj_out, proj_shape, sblk = _project_specs(pm, pg, pw, outs, rows=rows, tm=tm,
                                                                        cond_row=cond_row, seq=seq, rope=rope)
        kproj = (outs, rope, sblk)
    return pl.pallas_call(
        functools.partial(_ffn_kernel, n_side=len(side), proj=kproj, sub=sub),
        grid=(rows // tm,),
        in_specs=[
            pl.BlockSpec((tm, D_MODEL), lambda i: (i, 0)),
            pl.BlockSpec((None, N_MOD, D_MODEL), lambda i: (cond_row(i), 0, 0)),
            pl.BlockSpec((4, D_MODEL), lambda i: (0, 0)),
            _resident((D_MODEL, 2 * FFN_HIDDEN)),
            _resident((FFN_HIDDEN, D_MODEL)),
        ] + proj_in + side_in,
        out_specs=[pl.BlockSpec((tm, D_MODEL), lambda i: (i, 0))] + proj_out + side_out,
        out_shape=[jax.ShapeDtypeStruct((rows, D_MODEL), F32)] + proj_shape + side_shape,
        compiler_params=_params("arbitrary"),
        name="ffn",
    )(x, m, norm_g, w_gu, w_down, *proj_args, *side)


LAM_INIT_0 = 0.8 - 0.6 * math.exp(-0.3 * 0)
CTX_SEQS_PER_STEP = 4


def _trunk(x, m, caches, weights, *, batch, seq, cond_of_batch):
    (norm_g0, w_in, conv_w, conv_b, ln_g, ln_b, lam, subln_g, w_out0, w_gu0, w_down0,
     norm_g1, w_qkv, sink, w_out1, w_gu1, w_down1) = weights
    m0, m1 = m
    ctx = caches is None
    rope = not ctx
    tm = 1024 if ctx else 512
    tm_ffn = 512
    tm_ffn1 = 1024 if ctx else 512
    ffn_sub = 256
    tq = 256
    row_of = lambda t: (lambda i: cond_of_batch((i * t) // seq))

    o0 = 2 * CONV_CH
    ko, vo = o0 + DIFF_WIDTH, o0 + 2 * DIFF_WIDTH
    outs0 = [("glu", 0, o0, F32, False, None), ("nat", o0, DIFF_WIDTH, BF16, True, Q_SCALE),
             ("nat", ko, DIFF_WIDTH, BF16, True, None), ("T", vo, DIFF_WIDTH, BF16, False, None)]
    if ctx:
        outs0 += [("T", ko, DIFF_WIDTH, F32, False, None), ("heads", vo, DIFF_WIDTH, F32, False, None)]
    res = _pre(x, m0, norm_g0, w_in, tuple(outs0), tm=tm, cond_row=row_of(tm), seq=seq, rope=rope)
    u, q, k, vt = res[:4]
    new0 = tuple(res[4:])
    x, *conv0 = _mix0(u, q, k, vt, None if ctx else caches[0], x, m0, norm_g0, conv_w, conv_b,
                      ln_g, ln_b, lam, subln_g, w_out0, batch=batch, seq=seq, tq=tq, cond_row=cond_of_batch,
                      lam_init=LAM_INIT_0, side=(w_gu0, w_down0, w_qkv) if ctx else (),
                      nseq=CTX_SEQS_PER_STEP if ctx else 1)
    if ctx:
        w_gu0, w_down0, w_qkv = conv0

    outs1 = [("nat", 0, Q_WIDTH, BF16, True, Q_SCALE), ("nat", Q_WIDTH, KV_WIDTH, BF16, True, None)]
    if ctx:
        outs1 += [("T", Q_WIDTH + KV_WIDTH, KV_WIDTH, F32, False, None), ("T", Q_WIDTH, KV_WIDTH, F32, False, None)]
    else:
        outs1 += [("T", Q_WIDTH + KV_WIDTH, KV_WIDTH, BF16, False, None)]
    res = _ffn(x, m0, norm_g0, w_gu0, w_down0, tm=tm_ffn, sub=ffn_sub, cond_row=row_of(tm_ffn),
               side=(w_out1, w_gu1, w_down1) if ctx else (),
               proj=(m1, norm_g1, w_qkv, tuple(outs1), seq, rope))
    x, q, k, vt = res[:4]
    new1 = (res[4], vt) if ctx else ()
    if ctx:
        w_out1, w_gu1, w_down1 = res[5:]
    x = _mix1(q, k, vt, None if ctx else caches[1], x, m1, norm_g1, sink, w_out1,
              batch=batch, seq=seq, tq=tq, cond_row=cond_of_batch, nseq=CTX_SEQS_PER_STEP if ctx else 1)
    x, = _ffn(x, m1, norm_g1, w_gu1, w_down1, tm=tm_ffn1, sub=ffn_sub, cond_row=row_of(tm_ffn1))
    return x, new0 + new1, (w_gu0, w_down0, w_qkv, w_out1, w_gu1, w_down1)


def kernel(x_prompt, x_sample, cache_k0, cache_v0, cache_k1, cache_v1, c, c_ctx, l0_mod_w, l0_mod_b, l0_norm_g, l0_w_in, l0_conv_w, l0_conv_b, l0_conv_ln_g, l0_conv_ln_b, l0_lambda, l0_subln_g, l0_w_out, l0_w_gu, l0_w_down, l1_mod_w, l1_mod_b, l1_norm_g, l1_w_qkv, l1_sink, l1_w_out, l1_w_gu, l1_w_down):
    batch, seq, d = x_prompt.shape
    dec_batch, dec_seq, _ = x_sample.shape
    n_past = cache_k0.shape[1]
    assert d == D_MODEL and seq == 256 and batch % CTX_SEQS_PER_STEP == 0 and (batch * seq) % 1024 == 0
    assert dec_seq % 512 == 0 and dec_seq % GRID_W == 0 and n_past % LANES == 0 and 1 + dec_batch <= MOD_ROWS

    cond = jnp.concatenate([c_ctx[None, :], c, jnp.zeros((MOD_ROWS - 1 - dec_batch, d), F32)], axis=0)
    m = _modulation(cond, l0_mod_w, l0_mod_b, l1_mod_w, l1_mod_b)

    row = lambda v: v.reshape(1, -1)
    layer0 = (l0_norm_g, l0_w_in.astype(BF16), jnp.broadcast_to(l0_conv_w, (CONV_WIDTH, 8, CONV_CH)), row(l0_conv_b),
              row(l0_conv_ln_g), row(l0_conv_ln_b), l0_lambda, l0_subln_g.reshape(LANES, 1),
              l0_w_out.astype(BF16))
    weights = layer0 + (l0_w_gu, l0_w_down, l1_norm_g, l1_w_qkv, l1_sink, l1_w_out, l1_w_gu, l1_w_down)

    y_prompt, (kt0, v0, kt1, vt1), bf16_w = _trunk(x_prompt.reshape(batch * seq, d), m, None, weights,
                                                   batch=batch, seq=seq, cond_of_batch=lambda b: 0)
    w_gu0, w_down0, w_qkv, w_out1, w_gu1, w_down1 = bf16_w
    weights = layer0 + (w_gu0, w_down0, l1_norm_g, w_qkv, l1_sink, w_out1, w_gu1, w_down1)
    caches = ((cache_k0.reshape(dec_batch, n_past, -1),
               jnp.swapaxes(cache_v0.reshape(dec_batch, n_past, -1), 1, 2)),
              (cache_k1.reshape(dec_batch, n_past, -1),
               jnp.swapaxes(cache_v1.reshape(dec_batch, n_past, -1), 1, 2)))
    y_sample, _, _ = _trunk(x_sample.reshape(dec_batch * dec_seq, d), m, caches, weights,
                            batch=dec_batch, seq=dec_seq, cond_of_batch=lambda b: 1 + b)
    new_k0 = kt0.reshape(batch, DIFF_HEADS, 2, HEAD_DIM, seq).transpose(0, 4, 1, 2, 3)
    new_v0 = v0.reshape(batch, seq, DIFF_HEADS, 2 * HEAD_DIM)
    new_k1 = kt1.reshape(batch, GQA_KV_HEADS, HEAD_DIM, seq).transpose(0, 3, 1, 2)
    new_v1 = vt1.reshape(batch, GQA_KV_HEADS, HEAD_DIM, seq).transpose(0, 3, 1, 2)
    return (y_prompt.reshape(batch, seq, d), y_sample.reshape(dec_batch, dec_seq, d),
            new_k0, new_v0, new_k1, new_v1)
```
